```python
import jax, jax.numpy as jnp
from jax import lax
import numpy as np

D_MODEL = 1024
BATCH = 4
SEQ = 4096
DEPTH = 2

HEAD_DIM = 64
ROT_DIM = HEAD_DIM // 4
ROPE_THETA = 500000.0
BLK = 128
A_HEADS = 8
A_PATTERNS = ((128, 1), (512, 4), (2048, 16))
B_Q_HEADS = 8
B_KV_HEADS = 2
B_WINDOW = 128
C_HEADS = 16
C_KV_HEADS = 4
CMP_LEN = 32
CMP_STRIDE = 16
CMP_HIDDEN = 2 * HEAD_DIM
SLC_BLOCK = 64
SLC_TOP_N = 16
NSA_WINDOW = 512
NSA_QBLK = 64
SELECT_FORCE = 1.0e4
N_EXPERTS = 256
TOP_K = 8
N_GROUPS = 8
TOPK_GROUPS = 4
EXPERT_FF = 256
SHARED_FF = 256
ROUTED_SCALE = 2.5
MOE_BLOCK = 128
DEEPNORM_ALPHA = (2 * DEPTH) ** 0.25
DEEPNORM_BETA = (8 * DEPTH) ** -0.25
LN_EPS = 1e-5

EVEN_IN = 3 * A_HEADS * HEAD_DIM + B_Q_HEADS * HEAD_DIM + 2 * B_KV_HEADS * HEAD_DIM
EVEN_OUT = (A_HEADS + B_Q_HEADS) * HEAD_DIM
ODD_IN = C_HEADS * HEAD_DIM + 6 * C_KV_HEADS * HEAD_DIM + 3 * C_HEADS
ODD_OUT = C_HEADS * HEAD_DIM
N_EVEN = (DEPTH + 1) // 2
N_ODD = DEPTH // 2

kernel_name = 'hybrid_dilated_swa_nsa_moe_deepnorm'


def layer_norm(x, g, b):
    xf = x.astype(jnp.float32)
    mu = jnp.mean(xf, -1, keepdims=True)
    var = jnp.mean(jnp.square(xf - mu), -1, keepdims=True)
    y = (xf - mu) * lax.rsqrt(var + LN_EPS) * g.astype(jnp.float32) + b.astype(jnp.float32)
    return y.astype(x.dtype)


def rope_tables(positions, dtype):
    inv_freq = jnp.asarray(ROPE_THETA ** (-np.arange(0, ROT_DIM, 2) / ROT_DIM), jnp.float32)
    ang = positions.astype(jnp.float32)[..., None] * inv_freq
    return jnp.cos(ang)[:, :, None, :].astype(dtype), jnp.sin(ang)[:, :, None, :].astype(dtype)


def apply_rope(x, cos, sin):
    half = ROT_DIM // 2
    x1 = x[..., :half]
    x2 = x[..., half:ROT_DIM]
    return jnp.concatenate([x1 * cos - x2 * sin, x2 * cos + x1 * sin, x[..., ROT_DIM:]], axis=-1)


def banded_attention(q, k, v, max_dist, sinks=None):
    n, g, r, length, hd = q.shape
    nb = -(-length // BLK)
    lp = nb * BLK
    n_prev = -(-max_dist // BLK)
    q = jnp.pad(q, ((0, 0), (0, 0), (0, 0), (0, lp - length), (0, 0)))
    kv_pad = ((0, 0), (0, 0), (n_prev * BLK, lp - length), (0, 0))
    kb = jnp.pad(k, kv_pad).reshape(n, g, nb + n_prev, BLK, hd)
    vb = jnp.pad(v, kv_pad).reshape(n, g, nb + n_prev, BLK, hd)
    kw = jnp.concatenate([kb[:, :, j:j + nb] for j in range(n_prev + 1)], axis=3)
    vw = jnp.concatenate([vb[:, :, j:j + nb] for j in range(n_prev + 1)], axis=3)
    qb = q.reshape(n, g, r, nb, BLK, hd)
    s = jnp.einsum('ngrbqd,ngbkd->ngrbqk', qb, kw).astype(jnp.float32) * (hd ** -0.5)
    qpos = np.arange(nb)[:, None] * BLK + np.arange(BLK)[None, :]
    kpos = (np.arange(nb)[:, None] - n_prev) * BLK + np.arange((n_prev + 1) * BLK)[None, :]
    dist = qpos[:, :, None] - kpos[:, None, :]
    mask = (dist >= 0) & (dist <= max_dist) & (kpos[:, None, :] >= 0)
    s = jnp.where(mask, s, -jnp.inf)
    lse = jax.nn.logsumexp(s, axis=-1)
    if sinks is not None:
        lse = jnp.logaddexp(lse, sinks.astype(jnp.float32)[None, :, :, None, None])
    p = jnp.exp(s - lse[..., None]).astype(v.dtype)
    o = jnp.einsum('ngrbqk,ngbkd->ngrbqd', p, vw)
    return (o.reshape(n, g, r, lp, hd)[:, :, :, :length], lse.reshape(n, g, r, lp)[:, :, :, :length])


def dilated_window_attention(q, k, v, window, dilation):
    b, h, s, hd = q.shape
    length = s // dilation

    def by_residue(t):
        return t.reshape(b, h, length, dilation, hd).transpose(0, 3, 1, 2, 4).reshape(b * dilation, h, length, hd)

    o, lse = banded_attention(by_residue(q)[:, :, None], by_residue(k), by_residue(v), window // dilation)
    o = o.reshape(b, dilation, h, length, hd).transpose(0, 2, 3, 1, 4).reshape(b, h, s, hd)
    lse = lse.reshape(b, dilation, h, length).transpose(0, 2, 3, 1).reshape(b, h, s)
    return o, lse


def even_mixer(x, w_in, sinks, w_out, cos, sin):
    b, s, _ = x.shape
    sizes = [A_HEADS * HEAD_DIM] * 3 + [B_Q_HEADS * HEAD_DIM, B_KV_HEADS * HEAD_DIM, B_KV_HEADS * HEAD_DIM]
    qa, ka, va, qb, kb, vb = jnp.split(x @ w_in, np.cumsum(sizes)[:-1].tolist(), axis=-1)
    heads = lambda t, nh: t.reshape(b, s, nh, HEAD_DIM)
    qa = apply_rope(heads(qa, A_HEADS), cos, sin).transpose(0, 2, 1, 3)
    ka = apply_rope(heads(ka, A_HEADS), cos, sin).transpose(0, 2, 1, 3)
    va = heads(va, A_HEADS).transpose(0, 2, 1, 3)
    outs, lses = [], []
    for window, dilation in A_PATTERNS:
        o, lse = dilated_window_attention(qa, ka, va, window, dilation)
        outs.append(o)
        lses.append(lse)
    wts = jax.nn.softmax(jnp.stack(lses), axis=0).astype(x.dtype)
    oa = jnp.einsum('pbhs,pbhsd->bshd', wts, jnp.stack(outs)).reshape(b, s, A_HEADS * HEAD_DIM)
    rep = B_Q_HEADS // B_KV_HEADS
    qb = apply_rope(heads(qb, B_Q_HEADS), cos, sin).reshape(b, s, B_KV_HEADS, rep, HEAD_DIM).transpose(0, 2, 3, 1, 4)
    kb = apply_rope(heads(kb, B_KV_HEADS), cos, sin).transpose(0, 2, 1, 3)
    vb = heads(vb, B_KV_HEADS).transpose(0, 2, 1, 3)
    ob, _ = banded_attention(qb, kb, vb, B_WINDOW - 1, sinks)
    ob = ob.transpose(0, 3, 1, 2, 4).reshape(b, s, B_Q_HEADS * HEAD_DIM)
    return jnp.concatenate([oa, ob], axis=-1) @ w_out


def compress_blocks(k, pos_emb, w1, w2):
    b, g, s, hd = k.shape
    nc = (s - CMP_LEN) // CMP_STRIDE + 1
    idx = np.arange(nc)[:, None] * CMP_STRIDE + np.arange(CMP_LEN)[None, :]
    blocks = (k[:, :, idx] + pos_emb).reshape(b, g, nc, CMP_LEN * hd)
    return jax.nn.gelu(blocks @ w1) @ w2


def nsa_compressed_and_selected(q, kc, vc, ks, vs):
    b, g, r, s, hd = q.shape
    nc = kc.shape[2]
    nsb = s // SLC_BLOCK
    n_sel = min(SLC_TOP_N, nsb)
    scale = hd ** -0.5
    ksb = ks.reshape(b, g, nsb, SLC_BLOCK, hd)
    vsb = vs.reshape(b, g, nsb, SLC_BLOCK, hd)
    cmp_end = np.arange(nc) * CMP_STRIDE + CMP_LEN - 1
    cs = np.arange(nc)[:, None] * CMP_STRIDE
    js = np.arange(nsb)[None, :] * SLC_BLOCK
    overlap = jnp.asarray(np.clip(np.minimum(cs + CMP_LEN, js + SLC_BLOCK) - np.maximum(cs, js), 0, None) / CMP_LEN, jnp.float32)
    blk_ids = np.arange(nsb)[None, :]
    gather_blocks = jax.vmap(jax.vmap(lambda blocks, ix: blocks[ix]))

    def query_block(i):
        t = i * NSA_QBLK + jnp.arange(NSA_QBLK)
        qb = lax.dynamic_slice_in_dim(q, i * NSA_QBLK, NSA_QBLK, axis=3)
        sc = jnp.einsum('bgrqd,bgcd->bgrqc', qb, kc).astype(jnp.float32) * scale
        cmask = cmp_end[None, :] <= t[:, None]
        sc = jnp.where(cmask, sc, -jnp.inf)
        m = jnp.max(sc, axis=-1, keepdims=True)
        ex = jnp.where(cmask, jnp.exp(sc - jnp.where(jnp.isfinite(m), m, 0.0)), 0.0)
        den = jnp.sum(ex, -1, keepdims=True)
        pc = ex / jnp.where(den > 0, den, 1.0)
        o_cmp = jnp.einsum('bgrqc,bgcd->bgrqd', pc.astype(vc.dtype), vc)
        imp = jnp.einsum('bgrqc,cj->bgqj', pc, overlap)
        cur = (t // SLC_BLOCK)[:, None]
        future = blk_ids > cur
        forced = (blk_ids == 0) | (blk_ids == cur) | (blk_ids == cur - 1)
        score = jnp.where(future, -1.0, jnp.where(forced, SELECT_FORCE, imp))
        _, sel = lax.top_k(score, n_sel)
        k_sel = gather_blocks(ksb, sel).reshape(b, g, NSA_QBLK, n_sel * SLC_BLOCK, hd)
        v_sel = gather_blocks(vsb, sel).reshape(b, g, NSA_QBLK, n_sel * SLC_BLOCK, hd)
        kpos = (sel[..., None] * SLC_BLOCK + jnp.arange(SLC_BLOCK)).reshape(b, g, NSA_QBLK, n_sel * SLC_BLOCK)
        ss = jnp.einsum('bgrqd,bgqkd->bgrqk', qb, k_sel).astype(jnp.float32) * scale
        ss = jnp.where((kpos <= t[:, None])[:, :, None], ss, -jnp.inf)
        ps = jax.nn.softmax(ss, axis=-1).astype(v_sel.dtype)
        o_slc = jnp.einsum('bgrqk,bgqkd->bgrqd', ps, v_sel)
        return o_cmp, o_slc

    o_cmp, o_slc = lax.map(query_block, jnp.arange(s // NSA_QBLK))
    unblock = lambda o: o.transpose(1, 2, 3, 0, 4, 5).reshape(b, g, r, s, hd)
    return unblock(o_cmp), unblock(o_slc)


def odd_mixer(x, w_in, cmpk_pos, cmpk_w1, cmpk_w2, cmpv_pos, cmpv_w1, cmpv_w2, w_out, cos, sin):
    b, s, _ = x.shape
    kvw = C_KV_HEADS * HEAD_DIM
    sizes = [C_HEADS * HEAD_DIM] + [kvw] * 6 + [3 * C_HEADS]
    q, kc, vc, ks, vs, kw, vw, gate = jnp.split(x @ w_in, np.cumsum(sizes)[:-1].tolist(), axis=-1)
    rep = C_HEADS // C_KV_HEADS
    q = apply_rope(q.reshape(b, s, C_HEADS, HEAD_DIM), cos, sin).reshape(b, s, C_KV_HEADS, rep, HEAD_DIM).transpose(0, 2, 3, 1, 4)
    kv_heads = lambda t: t.reshape(b, s, C_KV_HEADS, HEAD_DIM)
    to_bgsd = lambda t: t.transpose(0, 2, 1, 3)
    kc = to_bgsd(apply_rope(kv_heads(kc), cos, sin))
    ks = to_bgsd(apply_rope(kv_heads(ks), cos, sin))
    kw = to_bgsd(apply_rope(kv_heads(kw), cos, sin))
    vc, vs, vw = to_bgsd(kv_heads(vc)), to_bgsd(kv_heads(vs)), to_bgsd(kv_heads(vw))
    kc = compress_blocks(kc, cmpk_pos, cmpk_w1, cmpk_w2)
    vc = compress_blocks(vc, cmpv_pos, cmpv_w1, cmpv_w2)
    o_cmp, o_slc = nsa_compressed_and_selected(q, kc, vc, ks, vs)
    o_win, _ = banded_attention(q, kw, vw, NSA_WINDOW - 1)
    g = jax.nn.sigmoid(gate.astype(jnp.float32)).astype(x.dtype).reshape(b, s, C_HEADS, 3)
    to_bshd = lambda o: o.transpose(0, 3, 1, 2, 4).reshape(b, s, C_HEADS, HEAD_DIM)
    o = g[..., 0:1] * to_bshd(o_cmp) + g[..., 1:2] * to_bshd(o_slc) + g[..., 2:3] * to_bshd(o_win)
    return o.reshape(b, s, C_HEADS * HEAD_DIM) @ w_out


def moe_ffn(x, router_w, router_b, w_gate, w_up, w_down, sh_gate, sh_up, sh_down):
    b, s, d = x.shape
    n_tok = b * s
    xt = x.reshape(n_tok, d)
    affinity = jax.nn.sigmoid((xt @ router_w).astype(jnp.float32))
    biased = affinity + router_b.astype(jnp.float32)
    grouped = biased.reshape(n_tok, N_GROUPS, N_EXPERTS // N_GROUPS)
    group_score = lax.top_k(grouped, 2)[0].sum(-1)
    _, top_groups = lax.top_k(group_score, TOPK_GROUPS)
    group_ok = jnp.any(top_groups[..., None] == jnp.arange(N_GROUPS), axis=-2)
    expert_ok = jnp.repeat(group_ok, N_EXPERTS // N_GROUPS, axis=-1)
    _, eidx = lax.top_k(jnp.where(expert_ok, biased, -jnp.inf), TOP_K)
    gates = jnp.take_along_axis(affinity, eidx, axis=-1)
    gates = gates / jnp.sum(gates, -1, keepdims=True) * ROUTED_SCALE
    n_asg = n_tok * TOP_K
    flat_e = eidx.reshape(n_asg)
    order = jnp.argsort(flat_e)
    se = flat_e[order]
    sg = gates.reshape(n_asg)[order]
    stok = (order // TOP_K).astype(jnp.int32)
    counts = jnp.bincount(flat_e, length=N_EXPERTS)
    starts = jnp.cumsum(counts) - counts
    padded = (counts + MOE_BLOCK - 1) // MOE_BLOCK * MOE_BLOCK
    pad_end = jnp.cumsum(padded)
    dest = (pad_end - padded)[se] + jnp.arange(n_asg) - starts[se]
    n_blocks = -(-n_asg // MOE_BLOCK) + N_EXPERTS
    n_rows = n_blocks * MOE_BLOCK
    row_tok = jnp.full((n_rows,), n_tok, jnp.int32).at[dest].set(stok)
    row_gate = jnp.zeros((n_rows,), jnp.float32).at[dest].set(sg)
    block_expert = jnp.minimum(jnp.searchsorted(pad_end, jnp.arange(n_blocks) * MOE_BLOCK, side='right'), N_EXPERTS - 1)
    x_pad = jnp.concatenate([xt, jnp.zeros((1, d), xt.dtype)], axis=0)

    def expert_block(args):
        tok, gate, e = args
        xb = x_pad[tok]
        h = jax.nn.silu(xb @ w_gate[e]) * (xb @ w_up[e])
        return (h @ w_down[e]) * gate[:, None].astype(xb.dtype)

    y = lax.map(expert_block, (row_tok.reshape(n_blocks, MOE_BLOCK), row_gate.reshape(n_blocks, MOE_BLOCK), block_expert))
    routed = jnp.zeros((n_tok + 1, d), x.dtype).at[row_tok].add(y.reshape(n_rows, d))[:n_tok]
    shared = (jax.nn.silu(xt @ sh_gate) * (xt @ sh_up)) @ sh_down
    return (routed + shared).reshape(b, s, d)


def setup_inputs(seed: int = 0) -> dict:
    key = jax.random.key(seed)
    ks = iter(jax.random.split(key, 32))
    nrm = lambda shape, scale: jax.random.normal(next(ks), shape, jnp.float32) * scale
    x = nrm((BATCH, SEQ, D_MODEL), 1.0)
    positions = (jax.random.randint(next(ks), (BATCH, 1), 0, 2048, dtype=jnp.int32) + jnp.arange(SEQ, dtype=jnp.int32)[None, :])
    return {
        'x': x,
        'positions': positions,
        'even_w_in': nrm((N_EVEN, D_MODEL, EVEN_IN), D_MODEL ** -0.5),
        'even_sinks': nrm((N_EVEN, B_KV_HEADS, B_Q_HEADS // B_KV_HEADS), 0.5),
        'even_w_out': nrm((N_EVEN, EVEN_OUT, D_MODEL), EVEN_OUT ** -0.5 * DEEPNORM_BETA),
        'odd_w_in': nrm((N_ODD, D_MODEL, ODD_IN), D_MODEL ** -0.5),
        'odd_cmpk_pos': nrm((N_ODD, CMP_LEN, HEAD_DIM), 0.1),
        'odd_cmpk_w1': nrm((N_ODD, CMP_LEN * HEAD_DIM, CMP_HIDDEN), (CMP_LEN * HEAD_DIM) ** -0.5),
        'odd_cmpk_w2': nrm((N_ODD, CMP_HIDDEN, HEAD_DIM), CMP_HIDDEN ** -0.5),
        'odd_cmpv_pos': nrm((N_ODD, CMP_LEN, HEAD_DIM), 0.1),
        'odd_cmpv_w1': nrm((N_ODD, CMP_LEN * HEAD_DIM, CMP_HIDDEN), (CMP_LEN * HEAD_DIM) ** -0.5),
        'odd_cmpv_w2': nrm((N_ODD, CMP_HIDDEN, HEAD_DIM), CMP_HIDDEN ** -0.5),
        'odd_w_out': nrm((N_ODD, ODD_OUT, D_MODEL), ODD_OUT ** -0.5 * DEEPNORM_BETA),
        'mix_ln_g': 1.0 + nrm((DEPTH, D_MODEL), 0.01),
        'mix_ln_b': nrm((DEPTH, D_MODEL), 0.01),
        'moe_router_w': nrm((DEPTH, D_MODEL, N_EXPERTS), D_MODEL ** -0.5),
        'moe_router_b': nrm((DEPTH, N_EXPERTS), 0.01),
        'moe_w_gate': nrm((DEPTH, N_EXPERTS, D_MODEL, EXPERT_FF), D_MODEL ** -0.5),
        'moe_w_up': nrm((DEPTH, N_EXPERTS, D_MODEL, EXPERT_FF), D_MODEL ** -0.5),
        'moe_w_down': nrm((DEPTH, N_EXPERTS, EXPERT_FF, D_MODEL), EXPERT_FF ** -0.5 * DEEPNORM_BETA),
        'moe_sh_gate': nrm((DEPTH, D_MODEL, SHARED_FF), D_MODEL ** -0.5),
        'moe_sh_up': nrm((DEPTH, D_MODEL, SHARED_FF), D_MODEL ** -0.5),
        'moe_sh_down': nrm((DEPTH, SHARED_FF, D_MODEL), SHARED_FF ** -0.5 * DEEPNORM_BETA),
        'ffn_ln_g': 1.0 + nrm((DEPTH, D_MODEL), 0.01),
        'ffn_ln_b': nrm((DEPTH, D_MODEL), 0.01),
    }


def reference(x, positions, even_w_in, even_sinks, even_w_out, odd_w_in, odd_cmpk_pos, odd_cmpk_w1, odd_cmpk_w2, odd_cmpv_pos, odd_cmpv_w1, odd_cmpv_w2, odd_w_out, mix_ln_g, mix_ln_b, moe_router_w, moe_router_b, moe_w_gate, moe_w_up, moe_w_down, moe_sh_gate, moe_sh_up, moe_sh_down, ffn_ln_g, ffn_ln_b):
    cos, sin = rope_tables(positions, x.dtype)
    for layer in range(DEPTH):
        j = layer // 2
        if layer % 2 == 0:
            mixed = even_mixer(x, even_w_in[j], even_sinks[j], even_w_out[j], cos, sin)
        else:
            mixed = odd_mixer(x, odd_w_in[j], odd_cmpk_pos[j], odd_cmpk_w1[j], odd_cmpk_w2[j], odd_cmpv_pos[j], odd_cmpv_w1[j], odd_cmpv_w2[j], odd_w_out[j], cos, sin)
        x = layer_norm(DEEPNORM_ALPHA * x + mixed, mix_ln_g[layer], mix_ln_b[layer])
        ffn = moe_ffn(x, moe_router_w[layer], moe_router_b[layer], moe_w_gate[layer], moe_w_up[layer], moe_w_down[layer], moe_sh_gate[layer], moe_sh_up[layer], moe_sh_down[layer])
        x = layer_norm(DEEPNORM_ALPHA * x + ffn, ffn_ln_g[layer], ffn_ln_b[layer])
    return x
```

```python
import functools

import numpy as np
import jax
import jax.numpy as jnp
from jax import lax
from jax.experimental import pallas as pl
from jax.experimental.pallas import tpu as pltpu

F32 = jnp.float32
BF16 = jnp.bfloat16

LANES = 128
HEAD_DIM = 64
ROT_DIM = HEAD_DIM // 4
ROT_HALF = ROT_DIM // 2
ROPE_THETA = 500000.0
QBLK = 128
A_HEADS = 8
A_PATTERNS = ((128, 1), (512, 4), (2048, 16))
B_Q_HEADS = 8
B_KV_HEADS = 2
B_WINDOW = 128
C_HEADS = 16
C_KV_HEADS = 4
CMP_LEN = 32
CMP_STRIDE = 16
CMP_HIDDEN = 2 * HEAD_DIM
SLC_BLOCK = 64
SLC_SHIFT = 6
SLC_TOP_N = 16
NSA_WINDOW = 512
SELECT_FORCE = 1.0e4
N_EXPERTS = 256
TOP_K = 8
N_GROUPS = 8
TOPK_GROUPS = 4
ROUTED_SCALE = 2.5
MOE_BLOCK = 128
DEPTH = 2
DEEPNORM_ALPHA = (2 * DEPTH) ** 0.25
LN_EPS = 1e-5
NEG = -1.0e30
VMEM_LIMIT = 56 * 1024 * 1024

_NT = (((1,), (1,)), ((), ()))


def _cparams(*sem):
    return pltpu.CompilerParams(dimension_semantics=sem, vmem_limit_bytes=VMEM_LIMIT)


def _split_bf16(a):
    hi = a.astype(BF16)
    lo = (a - hi.astype(F32)).astype(BF16)
    return hi, lo


def _layer_norm(y, g, b):
    mu = jnp.mean(y, axis=-1, keepdims=True)
    d = y - mu
    var = jnp.mean(d * d, axis=-1, keepdims=True)
    return d * lax.rsqrt(var + LN_EPS) * g + b


def _silu(a):
    return a * jax.nn.sigmoid(a)


def _proj_kernel(x_ref, w_ref, tab_ref, *out_refs, plan):
    x = x_ref[...].astype(BF16)
    nblk = len(plan)
    for c0 in range(0, nblk, 2):
        nb = min(2, nblk - c0)
        acc = jnp.dot(x, w_ref[:, c0 * LANES:(c0 + nb) * LANES], preferred_element_type=F32)
        for j in range(nb):
            blk = acc[:, j * LANES:(j + 1) * LANES]
            dst, dblk, mode = plan[c0 + j]
            if mode:
                off = (mode - 1) * 3 * LANES
                cos = tab_ref[:, off:off + LANES]
                s_lo = tab_ref[:, off + LANES:off + 2 * LANES]
                s_hi = tab_ref[:, off + 2 * LANES:off + 3 * LANES]
                blk = (blk * cos + pltpu.roll(blk, LANES - ROT_HALF, 1) * s_lo
                       + pltpu.roll(blk, ROT_HALF, 1) * s_hi)
            o_ref = out_refs[dst]
            o_ref[:, dblk * LANES:(dblk + 1) * LANES] = blk.astype(o_ref.dtype)


def _proj(x2, w, tabs, plan, out_cols, out_dtypes, tm=256):
    t, d = x2.shape
    ncol = w.shape[1]
    out_shape = [jax.ShapeDtypeStruct((t, c), dt) for c, dt in zip(out_cols, out_dtypes)]
    return pl.pallas_call(
        functools.partial(_proj_kernel, plan=tuple(plan)),
        grid=(t // tm,),
        in_specs=[pl.BlockSpec((tm, d), lambda i: (i, 0)),
                  pl.BlockSpec((d, ncol), lambda i: (0, 0)),
                  pl.BlockSpec((tm, tabs.shape[1]), lambda i: (i, 0))],
        out_specs=[pl.BlockSpec((tm, c), lambda i: (i, 0)) for c in out_cols],
        out_shape=out_shape,
        compiler_params=_cparams("parallel"),
        name="proj_rope",
    )(x2, w, tabs)


def _rope_tables(positions):
    t = positions.size
    inv_freq = jnp.asarray(ROPE_THETA ** (-np.arange(0, ROT_DIM, 2) / ROT_DIM), F32)
    ang = positions.astype(F32).reshape(t, 1) * inv_freq
    cos, sin = jnp.cos(ang), jnp.sin(ang)
    rest = HEAD_DIM - ROT_DIM
    cos64 = jnp.concatenate([cos, cos, jnp.ones((t, rest), F32)], axis=1)
    lo64 = jnp.concatenate([-sin, jnp.zeros((t, HEAD_DIM - ROT_HALF), F32)], axis=1)
    hi64 = jnp.concatenate([jnp.zeros((t, ROT_HALF), F32), sin, jnp.zeros((t, rest), F32)], axis=1)
    one, zero = jnp.ones((t, HEAD_DIM), F32), jnp.zeros((t, HEAD_DIM), F32)
    return jnp.concatenate([cos64, one, lo64, zero, hi64, zero,
                            cos64, cos64, lo64, lo64, hi64, hi64], axis=1)


def _pad_heads_cols(w, n_heads):
    d = w.shape[0]
    w = w.reshape(d, n_heads, HEAD_DIM)
    return jnp.pad(w, ((0, 0), (0, 0), (0, LANES - HEAD_DIM))).reshape(d, n_heads * LANES)


def _pad_heads_rows(w, n_heads):
    d = w.shape[1]
    w = w.reshape(n_heads, HEAD_DIM, d)
    return jnp.pad(w, ((0, 0), (0, LANES - HEAD_DIM), (0, 0))).reshape(n_heads * LANES, d)


def _band_kernel(*refs, n_prev, max_dist, length, window, has_sink):
    if has_sink:
        sink_ref, q_ref, k_ref, v_ref, o_ref = refs
    else:
        q_ref, k_ref, v_ref, o_ref = refs
    nq = length // QBLK
    scale = HEAD_DIM ** -0.5
    lane = lax.broadcasted_iota(jnp.int32, (QBLK, LANES), 1)
    row = lax.broadcasted_iota(jnp.int32, (QBLK, window), 0)
    col = lax.broadcasted_iota(jnp.int32, (QBLK, window), 1)
    sink = sink_ref[pl.program_id(2)] if has_sink else None

    def body(qi, carry):
        q0 = pl.multiple_of(qi * QBLK, QBLK)
        k0 = pl.multiple_of(jnp.maximum(qi - n_prev, 0) * QBLK, QBLK)
        q = q_ref[0, pl.ds(q0, QBLK), :]
        k = k_ref[0, pl.ds(k0, window), :]
        v = v_ref[0, pl.ds(k0, window), :]
        s = lax.dot_general(q, k, _NT, preferred_element_type=F32) * scale
        dist = (q0 + row) - (k0 + col)
        s = jnp.where((dist >= 0) & (dist <= max_dist), s, NEG)
        m = jnp.max(s, axis=-1, keepdims=True)
        e = jnp.exp(s - m)
        den = jnp.sum(e, axis=-1, keepdims=True)
        if has_sink:
            den = den + jnp.exp(sink - m)
        o = jnp.dot(e.astype(BF16), v, preferred_element_type=F32) / den
        lse = m + jnp.log(den)
        o_ref[0, pl.ds(q0, QBLK), :] = jnp.where(lane < HEAD_DIM, o, lse)
        return carry

    lax.fori_loop(0, nq, body, 0)


def _band_attention(qkv, *, batch, seq, dilation, nblk, q_off, k_off, v_off, n_q_heads, rep, max_dist, sinks=None):
    length = seq // dilation
    n_prev = -(-max_dist // QBLK)
    window = min((n_prev + 1) * QBLK, length)
    arr = qkv.reshape(batch, length, dilation * nblk * LANES)
    blk = lambda off, div: pl.BlockSpec((1, length, LANES), lambda b, r, h: (b, 0, r * nblk + off + h // div))
    in_specs = [blk(q_off, 1), blk(k_off, rep), blk(v_off, rep)]
    args = [arr, arr, arr]
    if sinks is not None:
        in_specs = [pl.BlockSpec(memory_space=pltpu.SMEM)] + in_specs
        args = [sinks.reshape(-1).astype(F32)] + args
    out = pl.pallas_call(
        functools.partial(_band_kernel, n_prev=n_prev, max_dist=max_dist, length=length, window=window,
                          has_sink=sinks is not None),
        grid=(batch, dilation, n_q_heads),
        in_specs=in_specs,
        out_specs=pl.BlockSpec((1, length, LANES), lambda b, r, h: (b, 0, r * n_q_heads + h)),
        out_shape=jax.ShapeDtypeStruct((batch, length, dilation * n_q_heads * LANES), F32),
        compiler_params=_cparams("parallel", "parallel", "parallel"),
        name="band_attention",
    )(*args)
    return out.reshape(batch * seq, n_q_heads * LANES)


def _outproj_even_kernel(o1_ref, o2_ref, o3_ref, ob_ref, x_ref, w_ref, g_ref, b_ref, out_ref, *, n_a, n_b):
    tm = x_ref.shape[0]
    lane = lax.broadcasted_iota(jnp.int32, (tm, LANES), 1)
    real = lane < HEAD_DIM
    parts = []
    for h in range(n_a):
        sl = slice(h * LANES, (h + 1) * LANES)
        outs = [r[:, sl] for r in (o1_ref, o2_ref, o3_ref)]
        lses = [jnp.where(real, pltpu.roll(a, HEAD_DIM, 1), a) for a in outs]
        m = jnp.maximum(jnp.maximum(lses[0], lses[1]), lses[2])
        es = [jnp.exp(l - m) for l in lses]
        num = es[0] * outs[0] + es[1] * outs[1] + es[2] * outs[2]
        den = es[0] + es[1] + es[2]
        parts.append(jnp.where(real, num / den, 0.0).astype(BF16))
    for h in range(n_b):
        parts.append(jnp.where(real, ob_ref[:, h * LANES:(h + 1) * LANES], 0.0).astype(BF16))
    a = jnp.concatenate(parts, axis=1)
    mixed = jnp.dot(a, w_ref[...], preferred_element_type=F32)
    y = DEEPNORM_ALPHA * x_ref[...] + mixed
    out_ref[...] = _layer_norm(y, g_ref[...], b_ref[...])


def _outproj_even(o1, o2, o3, ob, x2, w, g, b, tm=256):
    t, d = x2.shape
    n_a, n_b = o1.shape[1] // LANES, ob.shape[1] // LANES
    row = lambda c: pl.BlockSpec((tm, c), lambda i: (i, 0))
    full = lambda a: pl.BlockSpec(a.shape, lambda i: (0,) * a.ndim)
    return pl.pallas_call(
        functools.partial(_outproj_even_kernel, n_a=n_a, n_b=n_b),
        grid=(t // tm,),
        in_specs=[row(o1.shape[1]), row(o2.shape[1]), row(o3.shape[1]), row(ob.shape[1]), row(d),
                  full(w), full(g), full(b)],
        out_specs=row(d),
        out_shape=jax.ShapeDtypeStruct((t, d), F32),
        compiler_params=_cparams("parallel"),
        name="outproj_even_ln",
    )(o1, o2, o3, ob, x2, w, g, b)


def _outproj_odd_kernel(oc_ref, os_ref, ow_ref, gate_ref, e_ref, x_ref, w_ref, g_ref, b_ref, out_ref):
    gate = jax.nn.sigmoid(gate_ref[...])
    ghi, glo = _split_bf16(gate)
    acc = None
    for j, o_ref in enumerate((oc_ref, os_ref, ow_ref)):
        ej = e_ref[j]
        gfull = jnp.dot(ghi, ej, preferred_element_type=F32) + jnp.dot(glo, ej, preferred_element_type=F32)
        term = gfull * o_ref[...]
        acc = term if acc is None else acc + term
    mixed = jnp.dot(acc.astype(BF16), w_ref[...], preferred_element_type=F32)
    y = DEEPNORM_ALPHA * x_ref[...] + mixed
    out_ref[...] = _layer_norm(y, g_ref[...], b_ref[...])


def _gate_expanders(n_heads):
    e = np.zeros((3, LANES, n_heads * LANES), np.float32)
    for j in range(3):
        for h in range(n_heads):
            e[j, 3 * h + j, h * LANES:h * LANES + HEAD_DIM] = 1.0
    return jnp.asarray(e, BF16)


def _outproj_odd(oc, osl, ow, gate, x2, w, g, b, tm=256):
    t, d = x2.shape
    n_heads = oc.shape[1] // LANES
    e = _gate_expanders(n_heads)
    row = lambda c: pl.BlockSpec((tm, c), lambda i: (i, 0))
    full = lambda a: pl.BlockSpec(a.shape, lambda i: (0,) * a.ndim)
    return pl.pallas_call(
        _outproj_odd_kernel,
        grid=(t // tm,),
        in_specs=[row(oc.shape[1]), row(osl.shape[1]), row(ow.shape[1]), row(LANES), full(e), row(d),
                  full(w), full(g), full(b)],
        out_specs=row(d),
        out_shape=jax.ShapeDtypeStruct((t, d), F32),
        compiler_params=_cparams("parallel"),
        name="outproj_odd_ln",
    )(oc, osl, ow, gate, e, x2, w, g, b)


def _router_kernel(x_ref, whi_ref, wlo_ref, bias_ref, eidx_ref, gate_ref):
    n_exp = whi_ref.shape[0]
    tm = x_ref.shape[0]
    per_group = n_exp // N_GROUPS
    xhi, xlo = _split_bf16(x_ref[...])
    whi, wlo = whi_ref[...], wlo_ref[...]
    dg = lambda a, b: lax.dot_general(a, b, _NT, preferred_element_type=F32)
    logits = dg(whi, xhi) + dg(whi, xlo) + dg(wlo, xhi)
    aff = jax.nn.sigmoid(logits)
    biased = aff + bias_ref[...]
    gio = lax.broadcasted_iota(jnp.int32, (per_group, tm), 0).astype(F32)
    blocks, scores = [], []
    for g in range(N_GROUPS):
        blk = biased[g * per_group:(g + 1) * per_group, :]
        m1 = jnp.max(blk, axis=0, keepdims=True)
        first = jnp.min(jnp.where(blk == m1, gio, float(per_group)), axis=0, keepdims=True)
        m2 = jnp.max(jnp.where(gio == first, -jnp.inf, blk), axis=0, keepdims=True)
        blocks.append(blk)
        scores.append(m1 + m2)
    masked = []
    for g in range(N_GROUPS):
        rank = jnp.zeros((1, tm), F32)
        for o in range(N_GROUPS):
            if o == g:
                continue
            beats = scores[o] >= scores[g] if o < g else scores[o] > scores[g]
            rank = rank + jnp.where(beats, 1.0, 0.0)
        masked.append(jnp.where(rank < TOPK_GROUPS, blocks[g], -jnp.inf))
    cur = jnp.concatenate(masked, axis=0)
    eio = lax.broadcasted_iota(jnp.int32, (n_exp, tm), 0).astype(F32)
    ids, gs = [], []
    for _ in range(TOP_K):
        m = jnp.max(cur, axis=0, keepdims=True)
        idx = jnp.min(jnp.where(cur == m, eio, float(n_exp)), axis=0, keepdims=True)
        hit = eio == idx
        gs.append(jnp.sum(jnp.where(hit, aff, 0.0), axis=0, keepdims=True))
        ids.append(idx)
        cur = jnp.where(hit, -jnp.inf, cur)
    gates = jnp.concatenate(gs, axis=0)
    gates = gates / jnp.sum(gates, axis=0, keepdims=True) * ROUTED_SCALE
    eidx_ref[...] = jnp.concatenate(ids, axis=0).astype(jnp.int32)
    gate_ref[...] = gates


def _router(x2, router_w, router_b, tm=256):
    t, d = x2.shape
    n_exp = router_w.shape[1]
    whi, wlo = _split_bf16(router_w.T)
    bias = router_b.reshape(n_exp, 1).astype(F32)
    full = lambda a: pl.BlockSpec(a.shape, lambda i: (0,) * a.ndim)
    return pl.pallas_call(
        _router_kernel,
        grid=(t // tm,),
        in_specs=[pl.BlockSpec((tm, d), lambda i: (i, 0)), full(whi), full(wlo), full(bias)],
        out_specs=[pl.BlockSpec((TOP_K, tm), lambda i: (0, i)), pl.BlockSpec((TOP_K, tm), lambda i: (0, i))],
        out_shape=[jax.ShapeDtypeStruct((TOP_K, t), jnp.int32), jax.ShapeDtypeStruct((TOP_K, t), F32)],
        compiler_params=_cparams("parallel"),
        name="moe_router",
    )(x2, whi, wlo, bias)


def _moe_ffn_kernel(be_ref, nu_ref, tok_ref, x_hbm, wg_ref, wu_ref, wd_ref, y_ref, buf, sem):
    b = pl.program_id(0)

    @pl.when(b < nu_ref[0])
    def _():
        def issue(i, c):
            tok = tok_ref[0, 0, i]
            pltpu.make_async_copy(x_hbm.at[pl.ds(tok, 1)], buf.at[pl.ds(i, 1)], sem).start()
            return c

        lax.fori_loop(0, MOE_BLOCK, issue, 0)
        pltpu.make_async_copy(x_hbm.at[pl.ds(0, MOE_BLOCK)], buf, sem).wait()
        xb = buf[...].astype(BF16)
        gp = jnp.dot(xb, wg_ref[0].astype(BF16), preferred_element_type=F32)
        up = jnp.dot(xb, wu_ref[0].astype(BF16), preferred_element_type=F32)
        h = (_silu(gp) * up).astype(BF16)
        y_ref[...] = jnp.dot(h, wd_ref[0].astype(BF16), preferred_element_type=F32)

    @pl.when(b >= nu_ref[0])
    def _():
        y_ref[...] = jnp.zeros(y_ref.shape, y_ref.dtype)


def _moe_ffn(x2, row_tok, block_expert, n_used, w_gate, w_up, w_down):
    t, d = x2.shape
    n_blocks = block_expert.shape[0]
    ff = w_gate.shape[2]
    grid_spec = pltpu.PrefetchScalarGridSpec(
        num_scalar_prefetch=2,
        grid=(n_blocks,),
        in_specs=[
            pl.BlockSpec((1, 1, MOE_BLOCK), lambda b, be, nu: (b, 0, 0), memory_space=pltpu.SMEM),
            pl.BlockSpec(memory_space=pl.ANY),
            pl.BlockSpec((1, d, ff), lambda b, be, nu: (be[b], 0, 0)),
            pl.BlockSpec((1, d, ff), lambda b, be, nu: (be[b], 0, 0)),
            pl.BlockSpec((1, ff, d), lambda b, be, nu: (be[b], 0, 0)),
        ],
        out_specs=pl.BlockSpec((MOE_BLOCK, d), lambda b, be, nu: (b, 0)),
        scratch_shapes=[pltpu.VMEM((MOE_BLOCK, d), F32), pltpu.SemaphoreType.DMA(())],
    )
    return pl.pallas_call(
        _moe_ffn_kernel,
        grid_spec=grid_spec,
        out_shape=jax.ShapeDtypeStruct((n_blocks * MOE_BLOCK, d), F32),
        compiler_params=_cparams("arbitrary"),
        name="moe_expert_ffn",
    )(block_expert, n_used, row_tok.reshape(n_blocks, 1, MOE_BLOCK), x2, w_gate, w_up, w_down)


def _moe_combine_kernel(dest_ref, y_hbm, gate_ref, x_ref, sg_ref, su_ref, sd_ref, g_ref, b_ref, out_ref, buf, sem):
    tm = x_ref.shape[0]

    def issue(i, c):
        for k in range(TOP_K):
            r = dest_ref[k, i]
            pltpu.make_async_copy(y_hbm.at[pl.ds(r, 1)], buf.at[k, pl.ds(i, 1)], sem).start()
        return c

    lax.fori_loop(0, tm, issue, 0)
    x = x_ref[...]
    xb = x.astype(BF16)
    hs = _silu(jnp.dot(xb, sg_ref[...], preferred_element_type=F32)) * jnp.dot(xb, su_ref[...], preferred_element_type=F32)
    shared = jnp.dot(hs.astype(BF16), sd_ref[...], preferred_element_type=F32)
    for k in range(TOP_K):
        pltpu.make_async_copy(y_hbm.at[pl.ds(0, tm)], buf.at[k], sem).wait()
    gates = gate_ref[...]
    routed = buf[0] * gates[:, 0:1]
    for k in range(1, TOP_K):
        routed = routed + buf[k] * gates[:, k:k + 1]
    y = DEEPNORM_ALPHA * x + (routed + shared)
    out_ref[...] = _layer_norm(y, g_ref[...], b_ref[...])


def _moe_combine(dest, y, gates_t, x2, sh_gate, sh_up, sh_down, g, b, tm=128):
    t, d = x2.shape
    row = lambda c: pl.BlockSpec((tm, c), lambda i: (i, 0))
    full = lambda a: pl.BlockSpec(a.shape, lambda i: (0,) * a.ndim)
    return pl.pallas_call(
        _moe_combine_kernel,
        grid=(t // tm,),
        in_specs=[pl.BlockSpec((TOP_K, tm), lambda i: (0, i), memory_space=pltpu.SMEM),
                  pl.BlockSpec(memory_space=pl.ANY),
                  row(TOP_K), row(d), full(sh_gate), full(sh_up), full(sh_down), full(g), full(b)],
        out_specs=row(d),
        out_shape=jax.ShapeDtypeStruct((t, d), F32),
        scratch_shapes=[pltpu.VMEM((TOP_K, tm, d), F32), pltpu.SemaphoreType.DMA(())],
        compiler_params=_cparams("arbitrary"),
        name="moe_combine_ln",
    )(dest, y, gates_t, x2, sh_gate, sh_up, sh_down, g, b)


def _moe_plan(eidx):
    n_tok = eidx.shape[1]
    n_asg = n_tok * TOP_K
    flat_e = eidx.T.reshape(n_asg)
    order = jnp.argsort(flat_e)
    se = flat_e[order]
    stok = (order // TOP_K).astype(jnp.int32)
    counts = jnp.bincount(flat_e, length=N_EXPERTS)
    starts = jnp.cumsum(counts) - counts
    padded = (counts + MOE_BLOCK - 1) // MOE_BLOCK * MOE_BLOCK
    pad_end = jnp.cumsum(padded)
    dest = ((pad_end - padded)[se] + jnp.arange(n_asg) - starts[se]).astype(jnp.int32)
    n_blocks = -(-n_asg // MOE_BLOCK) + N_EXPERTS
    row_tok = jnp.zeros((n_blocks * MOE_BLOCK,), jnp.int32).at[dest].set(stok)
    block_expert = jnp.minimum(jnp.searchsorted(pad_end, jnp.arange(n_blocks) * MOE_BLOCK, side='right'),
                               N_EXPERTS - 1).astype(jnp.int32)
    n_used = (pad_end[-1] // MOE_BLOCK).astype(jnp.int32).reshape(1)
    dest_tk = jnp.zeros((n_asg,), jnp.int32).at[order].set(dest).reshape(n_tok, TOP_K).T
    return row_tok, block_expert, n_used, dest_tk


def _moe_layer(x2, router_w, router_b, w_gate, w_up, w_down, sh_gate, sh_up, sh_down, g, b):
    eidx, gates = _router(x2, router_w, router_b)
    row_tok, block_expert, n_used, dest_tk = _moe_plan(eidx)
    y = _moe_ffn(x2, row_tok, block_expert, n_used, w_gate, w_up, w_down)
    return _moe_combine(dest_tk, y, gates.T, x2, sh_gate.astype(BF16), sh_up.astype(BF16), sh_down.astype(BF16), g, b)


def _compress_kernel(x_ref, pa_ref, pb_ref, wa_ref, wb_ref, w2_ref, out_ref):
    x = x_ref[0]
    nc = x.shape[0]
    ha = jnp.dot((x + pa_ref[...]).astype(BF16), wa_ref[...], preferred_element_type=F32)
    hb = jnp.dot((x + pb_ref[...]).astype(BF16), wb_ref[...], preferred_element_type=F32)
    h = ha + pltpu.roll(hb, nc - 1, 0)
    h = jax.nn.gelu(h, approximate=True)
    out_ref[0] = jnp.dot(h.astype(BF16), w2_ref[...], preferred_element_type=F32).astype(out_ref.dtype)


def _compress(kc, pos, w1, w2, batch, seq):
    g = C_KV_HEADS
    nch = seq // CMP_STRIDE
    half = CMP_LEN // 2
    x = kc.reshape(batch, nch, half * g * HEAD_DIM)
    eye = jnp.eye(g, dtype=F32)
    w1r = w1.reshape(CMP_LEN, HEAD_DIM, CMP_HIDDEN)
    expand = lambda wpart: jnp.einsum('jdh,ge->jgdeh', wpart, eye).reshape(half * g * HEAD_DIM, g * CMP_HIDDEN)
    wa, wb = expand(w1r[:half]).astype(BF16), expand(w1r[half:]).astype(BF16)
    w2e = jnp.einsum('hd,ge->ghed', jnp.pad(w2, ((0, 0), (0, LANES - HEAD_DIM))), eye)
    w2e = w2e.reshape(g * CMP_HIDDEN, g * LANES).astype(BF16)
    tile_pos = lambda p: jnp.broadcast_to(p[:, None, :], (half, g, HEAD_DIM)).reshape(1, half * g * HEAD_DIM)
    pa, pb = tile_pos(pos[:half]), tile_pos(pos[half:])
    full = lambda a: pl.BlockSpec(a.shape, lambda i: (0,) * a.ndim)
    return pl.pallas_call(
        _compress_kernel,
        grid=(batch,),
        in_specs=[pl.BlockSpec((1, nch, x.shape[2]), lambda i: (i, 0, 0)), full(pa), full(pb), full(wa), full(wb),
                  full(w2e)],
        out_specs=pl.BlockSpec((1, nch, g * LANES), lambda i: (i, 0, 0)),
        out_shape=jax.ShapeDtypeStruct((batch, nch, g * LANES), BF16),
        compiler_params=_cparams("parallel"),
        name="nsa_compress",
    )(x, pa, pb, wa, wb, w2e)


def _nsa_cmp_kernel(q_ref, kc_ref, vc_ref, ovt_ref, o_ref, sel_ref, *, rep, n_sel, n_real):
    tq = q_ref.shape[1]
    nc = kc_ref.shape[1]
    nsb = ovt_ref.shape[0]
    t0 = pl.program_id(2) * tq
    scale = HEAD_DIM ** -0.5
    kc = kc_ref[0]
    vc = vc_ref[0]
    tpos = t0 + lax.broadcasted_iota(jnp.int32, (tq, nc), 0)
    cend = lax.broadcasted_iota(jnp.int32, (tq, nc), 1) * CMP_STRIDE + (CMP_LEN - 1)
    cmask = cend <= tpos
    psum = jnp.zeros((tq, nc), F32)
    outs = []
    for r in range(rep):
        q = q_ref[0, :, r * LANES:(r + 1) * LANES]
        sc = lax.dot_general(q, kc, _NT, preferred_element_type=F32) * scale
        sc = jnp.where(cmask, sc, NEG)
        m = jnp.max(sc, axis=-1, keepdims=True)
        ex = jnp.where(cmask, jnp.exp(sc - m), 0.0)
        den = jnp.sum(ex, axis=-1, keepdims=True)
        pc = ex / jnp.where(den > 0, den, 1.0)
        outs.append(jnp.dot(pc.astype(BF16), vc, preferred_element_type=F32))
        psum = psum + pc
    o_ref[0] = jnp.concatenate(outs, axis=1)
    phi, plo = _split_bf16(psum)
    ovt = ovt_ref[...]
    imp = (lax.dot_general(ovt, phi, _NT, preferred_element_type=F32)
           + lax.dot_general(ovt, plo, _NT, preferred_element_type=F32))
    jblk = lax.broadcasted_iota(jnp.int32, (nsb, tq), 0)
    cur = jnp.right_shift(t0 + lax.broadcasted_iota(jnp.int32, (nsb, tq), 1), SLC_SHIFT)
    forced = (jblk == 0) | (jblk == cur) | (jblk == cur - 1)
    score = jnp.where(jblk > cur, -1.0, jnp.where(forced, SELECT_FORCE, imp))
    rank = jnp.zeros((nsb, tq), F32)
    for k in range(n_real):
        rowk = score[k:k + 1, :]
        ge = jnp.where(rowk >= score, 1.0, 0.0)
        gt = jnp.where(rowk > score, 1.0, 0.0)
        rank = rank + jnp.where(jblk > k, ge, gt)
    sel = jnp.where(rank < n_sel, 1.0, 0.0)
    sel_ref[0, 0] = sel.T


def _nsa_cmp(q, kcmp, vcmp, batch, seq, q_off_blocks, tq=128):
    g = C_KV_HEADS
    rep = C_HEADS // g
    nc = kcmp.shape[1]
    nsb = seq // SLC_BLOCK
    n_sel = min(SLC_TOP_N, nsb)
    cs = np.arange(nc)[:, None] * CMP_STRIDE
    js = np.arange(nsb)[None, :] * SLC_BLOCK
    overlap = np.clip(np.minimum(cs + CMP_LEN, js + SLC_BLOCK) - np.maximum(cs, js), 0, None) / CMP_LEN
    overlap[(seq - CMP_LEN) // CMP_STRIDE + 1:] = 0.0
    nsb_pad = -(-nsb // LANES) * LANES
    ovt = jnp.asarray(np.pad(overlap.T, ((0, nsb_pad - nsb), (0, 0))), BF16)
    q3 = q.reshape(batch, seq, q.shape[1])
    n_real, nsb = nsb, nsb_pad
    o, sel = pl.pallas_call(
        functools.partial(_nsa_cmp_kernel, rep=rep, n_sel=n_sel, n_real=n_real),
        grid=(batch, g, seq // tq),
        in_specs=[pl.BlockSpec((1, tq, rep * LANES), lambda b, gi, i: (b, i, q_off_blocks // rep + gi)),
                  pl.BlockSpec((1, nc, LANES), lambda b, gi, i: (b, 0, gi)),
                  pl.BlockSpec((1, nc, LANES), lambda b, gi, i: (b, 0, gi)),
                  pl.BlockSpec(ovt.shape, lambda b, gi, i: (0, 0))],
        out_specs=[pl.BlockSpec((1, tq, rep * LANES), lambda b, gi, i: (b, i, gi)),
                   pl.BlockSpec((1, 1, tq, nsb), lambda b, gi, i: (b, gi, i, 0))],
        out_shape=[jax.ShapeDtypeStruct((batch, seq, C_HEADS * LANES), F32),
                   jax.ShapeDtypeStruct((batch, g, seq, nsb), F32)],
        compiler_params=_cparams("parallel", "parallel", "parallel"),
        name="nsa_compressed_select",
    )(q3, kcmp, vcmp, ovt)
    return o.reshape(batch * seq, C_HEADS * LANES), sel


def _nsa_slc_kernel(q_ref, k_ref, v_ref, sel_ref, o_ref, m_sc, l_sc, acc_sc, *, rep, kt):
    tq = q_ref.shape[1]
    nsb = sel_ref.shape[3]
    t0 = pl.program_id(2) * tq
    scale = HEAD_DIM ** -0.5
    sel = sel_ref[0, 0].astype(BF16)
    m_sc[...] = jnp.full(m_sc.shape, NEG, F32)
    l_sc[...] = jnp.zeros(l_sc.shape, F32)
    acc_sc[...] = jnp.zeros(acc_sc.shape, F32)
    tpos = t0 + lax.broadcasted_iota(jnp.int32, (tq, kt), 0)
    kcol = lax.broadcasted_iota(jnp.int32, (tq, kt), 1)
    eb = lax.broadcasted_iota(jnp.int32, (nsb, kt), 0)
    ek = lax.broadcasted_iota(jnp.int32, (nsb, kt), 1)
    n_kt = (t0 + tq + kt - 1) // kt

    def body(j, carry):
        k0 = pl.multiple_of(j * kt, kt)
        k = k_ref[0, pl.ds(k0, kt), :]
        v = v_ref[0, pl.ds(k0, kt), :]
        expand = jnp.where(jnp.right_shift(k0 + ek, SLC_SHIFT) == eb, 1.0, 0.0).astype(BF16)
        chosen = jnp.dot(sel, expand, preferred_element_type=F32)
        mask = (chosen > 0.5) & (k0 + kcol <= tpos)
        for r in range(rep):
            q = q_ref[0, :, r * LANES:(r + 1) * LANES]
            s = lax.dot_general(q, k, _NT, preferred_element_type=F32) * scale
            s = jnp.where(mask, s, NEG)
            m_old = m_sc[r]
            m_new = jnp.maximum(m_old, jnp.max(s, axis=-1, keepdims=True))
            e = jnp.where(mask, jnp.exp(s - m_new), 0.0)
            corr = jnp.exp(m_old - m_new)
            l_sc[r] = l_sc[r] * corr + jnp.sum(e, axis=-1, keepdims=True)
            acc_sc[r] = acc_sc[r] * corr + jnp.dot(e.astype(BF16), v, preferred_element_type=F32)
            m_sc[r] = m_new
        return carry

    lax.fori_loop(0, n_kt, body, 0)
    o_ref[0] = jnp.concatenate([acc_sc[r] / l_sc[r] for r in range(rep)], axis=1)


def _nsa_slc(qkv, sel, batch, seq, nblk, q_off, k_off, v_off, tq=128, kt=256):
    g = C_KV_HEADS
    rep = C_HEADS // g
    kt = min(kt, seq)
    nsb = sel.shape[3]
    arr = qkv.reshape(batch, seq, nblk * LANES)
    out = pl.pallas_call(
        functools.partial(_nsa_slc_kernel, rep=rep, kt=kt),
        grid=(batch, g, seq // tq),
        in_specs=[pl.BlockSpec((1, tq, rep * LANES), lambda b, gi, i: (b, i, q_off // rep + gi)),
                  pl.BlockSpec((1, seq, LANES), lambda b, gi, i: (b, 0, k_off + gi)),
                  pl.BlockSpec((1, seq, LANES), lambda b, gi, i: (b, 0, v_off + gi)),
                  pl.BlockSpec((1, 1, tq, nsb), lambda b, gi, i: (b, gi, i, 0))],
        out_specs=pl.BlockSpec((1, tq, rep * LANES), lambda b, gi, i: (b, i, gi)),
        out_shape=jax.ShapeDtypeStruct((batch, seq, C_HEADS * LANES), F32),
        scratch_shapes=[pltpu.VMEM((rep, tq, 1), F32), pltpu.VMEM((rep, tq, 1), F32),
                        pltpu.VMEM((rep, tq, LANES), F32)],
        compiler_params=_cparams("parallel", "parallel", "arbitrary"),
        name="nsa_selected",
    )(arr, arr, arr, sel)
    return out.reshape(batch * seq, C_HEADS * LANES)


def _even_mixer_layer(x2, batch, seq, tabs, w_in, sinks, w_out, g, b):
    d = x2.shape[1]
    n_heads_in = 3 * A_HEADS + B_Q_HEADS + 2 * B_KV_HEADS
    w = _pad_heads_cols(w_in, n_heads_in).astype(BF16)
    rope = [1] * (2 * A_HEADS) + [0] * A_HEADS + [1] * B_Q_HEADS + [1] * B_KV_HEADS + [0] * B_KV_HEADS
    plan = [(0, c, rope[c]) for c in range(n_heads_in)]
    (qkv,) = _proj(x2, w, tabs, plan, [n_heads_in * LANES], [BF16])
    outs = []
    for window, dilation in A_PATTERNS:
        outs.append(_band_attention(qkv, batch=batch, seq=seq, dilation=dilation, nblk=n_heads_in, q_off=0,
                                    k_off=A_HEADS, v_off=2 * A_HEADS, n_q_heads=A_HEADS, rep=1,
                                    max_dist=window // dilation))
    qb_off = 3 * A_HEADS
    ob = _band_attention(qkv, batch=batch, seq=seq, dilation=1, nblk=n_heads_in, q_off=qb_off,
                         k_off=qb_off + B_Q_HEADS, v_off=qb_off + B_Q_HEADS + B_KV_HEADS, n_q_heads=B_Q_HEADS,
                         rep=B_Q_HEADS // B_KV_HEADS, max_dist=B_WINDOW - 1, sinks=sinks)
    w_o = _pad_heads_rows(w_out, A_HEADS + B_Q_HEADS).astype(BF16)
    return _outproj_even(outs[0], outs[1], outs[2], ob, x2, w_o, g.reshape(1, d), b.reshape(1, d))


def _odd_mixer_layer(x2, batch, seq, tabs, w_in, cmpk_pos, cmpk_w1, cmpk_w2, cmpv_pos, cmpv_w1, cmpv_w2, w_out, g, b):
    d = x2.shape[1]
    kvw = C_KV_HEADS * HEAD_DIM
    qw = C_HEADS * HEAD_DIM
    sizes = [qw] + [kvw] * 6 + [3 * C_HEADS]
    offs = np.concatenate([[0], np.cumsum(sizes)])
    wq, wkc, wvc, wks, wvs, wkw, wvw, wgt = [w_in[:, offs[i]:offs[i + 1]] for i in range(8)]
    ph = lambda wpart, n: _pad_heads_cols(wpart, n)
    w = jnp.concatenate([ph(wq, C_HEADS), ph(wks, C_KV_HEADS), ph(wvs, C_KV_HEADS), ph(wkw, C_KV_HEADS),
                         ph(wvw, C_KV_HEADS), wkc, wvc, jnp.pad(wgt, ((0, 0), (0, LANES - 3 * C_HEADS)))],
                        axis=1).astype(BF16)
    n16 = C_HEADS + 4 * C_KV_HEADS
    rope16 = [1] * C_HEADS + [1] * C_KV_HEADS + [0] * C_KV_HEADS + [1] * C_KV_HEADS + [0] * C_KV_HEADS
    n_kc = kvw // LANES
    plan = ([(0, c, rope16[c]) for c in range(n16)] + [(1, c, 2) for c in range(n_kc)]
            + [(2, c, 0) for c in range(n_kc)] + [(3, 0, 0)])
    qkv, kc, vc, gate = _proj(x2, w, tabs, plan, [n16 * LANES, kvw, kvw, LANES], [BF16, F32, F32, F32])
    kcmp = _compress(kc, cmpk_pos, cmpk_w1, cmpk_w2, batch, seq)
    vcmp = _compress(vc, cmpv_pos, cmpv_w1, cmpv_w2, batch, seq)
    o_cmp, sel = _nsa_cmp(qkv, kcmp, vcmp, batch, seq, 0)
    ks_off = C_HEADS
    o_slc = _nsa_slc(qkv, sel, batch, seq, n16, 0, ks_off, ks_off + C_KV_HEADS)
    kw_off = ks_off + 2 * C_KV_HEADS
    o_win = _band_attention(qkv, batch=batch, seq=seq, dilation=1, nblk=n16, q_off=0, k_off=kw_off,
                            v_off=kw_off + C_KV_HEADS, n_q_heads=C_HEADS, rep=C_HEADS // C_KV_HEADS,
                            max_dist=NSA_WINDOW - 1)
    w_o = _pad_heads_rows(w_out, C_HEADS).astype(BF16)
    return _outproj_odd(o_cmp, o_slc, o_win, gate, x2, w_o, g.reshape(1, d), b.reshape(1, d))


def kernel(x, positions, even_w_in, even_sinks, even_w_out, odd_w_in, odd_cmpk_pos, odd_cmpk_w1, odd_cmpk_w2, odd_cmpv_pos, odd_cmpv_w1, odd_cmpv_w2, odd_w_out, mix_ln_g, mix_ln_b, moe_router_w, moe_router_b, moe_w_gate, moe_w_up, moe_w_down, moe_sh_gate, moe_sh_up, moe_sh_down, ffn_ln_g, ffn_ln_b):
    batch, seq, d = x.shape
    x2 = x.reshape(batch * seq, d)
    tabs = _rope_tables(positions)
    depth = mix_ln_g.shape[0]
    for layer in range(depth):
        j = layer // 2
        if layer % 2 == 0:
            x2 = _even_mixer_layer(x2, batch, seq, tabs, even_w_in[j], even_sinks[j], even_w_out[j],
                                   mix_ln_g[layer], mix_ln_b[layer])
        else:
            x2 = _odd_mixer_layer(x2, batch, seq, tabs, odd_w_in[j], odd_cmpk_pos[j], odd_cmpk_w1[j], odd_cmpk_w2[j],
                                  odd_cmpv_pos[j], odd_cmpv_w1[j], odd_cmpv_w2[j], odd_w_out[j],
                                  mix_ln_g[layer], mix_ln_b[layer])
        x2 = _moe_layer(x2, moe_router_w[layer], moe_router_b[layer], moe_w_gate[layer], moe_w_up[layer],
                        moe_w_down[layer], moe_sh_gate[layer], moe_sh_up[layer], moe_sh_down[layer],
                        ffn_ln_g[layer].reshape(1, d), ffn_ln_b[layer].reshape(1, d))
    return x2.reshape(batch, seq, d)
```

```python
import functools

import numpy as np
import jax
import jax.numpy as jnp
from jax import lax
from jax.experimental import pallas as pl
from jax.experimental.pallas import tpu as pltpu

F32 = jnp.float32
BF16 = jnp.bfloat16

LANES = 128
HEAD_DIM = 64
ROT_DIM = HEAD_DIM // 4
ROT_HALF = ROT_DIM // 2
ROPE_THETA = 500000.0
QBLK = 128
A_HEADS = 8
A_PATTERNS = ((128, 1), (512, 4), (2048, 16))
B_Q_HEADS = 8
B_KV_HEADS = 2
B_WINDOW = 128
C_HEADS = 16
C_KV_HEADS = 4
CMP_LEN = 32
CMP_STRIDE = 16
CMP_HIDDEN = 2 * HEAD_DIM
SLC_BLOCK = 64
SLC_SHIFT = 6
SLC_TOP_N = 16
NSA_WINDOW = 512
SELECT_FORCE = 1.0e4
N_EXPERTS = 256
TOP_K = 8
N_GROUPS = 8
TOPK_GROUPS = 4
ROUTED_SCALE = 2.5
MOE_BLOCK = 128
DEPTH = 2
DEEPNORM_ALPHA = (2 * DEPTH) ** 0.25
LN_EPS = 1e-5
NEG = -1.0e30
VMEM_LIMIT = 56 * 1024 * 1024

_NT = (((1,), (1,)), ((), ()))


def _cparams(*sem):
    return pltpu.CompilerParams(dimension_semantics=sem, vmem_limit_bytes=VMEM_LIMIT)


def _split_bf16(a):
    hi = a.astype(BF16)
    lo = (a - hi.astype(F32)).astype(BF16)
    return hi, lo


def _layer_norm(y, g, b):
    mu = jnp.mean(y, axis=-1, keepdims=True)
    d = y - mu
    var = jnp.mean(d * d, axis=-1, keepdims=True)
    return d * lax.rsqrt(var + LN_EPS) * g + b


def _silu(a):
    return a * jax.nn.sigmoid(a)


def _proj_kernel(*refs, plan, transposed):
    if transposed:
        x_ref, w_ref, tab_ref, wt_ref = refs[:4]
        out_refs = refs[4:]
    else:
        x_ref, w_ref, tab_ref = refs[:3]
        out_refs = refs[3:]
    x = x_ref[...].astype(BF16)
    if transposed:
        out_t = out_refs[-1]
        out_t[...] = lax.dot_general(wt_ref[...], x, _NT, preferred_element_type=F32).astype(out_t.dtype)
    nblk = len(plan)
    for c0 in range(0, nblk, 2):
        nb = min(2, nblk - c0)
        acc = jnp.dot(x, w_ref[:, c0 * LANES:(c0 + nb) * LANES], preferred_element_type=F32)
        for j in range(nb):
            blk = acc[:, j * LANES:(j + 1) * LANES]
            dst, dblk, mode = plan[c0 + j]
            if mode:
                off = (mode - 1) * 3 * LANES
                cos = tab_ref[:, off:off + LANES]
                s_lo = tab_ref[:, off + LANES:off + 2 * LANES]
                s_hi = tab_ref[:, off + 2 * LANES:off + 3 * LANES]
                blk = (blk * cos + pltpu.roll(blk, LANES - ROT_HALF, 1) * s_lo
                       + pltpu.roll(blk, ROT_HALF, 1) * s_hi)
            o_ref = out_refs[dst]
            o_ref[:, dblk * LANES:(dblk + 1) * LANES] = blk.astype(o_ref.dtype)


def _proj(x2, w, tabs, plan, out_cols, out_dtypes, wt=None, tm=256):
    t, d = x2.shape
    ncol = w.shape[1]
    out_shape = [jax.ShapeDtypeStruct((t, c), dt) for c, dt in zip(out_cols, out_dtypes)]
    in_specs = [pl.BlockSpec((tm, d), lambda i: (i, 0)),
                pl.BlockSpec((d, ncol), lambda i: (0, 0)),
                pl.BlockSpec((tm, tabs.shape[1]), lambda i: (i, 0))]
    out_specs = [pl.BlockSpec((tm, c), lambda i: (i, 0)) for c in out_cols]
    args = [x2, w, tabs]
    if wt is not None:
        in_specs.append(pl.BlockSpec(wt.shape, lambda i: (0, 0)))
        out_specs.append(pl.BlockSpec((wt.shape[0], tm), lambda i: (0, i)))
        out_shape.append(jax.ShapeDtypeStruct((wt.shape[0], t), BF16))
        args.append(wt)
    return pl.pallas_call(
        functools.partial(_proj_kernel, plan=tuple(plan), transposed=wt is not None),
        grid=(t // tm,),
        in_specs=in_specs,
        out_specs=out_specs,
        out_shape=out_shape,
        compiler_params=_cparams("parallel"),
        name="proj_rope",
    )(*args)


def _rope_tables(positions):
    t = positions.size
    inv_freq = jnp.asarray(ROPE_THETA ** (-np.arange(0, ROT_DIM, 2) / ROT_DIM), F32)
    ang = positions.astype(F32).reshape(t, 1) * inv_freq
    cos, sin = jnp.cos(ang), jnp.sin(ang)
    rest = HEAD_DIM - ROT_DIM
    cos64 = jnp.concatenate([cos, cos, jnp.ones((t, rest), F32)], axis=1)
    lo64 = jnp.concatenate([-sin, jnp.zeros((t, HEAD_DIM - ROT_HALF), F32)], axis=1)
    hi64 = jnp.concatenate([jnp.zeros((t, ROT_HALF), F32), sin, jnp.zeros((t, rest), F32)], axis=1)
    one, zero = jnp.ones((t, HEAD_DIM), F32), jnp.zeros((t, HEAD_DIM), F32)
    return jnp.concatenate([cos64, one, lo64, zero, hi64, zero,
                            cos64, cos64, lo64, lo64, hi64, hi64], axis=1)


def _pad_heads_cols(w, n_heads):
    d = w.shape[0]
    w = w.reshape(d, n_heads, HEAD_DIM)
    return jnp.pad(w, ((0, 0), (0, 0), (0, LANES - HEAD_DIM))).reshape(d, n_heads * LANES)


def _pad_heads_rows(w, n_heads):
    d = w.shape[1]
    w = w.reshape(n_heads, HEAD_DIM, d)
    return jnp.pad(w, ((0, 0), (0, LANES - HEAD_DIM), (0, 0))).reshape(n_heads * LANES, d)


def _band_kernel(*refs, n_prev, max_dist, length, window, has_sink):
    if has_sink:
        sink_ref, q_ref, k_ref, v_ref, o_ref = refs
    else:
        q_ref, k_ref, v_ref, o_ref = refs
    nq = length // QBLK
    scale = HEAD_DIM ** -0.5
    lane = lax.broadcasted_iota(jnp.int32, (QBLK, LANES), 1)
    row = lax.broadcasted_iota(jnp.int32, (QBLK, window), 0)
    col = lax.broadcasted_iota(jnp.int32, (QBLK, window), 1)
    sink = sink_ref[pl.program_id(2)] if has_sink else None

    def body(qi, carry):
        q0 = pl.multiple_of(qi * QBLK, QBLK)
        k0 = pl.multiple_of(jnp.maximum(qi - n_prev, 0) * QBLK, QBLK)
        q = q_ref[0, pl.ds(q0, QBLK), :]
        k = k_ref[0, pl.ds(k0, window), :]
        v = v_ref[0, pl.ds(k0, window), :]
        s = lax.dot_general(q, k, _NT, preferred_element_type=F32) * scale
        dist = (q0 + row) - (k0 + col)
        s = jnp.where((dist >= 0) & (dist <= max_dist), s, NEG)
        m = jnp.max(s, axis=-1, keepdims=True)
        e = jnp.exp(s - m)
        den = jnp.sum(e, axis=-1, keepdims=True)
        if has_sink:
            den = den + jnp.exp(sink - m)
        o = jnp.dot(e.astype(BF16), v, preferred_element_type=F32) / den
        lse = m + jnp.log(den)
        o_ref[0, pl.ds(q0, QBLK), :] = jnp.where(lane < HEAD_DIM, o, lse)
        return carry

    lax.fori_loop(0, nq, body, 0, unroll=2 if nq % 2 == 0 else 1)


def _band_attention(qkv, *, batch, seq, dilation, nblk, q_off, k_off, v_off, n_q_heads, rep, max_dist, sinks=None):
    length = seq // dilation
    n_prev = -(-max_dist // QBLK)
    window = min((n_prev + 1) * QBLK, length)
    arr = qkv.reshape(batch, length, dilation * nblk * LANES)
    blk = lambda off, div: pl.BlockSpec((1, length, LANES), lambda b, r, h: (b, 0, r * nblk + off + h // div))
    in_specs = [blk(q_off, 1), blk(k_off, rep), blk(v_off, rep)]
    args = [arr, arr, arr]
    if sinks is not None:
        in_specs = [pl.BlockSpec(memory_space=pltpu.SMEM)] + in_specs
        args = [sinks.reshape(-1).astype(F32)] + args
    out = pl.pallas_call(
        functools.partial(_band_kernel, n_prev=n_prev, max_dist=max_dist, length=length, window=window,
                          has_sink=sinks is not None),
        grid=(batch, dilation, n_q_heads),
        in_specs=in_specs,
        out_specs=pl.BlockSpec((1, length, LANES), lambda b, r, h: (b, 0, r * n_q_heads + h)),
        out_shape=jax.ShapeDtypeStruct((batch, length, dilation * n_q_heads * LANES), F32),
        compiler_params=_cparams("parallel", "parallel", "parallel"),
        name="band_attention",
    )(*args)
    return out.reshape(batch * seq, n_q_heads * LANES)


def _outproj_even_kernel(o1_ref, o2_ref, o3_ref, ob_ref, x_ref, w_ref, g_ref, b_ref, out_ref, *, n_a, n_b):
    tm = x_ref.shape[0]
    lane = lax.broadcasted_iota(jnp.int32, (tm, LANES), 1)
    real = lane < HEAD_DIM
    parts = []
    for h in range(n_a):
        sl = slice(h * LANES, (h + 1) * LANES)
        outs = [r[:, sl] for r in (o1_ref, o2_ref, o3_ref)]
        lses = [jnp.where(real, pltpu.roll(a, HEAD_DIM, 1), a) for a in outs]
        m = jnp.maximum(jnp.maximum(lses[0], lses[1]), lses[2])
        es = [jnp.exp(l - m) for l in lses]
        num = es[0] * outs[0] + es[1] * outs[1] + es[2] * outs[2]
        den = es[0] + es[1] + es[2]
        parts.append(jnp.where(real, num / den, 0.0).astype(BF16))
    for h in range(n_b):
        parts.append(jnp.where(real, ob_ref[:, h * LANES:(h + 1) * LANES], 0.0).astype(BF16))
    a = jnp.concatenate(parts, axis=1)
    mixed = jnp.dot(a, w_ref[...], preferred_element_type=F32)
    y = DEEPNORM_ALPHA * x_ref[...] + mixed
    out_ref[...] = _layer_norm(y, g_ref[...], b_ref[...])


def _outproj_even(o1, o2, o3, ob, x2, w, g, b, tm=256):
    t, d = x2.shape
    n_a, n_b = o1.shape[1] // LANES, ob.shape[1] // LANES
    row = lambda c: pl.BlockSpec((tm, c), lambda i: (i, 0))
    full = lambda a: pl.BlockSpec(a.shape, lambda i: (0,) * a.ndim)
    return pl.pallas_call(
        functools.partial(_outproj_even_kernel, n_a=n_a, n_b=n_b),
        grid=(t // tm,),
        in_specs=[row(o1.shape[1]), row(o2.shape[1]), row(o3.shape[1]), row(ob.shape[1]), row(d),
                  full(w), full(g), full(b)],
        out_specs=row(d),
        out_shape=jax.ShapeDtypeStruct((t, d), F32),
        compiler_params=_cparams("parallel"),
        name="outproj_even_ln",
    )(o1, o2, o3, ob, x2, w, g, b)


def _outproj_odd_kernel(oc_ref, os_ref, ow_ref, gate_ref, e_ref, x_ref, w_ref, g_ref, b_ref, out_ref):
    gate = jax.nn.sigmoid(gate_ref[...])
    ghi, glo = _split_bf16(gate)
    acc = None
    for j, o_ref in enumerate((oc_ref, os_ref, ow_ref)):
        ej = e_ref[j]
        gfull = jnp.dot(ghi, ej, preferred_element_type=F32) + jnp.dot(glo, ej, preferred_element_type=F32)
        term = gfull * o_ref[...]
        acc = term if acc is None else acc + term
    mixed = jnp.dot(acc.astype(BF16), w_ref[...], preferred_element_type=F32)
    y = DEEPNORM_ALPHA * x_ref[...] + mixed
    out_ref[...] = _layer_norm(y, g_ref[...], b_ref[...])


def _gate_expanders(n_heads):
    e = np.zeros((3, LANES, n_heads * LANES), np.float32)
    for j in range(3):
        for h in range(n_heads):
            e[j, 3 * h + j, h * LANES:h * LANES + HEAD_DIM] = 1.0
    return jnp.asarray(e, BF16)


def _outproj_odd(oc, osl, ow, gate, x2, w, g, b, tm=256):
    t, d = x2.shape
    n_heads = oc.shape[1] // LANES
    e = _gate_expanders(n_heads)
    row = lambda c: pl.BlockSpec((tm, c), lambda i: (i, 0))
    full = lambda a: pl.BlockSpec(a.shape, lambda i: (0,) * a.ndim)
    return pl.pallas_call(
        _outproj_odd_kernel,
        grid=(t // tm,),
        in_specs=[row(oc.shape[1]), row(osl.shape[1]), row(ow.shape[1]), row(LANES), full(e), row(d),
                  full(w), full(g), full(b)],
        out_specs=row(d),
        out_shape=jax.ShapeDtypeStruct((t, d), F32),
        compiler_params=_cparams("parallel"),
        name="outproj_odd_ln",
    )(oc, osl, ow, gate, e, x2, w, g, b)


def _router_kernel(x_ref, whi_ref, wlo_ref, bias_ref, eidx_ref, gate_ref, rank_ref, cnt_ref):
    n_exp = whi_ref.shape[0]
    tm = x_ref.shape[0]
    per_group = n_exp // N_GROUPS
    xhi, xlo = _split_bf16(x_ref[...])
    whi, wlo = whi_ref[...], wlo_ref[...]
    dg = lambda a, b: lax.dot_general(a, b, _NT, preferred_element_type=F32)
    logits = dg(whi, xhi) + dg(whi, xlo) + dg(wlo, xhi)
    aff = jax.nn.sigmoid(logits)
    biased = aff + bias_ref[...]
    gio = lax.broadcasted_iota(jnp.int32, (per_group, tm), 0).astype(F32)
    blocks, scores = [], []
    for g in range(N_GROUPS):
        blk = biased[g * per_group:(g + 1) * per_group, :]
        m1 = jnp.max(blk, axis=0, keepdims=True)
        first = jnp.min(jnp.where(blk == m1, gio, float(per_group)), axis=0, keepdims=True)
        m2 = jnp.max(jnp.where(gio == first, -jnp.inf, blk), axis=0, keepdims=True)
        blocks.append(blk)
        scores.append(m1 + m2)
    masked = []
    for g in range(N_GROUPS):
        rank = jnp.zeros((1, tm), F32)
        for o in range(N_GROUPS):
            if o == g:
                continue
            beats = scores[o] >= scores[g] if o < g else scores[o] > scores[g]
            rank = rank + jnp.where(beats, 1.0, 0.0)
        masked.append(jnp.where(rank < TOPK_GROUPS, blocks[g], -jnp.inf))
    cur = jnp.concatenate(masked, axis=0)
    eio = lax.broadcasted_iota(jnp.int32, (n_exp, tm), 0).astype(F32)
    ids, gs = [], []
    for _ in range(TOP_K):
        m = jnp.max(cur, axis=0, keepdims=True)
        idx = jnp.min(jnp.where(cur == m, eio, float(n_exp)), axis=0, keepdims=True)
        hit = eio == idx
        gs.append(jnp.sum(jnp.where(hit, aff, 0.0), axis=0, keepdims=True))
        ids.append(idx)
        cur = jnp.where(hit, -jnp.inf, cur)
    gates = jnp.concatenate(gs, axis=0)
    gates = gates / jnp.sum(gates, axis=0, keepdims=True) * ROUTED_SCALE
    eidx_ref[...] = jnp.concatenate(ids, axis=0).astype(jnp.int32)
    gate_ref[...] = gates
    @pl.when(pl.program_id(0) == 0)
    def _():
        cnt_ref[...] = jnp.zeros(cnt_ref.shape, F32)

    onehot = jnp.zeros((n_exp, tm), F32)
    for idx in ids:
        onehot = onehot + jnp.where(eio == idx, 1.0, 0.0)
    earlier = jnp.where(lax.broadcasted_iota(jnp.int32, (tm, tm), 0) < lax.broadcasted_iota(jnp.int32, (tm, tm), 1),
                        1.0, 0.0).astype(BF16)
    before = cnt_ref[...] + jnp.dot(onehot.astype(BF16), earlier, preferred_element_type=F32)
    ranks = [jnp.sum(jnp.where(eio == idx, before, 0.0), axis=0, keepdims=True) for idx in ids]
    rank_ref[...] = jnp.concatenate(ranks, axis=0).astype(jnp.int32)
    cnt_ref[...] = cnt_ref[...] + jnp.sum(onehot, axis=1, keepdims=True)


def _router(x2, router_w, router_b, tm=256):
    t, d = x2.shape
    n_exp = router_w.shape[1]
    whi, wlo = _split_bf16(router_w.T)
    bias = router_b.reshape(n_exp, 1).astype(F32)
    full = lambda a: pl.BlockSpec(a.shape, lambda i: (0,) * a.ndim)
    per_tok = pl.BlockSpec((TOP_K, tm), lambda i: (0, i))
    return pl.pallas_call(
        _router_kernel,
        grid=(t // tm,),
        in_specs=[pl.BlockSpec((tm, d), lambda i: (i, 0)), full(whi), full(wlo), full(bias)],
        out_specs=[per_tok, per_tok, per_tok, pl.BlockSpec((n_exp, 1), lambda i: (0, 0))],
        out_shape=[jax.ShapeDtypeStruct((TOP_K, t), jnp.int32), jax.ShapeDtypeStruct((TOP_K, t), F32),
                   jax.ShapeDtypeStruct((TOP_K, t), jnp.int32), jax.ShapeDtypeStruct((n_exp, 1), F32)],
        compiler_params=_cparams("arbitrary"),
        name="moe_router",
    )(x2, whi, wlo, bias)


def _moe_dest_kernel(eidx_ref, rank_ref, start_ref, dest_ref):
    n_exp = start_ref.shape[0]
    tm = eidx_ref.shape[1]
    eio = lax.broadcasted_iota(jnp.int32, (n_exp, tm), 0)
    start = start_ref[...]
    rows = []
    for k in range(TOP_K):
        seg = jnp.sum(jnp.where(eio == eidx_ref[k:k + 1, :], start, 0.0), axis=0, keepdims=True)
        rows.append(seg.astype(jnp.int32) + rank_ref[k:k + 1, :])
    dest_ref[...] = jnp.concatenate(rows, axis=0)


def _moe_dest(eidx, rank, seg_start, tm=256):
    t = eidx.shape[1]
    per_tok = pl.BlockSpec((TOP_K, tm), lambda i: (0, i))
    return pl.pallas_call(
        _moe_dest_kernel,
        grid=(t // tm,),
        in_specs=[per_tok, per_tok, pl.BlockSpec(seg_start.shape, lambda i: (0, 0))],
        out_specs=per_tok,
        out_shape=jax.ShapeDtypeStruct((TOP_K, t), jnp.int32),
        compiler_params=_cparams("parallel"),
        name="moe_dest",
    )(eidx, rank, seg_start)


def _moe_zero_kernel(blk_ref, xs_ref):
    xs_ref[...] = jnp.zeros(xs_ref.shape, xs_ref.dtype)


def _moe_zero_padding(zero_blocks, n_blocks, width):
    grid_spec = pltpu.PrefetchScalarGridSpec(
        num_scalar_prefetch=1,
        grid=(zero_blocks.shape[0],),
        in_specs=[],
        out_specs=pl.BlockSpec((MOE_BLOCK, width), lambda i, blk: (blk[i], 0)),
    )
    return pl.pallas_call(
        _moe_zero_kernel,
        grid_spec=grid_spec,
        out_shape=jax.ShapeDtypeStruct((n_blocks * MOE_BLOCK, width), jnp.uint32),
        compiler_params=_cparams("arbitrary"),
        name="moe_zero_padding",
    )(zero_blocks)


def _pack_bf16_pairs(x):
    half = x.shape[1] // 2
    lo = lax.bitcast_convert_type(x[:, :half].astype(BF16).astype(F32), jnp.uint32)
    hi = lax.bitcast_convert_type(x[:, half:].astype(BF16).astype(F32), jnp.uint32)
    return jnp.right_shift(lo, jnp.uint32(16)) | (hi & jnp.uint32(0xFFFF0000))


def _unpack_bf16_pairs(w):
    lo = lax.bitcast_convert_type(jnp.left_shift(w, jnp.uint32(16)), F32).astype(BF16)
    hi = lax.bitcast_convert_type(w & jnp.uint32(0xFFFF0000), F32).astype(BF16)
    return jnp.concatenate([lo, hi], axis=1)


def _moe_dispatch_kernel(dest_ref, x_ref, xs_in, xs_out, buf, sem):
    del xs_in
    tm = x_ref.shape[0]
    buf[...] = _pack_bf16_pairs(x_ref[...])

    def issue(i, c):
        for k in range(TOP_K):
            r = dest_ref[k, i]
            pltpu.make_async_copy(buf.at[pl.ds(i, 1)], xs_out.at[pl.ds(r, 1)], sem).start()
        return c

    lax.fori_loop(0, tm, issue, 0)
    for _ in range(TOP_K):
        pltpu.make_async_copy(buf, xs_out.at[pl.ds(0, tm)], sem).wait()


def _moe_dispatch(dest, x2, xs, tm=256):
    t, d = x2.shape
    return pl.pallas_call(
        _moe_dispatch_kernel,
        grid=(t // tm,),
        in_specs=[pl.BlockSpec((TOP_K, tm), lambda i: (0, i), memory_space=pltpu.SMEM),
                  pl.BlockSpec((tm, d), lambda i: (i, 0)),
                  pl.BlockSpec(memory_space=pl.ANY)],
        out_specs=pl.BlockSpec(memory_space=pl.ANY),
        out_shape=jax.ShapeDtypeStruct(xs.shape, xs.dtype),
        scratch_shapes=[pltpu.VMEM((tm, d // 2), jnp.uint32), pltpu.SemaphoreType.DMA(())],
        input_output_aliases={2: 0},
        compiler_params=_cparams("arbitrary"),
        name="moe_dispatch",
    )(dest, x2, xs)


def _moe_ffn_kernel(be_ref, nu_ref, xs_ref, wg_ref, wu_ref, wd_ref, y_ref, wg_sc, wu_sc, wd_sc):
    b = pl.program_id(0)

    @pl.when(b < nu_ref[0])
    def _():
        @pl.when((b == 0) | (be_ref[b] != be_ref[jnp.maximum(b - 1, 0)]))
        def _():
            wg_sc[...] = wg_ref[0].astype(BF16)
            wu_sc[...] = wu_ref[0].astype(BF16)
            wd_sc[...] = wd_ref[0].astype(BF16)

        xb = _unpack_bf16_pairs(xs_ref[...])
        gp = jnp.dot(xb, wg_sc[...], preferred_element_type=F32)
        up = jnp.dot(xb, wu_sc[...], preferred_element_type=F32)
        h = (_silu(gp) * up).astype(BF16)
        y_ref[...] = jnp.dot(h, wd_sc[...], preferred_element_type=F32)

    @pl.when(b >= nu_ref[0])
    def _():
        y_ref[...] = jnp.zeros(y_ref.shape, y_ref.dtype)


def _moe_ffn(xs, block_expert, n_used, w_gate, w_up, w_down):
    n_blocks = block_expert.shape[0]
    d, ff = w_gate.shape[1], w_gate.shape[2]
    last = lambda b, nu: jnp.minimum(b, nu[0] - 1)
    grid_spec = pltpu.PrefetchScalarGridSpec(
        num_scalar_prefetch=2,
        grid=(n_blocks,),
        in_specs=[
            pl.BlockSpec((MOE_BLOCK, d // 2), lambda b, be, nu: (last(b, nu), 0)),
            pl.BlockSpec((1, d, ff), lambda b, be, nu: (be[last(b, nu)], 0, 0)),
            pl.BlockSpec((1, d, ff), lambda b, be, nu: (be[last(b, nu)], 0, 0)),
            pl.BlockSpec((1, ff, d), lambda b, be, nu: (be[last(b, nu)], 0, 0)),
        ],
        out_specs=pl.BlockSpec((MOE_BLOCK, d), lambda b, be, nu: (b, 0)),
        scratch_shapes=[pltpu.VMEM((d, ff), BF16), pltpu.VMEM((d, ff), BF16), pltpu.VMEM((ff, d), BF16)],
    )
    return pl.pallas_call(
        _moe_ffn_kernel,
        grid_spec=grid_spec,
        out_shape=jax.ShapeDtypeStruct((n_blocks * MOE_BLOCK, d), F32),
        compiler_params=_cparams("arbitrary"),
        name="moe_expert_ffn",
    )(block_expert, n_used, xs, w_gate, w_up, w_down)


def _moe_combine_kernel(dest_ref, y_hbm, gate_ref, x_ref, sg_ref, su_ref, sd_ref, g_ref, b_ref, out_ref, buf, sem):
    tm = x_ref.shape[0]

    def issue(i, c):
        for k in range(TOP_K):
            r = dest_ref[k, i]
            pltpu.make_async_copy(y_hbm.at[pl.ds(r, 1)], buf.at[k, pl.ds(i, 1)], sem).start()
        return c

    lax.fori_loop(0, tm, issue, 0)
    x = x_ref[...]
    xb = x.astype(BF16)
    hs = _silu(jnp.dot(xb, sg_ref[...], preferred_element_type=F32)) * jnp.dot(xb, su_ref[...], preferred_element_type=F32)
    shared = jnp.dot(hs.astype(BF16), sd_ref[...], preferred_element_type=F32)
    for k in range(TOP_K):
        pltpu.make_async_copy(y_hbm.at[pl.ds(0, tm)], buf.at[k], sem).wait()
    gates = gate_ref[...]
    routed = buf[0] * gates[:, 0:1]
    for k in range(1, TOP_K):
        routed = routed + buf[k] * gates[:, k:k + 1]
    y = DEEPNORM_ALPHA * x + (routed + shared)
    out_ref[...] = _layer_norm(y, g_ref[...], b_ref[...])


def _moe_combine(dest, y, gates_t, x2, sh_gate, sh_up, sh_down, g, b, tm=128):
    t, d = x2.shape
    row = lambda c: pl.BlockSpec((tm, c), lambda i: (i, 0))
    full = lambda a: pl.BlockSpec(a.shape, lambda i: (0,) * a.ndim)
    return pl.pallas_call(
        _moe_combine_kernel,
        grid=(t // tm,),
        in_specs=[pl.BlockSpec((TOP_K, tm), lambda i: (0, i), memory_space=pltpu.SMEM),
                  pl.BlockSpec(memory_space=pl.ANY),
                  row(TOP_K), row(d), full(sh_gate), full(sh_up), full(sh_down), full(g), full(b)],
        out_specs=row(d),
        out_shape=jax.ShapeDtypeStruct((t, d), F32),
        scratch_shapes=[pltpu.VMEM((TOP_K, tm, d), F32), pltpu.SemaphoreType.DMA(())],
        compiler_params=_cparams("arbitrary"),
        name="moe_combine_ln",
    )(dest, y, gates_t, x2, sh_gate, sh_up, sh_down, g, b)


def _moe_segments(counts, n_tok):
    n_exp = counts.shape[0]
    n_blocks = -(-n_tok * TOP_K // MOE_BLOCK) + n_exp
    nblk = (counts.reshape(n_exp).astype(jnp.int32) + MOE_BLOCK - 1) // MOE_BLOCK
    blk_end = jnp.cumsum(nblk)
    seg_start = ((blk_end - nblk) * MOE_BLOCK).astype(F32).reshape(n_exp, 1)
    block_expert = jnp.minimum(jnp.searchsorted(blk_end, jnp.arange(n_blocks), side='right'), n_exp - 1)
    n_used = blk_end[-1]
    tail = jnp.minimum(n_used + jnp.arange(n_exp), n_blocks - 1)
    zero_blocks = jnp.concatenate([jnp.maximum(blk_end - 1, 0), tail]).astype(jnp.int32)
    return seg_start, block_expert.astype(jnp.int32), n_used.astype(jnp.int32).reshape(1), zero_blocks, n_blocks


def _moe_layer(x2, router_w, router_b, w_gate, w_up, w_down, sh_gate, sh_up, sh_down, g, b):
    t, d = x2.shape
    eidx, gates, rank, counts = _router(x2, router_w, router_b)
    seg_start, block_expert, n_used, zero_blocks, n_blocks = _moe_segments(counts, t)
    dest = _moe_dest(eidx, rank, seg_start)
    xs = _moe_dispatch(dest, x2, _moe_zero_padding(zero_blocks, n_blocks, d // 2))
    y = _moe_ffn(xs, block_expert, n_used, w_gate, w_up, w_down)
    return _moe_combine(dest, y, gates.T, x2, sh_gate.astype(BF16), sh_up.astype(BF16), sh_down.astype(BF16), g, b)


def _compress_kernel(x_ref, pa_ref, pb_ref, wa_ref, wb_ref, w2_ref, out_ref):
    x = x_ref[0]
    nc = x.shape[0]
    ha = jnp.dot((x + pa_ref[...]).astype(BF16), wa_ref[...], preferred_element_type=F32)
    hb = jnp.dot((x + pb_ref[...]).astype(BF16), wb_ref[...], preferred_element_type=F32)
    h = ha + pltpu.roll(hb, nc - 1, 0)
    h = jax.nn.gelu(h, approximate=True)
    out_ref[0] = jnp.dot(h.astype(BF16), w2_ref[...], preferred_element_type=F32).astype(out_ref.dtype)


def _compress(kc, pos, w1, w2, batch, seq):
    g = C_KV_HEADS
    nch = seq // CMP_STRIDE
    half = CMP_LEN // 2
    x = kc.reshape(batch, nch, half * g * HEAD_DIM)
    eye = jnp.eye(g, dtype=F32)
    w1r = w1.reshape(CMP_LEN, HEAD_DIM, CMP_HIDDEN)
    expand = lambda wpart: jnp.einsum('jdh,ge->jgdeh', wpart, eye).reshape(half * g * HEAD_DIM, g * CMP_HIDDEN)
    wa, wb = expand(w1r[:half]).astype(BF16), expand(w1r[half:]).astype(BF16)
    w2e = jnp.einsum('hd,ge->ghed', jnp.pad(w2, ((0, 0), (0, LANES - HEAD_DIM))), eye)
    w2e = w2e.reshape(g * CMP_HIDDEN, g * LANES).astype(BF16)
    tile_pos = lambda p: jnp.broadcast_to(p[:, None, :], (half, g, HEAD_DIM)).reshape(1, half * g * HEAD_DIM)
    pa, pb = tile_pos(pos[:half]), tile_pos(pos[half:])
    full = lambda a: pl.BlockSpec(a.shape, lambda i: (0,) * a.ndim)
    return pl.pallas_call(
        _compress_kernel,
        grid=(batch,),
        in_specs=[pl.BlockSpec((1, nch, x.shape[2]), lambda i: (i, 0, 0)), full(pa), full(pb), full(wa), full(wb),
                  full(w2e)],
        out_specs=pl.BlockSpec((1, nch, g * LANES), lambda i: (i, 0, 0)),
        out_shape=jax.ShapeDtypeStruct((batch, nch, g * LANES), BF16),
        compiler_params=_cparams("parallel"),
        name="nsa_compress",
    )(x, pa, pb, wa, wb, w2e)


def _nsa_cmp_kernel(q_ref, kc_ref, vc_ref, ovt_ref, o_ref, sel_ref, *, rep, n_sel, n_real):
    tq = q_ref.shape[1]
    nc = kc_ref.shape[1]
    nsb = ovt_ref.shape[0]
    t0 = pl.program_id(2) * tq
    scale = HEAD_DIM ** -0.5
    kc = kc_ref[0]
    vc = vc_ref[0]
    tpos = t0 + lax.broadcasted_iota(jnp.int32, (tq, nc), 0)
    cend = lax.broadcasted_iota(jnp.int32, (tq, nc), 1) * CMP_STRIDE + (CMP_LEN - 1)
    cmask = cend <= tpos
    psum = jnp.zeros((tq, nc), F32)
    outs = []
    for r in range(rep):
        q = q_ref[0, :, r * LANES:(r + 1) * LANES]
        sc = lax.dot_general(q, kc, _NT, preferred_element_type=F32) * scale
        sc = jnp.where(cmask, sc, NEG)
        m = jnp.max(sc, axis=-1, keepdims=True)
        ex = jnp.where(cmask, jnp.exp(sc - m), 0.0)
        den = jnp.sum(ex, axis=-1, keepdims=True)
        pc = ex / jnp.where(den > 0, den, 1.0)
        outs.append(jnp.dot(pc.astype(BF16), vc, preferred_element_type=F32))
        psum = psum + pc
    o_ref[0] = jnp.concatenate(outs, axis=1)
    phi, plo = _split_bf16(psum)
    ovt = ovt_ref[...]
    imp = (lax.dot_general(ovt, phi, _NT, preferred_element_type=F32)
           + lax.dot_general(ovt, plo, _NT, preferred_element_type=F32))
    jblk = lax.broadcasted_iota(jnp.int32, (nsb, tq), 0)
    cur = jnp.right_shift(t0 + lax.broadcasted_iota(jnp.int32, (nsb, tq), 1), SLC_SHIFT)
    forced = (jblk == 0) | (jblk == cur) | (jblk == cur - 1)
    score = jnp.where(jblk > cur, -1.0, jnp.where(forced, SELECT_FORCE, imp))
    rank = jnp.zeros((nsb, tq), F32)
    for k in range(n_real):
        rowk = score[k:k + 1, :]
        ge = jnp.where(rowk >= score, 1.0, 0.0)
        gt = jnp.where(rowk > score, 1.0, 0.0)
        rank = rank + jnp.where(jblk > k, ge, gt)
    sel_ref[0, 0] = jnp.where(rank < n_sel, 1.0, 0.0)


def _nsa_cmp(q, kcmp, vcmp, batch, seq, q_off_blocks, tq=128):
    g = C_KV_HEADS
    rep = C_HEADS // g
    nc = kcmp.shape[1]
    nsb = seq // SLC_BLOCK
    n_sel = min(SLC_TOP_N, nsb)
    cs = np.arange(nc)[:, None] * CMP_STRIDE
    js = np.arange(nsb)[None, :] * SLC_BLOCK
    overlap = np.clip(np.minimum(cs + CMP_LEN, js + SLC_BLOCK) - np.maximum(cs, js), 0, None) / CMP_LEN
    overlap[(seq - CMP_LEN) // CMP_STRIDE + 1:] = 0.0
    nsb_pad = -(-nsb // LANES) * LANES
    ovt = jnp.asarray(np.pad(overlap.T, ((0, nsb_pad - nsb), (0, 0))), BF16)
    q3 = q.reshape(batch, seq, q.shape[1])
    n_real, nsb = nsb, nsb_pad
    o, sel = pl.pallas_call(
        functools.partial(_nsa_cmp_kernel, rep=rep, n_sel=n_sel, n_real=n_real),
        grid=(batch, g, seq // tq),
        in_specs=[pl.BlockSpec((1, tq, rep * LANES), lambda b, gi, i: (b, i, q_off_blocks // rep + gi)),
                  pl.BlockSpec((1, nc, LANES), lambda b, gi, i: (b, 0, gi)),
                  pl.BlockSpec((1, nc, LANES), lambda b, gi, i: (b, 0, gi)),
                  pl.BlockSpec(ovt.shape, lambda b, gi, i: (0, 0))],
        out_specs=[pl.BlockSpec((1, tq, rep * LANES), lambda b, gi, i: (b, i, gi)),
                   pl.BlockSpec((1, 1, nsb, tq), lambda b, gi, i: (b, gi, 0, i))],
        out_shape=[jax.ShapeDtypeStruct((batch, seq, C_HEADS * LANES), F32),
                   jax.ShapeDtypeStruct((batch, g, nsb, seq), F32)],
        compiler_params=_cparams("parallel", "parallel", "parallel"),
        name="nsa_compressed_select",
    )(q3, kcmp, vcmp, ovt)
    return o.reshape(batch * seq, C_HEADS * LANES), sel


def _nsa_slc_kernel(q_ref, k_ref, vt_ref, selt_ref, o_ref, bias_sc, *, rep, kt):
    tq = q_ref.shape[1]
    nsb = selt_ref.shape[2]
    t0 = pl.program_id(2) * tq
    n_kt = (t0 + tq + kt - 1) // kt
    selt = selt_ref[0, 0].astype(BF16)
    key_blk = jnp.right_shift(lax.broadcasted_iota(jnp.int32, (kt, nsb), 0), SLC_SHIFT)
    blk = lax.broadcasted_iota(jnp.int32, (kt, nsb), 1)
    kpos = lax.broadcasted_iota(jnp.int32, (kt, tq), 0)
    tpos = t0 + lax.broadcasted_iota(jnp.int32, (kt, tq), 1)

    def make_bias(j, c):
        k0 = pl.multiple_of(j * kt, kt)
        expand = jnp.where(key_blk == blk - jnp.right_shift(k0, SLC_SHIFT), 1.0, 0.0).astype(BF16)
        chosen = jnp.dot(expand, selt, preferred_element_type=F32)
        ok = (chosen > 0.5) & (k0 + kpos <= tpos)
        bias_sc[pl.ds(k0, kt), :] = jnp.where(ok, 0.0, NEG)
        return c

    lax.fori_loop(0, n_kt, make_bias, 0)

    outs = []
    pair = 2
    for r0 in range(0, rep, pair):
        qs = [q_ref[0, :, r * LANES:(r + 1) * LANES] * (HEAD_DIM ** -0.5)
              for r in range(r0, r0 + pair)]

        def tile(j, carry, qs=qs):
            k0 = pl.multiple_of(j * kt, kt)
            k = k_ref[0, pl.ds(k0, kt), :]
            vt = vt_ref[:, pl.ds(k0, kt)]
            bias = bias_sc[pl.ds(k0, kt), :]
            new = []
            for qr, (m, l, acc) in zip(qs, carry):
                s = lax.dot_general(k, qr, _NT, preferred_element_type=F32) + bias
                m_new = jnp.maximum(m, jnp.max(s, axis=0, keepdims=True))
                e = jnp.exp(s - m_new)
                corr = jnp.exp(m - m_new)
                l = l * corr + jnp.sum(e, axis=0, keepdims=True)
                acc = acc * corr + jnp.dot(vt, e.astype(BF16), preferred_element_type=F32)
                new.append((m_new, l, acc))
            return tuple(new)

        init = tuple((jnp.full((1, tq), NEG, F32), jnp.zeros((1, tq), F32), jnp.zeros((LANES, tq), F32))
                     for _ in range(pair))
        for _, l, acc in lax.fori_loop(0, n_kt, tile, init):
            outs.append((acc / l).T)
    o_ref[0] = jnp.concatenate(outs, axis=1)


def _nsa_slc(qkv, vt, selt, batch, seq, nblk, q_off, k_off, tq=128, kt=256):
    g = C_KV_HEADS
    rep = C_HEADS // g
    kt = min(kt, seq)
    nsb = selt.shape[2]
    arr = qkv.reshape(batch, seq, nblk * LANES)
    out = pl.pallas_call(
        functools.partial(_nsa_slc_kernel, rep=rep, kt=kt),
        grid=(batch, g, seq // tq),
        in_specs=[pl.BlockSpec((1, tq, rep * LANES), lambda b, gi, i: (b, i, q_off // rep + gi)),
                  pl.BlockSpec((1, seq, LANES), lambda b, gi, i: (b, 0, k_off + gi)),
                  pl.BlockSpec((LANES, seq), lambda b, gi, i: (gi, b)),
                  pl.BlockSpec((1, 1, nsb, tq), lambda b, gi, i: (b, gi, 0, i))],
        out_specs=pl.BlockSpec((1, tq, rep * LANES), lambda b, gi, i: (b, i, gi)),
        out_shape=jax.ShapeDtypeStruct((batch, seq, C_HEADS * LANES), F32),
        scratch_shapes=[pltpu.VMEM((seq, tq), F32)],
        compiler_params=_cparams("parallel", "parallel", "arbitrary"),
        name="nsa_selected",
    )(arr, arr, vt, selt)
    return out.reshape(batch * seq, C_HEADS * LANES)


def _even_mixer_layer(x2, batch, seq, tabs, w_in, sinks, w_out, g, b):
    d = x2.shape[1]
    n_heads_in = 3 * A_HEADS + B_Q_HEADS + 2 * B_KV_HEADS
    w = _pad_heads_cols(w_in, n_heads_in).astype(BF16)
    rope = [1] * (2 * A_HEADS) + [0] * A_HEADS + [1] * B_Q_HEADS + [1] * B_KV_HEADS + [0] * B_KV_HEADS
    plan = [(0, c, rope[c]) for c in range(n_heads_in)]
    (qkv,) = _proj(x2, w, tabs, plan, [n_heads_in * LANES], [BF16])
    outs = []
    for window, dilation in A_PATTERNS:
        outs.append(_band_attention(qkv, batch=batch, seq=seq, dilation=dilation, nblk=n_heads_in, q_off=0,
                                    k_off=A_HEADS, v_off=2 * A_HEADS, n_q_heads=A_HEADS, rep=1,
                                    max_dist=window // dilation))
    qb_off = 3 * A_HEADS
    ob = _band_attention(qkv, batch=batch, seq=seq, dilation=1, nblk=n_heads_in, q_off=qb_off,
                         k_off=qb_off + B_Q_HEADS, v_off=qb_off + B_Q_HEADS + B_KV_HEADS, n_q_heads=B_Q_HEADS,
                         rep=B_Q_HEADS // B_KV_HEADS, max_dist=B_WINDOW - 1, sinks=sinks)
    w_o = _pad_heads_rows(w_out, A_HEADS + B_Q_HEADS).astype(BF16)
    return _outproj_even(outs[0], outs[1], outs[2], ob, x2, w_o, g.reshape(1, d), b.reshape(1, d))


def _odd_mixer_layer(x2, batch, seq, tabs, w_in, cmpk_pos, cmpk_w1, cmpk_w2, cmpv_pos, cmpv_w1, cmpv_w2, w_out, g, b):
    d = x2.shape[1]
    kvw = C_KV_HEADS * HEAD_DIM
    qw = C_HEADS * HEAD_DIM
    sizes = [qw] + [kvw] * 6 + [3 * C_HEADS]
    offs = np.concatenate([[0], np.cumsum(sizes)])
    wq, wkc, wvc, wks, wvs, wkw, wvw, wgt = [w_in[:, offs[i]:offs[i + 1]] for i in range(8)]
    ph = lambda wpart, n: _pad_heads_cols(wpart, n)
    w = jnp.concatenate([ph(wq, C_HEADS), ph(wks, C_KV_HEADS), ph(wkw, C_KV_HEADS), ph(wvw, C_KV_HEADS),
                         wkc, wvc, jnp.pad(wgt, ((0, 0), (0, LANES - 3 * C_HEADS)))], axis=1).astype(BF16)
    wvs_t = ph(wvs, C_KV_HEADS).T.astype(BF16)
    n16 = C_HEADS + 3 * C_KV_HEADS
    rope16 = [1] * C_HEADS + [1] * C_KV_HEADS + [1] * C_KV_HEADS + [0] * C_KV_HEADS
    n_kc = kvw // LANES
    plan = ([(0, c, rope16[c]) for c in range(n16)] + [(1, c, 2) for c in range(n_kc)]
            + [(2, c, 0) for c in range(n_kc)] + [(3, 0, 0)])
    qkv, kc, vc, gate, vs_t = _proj(x2, w, tabs, plan, [n16 * LANES, kvw, kvw, LANES], [BF16, F32, F32, F32],
                                    wt=wvs_t)
    kcmp = _compress(kc, cmpk_pos, cmpk_w1, cmpk_w2, batch, seq)
    vcmp = _compress(vc, cmpv_pos, cmpv_w1, cmpv_w2, batch, seq)
    o_cmp, selt = _nsa_cmp(qkv, kcmp, vcmp, batch, seq, 0)
    ks_off = C_HEADS
    o_slc = _nsa_slc(qkv, vs_t, selt, batch, seq, n16, 0, ks_off)
    kw_off = ks_off + C_KV_HEADS
    o_win = _band_attention(qkv, batch=batch, seq=seq, dilation=1, nblk=n16, q_off=0, k_off=kw_off,
                            v_off=kw_off + C_KV_HEADS, n_q_heads=C_HEADS, rep=C_HEADS // C_KV_HEADS,
                            max_dist=NSA_WINDOW - 1)
    w_o = _pad_heads_rows(w_out, C_HEADS).astype(BF16)
    return _outproj_odd(o_cmp, o_slc, o_win, gate, x2, w_o, g.reshape(1, d), b.reshape(1, d))


def kernel(x, positions, even_w_in, even_sinks, even_w_out, odd_w_in, odd_cmpk_pos, odd_cmpk_w1, odd_cmpk_w2, odd_cmpv_pos, odd_cmpv_w1, odd_cmpv_w2, odd_w_out, mix_ln_g, mix_ln_b, moe_router_w, moe_router_b, moe_w_gate, moe_w_up, moe_w_down, moe_sh_gate, moe_sh_up, moe_sh_down, ffn_ln_g, ffn_ln_b):
    batch, seq, d = x.shape
    x2 = x.reshape(batch * seq, d)
    tabs = _rope_tables(positions)
    depth = mix_ln_g.shape[0]
    for layer in range(depth):
        j = layer // 2
        if layer % 2 == 0:
            x2 = _even_mixer_layer(x2, batch, seq, tabs, even_w_in[j], even_sinks[j], even_w_out[j],
                                   mix_ln_g[layer], mix_ln_b[layer])
        else:
            x2 = _odd_mixer_layer(x2, batch, seq, tabs, odd_w_in[j], odd_cmpk_pos[j], odd_cmpk_w1[j], odd_cmpk_w2[j],
                                  odd_cmpv_pos[j], odd_cmpv_w1[j], odd_cmpv_w2[j], odd_w_out[j],
                                  mix_ln_g[layer], mix_ln_b[layer])
        x2 = _moe_layer(x2, moe_router_w[layer], moe_router_b[layer], moe_w_gate[layer], moe_w_up[layer],
                        moe_w_down[layer], moe_sh_gate[layer], moe_sh_up[layer], moe_sh_down[layer],
                        ffn_ln_g[layer].reshape(1, d), ffn_ln_b[layer].reshape(1, d))
    return x2.reshape(batch, seq, d)
```

```python
import functools

import numpy as np
import jax
import jax.numpy as jnp
from jax import lax
from jax.experimental import pallas as pl
from jax.experimental.pallas import tpu as pltpu

F32 = jnp.float32
BF16 = jnp.bfloat16

LANES = 128
HEAD_DIM = 64
ROT_DIM = HEAD_DIM // 4
ROT_HALF = ROT_DIM // 2
ROPE_THETA = 500000.0
QBLK = 128
A_HEADS = 8
A_PATTERNS = ((128, 1), (512, 4), (2048, 16))
B_Q_HEADS = 8
B_KV_HEADS = 2
B_WINDOW = 128
C_HEADS = 16
C_KV_HEADS = 4
CMP_LEN = 32
CMP_STRIDE = 16
CMP_HIDDEN = 2 * HEAD_DIM
SLC_BLOCK = 64
SLC_SHIFT = 6
SLC_TOP_N = 16
NSA_WINDOW = 512
SELECT_FORCE = 1.0e4
N_EXPERTS = 256
TOP_K = 8
N_GROUPS = 8
TOPK_GROUPS = 4
ROUTED_SCALE = 2.5
MOE_BLOCK = 256
DEPTH = 2
DEEPNORM_ALPHA = (2 * DEPTH) ** 0.25
LN_EPS = 1e-5
NEG = -1.0e30
VMEM_LIMIT = 56 * 1024 * 1024

_NT = (((1,), (1,)), ((), ()))


def _cparams(*sem):
    return pltpu.CompilerParams(dimension_semantics=sem, vmem_limit_bytes=VMEM_LIMIT)


def _split_bf16(a):
    hi = a.astype(BF16)
    lo = (a - hi.astype(F32)).astype(BF16)
    return hi, lo


def _layer_norm(y, g, b):
    mu = jnp.mean(y, axis=-1, keepdims=True)
    d = y - mu
    var = jnp.mean(d * d, axis=-1, keepdims=True)
    return d * lax.rsqrt(var + LN_EPS) * g + b


def _silu(a):
    return a * jax.nn.sigmoid(a)


def _proj_kernel(*refs, plan, transposed):
    if transposed:
        x_ref, w_ref, tab_ref, wt_ref = refs[:4]
        out_refs = refs[4:]
    else:
        x_ref, w_ref, tab_ref = refs[:3]
        out_refs = refs[3:]
    x = x_ref[...].astype(BF16)
    if transposed:
        out_t = out_refs[-1]
        out_t[...] = lax.dot_general(wt_ref[...], x, _NT, preferred_element_type=F32).astype(out_t.dtype)
    nblk = len(plan)
    for c0 in range(0, nblk, 2):
        nb = min(2, nblk - c0)
        acc = jnp.dot(x, w_ref[:, c0 * LANES:(c0 + nb) * LANES], preferred_element_type=F32)
        for j in range(nb):
            blk = acc[:, j * LANES:(j + 1) * LANES]
            dst, dblk, mode = plan[c0 + j]
            if mode:
                off = (mode - 1) * 3 * LANES
                cos = tab_ref[:, off:off + LANES]
                s_lo = tab_ref[:, off + LANES:off + 2 * LANES]
                s_hi = tab_ref[:, off + 2 * LANES:off + 3 * LANES]
                blk = (blk * cos + pltpu.roll(blk, LANES - ROT_HALF, 1) * s_lo
                       + pltpu.roll(blk, ROT_HALF, 1) * s_hi)
            o_ref = out_refs[dst]
            o_ref[:, dblk * LANES:(dblk + 1) * LANES] = blk.astype(o_ref.dtype)


def _proj(x2, w, tabs, plan, out_cols, out_dtypes, wt=None, tm=256):
    t, d = x2.shape
    ncol = w.shape[1]
    out_shape = [jax.ShapeDtypeStruct((t, c), dt) for c, dt in zip(out_cols, out_dtypes)]
    in_specs = [pl.BlockSpec((tm, d), lambda i: (i, 0)),
                pl.BlockSpec((d, ncol), lambda i: (0, 0)),
                pl.BlockSpec((tm, tabs.shape[1]), lambda i: (i, 0))]
    out_specs = [pl.BlockSpec((tm, c), lambda i: (i, 0)) for c in out_cols]
    args = [x2, w, tabs]
    if wt is not None:
        in_specs.append(pl.BlockSpec(wt.shape, lambda i: (0, 0)))
        out_specs.append(pl.BlockSpec((wt.shape[0], tm), lambda i: (0, i)))
        out_shape.append(jax.ShapeDtypeStruct((wt.shape[0], t), BF16))
        args.append(wt)
    return pl.pallas_call(
        functools.partial(_proj_kernel, plan=tuple(plan), transposed=wt is not None),
        grid=(t // tm,),
        in_specs=in_specs,
        out_specs=out_specs,
        out_shape=out_shape,
        compiler_params=_cparams("parallel"),
        name="proj_rope",
    )(*args)


def _rope_tables(positions):
    t = positions.size
    inv_freq = jnp.asarray(ROPE_THETA ** (-np.arange(0, ROT_DIM, 2) / ROT_DIM), F32)
    ang = positions.astype(F32).reshape(t, 1) * inv_freq
    trig = jnp.concatenate([jnp.cos(ang), jnp.sin(ang)], axis=1)
    place = np.zeros((2 * ROT_HALF, 6 * LANES), np.float32)
    const = np.zeros((6 * LANES,), np.float32)
    for base, heads in ((0, (0,)), (3 * LANES, (0, HEAD_DIM))):
        const[base:base + LANES] = 1.0
        for h in heads:
            for i in range(ROT_HALF):
                place[i, base + h + i] = 1.0
                place[i, base + h + ROT_HALF + i] = 1.0
                const[base + h + i] = const[base + h + ROT_HALF + i] = 0.0
                place[ROT_HALF + i, base + LANES + h + i] = -1.0
                place[ROT_HALF + i, base + 2 * LANES + h + ROT_HALF + i] = 1.0
    return jnp.dot(trig, jnp.asarray(place), precision=lax.Precision.HIGHEST) + jnp.asarray(const)


def _pad_heads_cols(w, n_heads):
    d = w.shape[0]
    w = w.reshape(d, n_heads, HEAD_DIM)
    return jnp.pad(w, ((0, 0), (0, 0), (0, LANES - HEAD_DIM))).reshape(d, n_heads * LANES)


def _pad_heads_rows(w, n_heads):
    d = w.shape[1]
    w = w.reshape(n_heads, HEAD_DIM, d)
    return jnp.pad(w, ((0, 0), (0, LANES - HEAD_DIM), (0, 0))).reshape(n_heads * LANES, d)


def _band_kernel(*refs, n_prev, max_dist, length, window, has_sink):
    if has_sink:
        sink_ref, q_ref, k_ref, v_ref, o_ref = refs
    else:
        q_ref, k_ref, v_ref, o_ref = refs
    nq = length // QBLK
    scale = HEAD_DIM ** -0.5
    lane = lax.broadcasted_iota(jnp.int32, (QBLK, LANES), 1)
    row = lax.broadcasted_iota(jnp.int32, (QBLK, window), 0)
    col = lax.broadcasted_iota(jnp.int32, (QBLK, window), 1)
    sink = sink_ref[pl.program_id(2)] if has_sink else None

    def body(qi, carry):
        q0 = pl.multiple_of(qi * QBLK, QBLK)
        k0 = pl.multiple_of(jnp.maximum(qi - n_prev, 0) * QBLK, QBLK)
        q = q_ref[0, pl.ds(q0, QBLK), :]
        k = k_ref[0, pl.ds(k0, window), :]
        v = v_ref[0, pl.ds(k0, window), :]
        s = lax.dot_general(q, k, _NT, preferred_element_type=F32) * scale
        dist = (q0 + row) - (k0 + col)
        s = jnp.where((dist >= 0) & (dist <= max_dist), s, NEG)
        m = jnp.max(s, axis=-1, keepdims=True)
        e = jnp.exp(s - m)
        den = jnp.sum(e, axis=-1, keepdims=True)
        if has_sink:
            den = den + jnp.exp(sink - m)
        o = jnp.dot(e.astype(BF16), v, preferred_element_type=F32) / den
        lse = m + jnp.log(den)
        o_ref[0, pl.ds(q0, QBLK), :] = jnp.where(lane < HEAD_DIM, o, lse)
        return carry

    lax.fori_loop(0, nq, body, 0, unroll=2 if nq % 2 == 0 else 1)


def _band_attention(qkv, *, batch, seq, dilation, nblk, q_off, k_off, v_off, n_q_heads, rep, max_dist, sinks=None):
    length = seq // dilation
    n_prev = -(-max_dist // QBLK)
    window = min((n_prev + 1) * QBLK, length)
    arr = qkv.reshape(batch, length, dilation * nblk * LANES)
    blk = lambda off, div: pl.BlockSpec((1, length, LANES), lambda b, r, h: (b, 0, r * nblk + off + h // div))
    in_specs = [blk(q_off, 1), blk(k_off, rep), blk(v_off, rep)]
    args = [arr, arr, arr]
    if sinks is not None:
        in_specs = [pl.BlockSpec(memory_space=pltpu.SMEM)] + in_specs
        args = [sinks.reshape(-1).astype(F32)] + args
    out = pl.pallas_call(
        functools.partial(_band_kernel, n_prev=n_prev, max_dist=max_dist, length=length, window=window,
                          has_sink=sinks is not None),
        grid=(batch, dilation, n_q_heads),
        in_specs=in_specs,
        out_specs=pl.BlockSpec((1, length, LANES), lambda b, r, h: (b, 0, r * n_q_heads + h)),
        out_shape=jax.ShapeDtypeStruct((batch, length, dilation * n_q_heads * LANES), F32),
        compiler_params=_cparams("parallel", "parallel", "parallel"),
        name="band_attention",
    )(*args)
    return out.reshape(batch * seq, n_q_heads * LANES)


def _outproj_even_kernel(o1_ref, o2_ref, o3_ref, ob_ref, x_ref, w_ref, g_ref, b_ref, out_ref, *, n_a, n_b):
    tm = x_ref.shape[0]
    lane = lax.broadcasted_iota(jnp.int32, (tm, LANES), 1)
    real = lane < HEAD_DIM
    parts = []
    for h in range(n_a):
        sl = slice(h * LANES, (h + 1) * LANES)
        outs = [r[:, sl] for r in (o1_ref, o2_ref, o3_ref)]
        lses = [jnp.where(real, pltpu.roll(a, HEAD_DIM, 1), a) for a in outs]
        m = jnp.maximum(jnp.maximum(lses[0], lses[1]), lses[2])
        es = [jnp.exp(l - m) for l in lses]
        num = es[0] * outs[0] + es[1] * outs[1] + es[2] * outs[2]
        den = es[0] + es[1] + es[2]
        parts.append(jnp.where(real, num / den, 0.0).astype(BF16))
    for h in range(n_b):
        parts.append(jnp.where(real, ob_ref[:, h * LANES:(h + 1) * LANES], 0.0).astype(BF16))
    a = jnp.concatenate(parts, axis=1)
    mixed = jnp.dot(a, w_ref[...], preferred_element_type=F32)
    y = DEEPNORM_ALPHA * x_ref[...] + mixed
    out_ref[...] = _layer_norm(y, g_ref[...], b_ref[...])


def _outproj_even(o1, o2, o3, ob, x2, w, g, b, tm=256):
    t, d = x2.shape
    n_a, n_b = o1.shape[1] // LANES, ob.shape[1] // LANES
    row = lambda c: pl.BlockSpec((tm, c), lambda i: (i, 0))
    full = lambda a: pl.BlockSpec(a.shape, lambda i: (0,) * a.ndim)
    return pl.pallas_call(
        functools.partial(_outproj_even_kernel, n_a=n_a, n_b=n_b),
        grid=(t // tm,),
        in_specs=[row(o1.shape[1]), row(o2.shape[1]), row(o3.shape[1]), row(ob.shape[1]), row(d),
                  full(w), full(g), full(b)],
        out_specs=row(d),
        out_shape=jax.ShapeDtypeStruct((t, d), F32),
        compiler_params=_cparams("parallel"),
        name="outproj_even_ln",
    )(o1, o2, o3, ob, x2, w, g, b)


def _outproj_odd_kernel(oc_ref, os_ref, ow_ref, gate_ref, e_ref, x_ref, w_ref, g_ref, b_ref, out_ref):
    gate = jax.nn.sigmoid(gate_ref[...])
    ghi, glo = _split_bf16(gate)
    acc = None
    for j, o_ref in enumerate((oc_ref, os_ref, ow_ref)):
        ej = e_ref[j]
        gfull = jnp.dot(ghi, ej, preferred_element_type=F32) + jnp.dot(glo, ej, preferred_element_type=F32)
        term = gfull * o_ref[...]
        acc = term if acc is None else acc + term
    mixed = jnp.dot(acc.astype(BF16), w_ref[...], preferred_element_type=F32)
    y = DEEPNORM_ALPHA * x_ref[...] + mixed
    out_ref[...] = _layer_norm(y, g_ref[...], b_ref[...])


def _gate_expanders(n_heads):
    e = np.zeros((3, LANES, n_heads * LANES), np.float32)
    for j in range(3):
        for h in range(n_heads):
            e[j, 3 * h + j, h * LANES:h * LANES + HEAD_DIM] = 1.0
    return jnp.asarray(e, BF16)


def _outproj_odd(oc, osl, ow, gate, x2, w, g, b, tm=256):
    t, d = x2.shape
    n_heads = oc.shape[1] // LANES
    e = _gate_expanders(n_heads)
    row = lambda c: pl.BlockSpec((tm, c), lambda i: (i, 0))
    full = lambda a: pl.BlockSpec(a.shape, lambda i: (0,) * a.ndim)
    return pl.pallas_call(
        _outproj_odd_kernel,
        grid=(t // tm,),
        in_specs=[row(oc.shape[1]), row(osl.shape[1]), row(ow.shape[1]), row(LANES), full(e), row(d),
                  full(w), full(g), full(b)],
        out_specs=row(d),
        out_shape=jax.ShapeDtypeStruct((t, d), F32),
        compiler_params=_cparams("parallel"),
        name="outproj_odd_ln",
    )(oc, osl, ow, gate, e, x2, w, g, b)


def _router_kernel(x_ref, whi_ref, wlo_ref, bias_ref, eidx_ref, gate_ref, rank_ref, cnt_ref):
    n_exp = whi_ref.shape[0]
    tm = x_ref.shape[0]
    per_group = n_exp // N_GROUPS
    xhi, xlo = _split_bf16(x_ref[...])
    whi, wlo = whi_ref[...], wlo_ref[...]
    dg = lambda a, b: lax.dot_general(a, b, _NT, preferred_element_type=F32)
    logits = dg(whi, xhi) + dg(whi, xlo) + dg(wlo, xhi)
    aff = jax.nn.sigmoid(logits)
    biased = aff + bias_ref[...]
    gio = lax.broadcasted_iota(jnp.int32, (per_group, tm), 0).astype(F32)
    blocks, scores = [], []
    for g in range(N_GROUPS):
        blk = biased[g * per_group:(g + 1) * per_group, :]
        m1 = jnp.max(blk, axis=0, keepdims=True)
        first = jnp.min(jnp.where(blk == m1, gio, float(per_group)), axis=0, keepdims=True)
        m2 = jnp.max(jnp.where(gio == first, -jnp.inf, blk), axis=0, keepdims=True)
        blocks.append(blk)
        scores.append(m1 + m2)
    masked = []
    for g in range(N_GROUPS):
        rank = jnp.zeros((1, tm), F32)
        for o in range(N_GROUPS):
            if o == g:
                continue
            beats = scores[o] >= scores[g] if o < g else scores[o] > scores[g]
            rank = rank + jnp.where(beats, 1.0, 0.0)
        masked.append(jnp.where(rank < TOPK_GROUPS, blocks[g], -jnp.inf))
    cur = jnp.concatenate(masked, axis=0)
    eio = lax.broadcasted_iota(jnp.int32, (n_exp, tm), 0).astype(F32)
    ids, gs = [], []
    for _ in range(TOP_K):
        m = jnp.max(cur, axis=0, keepdims=True)
        idx = jnp.min(jnp.where(cur == m, eio, float(n_exp)), axis=0, keepdims=True)
        hit = eio == idx
        gs.append(jnp.sum(jnp.where(hit, aff, 0.0), axis=0, keepdims=True))
        ids.append(idx)
        cur = jnp.where(hit, -jnp.inf, cur)
    gates = jnp.concatenate(gs, axis=0)
    gates = gates / jnp.sum(gates, axis=0, keepdims=True) * ROUTED_SCALE
    eidx_ref[...] = jnp.concatenate(ids, axis=0).astype(jnp.int32)
    gate_ref[...] = gates
    @pl.when(pl.program_id(0) == 0)
    def _():
        cnt_ref[...] = jnp.zeros(cnt_ref.shape, F32)

    onehot = jnp.zeros((n_exp, tm), F32)
    for idx in ids:
        onehot = onehot + jnp.where(eio == idx, 1.0, 0.0)
    earlier = jnp.where(lax.broadcasted_iota(jnp.int32, (tm, tm), 0) < lax.broadcasted_iota(jnp.int32, (tm, tm), 1),
                        1.0, 0.0).astype(BF16)
    before = cnt_ref[...] + jnp.dot(onehot.astype(BF16), earlier, preferred_element_type=F32)
    ranks = [jnp.sum(jnp.where(eio == idx, before, 0.0), axis=0, keepdims=True) for idx in ids]
    rank_ref[...] = jnp.concatenate(ranks, axis=0).astype(jnp.int32)
    cnt_ref[...] = cnt_ref[...] + jnp.sum(onehot, axis=1, keepdims=True)


def _router(x2, router_w, router_b, tm=256):
    t, d = x2.shape
    n_exp = router_w.shape[1]
    whi, wlo = _split_bf16(router_w.T)
    bias = router_b.reshape(n_exp, 1).astype(F32)
    full = lambda a: pl.BlockSpec(a.shape, lambda i: (0,) * a.ndim)
    per_tok = pl.BlockSpec((TOP_K, tm), lambda i: (0, i))
    return pl.pallas_call(
        _router_kernel,
        grid=(t // tm,),
        in_specs=[pl.BlockSpec((tm, d), lambda i: (i, 0)), full(whi), full(wlo), full(bias)],
        out_specs=[per_tok, per_tok, per_tok, pl.BlockSpec((n_exp, 1), lambda i: (0, 0))],
        out_shape=[jax.ShapeDtypeStruct((TOP_K, t), jnp.int32), jax.ShapeDtypeStruct((TOP_K, t), F32),
                   jax.ShapeDtypeStruct((TOP_K, t), jnp.int32), jax.ShapeDtypeStruct((n_exp, 1), F32)],
        compiler_params=_cparams("arbitrary"),
        name="moe_router",
    )(x2, whi, wlo, bias)


def _moe_dest_kernel(eidx_ref, rank_ref, start_ref, dest_ref):
    n_exp = start_ref.shape[0]
    tm = eidx_ref.shape[1]
    eio = lax.broadcasted_iota(jnp.int32, (n_exp, tm), 0)
    start = start_ref[...]
    rows = []
    for k in range(TOP_K):
        seg = jnp.sum(jnp.where(eio == eidx_ref[k:k + 1, :], start, 0.0), axis=0, keepdims=True)
        rows.append(seg.astype(jnp.int32) + rank_ref[k:k + 1, :])
    dest_ref[...] = jnp.concatenate(rows, axis=0)


def _moe_dest(eidx, rank, seg_start, tm=256):
    t = eidx.shape[1]
    per_tok = pl.BlockSpec((TOP_K, tm), lambda i: (0, i))
    return pl.pallas_call(
        _moe_dest_kernel,
        grid=(t // tm,),
        in_specs=[per_tok, per_tok, pl.BlockSpec(seg_start.shape, lambda i: (0, 0))],
        out_specs=per_tok,
        out_shape=jax.ShapeDtypeStruct((TOP_K, t), jnp.int32),
        compiler_params=_cparams("parallel"),
        name="moe_dest",
    )(eidx, rank, seg_start)


def _moe_zero_kernel(blk_ref, xs_ref):
    xs_ref[...] = jnp.zeros(xs_ref.shape, xs_ref.dtype)


def _moe_zero_padding(zero_blocks, n_blocks, width):
    grid_spec = pltpu.PrefetchScalarGridSpec(
        num_scalar_prefetch=1,
        grid=(zero_blocks.shape[0],),
        in_specs=[],
        out_specs=pl.BlockSpec((MOE_BLOCK, width), lambda i, blk: (blk[i], 0)),
    )
    return pl.pallas_call(
        _moe_zero_kernel,
        grid_spec=grid_spec,
        out_shape=jax.ShapeDtypeStruct((n_blocks * MOE_BLOCK, width), jnp.uint32),
        compiler_params=_cparams("arbitrary"),
        name="moe_zero_padding",
    )(zero_blocks)


def _pack_bf16_pairs(x):
    half = x.shape[1] // 2
    lo = lax.bitcast_convert_type(x[:, :half].astype(BF16).astype(F32), jnp.uint32)
    hi = lax.bitcast_convert_type(x[:, half:].astype(BF16).astype(F32), jnp.uint32)
    return jnp.right_shift(lo, jnp.uint32(16)) | (hi & jnp.uint32(0xFFFF0000))


def _unpack_bf16_pairs(w):
    lo = lax.bitcast_convert_type(jnp.left_shift(w, jnp.uint32(16)), F32).astype(BF16)
    hi = lax.bitcast_convert_type(w & jnp.uint32(0xFFFF0000), F32).astype(BF16)
    return jnp.concatenate([lo, hi], axis=1)


def _moe_dispatch_kernel(dest_ref, x_ref, xs_in, xs_out, buf, sem):
    del xs_in
    tm = x_ref.shape[0]
    buf[...] = _pack_bf16_pairs(x_ref[...])

    def issue(i, c):
        for k in range(TOP_K):
            r = dest_ref[k, i]
            pltpu.make_async_copy(buf.at[pl.ds(i, 1)], xs_out.at[pl.ds(r, 1)], sem).start()
        return c

    lax.fori_loop(0, tm, issue, 0)
    for _ in range(TOP_K):
        pltpu.make_async_copy(buf, xs_out.at[pl.ds(0, tm)], sem).wait()


def _moe_dispatch(dest, x2, xs, tm=256):
    t, d = x2.shape
    return pl.pallas_call(
        _moe_dispatch_kernel,
        grid=(t // tm,),
        in_specs=[pl.BlockSpec((TOP_K, tm), lambda i: (0, i), memory_space=pltpu.SMEM),
                  pl.BlockSpec((tm, d), lambda i: (i, 0)),
                  pl.BlockSpec(memory_space=pl.ANY)],
        out_specs=pl.BlockSpec(memory_space=pl.ANY),
        out_shape=jax.ShapeDtypeStruct(xs.shape, xs.dtype),
        scratch_shapes=[pltpu.VMEM((tm, d // 2), jnp.uint32), pltpu.SemaphoreType.DMA(())],
        input_output_aliases={2: 0},
        compiler_params=_cparams("arbitrary"),
        name="moe_dispatch",
    )(dest, x2, xs)


def _moe_ffn_kernel(be_ref, nu_ref, xs_ref, wg_ref, wu_ref, wd_ref, y_ref, wg_sc, wu_sc, wd_sc):
    b = pl.program_id(0)

    @pl.when(b < nu_ref[0])
    def _():
        @pl.when((b == 0) | (be_ref[b] != be_ref[jnp.maximum(b - 1, 0)]))
        def _():
            wg_sc[...] = wg_ref[0].astype(BF16)
            wu_sc[...] = wu_ref[0].astype(BF16)
            wd_sc[...] = wd_ref[0].astype(BF16)

        xb = _unpack_bf16_pairs(xs_ref[...])
        gp = jnp.dot(xb, wg_sc[...], preferred_element_type=F32)
        up = jnp.dot(xb, wu_sc[...], preferred_element_type=F32)
        h = (_silu(gp) * up).astype(BF16)
        y_ref[...] = jnp.dot(h, wd_sc[...], preferred_element_type=F32)

    @pl.when(b >= nu_ref[0])
    def _():
        y_ref[...] = jnp.zeros(y_ref.shape, y_ref.dtype)


def _moe_ffn(xs, block_expert, n_used, w_gate, w_up, w_down, layer):
    n_blocks = block_expert.shape[0]
    d, ff = w_gate.shape[2], w_gate.shape[3]
    last = lambda b, nu: jnp.minimum(b, nu[0] - 1)
    grid_spec = pltpu.PrefetchScalarGridSpec(
        num_scalar_prefetch=2,
        grid=(n_blocks,),
        in_specs=[
            pl.BlockSpec((MOE_BLOCK, d // 2), lambda b, be, nu: (last(b, nu), 0)),
            pl.BlockSpec((None, 1, d, ff), lambda b, be, nu: (layer, be[last(b, nu)], 0, 0)),
            pl.BlockSpec((None, 1, d, ff), lambda b, be, nu: (layer, be[last(b, nu)], 0, 0)),
            pl.BlockSpec((None, 1, ff, d), lambda b, be, nu: (layer, be[last(b, nu)], 0, 0)),
        ],
        out_specs=pl.BlockSpec((MOE_BLOCK, d), lambda b, be, nu: (b, 0)),
        scratch_shapes=[pltpu.VMEM((d, ff), BF16), pltpu.VMEM((d, ff), BF16), pltpu.VMEM((ff, d), BF16)],
    )
    return pl.pallas_call(
        _moe_ffn_kernel,
        grid_spec=grid_spec,
        out_shape=jax.ShapeDtypeStruct((n_blocks * MOE_BLOCK, d), F32),
        compiler_params=_cparams("arbitrary"),
        name="moe_expert_ffn",
    )(block_expert, n_used, xs, w_gate, w_up, w_down)


def _moe_combine_kernel(dest_ref, y_hbm, gate_ref, x_ref, sg_ref, su_ref, sd_ref, g_ref, b_ref, out_ref, buf, sem):
    tm = x_ref.shape[0]

    def issue(i, c):
        for k in range(TOP_K):
            r = dest_ref[k, i]
            pltpu.make_async_copy(y_hbm.at[pl.ds(r, 1)], buf.at[k, pl.ds(i, 1)], sem).start()
        return c

    lax.fori_loop(0, tm, issue, 0)
    x = x_ref[...]
    xb = x.astype(BF16)
    hs = _silu(jnp.dot(xb, sg_ref[...], preferred_element_type=F32)) * jnp.dot(xb, su_ref[...], preferred_element_type=F32)
    shared = jnp.dot(hs.astype(BF16), sd_ref[...], preferred_element_type=F32)
    for k in range(TOP_K):
        pltpu.make_async_copy(y_hbm.at[pl.ds(0, tm)], buf.at[k], sem).wait()
    gates = gate_ref[...]
    routed = buf[0] * gates[:, 0:1]
    for k in range(1, TOP_K):
        routed = routed + buf[k] * gates[:, k:k + 1]
    y = DEEPNORM_ALPHA * x + (routed + shared)
    out_ref[...] = _layer_norm(y, g_ref[...], b_ref[...])


def _moe_combine(dest, y, gates_t, x2, sh_gate, sh_up, sh_down, g, b, tm=128):
    t, d = x2.shape
    row = lambda c: pl.BlockSpec((tm, c), lambda i: (i, 0))
    full = lambda a: pl.BlockSpec(a.shape, lambda i: (0,) * a.ndim)
    return pl.pallas_call(
        _moe_combine_kernel,
        grid=(t // tm,),
        in_specs=[pl.BlockSpec((TOP_K, tm), lambda i: (0, i), memory_space=pltpu.SMEM),
                  pl.BlockSpec(memory_space=pl.ANY),
                  row(TOP_K), row(d), full(sh_gate), full(sh_up), full(sh_down), full(g), full(b)],
        out_specs=row(d),
        out_shape=jax.ShapeDtypeStruct((t, d), F32),
        scratch_shapes=[pltpu.VMEM((TOP_K, tm, d), F32), pltpu.SemaphoreType.DMA(())],
        compiler_params=_cparams("arbitrary"),
        name="moe_combine_ln",
    )(dest, y, gates_t, x2, sh_gate, sh_up, sh_down, g, b)


def _moe_segments(counts, n_tok):
    n_exp = counts.shape[0]
    n_blocks = -(-n_tok * TOP_K // MOE_BLOCK) + n_exp
    nblk = (counts.reshape(n_exp).astype(jnp.int32) + MOE_BLOCK - 1) // MOE_BLOCK
    blk_end = jnp.cumsum(nblk)
    seg_start = ((blk_end - nblk) * MOE_BLOCK).astype(F32).reshape(n_exp, 1)
    block_expert = jnp.sum((blk_end[None, :] <= jnp.arange(n_blocks)[:, None]).astype(jnp.int32), axis=1)
    block_expert = jnp.minimum(block_expert, n_exp - 1)
    n_used = blk_end[-1]
    tail = jnp.minimum(n_used + jnp.arange(n_exp), n_blocks - 1)
    zero_blocks = jnp.concatenate([jnp.maximum(blk_end - 1, 0), tail]).astype(jnp.int32)
    return seg_start, block_expert.astype(jnp.int32), n_used.astype(jnp.int32).reshape(1), zero_blocks, n_blocks


def _moe_layer(x2, router_w, router_b, w_gate, w_up, w_down, layer, sh_gate, sh_up, sh_down, g, b):
    t, d = x2.shape
    eidx, gates, rank, counts = _router(x2, router_w, router_b)
    seg_start, block_expert, n_used, zero_blocks, n_blocks = _moe_segments(counts, t)
    dest = _moe_dest(eidx, rank, seg_start)
    xs = _moe_dispatch(dest, x2, _moe_zero_padding(zero_blocks, n_blocks, d // 2))
    y = _moe_ffn(xs, block_expert, n_used, w_gate, w_up, w_down, layer)
    return _moe_combine(dest, y, gates.T, x2, sh_gate.astype(BF16), sh_up.astype(BF16), sh_down.astype(BF16), g, b)


def _compress_kernel(x_ref, pa_ref, pb_ref, wa_ref, wb_ref, w2_ref, out_ref):
    x = x_ref[0]
    nc = x.shape[0]
    ha = jnp.dot((x + pa_ref[...]).astype(BF16), wa_ref[...], preferred_element_type=F32)
    hb = jnp.dot((x + pb_ref[...]).astype(BF16), wb_ref[...], preferred_element_type=F32)
    h = ha + pltpu.roll(hb, nc - 1, 0)
    h = jax.nn.gelu(h, approximate=True)
    out_ref[0] = jnp.dot(h.astype(BF16), w2_ref[...], preferred_element_type=F32).astype(out_ref.dtype)


def _compress(kc, pos, w1, w2, batch, seq):
    g = C_KV_HEADS
    nch = seq // CMP_STRIDE
    half = CMP_LEN // 2
    x = kc.reshape(batch, nch, half * g * HEAD_DIM)
    eye = jnp.eye(g, dtype=F32)
    w1r = w1.reshape(CMP_LEN, HEAD_DIM, CMP_HIDDEN)
    expand = lambda wpart: jnp.einsum('jdh,ge->jgdeh', wpart, eye).reshape(half * g * HEAD_DIM, g * CMP_HIDDEN)
    wa, wb = expand(w1r[:half]).astype(BF16), expand(w1r[half:]).astype(BF16)
    w2e = jnp.einsum('hd,ge->ghed', jnp.pad(w2, ((0, 0), (0, LANES - HEAD_DIM))), eye)
    w2e = w2e.reshape(g * CMP_HIDDEN, g * LANES).astype(BF16)
    tile_pos = lambda p: jnp.broadcast_to(p[:, None, :], (half, g, HEAD_DIM)).reshape(1, half * g * HEAD_DIM)
    pa, pb = tile_pos(pos[:half]), tile_pos(pos[half:])
    full = lambda a: pl.BlockSpec(a.shape, lambda i: (0,) * a.ndim)
    return pl.pallas_call(
        _compress_kernel,
        grid=(batch,),
        in_specs=[pl.BlockSpec((1, nch, x.shape[2]), lambda i: (i, 0, 0)), full(pa), full(pb), full(wa), full(wb),
                  full(w2e)],
        out_specs=pl.BlockSpec((1, nch, g * LANES), lambda i: (i, 0, 0)),
        out_shape=jax.ShapeDtypeStruct((batch, nch, g * LANES), BF16),
        compiler_params=_cparams("parallel"),
        name="nsa_compress",
    )(x, pa, pb, wa, wb, w2e)


def _nsa_cmp_kernel(q_ref, kc_ref, vc_ref, ovt_ref, o_ref, sel_ref, *, rep, n_sel, n_real):
    tq = q_ref.shape[1]
    nc = kc_ref.shape[1]
    nsb = ovt_ref.shape[0]
    t0 = pl.program_id(2) * tq
    scale = HEAD_DIM ** -0.5
    kc = kc_ref[0]
    vc = vc_ref[0]
    tpos = t0 + lax.broadcasted_iota(jnp.int32, (tq, nc), 0)
    cend = lax.broadcasted_iota(jnp.int32, (tq, nc), 1) * CMP_STRIDE + (CMP_LEN - 1)
    cmask = cend <= tpos
    psum = jnp.zeros((tq, nc), F32)
    outs = []
    for r in range(rep):
        q = q_ref[0, :, r * LANES:(r + 1) * LANES]
        sc = lax.dot_general(q, kc, _NT, preferred_element_type=F32) * scale
        sc = jnp.where(cmask, sc, NEG)
        m = jnp.max(sc, axis=-1, keepdims=True)
        ex = jnp.where(cmask, jnp.exp(sc - m), 0.0)
        den = jnp.sum(ex, axis=-1, keepdims=True)
        pc = ex / jnp.where(den > 0, den, 1.0)
        outs.append(jnp.dot(pc.astype(BF16), vc, preferred_element_type=F32))
        psum = psum + pc
    o_ref[0] = jnp.concatenate(outs, axis=1)
    phi, plo = _split_bf16(psum)
    ovt = ovt_ref[...]
    imp = (lax.dot_general(ovt, phi, _NT, preferred_element_type=F32)
           + lax.dot_general(ovt, plo, _NT, preferred_element_type=F32))
    jblk = lax.broadcasted_iota(jnp.int32, (nsb, tq), 0)
    cur = jnp.right_shift(t0 + lax.broadcasted_iota(jnp.int32, (nsb, tq), 1), SLC_SHIFT)
    forced = (jblk == 0) | (jblk == cur) | (jblk == cur - 1)
    score = jnp.where(jblk > cur, -1.0, jnp.where(forced, SELECT_FORCE, imp))
    rank = jnp.zeros((nsb, tq), F32)
    for k in range(n_real):
        rowk = score[k:k + 1, :]
        ge = jnp.where(rowk >= score, 1.0, 0.0)
        gt = jnp.where(rowk > score, 1.0, 0.0)
        rank = rank + jnp.where(jblk > k, ge, gt)
    sel_ref[0, 0] = jnp.where(rank < n_sel, 1.0, 0.0)


def _nsa_cmp(q, kcmp, vcmp, batch, seq, q_off_blocks, tq=128):
    g = C_KV_HEADS
    rep = C_HEADS // g
    nc = kcmp.shape[1]
    nsb = seq // SLC_BLOCK
    n_sel = min(SLC_TOP_N, nsb)
    cs = np.arange(nc)[:, None] * CMP_STRIDE
    js = np.arange(nsb)[None, :] * SLC_BLOCK
    overlap = np.clip(np.minimum(cs + CMP_LEN, js + SLC_BLOCK) - np.maximum(cs, js), 0, None) / CMP_LEN
    overlap[(seq - CMP_LEN) // CMP_STRIDE + 1:] = 0.0
    nsb_pad = -(-nsb // LANES) * LANES
    ovt = jnp.asarray(np.pad(overlap.T, ((0, nsb_pad - nsb), (0, 0))), BF16)
    q3 = q.reshape(batch, seq, q.shape[1])
    n_real, nsb = nsb, nsb_pad
    o, sel = pl.pallas_call(
        functools.partial(_nsa_cmp_kernel, rep=rep, n_sel=n_sel, n_real=n_real),
        grid=(batch, g, seq // tq),
        in_specs=[pl.BlockSpec((1, tq, rep * LANES), lambda b, gi, i: (b, i, q_off_blocks // rep + gi)),
                  pl.BlockSpec((1, nc, LANES), lambda b, gi, i: (b, 0, gi)),
                  pl.BlockSpec((1, nc, LANES), lambda b, gi, i: (b, 0, gi)),
                  pl.BlockSpec(ovt.shape, lambda b, gi, i: (0, 0))],
        out_specs=[pl.BlockSpec((1, tq, rep * LANES), lambda b, gi, i: (b, i, gi)),
                   pl.BlockSpec((1, 1, nsb, tq), lambda b, gi, i: (b, gi, 0, i))],
        out_shape=[jax.ShapeDtypeStruct((batch, seq, C_HEADS * LANES), F32),
                   jax.ShapeDtypeStruct((batch, g, nsb, seq), F32)],
        compiler_params=_cparams("parallel", "parallel", "parallel"),
        name="nsa_compressed_select",
    )(q3, kcmp, vcmp, ovt)
    return o.reshape(batch * seq, C_HEADS * LANES), sel


def _nsa_slc_kernel(q_ref, k_ref, vt_ref, selt_ref, o_ref, bias_sc, *, rep, kt):
    tq = q_ref.shape[1]
    nsb = selt_ref.shape[2]
    t0 = pl.program_id(2) * tq
    n_kt = (t0 + tq + kt - 1) // kt
    selt = selt_ref[0, 0].astype(BF16)
    key_blk = jnp.right_shift(lax.broadcasted_iota(jnp.int32, (kt, nsb), 0), SLC_SHIFT)
    blk = lax.broadcasted_iota(jnp.int32, (kt, nsb), 1)
    kpos = lax.broadcasted_iota(jnp.int32, (kt, tq), 0)
    tpos = t0 + lax.broadcasted_iota(jnp.int32, (kt, tq), 1)

    def make_bias(j, c):
        k0 = pl.multiple_of(j * kt, kt)
        expand = jnp.where(key_blk == blk - jnp.right_shift(k0, SLC_SHIFT), 1.0, 0.0).astype(BF16)
        chosen = jnp.dot(expand, selt, preferred_element_type=F32)
        ok = (chosen > 0.5) & (k0 + kpos <= tpos)
        bias_sc[pl.ds(k0, kt), :] = jnp.where(ok, 0.0, NEG)
        return c

    lax.fori_loop(0, n_kt, make_bias, 0)

    outs = []
    pair = rep
    for r0 in range(0, rep, pair):
        qs = [q_ref[0, :, r * LANES:(r + 1) * LANES] * (HEAD_DIM ** -0.5)
              for r in range(r0, r0 + pair)]

        def tile(j, carry, qs=qs):
            k0 = pl.multiple_of(j * kt, kt)
            k = k_ref[0, pl.ds(k0, kt), :]
            vt = vt_ref[:, pl.ds(k0, kt)]
            bias = bias_sc[pl.ds(k0, kt), :]
            new = []
            for qr, (m, l, acc) in zip(qs, carry):
                s = lax.dot_general(k, qr, _NT, preferred_element_type=F32) + bias
                m_new = jnp.maximum(m, jnp.max(s, axis=0, keepdims=True))
                e = jnp.exp(s - m_new)
                corr = jnp.exp(m - m_new)
                l = l * corr + jnp.sum(e, axis=0, keepdims=True)
                acc = acc * corr + jnp.dot(vt, e.astype(BF16), preferred_element_type=F32)
                new.append((m_new, l, acc))
            return tuple(new)

        init = tuple((jnp.full((1, tq), NEG, F32), jnp.zeros((1, tq), F32), jnp.zeros((LANES, tq), F32))
                     for _ in range(pair))
        for _, l, acc in lax.fori_loop(0, n_kt, tile, init):
            outs.append((acc / l).T)
    o_ref[0] = jnp.concatenate(outs, axis=1)


def _nsa_slc(qkv, vt, selt, batch, seq, nblk, q_off, k_off, tq=128, kt=1024):
    g = C_KV_HEADS
    rep = C_HEADS // g
    kt = min(kt, seq)
    nsb = selt.shape[2]
    arr = qkv.reshape(batch, seq, nblk * LANES)
    out = pl.pallas_call(
        functools.partial(_nsa_slc_kernel, rep=rep, kt=kt),
        grid=(batch, g, seq // tq),
        in_specs=[pl.BlockSpec((1, tq, rep * LANES), lambda b, gi, i: (b, i, q_off // rep + gi)),
                  pl.BlockSpec((1, seq, LANES), lambda b, gi, i: (b, 0, k_off + gi)),
                  pl.BlockSpec((LANES, seq), lambda b, gi, i: (gi, b)),
                  pl.BlockSpec((1, 1, nsb, tq), lambda b, gi, i: (b, gi, 0, i))],
        out_specs=pl.BlockSpec((1, tq, rep * LANES), lambda b, gi, i: (b, i, gi)),
        out_shape=jax.ShapeDtypeStruct((batch, seq, C_HEADS * LANES), F32),
        scratch_shapes=[pltpu.VMEM((seq, tq), F32)],
        compiler_params=_cparams("parallel", "parallel", "arbitrary"),
        name="nsa_selected",
    )(arr, arr, vt, selt)
    return out.reshape(batch * seq, C_HEADS * LANES)


def _even_mixer_layer(x2, batch, seq, tabs, w_in, sinks, w_out, g, b):
    d = x2.shape[1]
    n_heads_in = 3 * A_HEADS + B_Q_HEADS + 2 * B_KV_HEADS
    w = _pad_heads_cols(w_in, n_heads_in).astype(BF16)
    rope = [1] * (2 * A_HEADS) + [0] * A_HEADS + [1] * B_Q_HEADS + [1] * B_KV_HEADS + [0] * B_KV_HEADS
    plan = [(0, c, rope[c]) for c in range(n_heads_in)]
    (qkv,) = _proj(x2, w, tabs, plan, [n_heads_in * LANES], [BF16])
    outs = []
    for window, dilation in A_PATTERNS:
        outs.append(_band_attention(qkv, batch=batch, seq=seq, dilation=dilation, nblk=n_heads_in, q_off=0,
                                    k_off=A_HEADS, v_off=2 * A_HEADS, n_q_heads=A_HEADS, rep=1,
                                    max_dist=window // dilation))
    qb_off = 3 * A_HEADS
    ob = _band_attention(qkv, batch=batch, seq=seq, dilation=1, nblk=n_heads_in, q_off=qb_off,
                         k_off=qb_off + B_Q_HEADS, v_off=qb_off + B_Q_HEADS + B_KV_HEADS, n_q_heads=B_Q_HEADS,
                         rep=B_Q_HEADS // B_KV_HEADS, max_dist=B_WINDOW - 1, sinks=sinks)
    w_o = _pad_heads_rows(w_out, A_HEADS + B_Q_HEADS).astype(BF16)
    return _outproj_even(outs[0], outs[1], outs[2], ob, x2, w_o, g.reshape(1, d), b.reshape(1, d))


def _odd_mixer_layer(x2, batch, seq, tabs, w_in, cmpk_pos, cmpk_w1, cmpk_w2, cmpv_pos, cmpv_w1, cmpv_w2, w_out, g, b):
    d = x2.shape[1]
    kvw = C_KV_HEADS * HEAD_DIM
    qw = C_HEADS * HEAD_DIM
    sizes = [qw] + [kvw] * 6 + [3 * C_HEADS]
    offs = np.concatenate([[0], np.cumsum(sizes)])
    wq, wkc, wvc, wks, wvs, wkw, wvw, wgt = [w_in[:, offs[i]:offs[i + 1]] for i in range(8)]
    ph = lambda wpart, n: _pad_heads_cols(wpart, n)
    w = jnp.concatenate([ph(wq, C_HEADS), ph(wks, C_KV_HEADS), ph(wkw, C_KV_HEADS), ph(wvw, C_KV_HEADS),
                         wkc, wvc, jnp.pad(wgt, ((0, 0), (0, LANES - 3 * C_HEADS)))], axis=1).astype(BF16)
    wvs_t = ph(wvs, C_KV_HEADS).T.astype(BF16)
    n16 = C_HEADS + 3 * C_KV_HEADS
    rope16 = [1] * C_HEADS + [1] * C_KV_HEADS + [1] * C_KV_HEADS + [0] * C_KV_HEADS
    n_kc = kvw // LANES
    plan = ([(0, c, rope16[c]) for c in range(n16)] + [(1, c, 2) for c in range(n_kc)]
            + [(2, c, 0) for c in range(n_kc)] + [(3, 0, 0)])
    qkv, kc, vc, gate, vs_t = _proj(x2, w, tabs, plan, [n16 * LANES, kvw, kvw, LANES], [BF16, F32, F32, F32],
                                    wt=wvs_t)
    kcmp = _compress(kc, cmpk_pos, cmpk_w1, cmpk_w2, batch, seq)
    vcmp = _compress(vc, cmpv_pos, cmpv_w1, cmpv_w2, batch, seq)
    o_cmp, selt = _nsa_cmp(qkv, kcmp, vcmp, batch, seq, 0)
    ks_off = C_HEADS
    o_slc = _nsa_slc(qkv, vs_t, selt, batch, seq, n16, 0, ks_off)
    kw_off = ks_off + C_KV_HEADS
    o_win = _band_attention(qkv, batch=batch, seq=seq, dilation=1, nblk=n16, q_off=0, k_off=kw_off,
                            v_off=kw_off + C_KV_HEADS, n_q_heads=C_HEADS, rep=C_HEADS // C_KV_HEADS,
                            max_dist=NSA_WINDOW - 1)
    w_o = _pad_heads_rows(w_out, C_HEADS).astype(BF16)
    return _outproj_odd(o_cmp, o_slc, o_win, gate, x2, w_o, g.reshape(1, d), b.reshape(1, d))


def kernel(x, positions, even_w_in, even_sinks, even_w_out, odd_w_in, odd_cmpk_pos, odd_cmpk_w1, odd_cmpk_w2, odd_cmpv_pos, odd_cmpv_w1, odd_cmpv_w2, odd_w_out, mix_ln_g, mix_ln_b, moe_router_w, moe_router_b, moe_w_gate, moe_w_up, moe_w_down, moe_sh_gate, moe_sh_up, moe_sh_down, ffn_ln_g, ffn_ln_b):
    batch, seq, d = x.shape
    x2 = x.reshape(batch * seq, d)
    tabs = _rope_tables(positions)
    depth = mix_ln_g.shape[0]
    for layer in range(depth):
        j = layer // 2
        if layer % 2 == 0:
            x2 = _even_mixer_layer(x2, batch, seq, tabs, even_w_in[j], even_sinks[j], even_w_out[j],
                                   mix_ln_g[layer], mix_ln_b[layer])
        else:
            x2 = _odd_mixer_layer(x2, batch, seq, tabs, odd_w_in[j], odd_cmpk_pos[j], odd_cmpk_w1[j], odd_cmpk_w2[j],
                                  odd_cmpv_pos[j], odd_cmpv_w1[j], odd_cmpv_w2[j], odd_w_out[j],
                                  mix_ln_g[layer], mix_ln_b[layer])
        x2 = _moe_layer(x2, moe_router_w[layer], moe_router_b[layer], moe_w_gate, moe_w_up, moe_w_down, layer,
                        moe_sh_gate[layer], moe_sh_up[layer], moe_sh_down[layer],
                        ffn_ln_g[layer].reshape(1, d), ffn_ln_b[layer].reshape(1, d))
    return x2.reshape(batch, seq, d)
```

```python
import functools

import numpy as np
import jax
import jax.numpy as jnp
from jax import lax
from jax.experimental import pallas as pl
from jax.experimental.pallas import tpu as pltpu

F32 = jnp.float32
BF16 = jnp.bfloat16

LANES = 128
HEAD_DIM = 64
ROT_DIM = HEAD_DIM // 4
ROT_HALF = ROT_DIM // 2
ROPE_THETA = 500000.0
QBLK = 128
A_HEADS = 8
A_PATTERNS = ((128, 1), (512, 4), (2048, 16))
B_Q_HEADS = 8
B_KV_HEADS = 2
B_WINDOW = 128
C_HEADS = 16
C_KV_HEADS = 4
CMP_LEN = 32
CMP_STRIDE = 16
CMP_HIDDEN = 2 * HEAD_DIM
SLC_BLOCK = 64
SLC_SHIFT = 6
SLC_TOP_N = 16
NSA_WINDOW = 512
SELECT_FORCE = 1.0e4
N_EXPERTS = 256
TOP_K = 8
N_GROUPS = 8
TOPK_GROUPS = 4
ROUTED_SCALE = 2.5
MOE_BLOCK = 256
DEPTH = 2
DEEPNORM_ALPHA = (2 * DEPTH) ** 0.25
LN_EPS = 1e-5
NEG = -1.0e30
VMEM_LIMIT = 56 * 1024 * 1024

_NT = (((1,), (1,)), ((), ()))


def _cparams(*sem):
    return pltpu.CompilerParams(dimension_semantics=sem, vmem_limit_bytes=VMEM_LIMIT)


def _split_bf16(a):
    hi = a.astype(BF16)
    lo = (a - hi.astype(F32)).astype(BF16)
    return hi, lo


def _layer_norm(y, g, b):
    mu = jnp.mean(y, axis=-1, keepdims=True)
    d = y - mu
    var = jnp.mean(d * d, axis=-1, keepdims=True)
    return d * lax.rsqrt(var + LN_EPS) * g + b


def _silu(a):
    return a * jax.nn.sigmoid(a)


def _proj_kernel(*refs, plan, transposed):
    if transposed:
        x_ref, w_ref, tab_ref, wt_ref = refs[:4]
        out_refs = refs[4:]
    else:
        x_ref, w_ref, tab_ref = refs[:3]
        out_refs = refs[3:]
    x = x_ref[...].astype(BF16)
    if transposed:
        out_t = out_refs[-1]
        out_t[...] = lax.dot_general(wt_ref[...], x, _NT, preferred_element_type=F32).astype(out_t.dtype)
    nblk = len(plan)
    for c0 in range(0, nblk, 2):
        nb = min(2, nblk - c0)
        acc = jnp.dot(x, w_ref[:, c0 * LANES:(c0 + nb) * LANES], preferred_element_type=F32)
        for j in range(nb):
            blk = acc[:, j * LANES:(j + 1) * LANES]
            dst, dblk, mode = plan[c0 + j]
            if mode:
                off = (mode - 1) * 3 * LANES
                cos = tab_ref[:, off:off + LANES]
                s_lo = tab_ref[:, off + LANES:off + 2 * LANES]
                s_hi = tab_ref[:, off + 2 * LANES:off + 3 * LANES]
                blk = (blk * cos + pltpu.roll(blk, LANES - ROT_HALF, 1) * s_lo
                       + pltpu.roll(blk, ROT_HALF, 1) * s_hi)
            o_ref = out_refs[dst]
            o_ref[:, dblk * LANES:(dblk + 1) * LANES] = blk.astype(o_ref.dtype)


def _proj(x2, w, tabs, plan, out_cols, out_dtypes, wt=None, tm=256):
    t, d = x2.shape
    ncol = w.shape[1]
    out_shape = [jax.ShapeDtypeStruct((t, c), dt) for c, dt in zip(out_cols, out_dtypes)]
    in_specs = [pl.BlockSpec((tm, d), lambda i: (i, 0)),
                pl.BlockSpec((d, ncol), lambda i: (0, 0)),
                pl.BlockSpec((tm, tabs.shape[1]), lambda i: (i, 0))]
    out_specs = [pl.BlockSpec((tm, c), lambda i: (i, 0)) for c in out_cols]
    args = [x2, w, tabs]
    if wt is not None:
        in_specs.append(pl.BlockSpec(wt.shape, lambda i: (0, 0)))
        out_specs.append(pl.BlockSpec((wt.shape[0], tm), lambda i: (0, i)))
        out_shape.append(jax.ShapeDtypeStruct((wt.shape[0], t), BF16))
        args.append(wt)
    return pl.pallas_call(
        functools.partial(_proj_kernel, plan=tuple(plan), transposed=wt is not None),
        grid=(t // tm,),
        in_specs=in_specs,
        out_specs=out_specs,
        out_shape=out_shape,
        compiler_params=_cparams("parallel"),
        name="proj_rope",
    )(*args)


def _rope_tables(positions):
    t = positions.size
    inv_freq = jnp.asarray(ROPE_THETA ** (-np.arange(0, ROT_DIM, 2) / ROT_DIM), F32)
    ang = positions.astype(F32).reshape(t, 1) * inv_freq
    trig = jnp.concatenate([jnp.cos(ang), jnp.sin(ang)], axis=1)
    place = np.zeros((2 * ROT_HALF, 6 * LANES), np.float32)
    const = np.zeros((6 * LANES,), np.float32)
    for base, heads in ((0, (0,)), (3 * LANES, (0, HEAD_DIM))):
        const[base:base + LANES] = 1.0
        for h in heads:
            for i in range(ROT_HALF):
                place[i, base + h + i] = 1.0
                place[i, base + h + ROT_HALF + i] = 1.0
                const[base + h + i] = const[base + h + ROT_HALF + i] = 0.0
                place[ROT_HALF + i, base + LANES + h + i] = -1.0
                place[ROT_HALF + i, base + 2 * LANES + h + ROT_HALF + i] = 1.0
    return jnp.dot(trig, jnp.asarray(place), precision=lax.Precision.HIGHEST) + jnp.asarray(const)


def _pad_heads_cols(w, n_heads):
    d = w.shape[0]
    w = w.reshape(d, n_heads, HEAD_DIM)
    return jnp.pad(w, ((0, 0), (0, 0), (0, LANES - HEAD_DIM))).reshape(d, n_heads * LANES)


def _pad_heads_rows(w, n_heads):
    d = w.shape[1]
    w = w.reshape(n_heads, HEAD_DIM, d)
    return jnp.pad(w, ((0, 0), (0, LANES - HEAD_DIM), (0, 0))).reshape(n_heads * LANES, d)


def _band_kernel(*refs, back, max_dist, length, qrows, heads, shared_kv, in_flight, has_sink):
    if has_sink:
        sink_ref, q_ref, k_ref, v_ref, o_ref, bias_sc = refs
    else:
        q_ref, k_ref, v_ref, o_ref, bias_sc = refs
    nq = length // qrows
    window = qrows + back
    lane = lax.broadcasted_iota(jnp.int32, (qrows, LANES), 1)
    first_head = pl.program_id(2) * heads

    def band_bias(offset):
        row = lax.broadcasted_iota(jnp.int32, (qrows, window), 0)
        col = lax.broadcasted_iota(jnp.int32, (qrows, window), 1)
        dist = offset + row - col
        return jnp.where((dist >= 0) & (dist <= max_dist), 0.0, NEG)

    bias_sc[...] = band_bias(back)

    def block(h, q0, k0, bias):
        hs = slice(h * LANES, (h + 1) * LANES)
        kvs = slice(0, LANES) if shared_kv else hs
        q = q_ref[0, pl.ds(q0, qrows), hs] * (HEAD_DIM ** -0.5)
        k = k_ref[0, pl.ds(k0, window), kvs]
        v = v_ref[0, pl.ds(k0, window), kvs]
        s = lax.dot_general(q, k, _NT, preferred_element_type=F32) + bias
        m = jnp.max(s, axis=-1, keepdims=True)
        e = jnp.exp(s - m)
        den = jnp.sum(e, axis=-1, keepdims=True)
        if has_sink:
            den = den + jnp.exp(sink_ref[first_head + h] - m)
        o = jnp.dot(e.astype(BF16), v, preferred_element_type=F32) / den
        lse = m + jnp.log(den)
        o_ref[0, pl.ds(q0, qrows), hs] = jnp.where(lane < HEAD_DIM, o, lse)

    n_clipped = min(-(-back // qrows), nq)
    for qi in range(n_clipped):
        bias = band_bias(qi * qrows)
        for h in range(heads):
            block(h, qi * qrows, 0, bias)

    steady = nq - n_clipped
    per_iter = max(1, min(in_flight // heads, steady))
    n_iter = steady // per_iter

    def body(it, carry):
        aligned = lambda v: v if isinstance(v, int) else pl.multiple_of(v, QBLK)
        for j in range(per_iter):
            q0 = aligned((n_clipped + it * per_iter + j) * qrows)
            for h in range(heads):
                block(h, q0, aligned(q0 - back), bias_sc[...])
        return carry

    if n_iter == 1:
        body(0, 0)
    elif n_iter:
        lax.fori_loop(0, n_iter, body, 0)
    for qi in range(n_clipped + n_iter * per_iter, nq):
        for h in range(heads):
            block(h, qi * qrows, qi * qrows - back, bias_sc[...])


def _band_attention(qkv, *, batch, seq, dilation, nblk, q_off, k_off, v_off, n_q_heads, rep, max_dist, sinks=None,
                    heads=1, in_flight=16):
    length = seq // dilation
    back = min(-(-max_dist // QBLK) * QBLK, length - QBLK)
    qrows = QBLK
    shared_kv = rep > 1
    assert n_q_heads % heads == 0 and (rep == 1 or rep % heads == 0)
    kv_heads = 1 if shared_kv else heads
    arr = qkv.reshape(batch, length, dilation * nblk * LANES)
    assert q_off % heads == 0 and k_off % kv_heads == 0 and v_off % kv_heads == 0 and nblk % heads == 0
    qspec = pl.BlockSpec((1, length, heads * LANES),
                         lambda b, r, h: (b, 0, (r * nblk + q_off) // heads + h))
    kvspec = lambda off: pl.BlockSpec(
        (1, length, kv_heads * LANES),
        lambda b, r, h: (b, 0, (r * nblk + off) // kv_heads + (h * heads // rep if shared_kv else h)))
    in_specs = [qspec, kvspec(k_off), kvspec(v_off)]
    args = [arr, arr, arr]
    if sinks is not None:
        in_specs = [pl.BlockSpec(memory_space=pltpu.SMEM)] + in_specs
        args = [sinks.reshape(-1).astype(F32)] + args
    out = pl.pallas_call(
        functools.partial(_band_kernel, back=back, max_dist=max_dist, length=length, qrows=qrows, heads=heads,
                          shared_kv=shared_kv, in_flight=in_flight, has_sink=sinks is not None),
        grid=(batch, dilation, n_q_heads // heads),
        in_specs=in_specs,
        out_specs=pl.BlockSpec((1, length, heads * LANES), lambda b, r, h: (b, 0, r * (n_q_heads // heads) + h)),
        out_shape=jax.ShapeDtypeStruct((batch, length, dilation * n_q_heads * LANES), F32),
        scratch_shapes=[pltpu.VMEM((qrows, qrows + back), F32)],
        compiler_params=_cparams("parallel", "parallel", "parallel"),
        name="band_attention",
    )(*args)
    return out.reshape(batch * seq, n_q_heads * LANES)


def _outproj_even_kernel(o1_ref, o2_ref, o3_ref, ob_ref, x_ref, w_ref, g_ref, b_ref, out_ref, *, n_a, n_b):
    tm = x_ref.shape[0]
    lane = lax.broadcasted_iota(jnp.int32, (tm, LANES), 1)
    real = lane < HEAD_DIM
    parts = []
    for h in range(n_a):
        sl = slice(h * LANES, (h + 1) * LANES)
        outs = [r[:, sl] for r in (o1_ref, o2_ref, o3_ref)]
        lses = [jnp.where(real, pltpu.roll(a, HEAD_DIM, 1), a) for a in outs]
        m = jnp.maximum(jnp.maximum(lses[0], lses[1]), lses[2])
        es = [jnp.exp(l - m) for l in lses]
        num = es[0] * outs[0] + es[1] * outs[1] + es[2] * outs[2]
        den = es[0] + es[1] + es[2]
        parts.append(jnp.where(real, num / den, 0.0).astype(BF16))
    for h in range(n_b):
        parts.append(jnp.where(real, ob_ref[:, h * LANES:(h + 1) * LANES], 0.0).astype(BF16))
    a = jnp.concatenate(parts, axis=1)
    mixed = jnp.dot(a, w_ref[...], preferred_element_type=F32)
    y = DEEPNORM_ALPHA * x_ref[...] + mixed
    out_ref[...] = _layer_norm(y, g_ref[...], b_ref[...])


def _outproj_even(o1, o2, o3, ob, x2, w, g, b, tm=256):
    t, d = x2.shape
    n_a, n_b = o1.shape[1] // LANES, ob.shape[1] // LANES
    row = lambda c: pl.BlockSpec((tm, c), lambda i: (i, 0))
    full = lambda a: pl.BlockSpec(a.shape, lambda i: (0,) * a.ndim)
    return pl.pallas_call(
        functools.partial(_outproj_even_kernel, n_a=n_a, n_b=n_b),
        grid=(t // tm,),
        in_specs=[row(o1.shape[1]), row(o2.shape[1]), row(o3.shape[1]), row(ob.shape[1]), row(d),
                  full(w), full(g), full(b)],
        out_specs=row(d),
        out_shape=jax.ShapeDtypeStruct((t, d), F32),
        compiler_params=_cparams("parallel"),
        name="outproj_even_ln",
    )(o1, o2, o3, ob, x2, w, g, b)


def _outproj_odd_kernel(oc_ref, os_ref, ow_ref, gate_ref, e_ref, x_ref, w_ref, g_ref, b_ref, out_ref):
    gate = jax.nn.sigmoid(gate_ref[...])
    ghi, glo = _split_bf16(gate)
    acc = None
    for j, o_ref in enumerate((oc_ref, os_ref, ow_ref)):
        ej = e_ref[j]
        gfull = jnp.dot(ghi, ej, preferred_element_type=F32) + jnp.dot(glo, ej, preferred_element_type=F32)
        term = gfull * o_ref[...]
        acc = term if acc is None else acc + term
    mixed = jnp.dot(acc.astype(BF16), w_ref[...], preferred_element_type=F32)
    y = DEEPNORM_ALPHA * x_ref[...] + mixed
    out_ref[...] = _layer_norm(y, g_ref[...], b_ref[...])


def _gate_expanders(n_heads):
    e = np.zeros((3, LANES, n_heads * LANES), np.float32)
    for j in range(3):
        for h in range(n_heads):
            e[j, 3 * h + j, h * LANES:h * LANES + HEAD_DIM] = 1.0
    return jnp.asarray(e, BF16)


def _outproj_odd(oc, osl, ow, gate, x2, w, g, b, tm=256):
    t, d = x2.shape
    n_heads = oc.shape[1] // LANES
    e = _gate_expanders(n_heads)
    row = lambda c: pl.BlockSpec((tm, c), lambda i: (i, 0))
    full = lambda a: pl.BlockSpec(a.shape, lambda i: (0,) * a.ndim)
    return pl.pallas_call(
        _outproj_odd_kernel,
        grid=(t // tm,),
        in_specs=[row(oc.shape[1]), row(osl.shape[1]), row(ow.shape[1]), row(LANES), full(e), row(d),
                  full(w), full(g), full(b)],
        out_specs=row(d),
        out_shape=jax.ShapeDtypeStruct((t, d), F32),
        compiler_params=_cparams("parallel"),
        name="outproj_odd_ln",
    )(oc, osl, ow, gate, e, x2, w, g, b)


def _router_kernel(x_ref, whi_ref, wlo_ref, bias_ref, eidx_ref, gate_ref, rank_ref, cnt_ref):
    n_exp = whi_ref.shape[0]
    tm = x_ref.shape[0]
    per_group = n_exp // N_GROUPS
    xhi, xlo = _split_bf16(x_ref[...])
    whi, wlo = whi_ref[...], wlo_ref[...]
    dg = lambda a, b: lax.dot_general(a, b, _NT, preferred_element_type=F32)
    logits = dg(whi, xhi) + dg(whi, xlo) + dg(wlo, xhi)
    aff = jax.nn.sigmoid(logits)
    biased = aff + bias_ref[...]
    gio = lax.broadcasted_iota(jnp.int32, (per_group, tm), 0).astype(F32)
    blocks, scores = [], []
    for g in range(N_GROUPS):
        blk = biased[g * per_group:(g + 1) * per_group, :]
        m1 = jnp.max(blk, axis=0, keepdims=True)
        first = jnp.min(jnp.where(blk == m1, gio, float(per_group)), axis=0, keepdims=True)
        m2 = jnp.max(jnp.where(gio == first, -jnp.inf, blk), axis=0, keepdims=True)
        blocks.append(blk)
        scores.append(m1 + m2)
    masked = []
    for g in range(N_GROUPS):
        rank = jnp.zeros((1, tm), F32)
        for o in range(N_GROUPS):
            if o == g:
                continue
            beats = scores[o] >= scores[g] if o < g else scores[o] > scores[g]
            rank = rank + jnp.where(beats, 1.0, 0.0)
        masked.append(jnp.where(rank < TOPK_GROUPS, blocks[g], -jnp.inf))
    cur = jnp.concatenate(masked, axis=0)
    eio = lax.broadcasted_iota(jnp.int32, (n_exp, tm), 0).astype(F32)
    ids, gs = [], []
    for _ in range(TOP_K):
        m = jnp.max(cur, axis=0, keepdims=True)
        idx = jnp.min(jnp.where(cur == m, eio, float(n_exp)), axis=0, keepdims=True)
        hit = eio == idx
        gs.append(jnp.sum(jnp.where(hit, aff, 0.0), axis=0, keepdims=True))
        ids.append(idx)
        cur = jnp.where(hit, -jnp.inf, cur)
    gates = jnp.concatenate(gs, axis=0)
    gates = gates / jnp.sum(gates, axis=0, keepdims=True) * ROUTED_SCALE
    eidx_ref[...] = jnp.concatenate(ids, axis=0).astype(jnp.int32)
    gate_ref[...] = gates
    @pl.when(pl.program_id(0) == 0)
    def _():
        cnt_ref[...] = jnp.zeros(cnt_ref.shape, F32)

    onehot = jnp.zeros((n_exp, tm), F32)
    for idx in ids:
        onehot = onehot + jnp.where(eio == idx, 1.0, 0.0)
    earlier = jnp.where(lax.broadcasted_iota(jnp.int32, (tm, tm), 0) < lax.broadcasted_iota(jnp.int32, (tm, tm), 1),
                        1.0, 0.0).astype(BF16)
    before = cnt_ref[...] + jnp.dot(onehot.astype(BF16), earlier, preferred_element_type=F32)
    ranks = [jnp.sum(jnp.where(eio == idx, before, 0.0), axis=0, keepdims=True) for idx in ids]
    rank_ref[...] = jnp.concatenate(ranks, axis=0).astype(jnp.int32)
    cnt_ref[...] = cnt_ref[...] + jnp.sum(onehot, axis=1, keepdims=True)


def _router(x2, router_w, router_b, tm=256):
    t, d = x2.shape
    n_exp = router_w.shape[1]
    whi, wlo = _split_bf16(router_w.T)
    bias = router_b.reshape(n_exp, 1).astype(F32)
    full = lambda a: pl.BlockSpec(a.shape, lambda i: (0,) * a.ndim)
    per_tok = pl.BlockSpec((TOP_K, tm), lambda i: (0, i))
    return pl.pallas_call(
        _router_kernel,
        grid=(t // tm,),
        in_specs=[pl.BlockSpec((tm, d), lambda i: (i, 0)), full(whi), full(wlo), full(bias)],
        out_specs=[per_tok, per_tok, per_tok, pl.BlockSpec((n_exp, 1), lambda i: (0, 0))],
        out_shape=[jax.ShapeDtypeStruct((TOP_K, t), jnp.int32), jax.ShapeDtypeStruct((TOP_K, t), F32),
                   jax.ShapeDtypeStruct((TOP_K, t), jnp.int32), jax.ShapeDtypeStruct((n_exp, 1), F32)],
        compiler_params=_cparams("arbitrary"),
        name="moe_router",
    )(x2, whi, wlo, bias)


def _moe_dest_kernel(eidx_ref, rank_ref, start_ref, dest_ref):
    n_exp = start_ref.shape[0]
    tm = eidx_ref.shape[1]
    eio = lax.broadcasted_iota(jnp.int32, (n_exp, tm), 0)
    start = start_ref[...]
    rows = []
    for k in range(TOP_K):
        seg = jnp.sum(jnp.where(eio == eidx_ref[k:k + 1, :], start, 0.0), axis=0, keepdims=True)
        rows.append(seg.astype(jnp.int32) + rank_ref[k:k + 1, :])
    dest_ref[...] = jnp.concatenate(rows, axis=0)


def _moe_dest(eidx, rank, seg_start, tm=256):
    t = eidx.shape[1]
    per_tok = pl.BlockSpec((TOP_K, tm), lambda i: (0, i))
    return pl.pallas_call(
        _moe_dest_kernel,
        grid=(t // tm,),
        in_specs=[per_tok, per_tok, pl.BlockSpec(seg_start.shape, lambda i: (0, 0))],
        out_specs=per_tok,
        out_shape=jax.ShapeDtypeStruct((TOP_K, t), jnp.int32),
        compiler_params=_cparams("parallel"),
        name="moe_dest",
    )(eidx, rank, seg_start)


def _moe_zero_kernel(blk_ref, xs_ref):
    xs_ref[...] = jnp.zeros(xs_ref.shape, xs_ref.dtype)


def _moe_zero_padding(zero_blocks, n_blocks, width):
    grid_spec = pltpu.PrefetchScalarGridSpec(
        num_scalar_prefetch=1,
        grid=(zero_blocks.shape[0],),
        in_specs=[],
        out_specs=pl.BlockSpec((MOE_BLOCK, width), lambda i, blk: (blk[i], 0)),
    )
    return pl.pallas_call(
        _moe_zero_kernel,
        grid_spec=grid_spec,
        out_shape=jax.ShapeDtypeStruct((n_blocks * MOE_BLOCK, width), jnp.uint32),
        compiler_params=_cparams("arbitrary"),
        name="moe_zero_padding",
    )(zero_blocks)


def _pack_bf16_pairs(x):
    half = x.shape[1] // 2
    lo = lax.bitcast_convert_type(x[:, :half].astype(BF16).astype(F32), jnp.uint32)
    hi = lax.bitcast_convert_type(x[:, half:].astype(BF16).astype(F32), jnp.uint32)
    return jnp.right_shift(lo, jnp.uint32(16)) | (hi & jnp.uint32(0xFFFF0000))


def _unpack_bf16_pairs(w):
    lo = lax.bitcast_convert_type(jnp.left_shift(w, jnp.uint32(16)), F32).astype(BF16)
    hi = lax.bitcast_convert_type(w & jnp.uint32(0xFFFF0000), F32).astype(BF16)
    return jnp.concatenate([lo, hi], axis=1)


def _moe_dispatch_kernel(dest_ref, x_ref, xs_in, xs_out, buf, sem):
    del xs_in
    tm = x_ref.shape[0]
    buf[...] = _pack_bf16_pairs(x_ref[...])

    def issue(i, c):
        for k in range(TOP_K):
            r = dest_ref[k, i]
            pltpu.make_async_copy(buf.at[pl.ds(i, 1)], xs_out.at[pl.ds(r, 1)], sem).start()
        return c

    lax.fori_loop(0, tm, issue, 0)
    for _ in range(TOP_K):
        pltpu.make_async_copy(buf, xs_out.at[pl.ds(0, tm)], sem).wait()


def _moe_dispatch(dest, x2, xs, tm=256):
    t, d = x2.shape
    return pl.pallas_call(
        _moe_dispatch_kernel,
        grid=(t // tm,),
        in_specs=[pl.BlockSpec((TOP_K, tm), lambda i: (0, i), memory_space=pltpu.SMEM),
                  pl.BlockSpec((tm, d), lambda i: (i, 0)),
                  pl.BlockSpec(memory_space=pl.ANY)],
        out_specs=pl.BlockSpec(memory_space=pl.ANY),
        out_shape=jax.ShapeDtypeStruct(xs.shape, xs.dtype),
        scratch_shapes=[pltpu.VMEM((tm, d // 2), jnp.uint32), pltpu.SemaphoreType.DMA(())],
        input_output_aliases={2: 0},
        compiler_params=_cparams("arbitrary"),
        name="moe_dispatch",
    )(dest, x2, xs)


def _moe_ffn_kernel(be_ref, nu_ref, xs_ref, wg_ref, wu_ref, wd_ref, y_ref, wg_sc, wu_sc, wd_sc):
    b = pl.program_id(0)

    @pl.when(b < nu_ref[0])
    def _():
        @pl.when((b == 0) | (be_ref[b] != be_ref[jnp.maximum(b - 1, 0)]))
        def _():
            wg_sc[...] = wg_ref[0].astype(BF16)
            wu_sc[...] = wu_ref[0].astype(BF16)
            wd_sc[...] = wd_ref[0].astype(BF16)

        xb = _unpack_bf16_pairs(xs_ref[...])
        gp = jnp.dot(xb, wg_sc[...], preferred_element_type=F32)
        up = jnp.dot(xb, wu_sc[...], preferred_element_type=F32)
        h = (_silu(gp) * up).astype(BF16)
        y_ref[...] = jnp.dot(h, wd_sc[...], preferred_element_type=F32)

    @pl.when(b >= nu_ref[0])
    def _():
        y_ref[...] = jnp.zeros(y_ref.shape, y_ref.dtype)


def _moe_ffn(xs, block_expert, n_used, w_gate, w_up, w_down, layer):
    n_blocks = block_expert.shape[0]
    d, ff = w_gate.shape[2], w_gate.shape[3]
    last = lambda b, nu: jnp.minimum(b, nu[0] - 1)
    grid_spec = pltpu.PrefetchScalarGridSpec(
        num_scalar_prefetch=2,
        grid=(n_blocks,),
        in_specs=[
            pl.BlockSpec((MOE_BLOCK, d // 2), lambda b, be, nu: (last(b, nu), 0)),
            pl.BlockSpec((None, 1, d, ff), lambda b, be, nu: (layer, be[last(b, nu)], 0, 0)),
            pl.BlockSpec((None, 1, d, ff), lambda b, be, nu: (layer, be[last(b, nu)], 0, 0)),
            pl.BlockSpec((None, 1, ff, d), lambda b, be, nu: (layer, be[last(b, nu)], 0, 0)),
        ],
        out_specs=pl.BlockSpec((MOE_BLOCK, d), lambda b, be, nu: (b, 0)),
        scratch_shapes=[pltpu.VMEM((d, ff), BF16), pltpu.VMEM((d, ff), BF16), pltpu.VMEM((ff, d), BF16)],
    )
    return pl.pallas_call(
        _moe_ffn_kernel,
        grid_spec=grid_spec,
        out_shape=jax.ShapeDtypeStruct((n_blocks * MOE_BLOCK, d), F32),
        compiler_params=_cparams("arbitrary"),
        name="moe_expert_ffn",
    )(block_expert, n_used, xs, w_gate, w_up, w_down)


def _moe_combine_kernel(dest_ref, y_hbm, gate_ref, x_ref, sg_ref, su_ref, sd_ref, g_ref, b_ref, out_ref, buf, sem):
    tm = x_ref.shape[0]

    def issue(i, c):
        for k in range(TOP_K):
            r = dest_ref[k, i]
            pltpu.make_async_copy(y_hbm.at[pl.ds(r, 1)], buf.at[k, pl.ds(i, 1)], sem).start()
        return c

    lax.fori_loop(0, tm, issue, 0)
    x = x_ref[...]
    xb = x.astype(BF16)
    hs = _silu(jnp.dot(xb, sg_ref[...], preferred_element_type=F32)) * jnp.dot(xb, su_ref[...], preferred_element_type=F32)
    shared = jnp.dot(hs.astype(BF16), sd_ref[...], preferred_element_type=F32)
    for k in range(TOP_K):
        pltpu.make_async_copy(y_hbm.at[pl.ds(0, tm)], buf.at[k], sem).wait()
    gates = gate_ref[...]
    routed = buf[0] * gates[:, 0:1]
    for k in range(1, TOP_K):
        routed = routed + buf[k] * gates[:, k:k + 1]
    y = DEEPNORM_ALPHA * x + (routed + shared)
    out_ref[...] = _layer_norm(y, g_ref[...], b_ref[...])


def _moe_combine(dest, y, gates_t, x2, sh_gate, sh_up, sh_down, g, b, tm=128):
    t, d = x2.shape
    row = lambda c: pl.BlockSpec((tm, c), lambda i: (i, 0))
    full = lambda a: pl.BlockSpec(a.shape, lambda i: (0,) * a.ndim)
    return pl.pallas_call(
        _moe_combine_kernel,
        grid=(t // tm,),
        in_specs=[pl.BlockSpec((TOP_K, tm), lambda i: (0, i), memory_space=pltpu.SMEM),
                  pl.BlockSpec(memory_space=pl.ANY),
                  row(TOP_K), row(d), full(sh_gate), full(sh_up), full(sh_down), full(g), full(b)],
        out_specs=row(d),
        out_shape=jax.ShapeDtypeStruct((t, d), F32),
        scratch_shapes=[pltpu.VMEM((TOP_K, tm, d), F32), pltpu.SemaphoreType.DMA(())],
        compiler_params=_cparams("arbitrary"),
        name="moe_combine_ln",
    )(dest, y, gates_t, x2, sh_gate, sh_up, sh_down, g, b)


def _moe_segments(counts, n_tok):
    n_exp = counts.shape[0]
    n_blocks = -(-n_tok * TOP_K // MOE_BLOCK) + n_exp
    nblk = (counts.reshape(n_exp).astype(jnp.int32) + MOE_BLOCK - 1) // MOE_BLOCK
    blk_end = jnp.cumsum(nblk)
    seg_start = ((blk_end - nblk) * MOE_BLOCK).astype(F32).reshape(n_exp, 1)
    block_expert = jnp.sum((blk_end[None, :] <= jnp.arange(n_blocks)[:, None]).astype(jnp.int32), axis=1)
    block_expert = jnp.minimum(block_expert, n_exp - 1)
    n_used = blk_end[-1]
    tail = jnp.minimum(n_used + jnp.arange(n_exp), n_blocks - 1)
    zero_blocks = jnp.concatenate([jnp.maximum(blk_end - 1, 0), tail]).astype(jnp.int32)
    return seg_start, block_expert.astype(jnp.int32), n_used.astype(jnp.int32).reshape(1), zero_blocks, n_blocks


def _moe_layer(x2, router_w, router_b, w_gate, w_up, w_down, layer, sh_gate, sh_up, sh_down, g, b):
    t, d = x2.shape
    eidx, gates, rank, counts = _router(x2, router_w, router_b)
    seg_start, block_expert, n_used, zero_blocks, n_blocks = _moe_segments(counts, t)
    dest = _moe_dest(eidx, rank, seg_start)
    xs = _moe_dispatch(dest, x2, _moe_zero_padding(zero_blocks, n_blocks, d // 2))
    y = _moe_ffn(xs, block_expert, n_used, w_gate, w_up, w_down, layer)
    return _moe_combine(dest, y, gates.T, x2, sh_gate.astype(BF16), sh_up.astype(BF16), sh_down.astype(BF16), g, b)


def _compress_kernel(x_ref, pa_ref, pb_ref, wa_ref, wb_ref, w2_ref, out_ref):
    x = x_ref[0]
    nc = x.shape[0]
    ha = jnp.dot((x + pa_ref[...]).astype(BF16), wa_ref[...], preferred_element_type=F32)
    hb = jnp.dot((x + pb_ref[...]).astype(BF16), wb_ref[...], preferred_element_type=F32)
    h = ha + pltpu.roll(hb, nc - 1, 0)
    h = jax.nn.gelu(h, approximate=True)
    out_ref[0] = jnp.dot(h.astype(BF16), w2_ref[...], preferred_element_type=F32).astype(out_ref.dtype)


def _compress(kc, pos, w1, w2, batch, seq):
    g = C_KV_HEADS
    nch = seq // CMP_STRIDE
    half = CMP_LEN // 2
    x = kc.reshape(batch, nch, half * g * HEAD_DIM)
    eye = jnp.eye(g, dtype=F32)
    w1r = w1.reshape(CMP_LEN, HEAD_DIM, CMP_HIDDEN)
    expand = lambda wpart: jnp.einsum('jdh,ge->jgdeh', wpart, eye).reshape(half * g * HEAD_DIM, g * CMP_HIDDEN)
    wa, wb = expand(w1r[:half]).astype(BF16), expand(w1r[half:]).astype(BF16)
    w2e = jnp.einsum('hd,ge->ghed', jnp.pad(w2, ((0, 0), (0, LANES - HEAD_DIM))), eye)
    w2e = w2e.reshape(g * CMP_HIDDEN, g * LANES).astype(BF16)
    tile_pos = lambda p: jnp.broadcast_to(p[:, None, :], (half, g, HEAD_DIM)).reshape(1, half * g * HEAD_DIM)
    pa, pb = tile_pos(pos[:half]), tile_pos(pos[half:])
    full = lambda a: pl.BlockSpec(a.shape, lambda i: (0,) * a.ndim)
    return pl.pallas_call(
        _compress_kernel,
        grid=(batch,),
        in_specs=[pl.BlockSpec((1, nch, x.shape[2]), lambda i: (i, 0, 0)), full(pa), full(pb), full(wa), full(wb),
                  full(w2e)],
        out_specs=pl.BlockSpec((1, nch, g * LANES), lambda i: (i, 0, 0)),
        out_shape=jax.ShapeDtypeStruct((batch, nch, g * LANES), BF16),
        compiler_params=_cparams("parallel"),
        name="nsa_compress",
    )(x, pa, pb, wa, wb, w2e)


def _nsa_cmp_kernel(q_ref, kc_ref, vc_ref, ovt_ref, o_ref, sel_ref, *, rep, n_sel, n_real):
    tq = q_ref.shape[1]
    nc = kc_ref.shape[1]
    nsb = ovt_ref.shape[0]
    t0 = pl.program_id(2) * tq
    scale = HEAD_DIM ** -0.5
    kc = kc_ref[0]
    vc = vc_ref[0]
    tpos = t0 + lax.broadcasted_iota(jnp.int32, (tq, nc), 0)
    cend = lax.broadcasted_iota(jnp.int32, (tq, nc), 1) * CMP_STRIDE + (CMP_LEN - 1)
    cmask = cend <= tpos
    psum = jnp.zeros((tq, nc), F32)
    outs = []
    for r in range(rep):
        q = q_ref[0, :, r * LANES:(r + 1) * LANES]
        sc = lax.dot_general(q, kc, _NT, preferred_element_type=F32) * scale
        sc = jnp.where(cmask, sc, NEG)
        m = jnp.max(sc, axis=-1, keepdims=True)
        ex = jnp.where(cmask, jnp.exp(sc - m), 0.0)
        den = jnp.sum(ex, axis=-1, keepdims=True)
        pc = ex / jnp.where(den > 0, den, 1.0)
        outs.append(jnp.dot(pc.astype(BF16), vc, preferred_element_type=F32))
        psum = psum + pc
    o_ref[0] = jnp.concatenate(outs, axis=1)
    phi, plo = _split_bf16(psum)
    ovt = ovt_ref[...]
    imp = (lax.dot_general(ovt, phi, _NT, preferred_element_type=F32)
           + lax.dot_general(ovt, plo, _NT, preferred_element_type=F32))
    jblk = lax.broadcasted_iota(jnp.int32, (nsb, tq), 0)
    cur = jnp.right_shift(t0 + lax.broadcasted_iota(jnp.int32, (nsb, tq), 1), SLC_SHIFT)
    forced = (jblk == 0) | (jblk == cur) | (jblk == cur - 1)
    score = jnp.where(jblk > cur, -1.0, jnp.where(forced, SELECT_FORCE, imp))
    rank = jnp.zeros((nsb, tq), F32)
    for k in range(n_real):
        rowk = score[k:k + 1, :]
        ge = jnp.where(rowk >= score, 1.0, 0.0)
        gt = jnp.where(rowk > score, 1.0, 0.0)
        rank = rank + jnp.where(jblk > k, ge, gt)
    sel_ref[0, 0] = jnp.where(rank < n_sel, 1.0, 0.0)


def _nsa_cmp(q, kcmp, vcmp, batch, seq, q_off_blocks, tq=128):
    g = C_KV_HEADS
    rep = C_HEADS // g
    nc = kcmp.shape[1]
    nsb = seq // SLC_BLOCK
    n_sel = min(SLC_TOP_N, nsb)
    cs = np.arange(nc)[:, None] * CMP_STRIDE
    js = np.arange(nsb)[None, :] * SLC_BLOCK
    overlap = np.clip(np.minimum(cs + CMP_LEN, js + SLC_BLOCK) - np.maximum(cs, js), 0, None) / CMP_LEN
    overlap[(seq - CMP_LEN) // CMP_STRIDE + 1:] = 0.0
    nsb_pad = -(-nsb // LANES) * LANES
    ovt = jnp.asarray(np.pad(overlap.T, ((0, nsb_pad - nsb), (0, 0))), BF16)
    q3 = q.reshape(batch, seq, q.shape[1])
    n_real, nsb = nsb, nsb_pad
    o, sel = pl.pallas_call(
        functools.partial(_nsa_cmp_kernel, rep=rep, n_sel=n_sel, n_real=n_real),
        grid=(batch, g, seq // tq),
        in_specs=[pl.BlockSpec((1, tq, rep * LANES), lambda b, gi, i: (b, i, q_off_blocks // rep + gi)),
                  pl.BlockSpec((1, nc, LANES), lambda b, gi, i: (b, 0, gi)),
                  pl.BlockSpec((1, nc, LANES), lambda b, gi, i: (b, 0, gi)),
                  pl.BlockSpec(ovt.shape, lambda b, gi, i: (0, 0))],
        out_specs=[pl.BlockSpec((1, tq, rep * LANES), lambda b, gi, i: (b, i, gi)),
                   pl.BlockSpec((1, 1, nsb, tq), lambda b, gi, i: (b, gi, 0, i))],
        out_shape=[jax.ShapeDtypeStruct((batch, seq, C_HEADS * LANES), F32),
                   jax.ShapeDtypeStruct((batch, g, nsb, seq), F32)],
        compiler_params=_cparams("parallel", "parallel", "parallel"),
        name="nsa_compressed_select",
    )(q3, kcmp, vcmp, ovt)
    return o.reshape(batch * seq, C_HEADS * LANES), sel


def _nsa_slc_kernel(q_ref, k_ref, vt_ref, selt_ref, o_ref, bias_sc, *, rep, kt):
    tq = q_ref.shape[1]
    nsb = selt_ref.shape[2]
    t0 = pl.program_id(2) * tq
    n_kt = (t0 + tq + kt - 1) // kt
    selt = selt_ref[0, 0].astype(BF16)
    key_blk = jnp.right_shift(lax.broadcasted_iota(jnp.int32, (kt, nsb), 0), SLC_SHIFT)
    blk = lax.broadcasted_iota(jnp.int32, (kt, nsb), 1)
    kpos = lax.broadcasted_iota(jnp.int32, (kt, tq), 0)
    tpos = t0 + lax.broadcasted_iota(jnp.int32, (kt, tq), 1)

    def make_bias(j, c):
        k0 = pl.multiple_of(j * kt, kt)
        expand = jnp.where(key_blk == blk - jnp.right_shift(k0, SLC_SHIFT), 1.0, 0.0).astype(BF16)
        chosen = jnp.dot(expand, selt, preferred_element_type=F32)
        ok = (chosen > 0.5) & (k0 + kpos <= tpos)
        bias_sc[pl.ds(k0, kt), :] = jnp.where(ok, 0.0, NEG)
        return c

    lax.fori_loop(0, n_kt, make_bias, 0)

    outs = []
    pair = rep
    for r0 in range(0, rep, pair):
        qs = [q_ref[0, :, r * LANES:(r + 1) * LANES] * (HEAD_DIM ** -0.5)
              for r in range(r0, r0 + pair)]

        def tile(j, carry, qs=qs):
            k0 = pl.multiple_of(j * kt, kt)
            k = k_ref[0, pl.ds(k0, kt), :]
            vt = vt_ref[:, pl.ds(k0, kt)]
            bias = bias_sc[pl.ds(k0, kt), :]
            new = []
            for qr, (m, l, acc) in zip(qs, carry):
                s = lax.dot_general(k, qr, _NT, preferred_element_type=F32) + bias
                m_new = jnp.maximum(m, jnp.max(s, axis=0, keepdims=True))
                e = jnp.exp(s - m_new)
                corr = jnp.exp(m - m_new)
                l = l * corr + jnp.sum(e, axis=0, keepdims=True)
                acc = acc * corr + jnp.dot(vt, e.astype(BF16), preferred_element_type=F32)
                new.append((m_new, l, acc))
            return tuple(new)

        init = tuple((jnp.full((1, tq), NEG, F32), jnp.zeros((1, tq), F32), jnp.zeros((LANES, tq), F32))
                     for _ in range(pair))
        for _, l, acc in lax.fori_loop(0, n_kt, tile, init):
            outs.append((acc / l).T)
    o_ref[0] = jnp.concatenate(outs, axis=1)


def _nsa_slc(qkv, vt, selt, batch, seq, nblk, q_off, k_off, tq=128, kt=1024):
    g = C_KV_HEADS
    rep = C_HEADS // g
    kt = min(kt, seq)
    nsb = selt.shape[2]
    arr = qkv.reshape(batch, seq, nblk * LANES)
    out = pl.pallas_call(
        functools.partial(_nsa_slc_kernel, rep=rep, kt=kt),
        grid=(batch, g, seq // tq),
        in_specs=[pl.BlockSpec((1, tq, rep * LANES), lambda b, gi, i: (b, i, q_off // rep + gi)),
                  pl.BlockSpec((1, seq, LANES), lambda b, gi, i: (b, 0, k_off + gi)),
                  pl.BlockSpec((LANES, seq), lambda b, gi, i: (gi, b)),
                  pl.BlockSpec((1, 1, nsb, tq), lambda b, gi, i: (b, gi, 0, i))],
        out_specs=pl.BlockSpec((1, tq, rep * LANES), lambda b, gi, i: (b, i, gi)),
        out_shape=jax.ShapeDtypeStruct((batch, seq, C_HEADS * LANES), F32),
        scratch_shapes=[pltpu.VMEM((seq, tq), F32)],
        compiler_params=_cparams("parallel", "parallel", "arbitrary"),
        name="nsa_selected",
    )(arr, arr, vt, selt)
    return out.reshape(batch * seq, C_HEADS * LANES)


def _even_mixer_layer(x2, batch, seq, tabs, w_in, sinks, w_out, g, b):
    d = x2.shape[1]
    n_heads_in = 3 * A_HEADS + B_Q_HEADS + 2 * B_KV_HEADS
    w = _pad_heads_cols(w_in, n_heads_in).astype(BF16)
    rope = [1] * (2 * A_HEADS) + [0] * A_HEADS + [1] * B_Q_HEADS + [1] * B_KV_HEADS + [0] * B_KV_HEADS
    plan = [(0, c, rope[c]) for c in range(n_heads_in)]
    (qkv,) = _proj(x2, w, tabs, plan, [n_heads_in * LANES], [BF16])
    outs = []
    for window, dilation in A_PATTERNS:
        outs.append(_band_attention(qkv, batch=batch, seq=seq, dilation=dilation, nblk=n_heads_in, q_off=0,
                                    k_off=A_HEADS, v_off=2 * A_HEADS, n_q_heads=A_HEADS, rep=1,
                                    max_dist=window // dilation,
                                    heads=min(4, dilation)))
    qb_off = 3 * A_HEADS
    ob = _band_attention(qkv, batch=batch, seq=seq, dilation=1, nblk=n_heads_in, q_off=qb_off,
                         k_off=qb_off + B_Q_HEADS, v_off=qb_off + B_Q_HEADS + B_KV_HEADS, n_q_heads=B_Q_HEADS,
                         rep=B_Q_HEADS // B_KV_HEADS, max_dist=B_WINDOW - 1, sinks=sinks)
    w_o = _pad_heads_rows(w_out, A_HEADS + B_Q_HEADS).astype(BF16)
    return _outproj_even(outs[0], outs[1], outs[2], ob, x2, w_o, g.reshape(1, d), b.reshape(1, d))


def _odd_mixer_layer(x2, batch, seq, tabs, w_in, cmpk_pos, cmpk_w1, cmpk_w2, cmpv_pos, cmpv_w1, cmpv_w2, w_out, g, b):
    d = x2.shape[1]
    kvw = C_KV_HEADS * HEAD_DIM
    qw = C_HEADS * HEAD_DIM
    sizes = [qw] + [kvw] * 6 + [3 * C_HEADS]
    offs = np.concatenate([[0], np.cumsum(sizes)])
    wq, wkc, wvc, wks, wvs, wkw, wvw, wgt = [w_in[:, offs[i]:offs[i + 1]] for i in range(8)]
    ph = lambda wpart, n: _pad_heads_cols(wpart, n)
    w = jnp.concatenate([ph(wq, C_HEADS), ph(wks, C_KV_HEADS), ph(wkw, C_KV_HEADS), ph(wvw, C_KV_HEADS),
                         wkc, wvc, jnp.pad(wgt, ((0, 0), (0, LANES - 3 * C_HEADS)))], axis=1).astype(BF16)
    wvs_t = ph(wvs, C_KV_HEADS).T.astype(BF16)
    n16 = C_HEADS + 3 * C_KV_HEADS
    rope16 = [1] * C_HEADS + [1] * C_KV_HEADS + [1] * C_KV_HEADS + [0] * C_KV_HEADS
    n_kc = kvw // LANES
    plan = ([(0, c, rope16[c]) for c in range(n16)] + [(1, c, 2) for c in range(n_kc)]
            + [(2, c, 0) for c in range(n_kc)] + [(3, 0, 0)])
    qkv, kc, vc, gate, vs_t = _proj(x2, w, tabs, plan, [n16 * LANES, kvw, kvw, LANES], [BF16, F32, F32, F32],
                                    wt=wvs_t)
    kcmp = _compress(kc, cmpk_pos, cmpk_w1, cmpk_w2, batch, seq)
    vcmp = _compress(vc, cmpv_pos, cmpv_w1, cmpv_w2, batch, seq)
    o_cmp, selt = _nsa_cmp(qkv, kcmp, vcmp, batch, seq, 0)
    ks_off = C_HEADS
    o_slc = _nsa_slc(qkv, vs_t, selt, batch, seq, n16, 0, ks_off)
    kw_off = ks_off + C_KV_HEADS
    o_win = _band_attention(qkv, batch=batch, seq=seq, dilation=1, nblk=n16, q_off=0, k_off=kw_off,
                            v_off=kw_off + C_KV_HEADS, n_q_heads=C_HEADS, rep=C_HEADS // C_KV_HEADS,
                            max_dist=NSA_WINDOW - 1)
    w_o = _pad_heads_rows(w_out, C_HEADS).astype(BF16)
    return _outproj_odd(o_cmp, o_slc, o_win, gate, x2, w_o, g.reshape(1, d), b.reshape(1, d))


def kernel(x, positions, even_w_in, even_sinks, even_w_out, odd_w_in, odd_cmpk_pos, odd_cmpk_w1, odd_cmpk_w2, odd_cmpv_pos, odd_cmpv_w1, odd_cmpv_w2, odd_w_out, mix_ln_g, mix_ln_b, moe_router_w, moe_router_b, moe_w_gate, moe_w_up, moe_w_down, moe_sh_gate, moe_sh_up, moe_sh_down, ffn_ln_g, ffn_ln_b):
    batch, seq, d = x.shape
    x2 = x.reshape(batch * seq, d)
    tabs = _rope_tables(positions)
    depth = mix_ln_g.shape[0]
    for layer in range(depth):
        j = layer // 2
        if layer % 2 == 0:
            x2 = _even_mixer_layer(x2, batch, seq, tabs, even_w_in[j], even_sinks[j], even_w_out[j],
                                   mix_ln_g[layer], mix_ln_b[layer])
        else:
            x2 = _odd_mixer_layer(x2, batch, seq, tabs, odd_w_in[j], odd_cmpk_pos[j], odd_cmpk_w1[j], odd_cmpk_w2[j],
                                  odd_cmpv_pos[j], odd_cmpv_w1[j], odd_cmpv_w2[j], odd_w_out[j],
                                  mix_ln_g[layer], mix_ln_b[layer])
        x2 = _moe_layer(x2, moe_router_w[layer], moe_router_b[layer], moe_w_gate, moe_w_up, moe_w_down, layer,
                        moe_sh_gate[layer], moe_sh_up[layer], moe_sh_down[layer],
                        ffn_ln_g[layer].reshape(1, d), ffn_ln_b[layer].reshape(1, d))
    return x2.reshape(batch, seq, d)
```

```python
import functools

import numpy as np
import jax
import jax.numpy as jnp
from jax import lax
from jax.experimental import pallas as pl
from jax.experimental.pallas import tpu as pltpu

F32 = jnp.float32
BF16 = jnp.bfloat16

LANES = 128
HEAD_DIM = 64
ROT_DIM = HEAD_DIM // 4
ROT_HALF = ROT_DIM // 2
ROPE_THETA = 500000.0
QBLK = 128
A_HEADS = 8
A_PATTERNS = ((128, 1), (512, 4), (2048, 16))
B_Q_HEADS = 8
B_KV_HEADS = 2
B_WINDOW = 128
C_HEADS = 16
C_KV_HEADS = 4
CMP_LEN = 32
CMP_STRIDE = 16
CMP_HIDDEN = 2 * HEAD_DIM
SLC_BLOCK = 64
SLC_SHIFT = 6
SLC_TOP_N = 16
NSA_WINDOW = 512
SELECT_FORCE = 1.0e4
N_EXPERTS = 256
TOP_K = 8
N_GROUPS = 8
TOPK_GROUPS = 4
ROUTED_SCALE = 2.5
MOE_BLOCK = 256
DEPTH = 2
DEEPNORM_ALPHA = (2 * DEPTH) ** 0.25
LN_EPS = 1e-5
NEG = -1.0e30
VMEM_LIMIT = 56 * 1024 * 1024

_NT = (((1,), (1,)), ((), ()))


def _cparams(*sem):
    return pltpu.CompilerParams(dimension_semantics=sem, vmem_limit_bytes=VMEM_LIMIT)


def _split_bf16(a):
    hi = a.astype(BF16)
    lo = (a - hi.astype(F32)).astype(BF16)
    return hi, lo


def _layer_norm(y, g, b):
    mu = jnp.mean(y, axis=-1, keepdims=True)
    d = y - mu
    var = jnp.mean(d * d, axis=-1, keepdims=True)
    return d * lax.rsqrt(var + LN_EPS) * g + b


def _silu(a):
    return a * jax.nn.sigmoid(a)


def _proj_kernel(*refs, plan, n_main, transposed, dilations, n_dil):
    n_in = 4 if transposed else 3
    x_ref, w_ref, tab_ref = refs[:3]
    out_refs = refs[n_in:n_in + n_main]
    extra = list(refs[n_in + n_main:])
    x = x_ref[...].astype(BF16)
    tm = x.shape[0]
    if transposed:
        out_t = extra.pop(0)
        out_t[...] = lax.dot_general(refs[3][...], x, _NT, preferred_element_type=F32).astype(out_t.dtype)
    dil_refs = [extra.pop(0) for _ in dilations]
    stage = extra.pop(0) if dilations else None
    nblk = len(plan)
    for c0 in range(0, nblk, 2):
        nb = min(2, nblk - c0)
        acc = jnp.dot(x, w_ref[:, c0 * LANES:(c0 + nb) * LANES], preferred_element_type=F32)
        for j in range(nb):
            blk = acc[:, j * LANES:(j + 1) * LANES]
            dst, dblk, mode = plan[c0 + j]
            if mode:
                off = (mode - 1) * 3 * LANES
                cos = tab_ref[:, off:off + LANES]
                s_lo = tab_ref[:, off + LANES:off + 2 * LANES]
                s_hi = tab_ref[:, off + 2 * LANES:off + 3 * LANES]
                blk = (blk * cos + pltpu.roll(blk, LANES - ROT_HALF, 1) * s_lo
                       + pltpu.roll(blk, ROT_HALF, 1) * s_hi)
            o_ref = out_refs[dst]
            o_ref[:, dblk * LANES:(dblk + 1) * LANES] = blk.astype(o_ref.dtype)
            c = c0 + j
            if dilations and c < n_dil:
                stage[...] = blk
                for dil, d_ref in zip(dilations, dil_refs):
                    for r in range(dil):
                        col = (r * n_dil + c) * LANES
                        d_ref[:, col:col + LANES] = stage[pl.ds(r, tm // dil, stride=dil), :].astype(d_ref.dtype)


def _proj(x2, w, tabs, plan, out_cols, out_dtypes, wt=None, dilations=(), n_dil=0, tm=256):
    t, d = x2.shape
    ncol = w.shape[1]
    out_shape = [jax.ShapeDtypeStruct((t, c), dt) for c, dt in zip(out_cols, out_dtypes)]
    in_specs = [pl.BlockSpec((tm, d), lambda i: (i, 0)),
                pl.BlockSpec((d, ncol), lambda i: (0, 0)),
                pl.BlockSpec((tm, tabs.shape[1]), lambda i: (i, 0))]
    out_specs = [pl.BlockSpec((tm, c), lambda i: (i, 0)) for c in out_cols]
    args = [x2, w, tabs]
    if wt is not None:
        in_specs.append(pl.BlockSpec(wt.shape, lambda i: (0, 0)))
        out_specs.append(pl.BlockSpec((wt.shape[0], tm), lambda i: (0, i)))
        out_shape.append(jax.ShapeDtypeStruct((wt.shape[0], t), BF16))
        args.append(wt)
    for dil in dilations:
        out_specs.append(pl.BlockSpec((tm // dil, dil * n_dil * LANES), lambda i: (i, 0)))
        out_shape.append(jax.ShapeDtypeStruct((t // dil, dil * n_dil * LANES), BF16))
    return pl.pallas_call(
        functools.partial(_proj_kernel, plan=tuple(plan), n_main=len(out_cols), transposed=wt is not None,
                          dilations=tuple(dilations), n_dil=n_dil),
        grid=(t // tm,),
        in_specs=in_specs,
        out_specs=out_specs,
        out_shape=out_shape,
        scratch_shapes=[pltpu.VMEM((tm, LANES), F32)] if dilations else [],
        compiler_params=_cparams("parallel"),
        name="proj_rope",
    )(*args)


def _rope_tables(positions):
    t = positions.size
    inv_freq = jnp.asarray(ROPE_THETA ** (-np.arange(0, ROT_DIM, 2) / ROT_DIM), F32)
    ang = positions.astype(F32).reshape(t, 1) * inv_freq
    trig = jnp.concatenate([jnp.cos(ang), jnp.sin(ang)], axis=1)
    place = np.zeros((2 * ROT_HALF, 6 * LANES), np.float32)
    const = np.zeros((6 * LANES,), np.float32)
    for base, heads in ((0, (0,)), (3 * LANES, (0, HEAD_DIM))):
        const[base:base + LANES] = 1.0
        for h in heads:
            for i in range(ROT_HALF):
                place[i, base + h + i] = 1.0
                place[i, base + h + ROT_HALF + i] = 1.0
                const[base + h + i] = const[base + h + ROT_HALF + i] = 0.0
                place[ROT_HALF + i, base + LANES + h + i] = -1.0
                place[ROT_HALF + i, base + 2 * LANES + h + ROT_HALF + i] = 1.0
    return jnp.dot(trig, jnp.asarray(place), precision=lax.Precision.HIGHEST) + jnp.asarray(const)


def _pad_heads_cols(w, n_heads):
    d = w.shape[0]
    w = w.reshape(d, n_heads, HEAD_DIM)
    return jnp.pad(w, ((0, 0), (0, 0), (0, LANES - HEAD_DIM))).reshape(d, n_heads * LANES)


def _pad_heads_rows(w, n_heads):
    d = w.shape[1]
    w = w.reshape(n_heads, HEAD_DIM, d)
    return jnp.pad(w, ((0, 0), (0, LANES - HEAD_DIM), (0, 0))).reshape(n_heads * LANES, d)


def _band_kernel(*refs, back, max_dist, length, qrows, heads, shared_kv, in_flight, has_sink):
    if has_sink:
        sink_ref, q_ref, k_ref, v_ref, o_ref, bias_sc = refs
    else:
        q_ref, k_ref, v_ref, o_ref, bias_sc = refs
    nq = length // qrows
    window = qrows + back
    lane = lax.broadcasted_iota(jnp.int32, (qrows, LANES), 1)
    first_head = pl.program_id(2) * heads

    def band_bias(offset):
        row = lax.broadcasted_iota(jnp.int32, (qrows, window), 0)
        col = lax.broadcasted_iota(jnp.int32, (qrows, window), 1)
        dist = offset + row - col
        return jnp.where((dist >= 0) & (dist <= max_dist), 0.0, NEG)

    bias_sc[...] = band_bias(back)

    def block(h, q0, k0, bias):
        hs = slice(h * LANES, (h + 1) * LANES)
        kvs = slice(0, LANES) if shared_kv else hs
        q = q_ref[0, pl.ds(q0, qrows), hs] * (HEAD_DIM ** -0.5)
        k = k_ref[0, pl.ds(k0, window), kvs]
        v = v_ref[0, pl.ds(k0, window), kvs]
        s = lax.dot_general(q, k, _NT, preferred_element_type=F32) + bias
        m = jnp.max(s, axis=-1, keepdims=True)
        e = jnp.exp(s - m)
        den = jnp.sum(e, axis=-1, keepdims=True)
        if has_sink:
            den = den + jnp.exp(sink_ref[first_head + h] - m)
        o = jnp.dot(e.astype(BF16), v, preferred_element_type=F32) / den
        lse = m + jnp.log(den)
        o_ref[0, pl.ds(q0, qrows), hs] = jnp.where(lane < HEAD_DIM, o, lse)

    n_clipped = min(-(-back // qrows), nq)
    for qi in range(n_clipped):
        bias = band_bias(qi * qrows)
        for h in range(heads):
            block(h, qi * qrows, 0, bias)

    steady = nq - n_clipped
    per_iter = max(1, min(in_flight // heads, steady))
    n_iter = steady // per_iter

    def body(it, carry):
        aligned = lambda v: v if isinstance(v, int) else pl.multiple_of(v, QBLK)
        for j in range(per_iter):
            q0 = aligned((n_clipped + it * per_iter + j) * qrows)
            for h in range(heads):
                block(h, q0, aligned(q0 - back), bias_sc[...])
        return carry

    if n_iter == 1:
        body(0, 0)
    elif n_iter:
        lax.fori_loop(0, n_iter, body, 0)
    for qi in range(n_clipped + n_iter * per_iter, nq):
        for h in range(heads):
            block(h, qi * qrows, qi * qrows - back, bias_sc[...])


def _band_attention(qkv, *, batch, seq, dilation, nblk, q_off, k_off, v_off, n_q_heads, rep, max_dist, sinks=None,
                    heads=1, in_flight=16):
    length = seq // dilation
    back = min(-(-max_dist // QBLK) * QBLK, length - QBLK)
    qrows = QBLK
    shared_kv = rep > 1
    assert n_q_heads % heads == 0 and (rep == 1 or rep % heads == 0)
    kv_heads = 1 if shared_kv else heads
    arr = qkv.reshape(batch, length, dilation * nblk * LANES)
    assert q_off % heads == 0 and k_off % kv_heads == 0 and v_off % kv_heads == 0 and nblk % heads == 0
    qspec = pl.BlockSpec((1, length, heads * LANES),
                         lambda b, r, h: (b, 0, (r * nblk + q_off) // heads + h))
    kvspec = lambda off: pl.BlockSpec(
        (1, length, kv_heads * LANES),
        lambda b, r, h: (b, 0, (r * nblk + off) // kv_heads + (h * heads // rep if shared_kv else h)))
    in_specs = [qspec, kvspec(k_off), kvspec(v_off)]
    args = [arr, arr, arr]
    if sinks is not None:
        in_specs = [pl.BlockSpec(memory_space=pltpu.SMEM)] + in_specs
        args = [sinks.reshape(-1).astype(F32)] + args
    out = pl.pallas_call(
        functools.partial(_band_kernel, back=back, max_dist=max_dist, length=length, qrows=qrows, heads=heads,
                          shared_kv=shared_kv, in_flight=in_flight, has_sink=sinks is not None),
        grid=(batch, dilation, n_q_heads // heads),
        in_specs=in_specs,
        out_specs=pl.BlockSpec((1, length, heads * LANES), lambda b, r, h: (b, 0, r * (n_q_heads // heads) + h)),
        out_shape=jax.ShapeDtypeStruct((batch, length, dilation * n_q_heads * LANES), F32),
        scratch_shapes=[pltpu.VMEM((qrows, qrows + back), F32)],
        compiler_params=_cparams("parallel", "parallel", "parallel"),
        name="band_attention",
    )(*args)
    return out.reshape(batch * length, dilation * n_q_heads * LANES)


def _outproj_even_kernel(o1_ref, o2_ref, o3_ref, ob_ref, x_ref, w_ref, g_ref, b_ref, out_ref, *nat_refs,
                         n_a, n_b, dilations):
    tm = x_ref.shape[0]
    lane = lax.broadcasted_iota(jnp.int32, (tm, LANES), 1)
    real = lane < HEAD_DIM
    pattern_refs = []
    nat_refs = list(nat_refs)
    for o_ref, dil in zip((o1_ref, o2_ref, o3_ref), dilations):
        if dil == 1:
            pattern_refs.append(o_ref)
            continue
        nat = nat_refs.pop(0)
        for r in range(dil):
            for h in range(n_a):
                col = (r * n_a + h) * LANES
                nat[h, pl.ds(r, tm // dil, stride=dil), :] = o_ref[:, col:col + LANES]
        pattern_refs.append(nat)
    parts = []
    for h in range(n_a):
        sl = slice(h * LANES, (h + 1) * LANES)
        outs = [r[:, sl] if r.ndim == 2 else r[h] for r in pattern_refs]
        lses = [jnp.where(real, pltpu.roll(a, HEAD_DIM, 1), a) for a in outs]
        m = jnp.maximum(jnp.maximum(lses[0], lses[1]), lses[2])
        es = [jnp.exp(l - m) for l in lses]
        num = es[0] * outs[0] + es[1] * outs[1] + es[2] * outs[2]
        den = es[0] + es[1] + es[2]
        parts.append(jnp.where(real, num / den, 0.0).astype(BF16))
    for h in range(n_b):
        parts.append(jnp.where(real, ob_ref[:, h * LANES:(h + 1) * LANES], 0.0).astype(BF16))
    a = jnp.concatenate(parts, axis=1)
    mixed = jnp.dot(a, w_ref[...], preferred_element_type=F32)
    y = DEEPNORM_ALPHA * x_ref[...] + mixed
    out_ref[...] = _layer_norm(y, g_ref[...], b_ref[...])


def _outproj_even(outs, dilations, ob, x2, w, g, b, tm=256):
    t, d = x2.shape
    n_a, n_b = outs[0].shape[1] // (dilations[0] * LANES), ob.shape[1] // LANES
    row = lambda c: pl.BlockSpec((tm, c), lambda i: (i, 0))
    grouped = lambda a, dil: pl.BlockSpec((tm // dil, a.shape[1]), lambda i: (i, 0))
    full = lambda a: pl.BlockSpec(a.shape, lambda i: (0,) * a.ndim)
    return pl.pallas_call(
        functools.partial(_outproj_even_kernel, n_a=n_a, n_b=n_b, dilations=tuple(dilations)),
        grid=(t // tm,),
        in_specs=[grouped(o, dil) for o, dil in zip(outs, dilations)] + [row(ob.shape[1]), row(d),
                                                                          full(w), full(g), full(b)],
        out_specs=row(d),
        out_shape=jax.ShapeDtypeStruct((t, d), F32),
        scratch_shapes=[pltpu.VMEM((n_a, tm, LANES), F32) for dil in dilations if dil > 1],
        compiler_params=_cparams("parallel"),
        name="outproj_even_ln",
    )(*outs, ob, x2, w, g, b)


def _outproj_odd_kernel(oc_ref, os_ref, ow_ref, gate_ref, e_ref, x_ref, w_ref, g_ref, b_ref, out_ref):
    gate = jax.nn.sigmoid(gate_ref[...])
    ghi, glo = _split_bf16(gate)
    acc = None
    for j, o_ref in enumerate((oc_ref, os_ref, ow_ref)):
        ej = e_ref[j]
        gfull = jnp.dot(ghi, ej, preferred_element_type=F32) + jnp.dot(glo, ej, preferred_element_type=F32)
        term = gfull * o_ref[...]
        acc = term if acc is None else acc + term
    mixed = jnp.dot(acc.astype(BF16), w_ref[...], preferred_element_type=F32)
    y = DEEPNORM_ALPHA * x_ref[...] + mixed
    out_ref[...] = _layer_norm(y, g_ref[...], b_ref[...])


def _gate_expanders(n_heads):
    e = np.zeros((3, LANES, n_heads * LANES), np.float32)
    for j in range(3):
        for h in range(n_heads):
            e[j, 3 * h + j, h * LANES:h * LANES + HEAD_DIM] = 1.0
    return jnp.asarray(e, BF16)


def _outproj_odd(oc, osl, ow, gate, x2, w, g, b, tm=256):
    t, d = x2.shape
    n_heads = oc.shape[1] // LANES
    e = _gate_expanders(n_heads)
    row = lambda c: pl.BlockSpec((tm, c), lambda i: (i, 0))
    full = lambda a: pl.BlockSpec(a.shape, lambda i: (0,) * a.ndim)
    return pl.pallas_call(
        _outproj_odd_kernel,
        grid=(t // tm,),
        in_specs=[row(oc.shape[1]), row(osl.shape[1]), row(ow.shape[1]), row(LANES), full(e), row(d),
                  full(w), full(g), full(b)],
        out_specs=row(d),
        out_shape=jax.ShapeDtypeStruct((t, d), F32),
        compiler_params=_cparams("parallel"),
        name="outproj_odd_ln",
    )(oc, osl, ow, gate, e, x2, w, g, b)


def _router_kernel(x_ref, whi_ref, wlo_ref, bias_ref, eidx_ref, gate_ref, rank_ref, cnt_ref):
    n_exp = whi_ref.shape[0]
    tm = x_ref.shape[0]
    per_group = n_exp // N_GROUPS
    xhi, xlo = _split_bf16(x_ref[...])
    whi, wlo = whi_ref[...], wlo_ref[...]
    dg = lambda a, b: lax.dot_general(a, b, _NT, preferred_element_type=F32)
    logits = dg(whi, xhi) + dg(whi, xlo) + dg(wlo, xhi)
    aff = jax.nn.sigmoid(logits)
    biased = aff + bias_ref[...]
    gio = lax.broadcasted_iota(jnp.int32, (per_group, tm), 0).astype(F32)
    blocks, scores = [], []
    for g in range(N_GROUPS):
        blk = biased[g * per_group:(g + 1) * per_group, :]
        m1 = jnp.max(blk, axis=0, keepdims=True)
        first = jnp.min(jnp.where(blk == m1, gio, float(per_group)), axis=0, keepdims=True)
        m2 = jnp.max(jnp.where(gio == first, -jnp.inf, blk), axis=0, keepdims=True)
        blocks.append(blk)
        scores.append(m1 + m2)
    masked = []
    for g in range(N_GROUPS):
        rank = jnp.zeros((1, tm), F32)
        for o in range(N_GROUPS):
            if o == g:
                continue
            beats = scores[o] >= scores[g] if o < g else scores[o] > scores[g]
            rank = rank + jnp.where(beats, 1.0, 0.0)
        masked.append(jnp.where(rank < TOPK_GROUPS, blocks[g], -jnp.inf))
    cur = jnp.concatenate(masked, axis=0)
    eio = lax.broadcasted_iota(jnp.int32, (n_exp, tm), 0).astype(F32)
    ids, gs = [], []
    for _ in range(TOP_K):
        m = jnp.max(cur, axis=0, keepdims=True)
        idx = jnp.min(jnp.where(cur == m, eio, float(n_exp)), axis=0, keepdims=True)
        hit = eio == idx
        gs.append(jnp.sum(jnp.where(hit, aff, 0.0), axis=0, keepdims=True))
        ids.append(idx)
        cur = jnp.where(hit, -jnp.inf, cur)
    gates = jnp.concatenate(gs, axis=0)
    gates = gates / jnp.sum(gates, axis=0, keepdims=True) * ROUTED_SCALE
    eidx_ref[...] = jnp.concatenate(ids, axis=0).astype(jnp.int32)
    gate_ref[...] = gates
    @pl.when(pl.program_id(0) == 0)
    def _():
        cnt_ref[...] = jnp.zeros(cnt_ref.shape, F32)

    onehot = jnp.zeros((n_exp, tm), F32)
    for idx in ids:
        onehot = onehot + jnp.where(eio == idx, 1.0, 0.0)
    earlier = jnp.where(lax.broadcasted_iota(jnp.int32, (tm, tm), 0) < lax.broadcasted_iota(jnp.int32, (tm, tm), 1),
                        1.0, 0.0).astype(BF16)
    before = cnt_ref[...] + jnp.dot(onehot.astype(BF16), earlier, preferred_element_type=F32)
    ranks = [jnp.sum(jnp.where(eio == idx, before, 0.0), axis=0, keepdims=True) for idx in ids]
    rank_ref[...] = jnp.concatenate(ranks, axis=0).astype(jnp.int32)
    cnt_ref[...] = cnt_ref[...] + jnp.sum(onehot, axis=1, keepdims=True)


def _router(x2, router_w, router_b, tm=256):
    t, d = x2.shape
    n_exp = router_w.shape[1]
    whi, wlo = _split_bf16(router_w.T)
    bias = router_b.reshape(n_exp, 1).astype(F32)
    full = lambda a: pl.BlockSpec(a.shape, lambda i: (0,) * a.ndim)
    per_tok = pl.BlockSpec((TOP_K, tm), lambda i: (0, i))
    return pl.pallas_call(
        _router_kernel,
        grid=(t // tm,),
        in_specs=[pl.BlockSpec((tm, d), lambda i: (i, 0)), full(whi), full(wlo), full(bias)],
        out_specs=[per_tok, per_tok, per_tok, pl.BlockSpec((n_exp, 1), lambda i: (0, 0))],
        out_shape=[jax.ShapeDtypeStruct((TOP_K, t), jnp.int32), jax.ShapeDtypeStruct((TOP_K, t), F32),
                   jax.ShapeDtypeStruct((TOP_K, t), jnp.int32), jax.ShapeDtypeStruct((n_exp, 1), F32)],
        compiler_params=_cparams("arbitrary"),
        name="moe_router",
    )(x2, whi, wlo, bias)


def _moe_dest_kernel(eidx_ref, rank_ref, start_ref, dest_ref):
    n_exp = start_ref.shape[0]
    tm = eidx_ref.shape[1]
    eio = lax.broadcasted_iota(jnp.int32, (n_exp, tm), 0)
    start = start_ref[...]
    rows = []
    for k in range(TOP_K):
        seg = jnp.sum(jnp.where(eio == eidx_ref[k:k + 1, :], start, 0.0), axis=0, keepdims=True)
        rows.append(seg.astype(jnp.int32) + rank_ref[k:k + 1, :])
    dest_ref[...] = jnp.concatenate(rows, axis=0)


def _moe_dest(eidx, rank, seg_start, tm=256):
    t = eidx.shape[1]
    per_tok = pl.BlockSpec((TOP_K, tm), lambda i: (0, i))
    return pl.pallas_call(
        _moe_dest_kernel,
        grid=(t // tm,),
        in_specs=[per_tok, per_tok, pl.BlockSpec(seg_start.shape, lambda i: (0, 0))],
        out_specs=per_tok,
        out_shape=jax.ShapeDtypeStruct((TOP_K, t), jnp.int32),
        compiler_params=_cparams("parallel"),
        name="moe_dest",
    )(eidx, rank, seg_start)


def _moe_zero_kernel(blk_ref, xs_ref):
    xs_ref[...] = jnp.zeros(xs_ref.shape, xs_ref.dtype)


def _moe_zero_padding(zero_blocks, n_blocks, width):
    grid_spec = pltpu.PrefetchScalarGridSpec(
        num_scalar_prefetch=1,
        grid=(zero_blocks.shape[0],),
        in_specs=[],
        out_specs=pl.BlockSpec((MOE_BLOCK, width), lambda i, blk: (blk[i], 0)),
    )
    return pl.pallas_call(
        _moe_zero_kernel,
        grid_spec=grid_spec,
        out_shape=jax.ShapeDtypeStruct((n_blocks * MOE_BLOCK, width), jnp.uint32),
        compiler_params=_cparams("arbitrary"),
        name="moe_zero_padding",
    )(zero_blocks)


def _pack_bf16_pairs(x):
    half = x.shape[1] // 2
    lo = lax.bitcast_convert_type(x[:, :half].astype(BF16).astype(F32), jnp.uint32)
    hi = lax.bitcast_convert_type(x[:, half:].astype(BF16).astype(F32), jnp.uint32)
    return jnp.right_shift(lo, jnp.uint32(16)) | (hi & jnp.uint32(0xFFFF0000))


def _unpack_bf16_pairs(w):
    lo = lax.bitcast_convert_type(jnp.left_shift(w, jnp.uint32(16)), F32).astype(BF16)
    hi = lax.bitcast_convert_type(w & jnp.uint32(0xFFFF0000), F32).astype(BF16)
    return jnp.concatenate([lo, hi], axis=1)


def _moe_dispatch_kernel(dest_ref, x_ref, xs_in, xs_out, buf, sem):
    del xs_in
    tm = x_ref.shape[0]
    buf[...] = _pack_bf16_pairs(x_ref[...])

    def issue(i, c):
        for k in range(TOP_K):
            r = dest_ref[k, i]
            pltpu.make_async_copy(buf.at[pl.ds(i, 1)], xs_out.at[pl.ds(r, 1)], sem).start()
        return c

    lax.fori_loop(0, tm, issue, 0)
    for _ in range(TOP_K):
        pltpu.make_async_copy(buf, xs_out.at[pl.ds(0, tm)], sem).wait()


def _moe_dispatch(dest, x2, xs, tm=256):
    t, d = x2.shape
    return pl.pallas_call(
        _moe_dispatch_kernel,
        grid=(t // tm,),
        in_specs=[pl.BlockSpec((TOP_K, tm), lambda i: (0, i), memory_space=pltpu.SMEM),
                  pl.BlockSpec((tm, d), lambda i: (i, 0)),
                  pl.BlockSpec(memory_space=pl.ANY)],
        out_specs=pl.BlockSpec(memory_space=pl.ANY),
        out_shape=jax.ShapeDtypeStruct(xs.shape, xs.dtype),
        scratch_shapes=[pltpu.VMEM((tm, d // 2), jnp.uint32), pltpu.SemaphoreType.DMA(())],
        input_output_aliases={2: 0},
        compiler_params=_cparams("arbitrary"),
        name="moe_dispatch",
    )(dest, x2, xs)


def _moe_ffn_kernel(be_ref, nu_ref, xs_ref, wg_ref, wu_ref, wd_ref, y_ref, wg_sc, wu_sc, wd_sc):
    b = pl.program_id(0)

    @pl.when(b < nu_ref[0])
    def _():
        @pl.when((b == 0) | (be_ref[b] != be_ref[jnp.maximum(b - 1, 0)]))
        def _():
            wg_sc[...] = wg_ref[0].astype(BF16)
            wu_sc[...] = wu_ref[0].astype(BF16)
            wd_sc[...] = wd_ref[0].astype(BF16)

        xb = _unpack_bf16_pairs(xs_ref[...])
        gp = jnp.dot(xb, wg_sc[...], preferred_element_type=F32)
        up = jnp.dot(xb, wu_sc[...], preferred_element_type=F32)
        h = (_silu(gp) * up).astype(BF16)
        y_ref[...] = jnp.dot(h, wd_sc[...], preferred_element_type=F32)

    @pl.when(b >= nu_ref[0])
    def _():
        y_ref[...] = jnp.zeros(y_ref.shape, y_ref.dtype)


def _moe_ffn(xs, block_expert, n_used, w_gate, w_up, w_down, layer):
    n_blocks = block_expert.shape[0]
    d, ff = w_gate.shape[2], w_gate.shape[3]
    last = lambda b, nu: jnp.minimum(b, nu[0] - 1)
    grid_spec = pltpu.PrefetchScalarGridSpec(
        num_scalar_prefetch=2,
        grid=(n_blocks,),
        in_specs=[
            pl.BlockSpec((MOE_BLOCK, d // 2), lambda b, be, nu: (last(b, nu), 0)),
            pl.BlockSpec((None, 1, d, ff), lambda b, be, nu: (layer, be[last(b, nu)], 0, 0)),
            pl.BlockSpec((None, 1, d, ff), lambda b, be, nu: (layer, be[last(b, nu)], 0, 0)),
            pl.BlockSpec((None, 1, ff, d), lambda b, be, nu: (layer, be[last(b, nu)], 0, 0)),
        ],
        out_specs=pl.BlockSpec((MOE_BLOCK, d), lambda b, be, nu: (b, 0)),
        scratch_shapes=[pltpu.VMEM((d, ff), BF16), pltpu.VMEM((d, ff), BF16), pltpu.VMEM((ff, d), BF16)],
    )
    return pl.pallas_call(
        _moe_ffn_kernel,
        grid_spec=grid_spec,
        out_shape=jax.ShapeDtypeStruct((n_blocks * MOE_BLOCK, d), F32),
        compiler_params=_cparams("arbitrary"),
        name="moe_expert_ffn",
    )(block_expert, n_used, xs, w_gate, w_up, w_down)


def _moe_combine_kernel(dest_ref, y_hbm, gate_ref, x_ref, sg_ref, su_ref, sd_ref, g_ref, b_ref, out_ref, buf, sem):
    tm = x_ref.shape[0]

    def issue(i, c):
        for k in range(TOP_K):
            r = dest_ref[k, i]
            pltpu.make_async_copy(y_hbm.at[pl.ds(r, 1)], buf.at[k, pl.ds(i, 1)], sem).start()
        return c

    lax.fori_loop(0, tm, issue, 0)
    x = x_ref[...]
    xb = x.astype(BF16)
    hs = _silu(jnp.dot(xb, sg_ref[...], preferred_element_type=F32)) * jnp.dot(xb, su_ref[...], preferred_element_type=F32)
    shared = jnp.dot(hs.astype(BF16), sd_ref[...], preferred_element_type=F32)
    for k in range(TOP_K):
        pltpu.make_async_copy(y_hbm.at[pl.ds(0, tm)], buf.at[k], sem).wait()
    gates = gate_ref[...]
    routed = buf[0] * gates[:, 0:1]
    for k in range(1, TOP_K):
        routed = routed + buf[k] * gates[:, k:k + 1]
    y = DEEPNORM_ALPHA * x + (routed + shared)
    out_ref[...] = _layer_norm(y, g_ref[...], b_ref[...])


def _moe_combine(dest, y, gates_t, x2, sh_gate, sh_up, sh_down, g, b, tm=128):
    t, d = x2.shape
    row = lambda c: pl.BlockSpec((tm, c), lambda i: (i, 0))
    full = lambda a: pl.BlockSpec(a.shape, lambda i: (0,) * a.ndim)
    return pl.pallas_call(
        _moe_combine_kernel,
        grid=(t // tm,),
        in_specs=[pl.BlockSpec((TOP_K, tm), lambda i: (0, i), memory_space=pltpu.SMEM),
                  pl.BlockSpec(memory_space=pl.ANY),
                  row(TOP_K), row(d), full(sh_gate), full(sh_up), full(sh_down), full(g), full(b)],
        out_specs=row(d),
        out_shape=jax.ShapeDtypeStruct((t, d), F32),
        scratch_shapes=[pltpu.VMEM((TOP_K, tm, d), F32), pltpu.SemaphoreType.DMA(())],
        compiler_params=_cparams("arbitrary"),
        name="moe_combine_ln",
    )(dest, y, gates_t, x2, sh_gate, sh_up, sh_down, g, b)


def _moe_segments(counts, n_tok):
    n_exp = counts.shape[0]
    n_blocks = -(-n_tok * TOP_K // MOE_BLOCK) + n_exp
    nblk = (counts.reshape(n_exp).astype(jnp.int32) + MOE_BLOCK - 1) // MOE_BLOCK
    blk_end = jnp.cumsum(nblk)
    seg_start = ((blk_end - nblk) * MOE_BLOCK).astype(F32).reshape(n_exp, 1)
    block_expert = jnp.sum((blk_end[None, :] <= jnp.arange(n_blocks)[:, None]).astype(jnp.int32), axis=1)
    block_expert = jnp.minimum(block_expert, n_exp - 1)
    n_used = blk_end[-1]
    tail = jnp.minimum(n_used + jnp.arange(n_exp), n_blocks - 1)
    zero_blocks = jnp.concatenate([jnp.maximum(blk_end - 1, 0), tail]).astype(jnp.int32)
    return seg_start, block_expert.astype(jnp.int32), n_used.astype(jnp.int32).reshape(1), zero_blocks, n_blocks


def _moe_layer(x2, router_w, router_b, w_gate, w_up, w_down, layer, sh_gate, sh_up, sh_down, g, b):
    t, d = x2.shape
    eidx, gates, rank, counts = _router(x2, router_w, router_b)
    seg_start, block_expert, n_used, zero_blocks, n_blocks = _moe_segments(counts, t)
    dest = _moe_dest(eidx, rank, seg_start)
    xs = _moe_dispatch(dest, x2, _moe_zero_padding(zero_blocks, n_blocks, d // 2))
    y = _moe_ffn(xs, block_expert, n_used, w_gate, w_up, w_down, layer)
    return _moe_combine(dest, y, gates.T, x2, sh_gate.astype(BF16), sh_up.astype(BF16), sh_down.astype(BF16), g, b)


def _compress_kernel(x_ref, pa_ref, pb_ref, wa_ref, wb_ref, w2_ref, out_ref):
    x = x_ref[0]
    nc = x.shape[0]
    ha = jnp.dot((x + pa_ref[...]).astype(BF16), wa_ref[...], preferred_element_type=F32)
    hb = jnp.dot((x + pb_ref[...]).astype(BF16), wb_ref[...], preferred_element_type=F32)
    h = ha + pltpu.roll(hb, nc - 1, 0)
    h = jax.nn.gelu(h, approximate=True)
    out_ref[0] = jnp.dot(h.astype(BF16), w2_ref[...], preferred_element_type=F32).astype(out_ref.dtype)


def _compress(kc, pos, w1, w2, batch, seq):
    g = C_KV_HEADS
    nch = seq // CMP_STRIDE
    half = CMP_LEN // 2
    x = kc.reshape(batch, nch, half * g * HEAD_DIM)
    eye = jnp.eye(g, dtype=F32)
    w1r = w1.reshape(CMP_LEN, HEAD_DIM, CMP_HIDDEN)
    expand = lambda wpart: jnp.einsum('jdh,ge->jgdeh', wpart, eye).reshape(half * g * HEAD_DIM, g * CMP_HIDDEN)
    wa, wb = expand(w1r[:half]).astype(BF16), expand(w1r[half:]).astype(BF16)
    w2e = jnp.einsum('hd,ge->ghed', jnp.pad(w2, ((0, 0), (0, LANES - HEAD_DIM))), eye)
    w2e = w2e.reshape(g * CMP_HIDDEN, g * LANES).astype(BF16)
    tile_pos = lambda p: jnp.broadcast_to(p[:, None, :], (half, g, HEAD_DIM)).reshape(1, half * g * HEAD_DIM)
    pa, pb = tile_pos(pos[:half]), tile_pos(pos[half:])
    full = lambda a: pl.BlockSpec(a.shape, lambda i: (0,) * a.ndim)
    return pl.pallas_call(
        _compress_kernel,
        grid=(batch,),
        in_specs=[pl.BlockSpec((1, nch, x.shape[2]), lambda i: (i, 0, 0)), full(pa), full(pb), full(wa), full(wb),
                  full(w2e)],
        out_specs=pl.BlockSpec((1, nch, g * LANES), lambda i: (i, 0, 0)),
        out_shape=jax.ShapeDtypeStruct((batch, nch, g * LANES), BF16),
        compiler_params=_cparams("parallel"),
        name="nsa_compress",
    )(x, pa, pb, wa, wb, w2e)


def _nsa_cmp_kernel(q_ref, kc_ref, vc_ref, ovt_ref, o_ref, sel_ref, *, rep, n_sel, n_real):
    tq = q_ref.shape[1]
    nc = kc_ref.shape[1]
    nsb = ovt_ref.shape[0]
    t0 = pl.program_id(2) * tq
    scale = HEAD_DIM ** -0.5
    kc = kc_ref[0]
    vc = vc_ref[0]
    tpos = t0 + lax.broadcasted_iota(jnp.int32, (tq, nc), 0)
    cend = lax.broadcasted_iota(jnp.int32, (tq, nc), 1) * CMP_STRIDE + (CMP_LEN - 1)
    cmask = cend <= tpos
    psum = jnp.zeros((tq, nc), F32)
    outs = []
    for r in range(rep):
        q = q_ref[0, :, r * LANES:(r + 1) * LANES]
        sc = lax.dot_general(q, kc, _NT, preferred_element_type=F32) * scale
        sc = jnp.where(cmask, sc, NEG)
        m = jnp.max(sc, axis=-1, keepdims=True)
        ex = jnp.where(cmask, jnp.exp(sc - m), 0.0)
        den = jnp.sum(ex, axis=-1, keepdims=True)
        pc = ex / jnp.where(den > 0, den, 1.0)
        outs.append(jnp.dot(pc.astype(BF16), vc, preferred_element_type=F32))
        psum = psum + pc
    o_ref[0] = jnp.concatenate(outs, axis=1)
    phi, plo = _split_bf16(psum)
    ovt = ovt_ref[...]
    imp = (lax.dot_general(ovt, phi, _NT, preferred_element_type=F32)
           + lax.dot_general(ovt, plo, _NT, preferred_element_type=F32))
    jblk = lax.broadcasted_iota(jnp.int32, (nsb, tq), 0)
    cur = jnp.right_shift(t0 + lax.broadcasted_iota(jnp.int32, (nsb, tq), 1), SLC_SHIFT)
    forced = (jblk == 0) | (jblk == cur) | (jblk == cur - 1)
    score = jnp.where(jblk > cur, -1.0, jnp.where(forced, SELECT_FORCE, imp))
    rank = jnp.zeros((nsb, tq), F32)
    for k in range(n_real):
        rowk = score[k:k + 1, :]
        ge = jnp.where(rowk >= score, 1.0, 0.0)
        gt = jnp.where(rowk > score, 1.0, 0.0)
        rank = rank + jnp.where(jblk > k, ge, gt)
    sel_ref[0, 0] = jnp.where(rank < n_sel, 1.0, 0.0)


def _nsa_cmp(q, kcmp, vcmp, batch, seq, q_off_blocks, tq=128):
    g = C_KV_HEADS
    rep = C_HEADS // g
    nc = kcmp.shape[1]
    nsb = seq // SLC_BLOCK
    n_sel = min(SLC_TOP_N, nsb)
    cs = np.arange(nc)[:, None] * CMP_STRIDE
    js = np.arange(nsb)[None, :] * SLC_BLOCK
    overlap = np.clip(np.minimum(cs + CMP_LEN, js + SLC_BLOCK) - np.maximum(cs, js), 0, None) / CMP_LEN
    overlap[(seq - CMP_LEN) // CMP_STRIDE + 1:] = 0.0
    nsb_pad = -(-nsb // 8) * 8
    ovt = jnp.asarray(np.pad(overlap.T, ((0, nsb_pad - nsb), (0, 0))), BF16)
    q3 = q.reshape(batch, seq, q.shape[1])
    n_real, nsb = nsb, nsb_pad
    o, sel = pl.pallas_call(
        functools.partial(_nsa_cmp_kernel, rep=rep, n_sel=n_sel, n_real=n_real),
        grid=(batch, g, seq // tq),
        in_specs=[pl.BlockSpec((1, tq, rep * LANES), lambda b, gi, i: (b, i, q_off_blocks // rep + gi)),
                  pl.BlockSpec((1, nc, LANES), lambda b, gi, i: (b, 0, gi)),
                  pl.BlockSpec((1, nc, LANES), lambda b, gi, i: (b, 0, gi)),
                  pl.BlockSpec(ovt.shape, lambda b, gi, i: (0, 0))],
        out_specs=[pl.BlockSpec((1, tq, rep * LANES), lambda b, gi, i: (b, i, gi)),
                   pl.BlockSpec((1, 1, nsb, tq), lambda b, gi, i: (b, gi, 0, i))],
        out_shape=[jax.ShapeDtypeStruct((batch, seq, C_HEADS * LANES), F32),
                   jax.ShapeDtypeStruct((batch, g, nsb, seq), F32)],
        compiler_params=_cparams("parallel", "parallel", "parallel"),
        name="nsa_compressed_select",
    )(q3, kcmp, vcmp, ovt)
    return o.reshape(batch * seq, C_HEADS * LANES), sel


def _nsa_slc_kernel(q_ref, k_ref, vt_ref, selt_ref, o_ref, bias_sc, *, rep, kt):
    tq = q_ref.shape[1]
    nsb = selt_ref.shape[2]
    t0 = pl.program_id(2) * tq
    n_kt = (t0 + tq + kt - 1) // kt
    selt = selt_ref[0, 0].astype(BF16)
    key_blk = jnp.right_shift(lax.broadcasted_iota(jnp.int32, (kt, nsb), 0), SLC_SHIFT)
    blk = lax.broadcasted_iota(jnp.int32, (kt, nsb), 1)
    kpos = lax.broadcasted_iota(jnp.int32, (kt, tq), 0)
    tpos = t0 + lax.broadcasted_iota(jnp.int32, (kt, tq), 1)

    def make_bias(j, c):
        k0 = pl.multiple_of(j * kt, kt)
        expand = jnp.where(key_blk == blk - jnp.right_shift(k0, SLC_SHIFT), 1.0, 0.0).astype(BF16)
        chosen = jnp.dot(expand, selt, preferred_element_type=F32)
        ok = (chosen > 0.5) & (k0 + kpos <= tpos)
        bias_sc[pl.ds(k0, kt), :] = jnp.where(ok, 0.0, NEG)
        return c

    lax.fori_loop(0, n_kt, make_bias, 0)

    outs = []
    pair = rep
    for r0 in range(0, rep, pair):
        qs = [q_ref[0, :, r * LANES:(r + 1) * LANES] * (HEAD_DIM ** -0.5)
              for r in range(r0, r0 + pair)]

        def tile(j, carry, qs=qs):
            k0 = pl.multiple_of(j * kt, kt)
            k = k_ref[0, pl.ds(k0, kt), :]
            vt = vt_ref[:, pl.ds(k0, kt)]
            bias = bias_sc[pl.ds(k0, kt), :]
            new = []
            for qr, (m, l, acc) in zip(qs, carry):
                s = lax.dot_general(k, qr, _NT, preferred_element_type=F32) + bias
                m_new = jnp.maximum(m, jnp.max(s, axis=0, keepdims=True))
                e = jnp.exp(s - m_new)
                corr = jnp.exp(m - m_new)
                l = l * corr + jnp.sum(e, axis=0, keepdims=True)
                acc = acc * corr + jnp.dot(vt, e.astype(BF16), preferred_element_type=F32)
                new.append((m_new, l, acc))
            return tuple(new)

        init = tuple((jnp.full((1, tq), NEG, F32), jnp.zeros((1, tq), F32), jnp.zeros((LANES, tq), F32))
                     for _ in range(pair))
        for _, l, acc in lax.fori_loop(0, n_kt, tile, init):
            outs.append((acc / l).T)
    o_ref[0] = jnp.concatenate(outs, axis=1)


def _nsa_slc(qkv, vt, selt, batch, seq, nblk, q_off, k_off, tq=128, kt=1024):
    g = C_KV_HEADS
    rep = C_HEADS // g
    kt = min(kt, seq)
    nsb = selt.shape[2]
    arr = qkv.reshape(batch, seq, nblk * LANES)
    out = pl.pallas_call(
        functools.partial(_nsa_slc_kernel, rep=rep, kt=kt),
        grid=(batch, g, seq // tq),
        in_specs=[pl.BlockSpec((1, tq, rep * LANES), lambda b, gi, i: (b, i, q_off // rep + gi)),
                  pl.BlockSpec((1, seq, LANES), lambda b, gi, i: (b, 0, k_off + gi)),
                  pl.BlockSpec((LANES, seq), lambda b, gi, i: (gi, b)),
                  pl.BlockSpec((1, 1, nsb, tq), lambda b, gi, i: (b, gi, 0, i))],
        out_specs=pl.BlockSpec((1, tq, rep * LANES), lambda b, gi, i: (b, i, gi)),
        out_shape=jax.ShapeDtypeStruct((batch, seq, C_HEADS * LANES), F32),
        scratch_shapes=[pltpu.VMEM((seq, tq), F32)],
        compiler_params=_cparams("parallel", "parallel", "arbitrary"),
        name="nsa_selected",
    )(arr, arr, vt, selt)
    return out.reshape(batch * seq, C_HEADS * LANES)


def _even_mixer_layer(x2, batch, seq, tabs, w_in, sinks, w_out, g, b):
    d = x2.shape[1]
    n_heads_in = 3 * A_HEADS + B_Q_HEADS + 2 * B_KV_HEADS
    w = _pad_heads_cols(w_in, n_heads_in).astype(BF16)
    rope = [1] * (2 * A_HEADS) + [0] * A_HEADS + [1] * B_Q_HEADS + [1] * B_KV_HEADS + [0] * B_KV_HEADS
    plan = [(0, c, rope[c]) for c in range(n_heads_in)]
    dilations = [dil for _, dil in A_PATTERNS]
    regroup = [dil for dil in dilations if dil > 1]
    n_a_blocks = 3 * A_HEADS
    qkv, *grouped = _proj(x2, w, tabs, plan, [n_heads_in * LANES], [BF16], dilations=regroup, n_dil=n_a_blocks)
    outs = []
    for window, dilation in A_PATTERNS:
        src, nblk = (qkv, n_heads_in) if dilation == 1 else (grouped[regroup.index(dilation)], n_a_blocks)
        outs.append(_band_attention(src, batch=batch, seq=seq, dilation=dilation, nblk=nblk, q_off=0,
                                    k_off=A_HEADS, v_off=2 * A_HEADS, n_q_heads=A_HEADS, rep=1,
                                    max_dist=window // dilation,
                                    heads=min(4, dilation)))
    qb_off = 3 * A_HEADS
    ob = _band_attention(qkv, batch=batch, seq=seq, dilation=1, nblk=n_heads_in, q_off=qb_off,
                         k_off=qb_off + B_Q_HEADS, v_off=qb_off + B_Q_HEADS + B_KV_HEADS, n_q_heads=B_Q_HEADS,
                         rep=B_Q_HEADS // B_KV_HEADS, max_dist=B_WINDOW - 1, sinks=sinks)
    w_o = _pad_heads_rows(w_out, A_HEADS + B_Q_HEADS).astype(BF16)
    return _outproj_even(outs, dilations, ob, x2, w_o, g.reshape(1, d), b.reshape(1, d))


def _odd_mixer_layer(x2, batch, seq, tabs, w_in, cmpk_pos, cmpk_w1, cmpk_w2, cmpv_pos, cmpv_w1, cmpv_w2, w_out, g, b):
    d = x2.shape[1]
    kvw = C_KV_HEADS * HEAD_DIM
    qw = C_HEADS * HEAD_DIM
    sizes = [qw] + [kvw] * 6 + [3 * C_HEADS]
    offs = np.concatenate([[0], np.cumsum(sizes)])
    wq, wkc, wvc, wks, wvs, wkw, wvw, wgt = [w_in[:, offs[i]:offs[i + 1]] for i in range(8)]
    ph = lambda wpart, n: _pad_heads_cols(wpart, n)
    w = jnp.concatenate([ph(wq, C_HEADS), ph(wks, C_KV_HEADS), ph(wkw, C_KV_HEADS), ph(wvw, C_KV_HEADS),
                         wkc, wvc, jnp.pad(wgt, ((0, 0), (0, LANES - 3 * C_HEADS)))], axis=1).astype(BF16)
    wvs_t = ph(wvs, C_KV_HEADS).T.astype(BF16)
    n16 = C_HEADS + 3 * C_KV_HEADS
    rope16 = [1] * C_HEADS + [1] * C_KV_HEADS + [1] * C_KV_HEADS + [0] * C_KV_HEADS
    n_kc = kvw // LANES
    plan = ([(0, c, rope16[c]) for c in range(n16)] + [(1, c, 2) for c in range(n_kc)]
            + [(2, c, 0) for c in range(n_kc)] + [(3, 0, 0)])
    qkv, kc, vc, gate, vs_t = _proj(x2, w, tabs, plan, [n16 * LANES, kvw, kvw, LANES], [BF16, F32, F32, F32],
                                    wt=wvs_t)
    kcmp = _compress(kc, cmpk_pos, cmpk_w1, cmpk_w2, batch, seq)
    vcmp = _compress(vc, cmpv_pos, cmpv_w1, cmpv_w2, batch, seq)
    o_cmp, selt = _nsa_cmp(qkv, kcmp, vcmp, batch, seq, 0)
    ks_off = C_HEADS
    o_slc = _nsa_slc(qkv, vs_t, selt, batch, seq, n16, 0, ks_off)
    kw_off = ks_off + C_KV_HEADS
    o_win = _band_attention(qkv, batch=batch, seq=seq, dilation=1, nblk=n16, q_off=0, k_off=kw_off,
                            v_off=kw_off + C_KV_HEADS, n_q_heads=C_HEADS, rep=C_HEADS // C_KV_HEADS,
                            max_dist=NSA_WINDOW - 1)
    w_o = _pad_heads_rows(w_out, C_HEADS).astype(BF16)
    return _outproj_odd(o_cmp, o_slc, o_win, gate, x2, w_o, g.reshape(1, d), b.reshape(1, d))


def kernel(x, positions, even_w_in, even_sinks, even_w_out, odd_w_in, odd_cmpk_pos, odd_cmpk_w1, odd_cmpk_w2, odd_cmpv_pos, odd_cmpv_w1, odd_cmpv_w2, odd_w_out, mix_ln_g, mix_ln_b, moe_router_w, moe_router_b, moe_w_gate, moe_w_up, moe_w_down, moe_sh_gate, moe_sh_up, moe_sh_down, ffn_ln_g, ffn_ln_b):
    batch, seq, d = x.shape
    x2 = x.reshape(batch * seq, d)
    tabs = _rope_tables(positions)
    depth = mix_ln_g.shape[0]
    for layer in range(depth):
        j = layer // 2
        if layer % 2 == 0:
            x2 = _even_mixer_layer(x2, batch, seq, tabs, even_w_in[j], even_sinks[j], even_w_out[j],
                                   mix_ln_g[layer], mix_ln_b[layer])
        else:
            x2 = _odd_mixer_layer(x2, batch, seq, tabs, odd_w_in[j], odd_cmpk_pos[j], odd_cmpk_w1[j], odd_cmpk_w2[j],
                                  odd_cmpv_pos[j], odd_cmpv_w1[j], odd_cmpv_w2[j], odd_w_out[j],
                                  mix_ln_g[layer], mix_ln_b[layer])
        x2 = _moe_layer(x2, moe_router_w[layer], moe_router_b[layer], moe_w_gate, moe_w_up, moe_w_down, layer,
                        moe_sh_gate[layer], moe_sh_up[layer], moe_sh_down[layer],
                        ffn_ln_g[layer].reshape(1, d), ffn_ln_b[layer].reshape(1, d))
    return x2.reshape(batch, seq, d)
```

```python
import functools

import numpy as np
import jax
import jax.numpy as jnp
from jax import lax
from jax.experimental import pallas as pl
from jax.experimental.pallas import tpu as pltpu

F32 = jnp.float32
BF16 = jnp.bfloat16

LANES = 128
HEAD_DIM = 64
ROT_DIM = HEAD_DIM // 4
ROT_HALF = ROT_DIM // 2
ROPE_THETA = 500000.0
QBLK = 128
A_HEADS = 8
A_PATTERNS = ((128, 1), (512, 4), (2048, 16))
B_Q_HEADS = 8
B_KV_HEADS = 2
B_WINDOW = 128
C_HEADS = 16
C_KV_HEADS = 4
CMP_LEN = 32
CMP_STRIDE = 16
CMP_HIDDEN = 2 * HEAD_DIM
SLC_BLOCK = 64
SLC_SHIFT = 6
SLC_TOP_N = 16
NSA_WINDOW = 512
SELECT_FORCE = 1.0e4
N_EXPERTS = 256
TOP_K = 8
N_GROUPS = 8
TOPK_GROUPS = 4
ROUTED_SCALE = 2.5
MOE_BLOCK = 256
DEPTH = 2
DEEPNORM_ALPHA = (2 * DEPTH) ** 0.25
LN_EPS = 1e-5
NEG = -1.0e30
VMEM_LIMIT = 56 * 1024 * 1024

_NT = (((1,), (1,)), ((), ()))


def _cparams(*sem):
    return pltpu.CompilerParams(dimension_semantics=sem, vmem_limit_bytes=VMEM_LIMIT)


def _split_bf16(a):
    hi = a.astype(BF16)
    lo = (a - hi.astype(F32)).astype(BF16)
    return hi, lo


def _layer_norm(y, g, b):
    mu = jnp.mean(y, axis=-1, keepdims=True)
    d = y - mu
    var = jnp.mean(d * d, axis=-1, keepdims=True)
    return d * lax.rsqrt(var + LN_EPS) * g + b


def _silu(a):
    return a * jax.nn.sigmoid(a)


TAG_BLOCK = 3


def _proj_kernel(*refs, plan, n_main, transposed, dilations, n_dil, seq):
    n_in = 4 if transposed else 3
    x_ref, w_ref, tab_ref = refs[:3]
    out_refs = refs[n_in:n_in + n_main]
    extra = list(refs[n_in + n_main:])
    x = x_ref[...].astype(BF16)
    tm = x.shape[0]
    if transposed:
        out_t = extra.pop(0)
        out_t[...] = lax.dot_general(refs[3][...], x, _NT, preferred_element_type=F32).astype(out_t.dtype)
    dil_refs = [extra.pop(0) for _ in dilations]
    stage = extra.pop(0) if dilations else None
    nblk = len(plan)
    for c0 in range(0, nblk, 2):
        nb = min(2, nblk - c0)
        acc = jnp.dot(x, w_ref[:, c0 * LANES:(c0 + nb) * LANES], preferred_element_type=F32)
        for j in range(nb):
            blk = acc[:, j * LANES:(j + 1) * LANES]
            dst, dblk, mode = plan[c0 + j]
            if mode == TAG_BLOCK:
                pos = lax.rem(pl.program_id(0) * tm, seq) + lax.broadcasted_iota(jnp.int32, (tm, LANES), 0)
                tag = lax.broadcasted_iota(jnp.int32, (tm, LANES), 1) - HEAD_DIM == jnp.right_shift(pos, SLC_SHIFT)
                mode = 1
            else:
                tag = None
            if mode:
                off = (mode - 1) * 3 * LANES
                cos = tab_ref[:, off:off + LANES]
                s_lo = tab_ref[:, off + LANES:off + 2 * LANES]
                s_hi = tab_ref[:, off + 2 * LANES:off + 3 * LANES]
                blk = (blk * cos + pltpu.roll(blk, LANES - ROT_HALF, 1) * s_lo
                       + pltpu.roll(blk, ROT_HALF, 1) * s_hi)
            if tag is not None:
                blk = jnp.where(tag, NEG, blk)
            o_ref = out_refs[dst]
            o_ref[:, dblk * LANES:(dblk + 1) * LANES] = blk.astype(o_ref.dtype)
            c = c0 + j
            if dilations and c < n_dil:
                stage[...] = blk
                for dil, d_ref in zip(dilations, dil_refs):
                    for r in range(dil):
                        col = (r * n_dil + c) * LANES
                        d_ref[:, col:col + LANES] = stage[pl.ds(r, tm // dil, stride=dil), :].astype(d_ref.dtype)


def _proj(x2, w, tabs, plan, out_cols, out_dtypes, wt=None, dilations=(), n_dil=0, seq=0, tm=512):
    t, d = x2.shape
    ncol = w.shape[1]
    out_shape = [jax.ShapeDtypeStruct((t, c), dt) for c, dt in zip(out_cols, out_dtypes)]
    once = pl.Buffered(1)
    in_specs = [pl.BlockSpec((tm, d), lambda i: (i, 0)),
                pl.BlockSpec((d, ncol), lambda i: (0, 0), pipeline_mode=once),
                pl.BlockSpec((tm, tabs.shape[1]), lambda i: (i, 0))]
    out_specs = [pl.BlockSpec((tm, c), lambda i: (i, 0)) for c in out_cols]
    args = [x2, w, tabs]
    if wt is not None:
        in_specs.append(pl.BlockSpec(wt.shape, lambda i: (0, 0), pipeline_mode=once))
        out_specs.append(pl.BlockSpec((wt.shape[0], tm), lambda i: (0, i)))
        out_shape.append(jax.ShapeDtypeStruct((wt.shape[0], t), BF16))
        args.append(wt)
    for dil in dilations:
        out_specs.append(pl.BlockSpec((tm // dil, dil * n_dil * LANES), lambda i: (i, 0)))
        out_shape.append(jax.ShapeDtypeStruct((t // dil, dil * n_dil * LANES), BF16))
    return pl.pallas_call(
        functools.partial(_proj_kernel, plan=tuple(plan), n_main=len(out_cols), transposed=wt is not None,
                          dilations=tuple(dilations), n_dil=n_dil, seq=seq),
        grid=(t // tm,),
        in_specs=in_specs,
        out_specs=out_specs,
        out_shape=out_shape,
        scratch_shapes=[pltpu.VMEM((tm, LANES), F32)] if dilations else [],
        compiler_params=_cparams("parallel"),
        name="proj_rope",
    )(*args)


def _rope_tables(positions):
    t = positions.size
    inv_freq = jnp.asarray(ROPE_THETA ** (-np.arange(0, ROT_DIM, 2) / ROT_DIM), F32)
    ang = positions.astype(F32).reshape(t, 1) * inv_freq
    trig = jnp.concatenate([jnp.cos(ang), jnp.sin(ang)], axis=1)
    place = np.zeros((2 * ROT_HALF, 6 * LANES), np.float32)
    const = np.zeros((6 * LANES,), np.float32)
    for base, heads in ((0, (0,)), (3 * LANES, (0, HEAD_DIM))):
        const[base:base + LANES] = 1.0
        for h in heads:
            for i in range(ROT_HALF):
                place[i, base + h + i] = 1.0
                place[i, base + h + ROT_HALF + i] = 1.0
                const[base + h + i] = const[base + h + ROT_HALF + i] = 0.0
                place[ROT_HALF + i, base + LANES + h + i] = -1.0
                place[ROT_HALF + i, base + 2 * LANES + h + ROT_HALF + i] = 1.0
    return jnp.dot(trig, jnp.asarray(place), precision=lax.Precision.HIGHEST) + jnp.asarray(const)


def _pad_heads_cols(w, n_heads):
    d = w.shape[0]
    w = w.reshape(d, n_heads, HEAD_DIM)
    return jnp.pad(w, ((0, 0), (0, 0), (0, LANES - HEAD_DIM))).reshape(d, n_heads * LANES)


def _pad_heads_rows(w, n_heads):
    d = w.shape[1]
    w = w.reshape(n_heads, HEAD_DIM, d)
    return jnp.pad(w, ((0, 0), (0, LANES - HEAD_DIM), (0, 0))).reshape(n_heads * LANES, d)


def _band_kernel(*refs, back, max_dist, length, qrows, heads, shared_kv, in_flight, has_sink):
    if has_sink:
        sink_ref, q_ref, k_ref, v_ref, o_ref, bias_sc = refs
    else:
        q_ref, k_ref, v_ref, o_ref, bias_sc = refs
    nq = length // qrows
    window = qrows + back
    lane = lax.broadcasted_iota(jnp.int32, (qrows, LANES), 1)
    first_head = pl.program_id(2) * heads

    def band_bias(offset):
        row = lax.broadcasted_iota(jnp.int32, (qrows, window), 0)
        col = lax.broadcasted_iota(jnp.int32, (qrows, window), 1)
        dist = offset + row - col
        return jnp.where((dist >= 0) & (dist <= max_dist), 0.0, NEG)

    bias_sc[...] = band_bias(back)

    def block(h, q0, k0, bias):
        hs = slice(h * LANES, (h + 1) * LANES)
        kvs = slice(0, LANES) if shared_kv else hs
        q = q_ref[0, pl.ds(q0, qrows), hs] * (HEAD_DIM ** -0.5)
        k = k_ref[0, pl.ds(k0, window), kvs]
        v = v_ref[0, pl.ds(k0, window), kvs]
        s = lax.dot_general(q, k, _NT, preferred_element_type=F32) + bias
        m = jnp.max(s, axis=-1, keepdims=True)
        e = jnp.exp(s - m)
        den = jnp.sum(e, axis=-1, keepdims=True)
        if has_sink:
            den = den + jnp.exp(sink_ref[first_head + h] - m)
        o = jnp.dot(e.astype(BF16), v, preferred_element_type=F32) / den
        lse = m + jnp.log(den)
        o_ref[0, pl.ds(q0, qrows), hs] = jnp.where(lane < HEAD_DIM, o, lse)

    n_clipped = min(-(-back // qrows), nq)
    for qi in range(n_clipped):
        bias = band_bias(qi * qrows)
        for h in range(heads):
            block(h, qi * qrows, 0, bias)

    steady = nq - n_clipped
    per_iter = max(1, min(in_flight // heads, steady))
    n_iter = steady // per_iter

    def body(it, carry):
        aligned = lambda v: v if isinstance(v, int) else pl.multiple_of(v, QBLK)
        for j in range(per_iter):
            q0 = aligned((n_clipped + it * per_iter + j) * qrows)
            for h in range(heads):
                block(h, q0, aligned(q0 - back), bias_sc[...])
        return carry

    if n_iter == 1:
        body(0, 0)
    elif n_iter:
        lax.fori_loop(0, n_iter, body, 0)
    for qi in range(n_clipped + n_iter * per_iter, nq):
        for h in range(heads):
            block(h, qi * qrows, qi * qrows - back, bias_sc[...])


def _band_attention(qkv, *, batch, seq, dilation, nblk, q_off, k_off, v_off, n_q_heads, rep, max_dist, sinks=None,
                    heads=1, in_flight=16):
    length = seq // dilation
    back = min(-(-max_dist // QBLK) * QBLK, length - QBLK)
    qrows = QBLK
    shared_kv = rep > 1
    assert n_q_heads % heads == 0 and (rep == 1 or rep % heads == 0)
    kv_heads = 1 if shared_kv else heads
    arr = qkv.reshape(batch, length, dilation * nblk * LANES)
    assert q_off % heads == 0 and k_off % kv_heads == 0 and v_off % kv_heads == 0 and nblk % heads == 0
    qspec = pl.BlockSpec((1, length, heads * LANES),
                         lambda b, r, h: (b, 0, (r * nblk + q_off) // heads + h))
    kvspec = lambda off: pl.BlockSpec(
        (1, length, kv_heads * LANES),
        lambda b, r, h: (b, 0, (r * nblk + off) // kv_heads + (h * heads // rep if shared_kv else h)))
    in_specs = [qspec, kvspec(k_off), kvspec(v_off)]
    args = [arr, arr, arr]
    if sinks is not None:
        in_specs = [pl.BlockSpec(memory_space=pltpu.SMEM)] + in_specs
        args = [sinks.reshape(-1).astype(F32)] + args
    out = pl.pallas_call(
        functools.partial(_band_kernel, back=back, max_dist=max_dist, length=length, qrows=qrows, heads=heads,
                          shared_kv=shared_kv, in_flight=in_flight, has_sink=sinks is not None),
        grid=(batch, dilation, n_q_heads // heads),
        in_specs=in_specs,
        out_specs=pl.BlockSpec((1, length, heads * LANES), lambda b, r, h: (b, 0, r * (n_q_heads // heads) + h)),
        out_shape=jax.ShapeDtypeStruct((batch, length, dilation * n_q_heads * LANES), F32),
        scratch_shapes=[pltpu.VMEM((qrows, qrows + back), F32)],
        compiler_params=_cparams("parallel", "parallel", "parallel"),
        name="band_attention",
    )(*args)
    return out.reshape(batch * length, dilation * n_q_heads * LANES)


def _outproj_even_kernel(o1_ref, o2_ref, o3_ref, ob_ref, x_ref, w_ref, g_ref, b_ref, out_ref, *nat_refs,
                         n_a, n_b, dilations):
    tm = x_ref.shape[0]
    lane = lax.broadcasted_iota(jnp.int32, (tm, LANES), 1)
    real = lane < HEAD_DIM
    pattern_refs = []
    nat_refs = list(nat_refs)
    for o_ref, dil in zip((o1_ref, o2_ref, o3_ref), dilations):
        if dil == 1:
            pattern_refs.append(o_ref)
            continue
        nat = nat_refs.pop(0)
        for r in range(dil):
            for h in range(n_a):
                col = (r * n_a + h) * LANES
                nat[h, pl.ds(r, tm // dil, stride=dil), :] = o_ref[:, col:col + LANES]
        pattern_refs.append(nat)
    parts = []
    for h in range(n_a):
        sl = slice(h * LANES, (h + 1) * LANES)
        outs = [r[:, sl] if r.ndim == 2 else r[h] for r in pattern_refs]
        lses = [jnp.where(real, pltpu.roll(a, HEAD_DIM, 1), a) for a in outs]
        m = jnp.maximum(jnp.maximum(lses[0], lses[1]), lses[2])
        es = [jnp.exp(l - m) for l in lses]
        num = es[0] * outs[0] + es[1] * outs[1] + es[2] * outs[2]
        den = es[0] + es[1] + es[2]
        parts.append(jnp.where(real, num / den, 0.0).astype(BF16))
    for h in range(n_b):
        parts.append(jnp.where(real, ob_ref[:, h * LANES:(h + 1) * LANES], 0.0).astype(BF16))
    a = jnp.concatenate(parts, axis=1)
    mixed = jnp.dot(a, w_ref[...], preferred_element_type=F32)
    y = DEEPNORM_ALPHA * x_ref[...] + mixed
    out_ref[...] = _layer_norm(y, g_ref[...], b_ref[...])


def _outproj_even(outs, dilations, ob, x2, w, g, b, tm=256):
    t, d = x2.shape
    n_a, n_b = outs[0].shape[1] // (dilations[0] * LANES), ob.shape[1] // LANES
    row = lambda c: pl.BlockSpec((tm, c), lambda i: (i, 0))
    grouped = lambda a, dil: pl.BlockSpec((tm // dil, a.shape[1]), lambda i: (i, 0))
    full = lambda a: pl.BlockSpec(a.shape, lambda i: (0,) * a.ndim)
    return pl.pallas_call(
        functools.partial(_outproj_even_kernel, n_a=n_a, n_b=n_b, dilations=tuple(dilations)),
        grid=(t // tm,),
        in_specs=[grouped(o, dil) for o, dil in zip(outs, dilations)] + [row(ob.shape[1]), row(d),
                                                                          full(w), full(g), full(b)],
        out_specs=row(d),
        out_shape=jax.ShapeDtypeStruct((t, d), F32),
        scratch_shapes=[pltpu.VMEM((n_a, tm, LANES), F32) for dil in dilations if dil > 1],
        compiler_params=_cparams("parallel"),
        name="outproj_even_ln",
    )(*outs, ob, x2, w, g, b)


def _outproj_odd_kernel(oc_ref, os_ref, ow_ref, gate_ref, e_ref, x_ref, w_ref, g_ref, b_ref, out_ref):
    gate = jax.nn.sigmoid(gate_ref[...])
    ghi, glo = _split_bf16(gate)
    acc = None
    for j, o_ref in enumerate((oc_ref, os_ref, ow_ref)):
        ej = e_ref[j]
        gfull = jnp.dot(ghi, ej, preferred_element_type=F32) + jnp.dot(glo, ej, preferred_element_type=F32)
        term = gfull * o_ref[...]
        acc = term if acc is None else acc + term
    mixed = jnp.dot(acc.astype(BF16), w_ref[...], preferred_element_type=F32)
    y = DEEPNORM_ALPHA * x_ref[...] + mixed
    out_ref[...] = _layer_norm(y, g_ref[...], b_ref[...])


def _gate_expanders(n_heads):
    e = np.zeros((3, LANES, n_heads * LANES), np.float32)
    for j in range(3):
        for h in range(n_heads):
            e[j, 3 * h + j, h * LANES:h * LANES + HEAD_DIM] = 1.0
    return jnp.asarray(e, BF16)


def _outproj_odd(oc, osl, ow, gate, x2, w, g, b, tm=256):
    t, d = x2.shape
    n_heads = oc.shape[1] // LANES
    e = _gate_expanders(n_heads)
    row = lambda c: pl.BlockSpec((tm, c), lambda i: (i, 0))
    full = lambda a: pl.BlockSpec(a.shape, lambda i: (0,) * a.ndim)
    return pl.pallas_call(
        _outproj_odd_kernel,
        grid=(t // tm,),
        in_specs=[row(oc.shape[1]), row(osl.shape[1]), row(ow.shape[1]), row(LANES), full(e), row(d),
                  full(w), full(g), full(b)],
        out_specs=row(d),
        out_shape=jax.ShapeDtypeStruct((t, d), F32),
        compiler_params=_cparams("parallel"),
        name="outproj_odd_ln",
    )(oc, osl, ow, gate, e, x2, w, g, b)


def _router_kernel(x_ref, whi_ref, wlo_ref, bias_ref, eidx_ref, gate_ref, rank_ref, cnt_ref):
    n_exp = whi_ref.shape[0]
    tm = x_ref.shape[0]
    per_group = n_exp // N_GROUPS
    xhi, xlo = _split_bf16(x_ref[...])
    whi, wlo = whi_ref[...], wlo_ref[...]
    dg = lambda a, b: lax.dot_general(a, b, _NT, preferred_element_type=F32)
    logits = dg(whi, xhi) + dg(whi, xlo) + dg(wlo, xhi)
    aff = jax.nn.sigmoid(logits)
    biased = aff + bias_ref[...]
    gio = lax.broadcasted_iota(jnp.int32, (per_group, tm), 0).astype(F32)
    blocks, scores = [], []
    for g in range(N_GROUPS):
        blk = biased[g * per_group:(g + 1) * per_group, :]
        m1 = jnp.max(blk, axis=0, keepdims=True)
        first = jnp.min(jnp.where(blk == m1, gio, float(per_group)), axis=0, keepdims=True)
        m2 = jnp.max(jnp.where(gio == first, -jnp.inf, blk), axis=0, keepdims=True)
        blocks.append(blk)
        scores.append(m1 + m2)
    masked = []
    for g in range(N_GROUPS):
        rank = jnp.zeros((1, tm), F32)
        for o in range(N_GROUPS):
            if o == g:
                continue
            beats = scores[o] >= scores[g] if o < g else scores[o] > scores[g]
            rank = rank + jnp.where(beats, 1.0, 0.0)
        masked.append(jnp.where(rank < TOPK_GROUPS, blocks[g], -jnp.inf))
    cur = jnp.concatenate(masked, axis=0)
    eio = lax.broadcasted_iota(jnp.int32, (n_exp, tm), 0).astype(F32)
    ids, gs = [], []
    for _ in range(TOP_K):
        m = jnp.max(cur, axis=0, keepdims=True)
        idx = jnp.min(jnp.where(cur == m, eio, float(n_exp)), axis=0, keepdims=True)
        hit = eio == idx
        gs.append(jnp.sum(jnp.where(hit, aff, 0.0), axis=0, keepdims=True))
        ids.append(idx)
        cur = jnp.where(hit, -jnp.inf, cur)
    gates = jnp.concatenate(gs, axis=0)
    gates = gates / jnp.sum(gates, axis=0, keepdims=True) * ROUTED_SCALE
    eidx_ref[...] = jnp.concatenate(ids, axis=0).astype(jnp.int32)
    gate_ref[...] = gates
    @pl.when(pl.program_id(0) == 0)
    def _():
        cnt_ref[...] = jnp.zeros(cnt_ref.shape, F32)

    onehot = jnp.zeros((n_exp, tm), F32)
    for idx in ids:
        onehot = onehot + jnp.where(eio == idx, 1.0, 0.0)
    earlier = jnp.where(lax.broadcasted_iota(jnp.int32, (tm, tm), 0) < lax.broadcasted_iota(jnp.int32, (tm, tm), 1),
                        1.0, 0.0).astype(BF16)
    before = cnt_ref[...] + jnp.dot(onehot.astype(BF16), earlier, preferred_element_type=F32)
    ranks = [jnp.sum(jnp.where(eio == idx, before, 0.0), axis=0, keepdims=True) for idx in ids]
    rank_ref[...] = jnp.concatenate(ranks, axis=0).astype(jnp.int32)
    cnt_ref[...] = cnt_ref[...] + jnp.sum(onehot, axis=1, keepdims=True)


def _router(x2, router_w, router_b, tm=256):
    t, d = x2.shape
    n_exp = router_w.shape[1]
    whi, wlo = _split_bf16(router_w.T)
    bias = router_b.reshape(n_exp, 1).astype(F32)
    full = lambda a: pl.BlockSpec(a.shape, lambda i: (0,) * a.ndim)
    per_tok = pl.BlockSpec((TOP_K, tm), lambda i: (0, i))
    return pl.pallas_call(
        _router_kernel,
        grid=(t // tm,),
        in_specs=[pl.BlockSpec((tm, d), lambda i: (i, 0)), full(whi), full(wlo), full(bias)],
        out_specs=[per_tok, per_tok, per_tok, pl.BlockSpec((n_exp, 1), lambda i: (0, 0))],
        out_shape=[jax.ShapeDtypeStruct((TOP_K, t), jnp.int32), jax.ShapeDtypeStruct((TOP_K, t), F32),
                   jax.ShapeDtypeStruct((TOP_K, t), jnp.int32), jax.ShapeDtypeStruct((n_exp, 1), F32)],
        compiler_params=_cparams("arbitrary"),
        name="moe_router",
    )(x2, whi, wlo, bias)


def _moe_dest_kernel(eidx_ref, rank_ref, start_ref, dest_ref):
    n_exp = start_ref.shape[0]
    tm = eidx_ref.shape[1]
    eio = lax.broadcasted_iota(jnp.int32, (n_exp, tm), 0)
    start = start_ref[...]
    rows = []
    for k in range(TOP_K):
        seg = jnp.sum(jnp.where(eio == eidx_ref[k:k + 1, :], start, 0.0), axis=0, keepdims=True)
        rows.append(seg.astype(jnp.int32) + rank_ref[k:k + 1, :])
    dest_ref[...] = jnp.concatenate(rows, axis=0)


def _moe_dest(eidx, rank, seg_start, tm=256):
    t = eidx.shape[1]
    per_tok = pl.BlockSpec((TOP_K, tm), lambda i: (0, i))
    return pl.pallas_call(
        _moe_dest_kernel,
        grid=(t // tm,),
        in_specs=[per_tok, per_tok, pl.BlockSpec(seg_start.shape, lambda i: (0, 0))],
        out_specs=per_tok,
        out_shape=jax.ShapeDtypeStruct((TOP_K, t), jnp.int32),
        compiler_params=_cparams("parallel"),
        name="moe_dest",
    )(eidx, rank, seg_start)


def _moe_zero_kernel(blk_ref, xs_ref):
    xs_ref[...] = jnp.zeros(xs_ref.shape, xs_ref.dtype)


def _moe_zero_padding(zero_blocks, n_blocks, width):
    grid_spec = pltpu.PrefetchScalarGridSpec(
        num_scalar_prefetch=1,
        grid=(zero_blocks.shape[0],),
        in_specs=[],
        out_specs=pl.BlockSpec((MOE_BLOCK, width), lambda i, blk: (blk[i], 0)),
    )
    return pl.pallas_call(
        _moe_zero_kernel,
        grid_spec=grid_spec,
        out_shape=jax.ShapeDtypeStruct((n_blocks * MOE_BLOCK, width), jnp.uint32),
        compiler_params=_cparams("arbitrary"),
        name="moe_zero_padding",
    )(zero_blocks)


def _pack_bf16_pairs(x):
    half = x.shape[1] // 2
    lo = lax.bitcast_convert_type(x[:, :half].astype(BF16).astype(F32), jnp.uint32)
    hi = lax.bitcast_convert_type(x[:, half:].astype(BF16).astype(F32), jnp.uint32)
    return jnp.right_shift(lo, jnp.uint32(16)) | (hi & jnp.uint32(0xFFFF0000))


def _unpack_bf16_pairs(w):
    lo = lax.bitcast_convert_type(jnp.left_shift(w, jnp.uint32(16)), F32).astype(BF16)
    hi = lax.bitcast_convert_type(w & jnp.uint32(0xFFFF0000), F32).astype(BF16)
    return jnp.concatenate([lo, hi], axis=1)


def _moe_dispatch_kernel(dest_ref, x_ref, xs_in, xs_out, buf, sem):
    del xs_in
    tm = x_ref.shape[0]
    buf[...] = _pack_bf16_pairs(x_ref[...])

    def issue(i, c):
        for k in range(TOP_K):
            r = dest_ref[k, i]
            pltpu.make_async_copy(buf.at[pl.ds(i, 1)], xs_out.at[pl.ds(r, 1)], sem).start()
        return c

    lax.fori_loop(0, tm, issue, 0)
    for _ in range(TOP_K):
        pltpu.make_async_copy(buf, xs_out.at[pl.ds(0, tm)], sem).wait()


def _moe_dispatch(dest, x2, xs, tm=256):
    t, d = x2.shape
    return pl.pallas_call(
        _moe_dispatch_kernel,
        grid=(t // tm,),
        in_specs=[pl.BlockSpec((TOP_K, tm), lambda i: (0, i), memory_space=pltpu.SMEM),
                  pl.BlockSpec((tm, d), lambda i: (i, 0)),
                  pl.BlockSpec(memory_space=pl.ANY)],
        out_specs=pl.BlockSpec(memory_space=pl.ANY),
        out_shape=jax.ShapeDtypeStruct(xs.shape, xs.dtype),
        scratch_shapes=[pltpu.VMEM((tm, d // 2), jnp.uint32), pltpu.SemaphoreType.DMA(())],
        input_output_aliases={2: 0},
        compiler_params=_cparams("arbitrary"),
        name="moe_dispatch",
    )(dest, x2, xs)


def _moe_ffn_kernel(be_ref, nu_ref, xs_ref, wg_ref, wu_ref, wd_ref, y_ref, wg_sc, wu_sc, wd_sc):
    b = pl.program_id(0)

    @pl.when(b < nu_ref[0])
    def _():
        @pl.when((b == 0) | (be_ref[b] != be_ref[jnp.maximum(b - 1, 0)]))
        def _():
            wg_sc[...] = wg_ref[0].astype(BF16)
            wu_sc[...] = wu_ref[0].astype(BF16)
            wd_sc[...] = wd_ref[0].astype(BF16)

        xb = _unpack_bf16_pairs(xs_ref[...])
        gp = jnp.dot(xb, wg_sc[...], preferred_element_type=F32)
        up = jnp.dot(xb, wu_sc[...], preferred_element_type=F32)
        h = (_silu(gp) * up).astype(BF16)
        y_ref[...] = jnp.dot(h, wd_sc[...], preferred_element_type=F32)

    @pl.when(b >= nu_ref[0])
    def _():
        y_ref[...] = jnp.zeros(y_ref.shape, y_ref.dtype)


def _moe_ffn(xs, block_expert, n_used, w_gate, w_up, w_down, layer):
    n_blocks = block_expert.shape[0]
    d, ff = w_gate.shape[2], w_gate.shape[3]
    last = lambda b, nu: jnp.minimum(b, nu[0] - 1)
    grid_spec = pltpu.PrefetchScalarGridSpec(
        num_scalar_prefetch=2,
        grid=(n_blocks,),
        in_specs=[
            pl.BlockSpec((MOE_BLOCK, d // 2), lambda b, be, nu: (last(b, nu), 0)),
            pl.BlockSpec((None, 1, d, ff), lambda b, be, nu: (layer, be[last(b, nu)], 0, 0)),
            pl.BlockSpec((None, 1, d, ff), lambda b, be, nu: (layer, be[last(b, nu)], 0, 0)),
            pl.BlockSpec((None, 1, ff, d), lambda b, be, nu: (layer, be[last(b, nu)], 0, 0)),
        ],
        out_specs=pl.BlockSpec((MOE_BLOCK, d), lambda b, be, nu: (b, 0)),
        scratch_shapes=[pltpu.VMEM((d, ff), BF16), pltpu.VMEM((d, ff), BF16), pltpu.VMEM((ff, d), BF16)],
    )
    return pl.pallas_call(
        _moe_ffn_kernel,
        grid_spec=grid_spec,
        out_shape=jax.ShapeDtypeStruct((n_blocks * MOE_BLOCK, d), F32),
        compiler_params=_cparams("arbitrary"),
        name="moe_expert_ffn",
    )(block_expert, n_used, xs, w_gate, w_up, w_down)


def _moe_combine_kernel(dest_ref, y_hbm, gate_ref, x_ref, sg_ref, su_ref, sd_ref, g_ref, b_ref, out_ref, buf, sem):
    tm = x_ref.shape[0]

    def issue(i, c):
        for k in range(TOP_K):
            r = dest_ref[k, i]
            pltpu.make_async_copy(y_hbm.at[pl.ds(r, 1)], buf.at[k, pl.ds(i, 1)], sem).start()
        return c

    lax.fori_loop(0, tm, issue, 0)
    x = x_ref[...]
    xb = x.astype(BF16)
    hs = _silu(jnp.dot(xb, sg_ref[...], preferred_element_type=F32)) * jnp.dot(xb, su_ref[...], preferred_element_type=F32)
    shared = jnp.dot(hs.astype(BF16), sd_ref[...], preferred_element_type=F32)
    for k in range(TOP_K):
        pltpu.make_async_copy(y_hbm.at[pl.ds(0, tm)], buf.at[k], sem).wait()
    gates = gate_ref[...]
    routed = buf[0] * gates[:, 0:1]
    for k in range(1, TOP_K):
        routed = routed + buf[k] * gates[:, k:k + 1]
    y = DEEPNORM_ALPHA * x + (routed + shared)
    out_ref[...] = _layer_norm(y, g_ref[...], b_ref[...])


def _moe_combine(dest, y, gates_t, x2, sh_gate, sh_up, sh_down, g, b, tm=128):
    t, d = x2.shape
    row = lambda c: pl.BlockSpec((tm, c), lambda i: (i, 0))
    full = lambda a: pl.BlockSpec(a.shape, lambda i: (0,) * a.ndim)
    return pl.pallas_call(
        _moe_combine_kernel,
        grid=(t // tm,),
        in_specs=[pl.BlockSpec((TOP_K, tm), lambda i: (0, i), memory_space=pltpu.SMEM),
                  pl.BlockSpec(memory_space=pl.ANY),
                  row(TOP_K), row(d), full(sh_gate), full(sh_up), full(sh_down), full(g), full(b)],
        out_specs=row(d),
        out_shape=jax.ShapeDtypeStruct((t, d), F32),
        scratch_shapes=[pltpu.VMEM((TOP_K, tm, d), F32), pltpu.SemaphoreType.DMA(())],
        compiler_params=_cparams("arbitrary"),
        name="moe_combine_ln",
    )(dest, y, gates_t, x2, sh_gate, sh_up, sh_down, g, b)


def _moe_segments(counts, n_tok):
    n_exp = counts.shape[0]
    n_blocks = -(-n_tok * TOP_K // MOE_BLOCK) + n_exp
    nblk = (counts.reshape(n_exp).astype(jnp.int32) + MOE_BLOCK - 1) // MOE_BLOCK
    blk_end = jnp.cumsum(nblk)
    seg_start = ((blk_end - nblk) * MOE_BLOCK).astype(F32).reshape(n_exp, 1)
    block_expert = jnp.sum((blk_end[None, :] <= jnp.arange(n_blocks)[:, None]).astype(jnp.int32), axis=1)
    block_expert = jnp.minimum(block_expert, n_exp - 1)
    n_used = blk_end[-1]
    tail = jnp.minimum(n_used + jnp.arange(n_exp), n_blocks - 1)
    zero_blocks = jnp.concatenate([jnp.maximum(blk_end - 1, 0), tail]).astype(jnp.int32)
    return seg_start, block_expert.astype(jnp.int32), n_used.astype(jnp.int32).reshape(1), zero_blocks, n_blocks


def _moe_layer(x2, router_w, router_b, w_gate, w_up, w_down, layer, sh_gate, sh_up, sh_down, g, b):
    t, d = x2.shape
    eidx, gates, rank, counts = _router(x2, router_w, router_b)
    seg_start, block_expert, n_used, zero_blocks, n_blocks = _moe_segments(counts, t)
    dest = _moe_dest(eidx, rank, seg_start)
    xs = _moe_dispatch(dest, x2, _moe_zero_padding(zero_blocks, n_blocks, d // 2))
    y = _moe_ffn(xs, block_expert, n_used, w_gate, w_up, w_down, layer)
    return _moe_combine(dest, y, gates.T, x2, sh_gate.astype(BF16), sh_up.astype(BF16), sh_down.astype(BF16), g, b)


def _compress_kernel(x_ref, pa_ref, pb_ref, wa_ref, wb_ref, w2_ref, out_ref):
    x = x_ref[0]
    nc = x.shape[0]
    ha = jnp.dot((x + pa_ref[...]).astype(BF16), wa_ref[...], preferred_element_type=F32)
    hb = jnp.dot((x + pb_ref[...]).astype(BF16), wb_ref[...], preferred_element_type=F32)
    h = ha + pltpu.roll(hb, nc - 1, 0)
    h = jax.nn.gelu(h, approximate=True)
    out_ref[0] = jnp.dot(h.astype(BF16), w2_ref[...], preferred_element_type=F32).astype(out_ref.dtype)


def _compress(kc, pos, w1, w2, batch, seq):
    g = C_KV_HEADS
    nch = seq // CMP_STRIDE
    half = CMP_LEN // 2
    x = kc.reshape(batch, nch, half * g * HEAD_DIM)
    eye = jnp.eye(g, dtype=F32)
    w1r = w1.reshape(CMP_LEN, HEAD_DIM, CMP_HIDDEN)
    expand = lambda wpart: jnp.einsum('jdh,ge->jgdeh', wpart, eye).reshape(half * g * HEAD_DIM, g * CMP_HIDDEN)
    wa, wb = expand(w1r[:half]).astype(BF16), expand(w1r[half:]).astype(BF16)
    w2e = jnp.einsum('hd,ge->ghed', jnp.pad(w2, ((0, 0), (0, LANES - HEAD_DIM))), eye)
    w2e = w2e.reshape(g * CMP_HIDDEN, g * LANES).astype(BF16)
    tile_pos = lambda p: jnp.broadcast_to(p[:, None, :], (half, g, HEAD_DIM)).reshape(1, half * g * HEAD_DIM)
    pa, pb = tile_pos(pos[:half]), tile_pos(pos[half:])
    full = lambda a: pl.BlockSpec(a.shape, lambda i: (0,) * a.ndim)
    return pl.pallas_call(
        _compress_kernel,
        grid=(batch,),
        in_specs=[pl.BlockSpec((1, nch, x.shape[2]), lambda i: (i, 0, 0)), full(pa), full(pb), full(wa), full(wb),
                  full(w2e)],
        out_specs=pl.BlockSpec((1, nch, g * LANES), lambda i: (i, 0, 0)),
        out_shape=jax.ShapeDtypeStruct((batch, nch, g * LANES), BF16),
        compiler_params=_cparams("parallel"),
        name="nsa_compress",
    )(x, pa, pb, wa, wb, w2e)


def _nsa_cmp_kernel(q_ref, kc_ref, vc_ref, ovt_ref, o_ref, drop_ref, *, rep, n_sel, n_real):
    tq = q_ref.shape[1]
    nc = kc_ref.shape[1]
    nsb = ovt_ref.shape[0]
    t0 = pl.program_id(2) * tq
    scale = HEAD_DIM ** -0.5
    kc = kc_ref[0]
    vc = vc_ref[0]
    tpos = t0 + lax.broadcasted_iota(jnp.int32, (tq, nc), 0)
    cend = lax.broadcasted_iota(jnp.int32, (tq, nc), 1) * CMP_STRIDE + (CMP_LEN - 1)
    cmask = cend <= tpos
    psum = jnp.zeros((tq, nc), F32)
    outs = []
    for r in range(rep):
        q = q_ref[0, :, r * LANES:(r + 1) * LANES]
        sc = lax.dot_general(q, kc, _NT, preferred_element_type=F32) * scale
        sc = jnp.where(cmask, sc, NEG)
        m = jnp.max(sc, axis=-1, keepdims=True)
        ex = jnp.where(cmask, jnp.exp(sc - m), 0.0)
        den = jnp.sum(ex, axis=-1, keepdims=True)
        pc = ex / jnp.where(den > 0, den, 1.0)
        outs.append(jnp.dot(pc.astype(BF16), vc, preferred_element_type=F32))
        psum = psum + pc
    o_ref[0] = jnp.concatenate(outs, axis=1)
    phi, plo = _split_bf16(psum)
    ovt = ovt_ref[...]
    imp = (lax.dot_general(ovt, phi, _NT, preferred_element_type=F32)
           + lax.dot_general(ovt, plo, _NT, preferred_element_type=F32))
    jblk = lax.broadcasted_iota(jnp.int32, (nsb, tq), 0)
    cur = jnp.right_shift(t0 + lax.broadcasted_iota(jnp.int32, (nsb, tq), 1), SLC_SHIFT)
    forced = (jblk == 0) | (jblk == cur) | (jblk == cur - 1)
    score = jnp.where(jblk > cur, -1.0, jnp.where(forced, SELECT_FORCE, imp))
    rank = jnp.zeros((nsb, tq), F32)
    for k in range(n_real):
        rowk = score[k:k + 1, :]
        ge = jnp.where(rowk >= score, 1.0, 0.0)
        gt = jnp.where(rowk > score, 1.0, 0.0)
        rank = rank + jnp.where(jblk > k, ge, gt)
    drop = jnp.where(jblk <= cur, jnp.where(rank < n_sel, 0.0, 1.0), 1.0)
    drop = jnp.concatenate([drop, jnp.ones((LANES - nsb, tq), F32)], axis=0).T
    drop_ref[0, 0] = pltpu.roll(drop, HEAD_DIM, 1).astype(drop_ref.dtype)


def _nsa_cmp(q, kcmp, vcmp, batch, seq, q_off_blocks, tq=128):
    g = C_KV_HEADS
    rep = C_HEADS // g
    nc = kcmp.shape[1]
    nsb = seq // SLC_BLOCK
    n_sel = min(SLC_TOP_N, nsb)
    cs = np.arange(nc)[:, None] * CMP_STRIDE
    js = np.arange(nsb)[None, :] * SLC_BLOCK
    overlap = np.clip(np.minimum(cs + CMP_LEN, js + SLC_BLOCK) - np.maximum(cs, js), 0, None) / CMP_LEN
    overlap[(seq - CMP_LEN) // CMP_STRIDE + 1:] = 0.0
    nsb_pad = -(-nsb // 8) * 8
    ovt = jnp.asarray(np.pad(overlap.T, ((0, nsb_pad - nsb), (0, 0))), BF16)
    q3 = q.reshape(batch, seq, q.shape[1])
    n_real, nsb = nsb, nsb_pad
    assert nsb <= LANES - HEAD_DIM
    o, drop = pl.pallas_call(
        functools.partial(_nsa_cmp_kernel, rep=rep, n_sel=n_sel, n_real=n_real),
        grid=(batch, g, seq // tq),
        in_specs=[pl.BlockSpec((1, tq, rep * LANES), lambda b, gi, i: (b, i, q_off_blocks // rep + gi)),
                  pl.BlockSpec((1, nc, LANES), lambda b, gi, i: (b, 0, gi)),
                  pl.BlockSpec((1, nc, LANES), lambda b, gi, i: (b, 0, gi)),
                  pl.BlockSpec(ovt.shape, lambda b, gi, i: (0, 0))],
        out_specs=[pl.BlockSpec((1, tq, rep * LANES), lambda b, gi, i: (b, i, gi)),
                   pl.BlockSpec((1, 1, tq, LANES), lambda b, gi, i: (b, gi, i, 0))],
        out_shape=[jax.ShapeDtypeStruct((batch, seq, C_HEADS * LANES), F32),
                   jax.ShapeDtypeStruct((batch, g, seq, LANES), BF16)],
        compiler_params=_cparams("parallel", "parallel", "parallel"),
        name="nsa_compressed_select",
    )(q3, kcmp, vcmp, ovt)
    return o.reshape(batch * seq, C_HEADS * LANES), drop


def _nsa_slc_kernel(q_ref, k_ref, vt_ref, drop_ref, o_ref, *, rep, kt):
    tq = q_ref.shape[1]
    t0 = pl.program_id(2) * tq
    n_kt = (t0 + tq + kt - 1) // kt
    upper = lax.broadcasted_iota(jnp.int32, (tq, LANES), 1) >= HEAD_DIM
    drop = drop_ref[0, 0]
    qs = [jnp.where(upper, drop, q_ref[0, :, r * LANES:(r + 1) * LANES] * (HEAD_DIM ** -0.5))
          for r in range(rep)]

    def tile(k0, carry, bias):
        k = k_ref[0, pl.ds(k0, kt), :]
        vt = vt_ref[:, pl.ds(k0, kt)]
        new = []
        for qr, (m, l, acc) in zip(qs, carry):
            s = lax.dot_general(k, qr, _NT, preferred_element_type=F32)
            if bias is not None:
                s = s + bias
            m_new = jnp.maximum(m, jnp.max(s, axis=0, keepdims=True))
            e = jnp.exp(s - m_new)
            corr = jnp.exp(m - m_new)
            l = l * corr + jnp.sum(e, axis=0, keepdims=True)
            acc = acc * corr + jnp.dot(vt, e.astype(BF16), preferred_element_type=F32)
            new.append((m_new, l, acc))
        return tuple(new)

    init = tuple((jnp.full((1, tq), NEG, F32), jnp.zeros((1, tq), F32), jnp.zeros((LANES, tq), F32))
                 for _ in range(rep))
    carry = lax.fori_loop(0, n_kt - 1, lambda j, c: tile(pl.multiple_of(j * kt, kt), c, None), init)
    k_last = pl.multiple_of((n_kt - 1) * kt, kt)
    kpos = k_last + lax.broadcasted_iota(jnp.int32, (kt, tq), 0)
    tpos = t0 + lax.broadcasted_iota(jnp.int32, (kt, tq), 1)
    carry = tile(k_last, carry, jnp.where(kpos <= tpos, 0.0, NEG))
    o_ref[0] = jnp.concatenate([(acc / l).T for _, l, acc in carry], axis=1)


def _nsa_slc(qkv, vt, drop, batch, seq, nblk, q_off, k_off, tq=128, kt=1024):
    g = C_KV_HEADS
    rep = C_HEADS // g
    kt = min(kt, seq)
    arr = qkv.reshape(batch, seq, nblk * LANES)
    out = pl.pallas_call(
        functools.partial(_nsa_slc_kernel, rep=rep, kt=kt),
        grid=(batch, g, seq // tq),
        in_specs=[pl.BlockSpec((1, tq, rep * LANES), lambda b, gi, i: (b, i, q_off // rep + gi)),
                  pl.BlockSpec((1, seq, LANES), lambda b, gi, i: (b, 0, k_off + gi)),
                  pl.BlockSpec((LANES, seq), lambda b, gi, i: (gi, b)),
                  pl.BlockSpec((1, 1, tq, LANES), lambda b, gi, i: (b, gi, i, 0))],
        out_specs=pl.BlockSpec((1, tq, rep * LANES), lambda b, gi, i: (b, i, gi)),
        out_shape=jax.ShapeDtypeStruct((batch, seq, C_HEADS * LANES), F32),
        compiler_params=_cparams("parallel", "parallel", "parallel"),
        name="nsa_selected",
    )(arr, arr, vt, drop)
    return out.reshape(batch * seq, C_HEADS * LANES)


def _even_mixer_layer(x2, batch, seq, tabs, w_in, sinks, w_out, g, b):
    d = x2.shape[1]
    n_heads_in = 3 * A_HEADS + B_Q_HEADS + 2 * B_KV_HEADS
    w = _pad_heads_cols(w_in, n_heads_in).astype(BF16)
    rope = [1] * (2 * A_HEADS) + [0] * A_HEADS + [1] * B_Q_HEADS + [1] * B_KV_HEADS + [0] * B_KV_HEADS
    plan = [(0, c, rope[c]) for c in range(n_heads_in)]
    dilations = [dil for _, dil in A_PATTERNS]
    regroup = [dil for dil in dilations if dil > 1]
    n_a_blocks = 3 * A_HEADS
    qkv, *grouped = _proj(x2, w, tabs, plan, [n_heads_in * LANES], [BF16], dilations=regroup, n_dil=n_a_blocks)
    outs = []
    for window, dilation in A_PATTERNS:
        src, nblk = (qkv, n_heads_in) if dilation == 1 else (grouped[regroup.index(dilation)], n_a_blocks)
        outs.append(_band_attention(src, batch=batch, seq=seq, dilation=dilation, nblk=nblk, q_off=0,
                                    k_off=A_HEADS, v_off=2 * A_HEADS, n_q_heads=A_HEADS, rep=1,
                                    max_dist=window // dilation,
                                    heads=min(4, dilation)))
    qb_off = 3 * A_HEADS
    ob = _band_attention(qkv, batch=batch, seq=seq, dilation=1, nblk=n_heads_in, q_off=qb_off,
                         k_off=qb_off + B_Q_HEADS, v_off=qb_off + B_Q_HEADS + B_KV_HEADS, n_q_heads=B_Q_HEADS,
                         rep=B_Q_HEADS // B_KV_HEADS, max_dist=B_WINDOW - 1, sinks=sinks)
    w_o = _pad_heads_rows(w_out, A_HEADS + B_Q_HEADS).astype(BF16)
    return _outproj_even(outs, dilations, ob, x2, w_o, g.reshape(1, d), b.reshape(1, d))


def _odd_mixer_layer(x2, batch, seq, tabs, w_in, cmpk_pos, cmpk_w1, cmpk_w2, cmpv_pos, cmpv_w1, cmpv_w2, w_out, g, b):
    d = x2.shape[1]
    kvw = C_KV_HEADS * HEAD_DIM
    qw = C_HEADS * HEAD_DIM
    sizes = [qw] + [kvw] * 6 + [3 * C_HEADS]
    offs = np.concatenate([[0], np.cumsum(sizes)])
    wq, wkc, wvc, wks, wvs, wkw, wvw, wgt = [w_in[:, offs[i]:offs[i + 1]] for i in range(8)]
    ph = lambda wpart, n: _pad_heads_cols(wpart, n)
    w = jnp.concatenate([ph(wq, C_HEADS), ph(wks, C_KV_HEADS), ph(wkw, C_KV_HEADS), ph(wvw, C_KV_HEADS),
                         wkc, wvc, jnp.pad(wgt, ((0, 0), (0, LANES - 3 * C_HEADS)))], axis=1).astype(BF16)
    wvs_t = ph(wvs, C_KV_HEADS).T.astype(BF16)
    n16 = C_HEADS + 3 * C_KV_HEADS
    rope16 = [1] * C_HEADS + [TAG_BLOCK] * C_KV_HEADS + [1] * C_KV_HEADS + [0] * C_KV_HEADS
    n_kc = kvw // LANES
    plan = ([(0, c, rope16[c]) for c in range(n16)] + [(1, c, 2) for c in range(n_kc)]
            + [(2, c, 0) for c in range(n_kc)] + [(3, 0, 0)])
    qkv, kc, vc, gate, vs_t = _proj(x2, w, tabs, plan, [n16 * LANES, kvw, kvw, LANES], [BF16, F32, F32, F32],
                                    wt=wvs_t, seq=seq)
    kcmp = _compress(kc, cmpk_pos, cmpk_w1, cmpk_w2, batch, seq)
    vcmp = _compress(vc, cmpv_pos, cmpv_w1, cmpv_w2, batch, seq)
    o_cmp, drop = _nsa_cmp(qkv, kcmp, vcmp, batch, seq, 0)
    ks_off = C_HEADS
    o_slc = _nsa_slc(qkv, vs_t, drop, batch, seq, n16, 0, ks_off)
    kw_off = ks_off + C_KV_HEADS
    o_win = _band_attention(qkv, batch=batch, seq=seq, dilation=1, nblk=n16, q_off=0, k_off=kw_off,
                            v_off=kw_off + C_KV_HEADS, n_q_heads=C_HEADS, rep=C_HEADS // C_KV_HEADS,
                            max_dist=NSA_WINDOW - 1)
    w_o = _pad_heads_rows(w_out, C_HEADS).astype(BF16)
    return _outproj_odd(o_cmp, o_slc, o_win, gate, x2, w_o, g.reshape(1, d), b.reshape(1, d))


def kernel(x, positions, even_w_in, even_sinks, even_w_out, odd_w_in, odd_cmpk_pos, odd_cmpk_w1, odd_cmpk_w2, odd_cmpv_pos, odd_cmpv_w1, odd_cmpv_w2, odd_w_out, mix_ln_g, mix_ln_b, moe_router_w, moe_router_b, moe_w_gate, moe_w_up, moe_w_down, moe_sh_gate, moe_sh_up, moe_sh_down, ffn_ln_g, ffn_ln_b):
    batch, seq, d = x.shape
    x2 = x.reshape(batch * seq, d)
    tabs = _rope_tables(positions)
    depth = mix_ln_g.shape[0]
    for layer in range(depth):
        j = layer // 2
        if layer % 2 == 0:
            x2 = _even_mixer_layer(x2, batch, seq, tabs, even_w_in[j], even_sinks[j], even_w_out[j],
                                   mix_ln_g[layer], mix_ln_b[layer])
        else:
            x2 = _odd_mixer_layer(x2, batch, seq, tabs, odd_w_in[j], odd_cmpk_pos[j], odd_cmpk_w1[j], odd_cmpk_w2[j],
                                  odd_cmpv_pos[j], odd_cmpv_w1[j], odd_cmpv_w2[j], odd_w_out[j],
                                  mix_ln_g[layer], mix_ln_b[layer])
        x2 = _moe_layer(x2, moe_router_w[layer], moe_router_b[layer], moe_w_gate, moe_w_up, moe_w_down, layer,
                        moe_sh_gate[layer], moe_sh_up[layer], moe_sh_down[layer],
                        ffn_ln_g[layer].reshape(1, d), ffn_ln_b[layer].reshape(1, d))
    return x2.reshape(batch, seq, d)
```

```python
import functools

import numpy as np
import jax
import jax.numpy as jnp
from jax import lax
from jax.experimental import pallas as pl
from jax.experimental.pallas import tpu as pltpu

F32 = jnp.float32
BF16 = jnp.bfloat16

LANES = 128
HEAD_DIM = 64
ROT_DIM = HEAD_DIM // 4
ROT_HALF = ROT_DIM // 2
ROPE_THETA = 500000.0
QBLK = 128
A_HEADS = 8
A_PATTERNS = ((128, 1), (512, 4), (2048, 16))
B_Q_HEADS = 8
B_KV_HEADS = 2
B_WINDOW = 128
C_HEADS = 16
C_KV_HEADS = 4
CMP_LEN = 32
CMP_STRIDE = 16
CMP_HIDDEN = 2 * HEAD_DIM
SLC_BLOCK = 64
SLC_SHIFT = 6
SLC_TOP_N = 16
NSA_WINDOW = 512
SELECT_FORCE = 1.0e4
N_EXPERTS = 256
TOP_K = 8
N_GROUPS = 8
TOPK_GROUPS = 4
ROUTED_SCALE = 2.5
MOE_BLOCK = 256
DEPTH = 2
DEEPNORM_ALPHA = (2 * DEPTH) ** 0.25
LN_EPS = 1e-5
NEG = -1.0e30
VMEM_LIMIT = 56 * 1024 * 1024

_NT = (((1,), (1,)), ((), ()))


def _cparams(*sem):
    return pltpu.CompilerParams(dimension_semantics=sem, vmem_limit_bytes=VMEM_LIMIT)


def _split_bf16(a):
    hi = a.astype(BF16)
    lo = (a - hi.astype(F32)).astype(BF16)
    return hi, lo


def _layer_norm(y, g, b):
    mu = jnp.mean(y, axis=-1, keepdims=True)
    d = y - mu
    var = jnp.mean(d * d, axis=-1, keepdims=True)
    return d * lax.rsqrt(var + LN_EPS) * g + b


def _silu(a):
    return a * jax.nn.sigmoid(a)


TAG_BLOCK = 3


def _proj_kernel(*refs, plan, n_main, transposed, dilations, n_dil, seq):
    n_in = 4 if transposed else 3
    x_ref, w_ref, tab_ref = refs[:3]
    out_refs = refs[n_in:n_in + n_main]
    extra = list(refs[n_in + n_main:])
    x = x_ref[...].astype(BF16)
    tm = x.shape[0]
    if transposed:
        out_t = extra.pop(0)
        out_t[...] = lax.dot_general(refs[3][...], x, _NT, preferred_element_type=F32).astype(out_t.dtype)
    dil_refs = [extra.pop(0) for _ in dilations]
    stage = extra.pop(0) if dilations else None
    nblk = len(plan)
    for c0 in range(0, nblk, 2):
        nb = min(2, nblk - c0)
        acc = jnp.dot(x, w_ref[:, c0 * LANES:(c0 + nb) * LANES], preferred_element_type=F32)
        for j in range(nb):
            blk = acc[:, j * LANES:(j + 1) * LANES]
            dst, dblk, mode = plan[c0 + j]
            if mode == TAG_BLOCK:
                pos = lax.rem(pl.program_id(0) * tm, seq) + lax.broadcasted_iota(jnp.int32, (tm, LANES), 0)
                tag = lax.broadcasted_iota(jnp.int32, (tm, LANES), 1) - HEAD_DIM == jnp.right_shift(pos, SLC_SHIFT)
                mode = 1
            else:
                tag = None
            if mode:
                off = (mode - 1) * 3 * LANES
                cos = tab_ref[:, off:off + LANES]
                s_lo = tab_ref[:, off + LANES:off + 2 * LANES]
                s_hi = tab_ref[:, off + 2 * LANES:off + 3 * LANES]
                blk = (blk * cos + pltpu.roll(blk, LANES - ROT_HALF, 1) * s_lo
                       + pltpu.roll(blk, ROT_HALF, 1) * s_hi)
            if tag is not None:
                blk = jnp.where(tag, NEG, blk)
            o_ref = out_refs[dst]
            o_ref[:, dblk * LANES:(dblk + 1) * LANES] = blk.astype(o_ref.dtype)
            c = c0 + j
            if dilations and c < n_dil:
                stage[...] = blk
                for dil, d_ref in zip(dilations, dil_refs):
                    for r in range(dil):
                        col = (r * n_dil + c) * LANES
                        d_ref[:, col:col + LANES] = stage[pl.ds(r, tm // dil, stride=dil), :].astype(d_ref.dtype)


def _proj(x2, w, tabs, plan, out_cols, out_dtypes, wt=None, dilations=(), n_dil=0, seq=0, tm=512):
    t, d = x2.shape
    ncol = w.shape[1]
    out_shape = [jax.ShapeDtypeStruct((t, c), dt) for c, dt in zip(out_cols, out_dtypes)]
    once = pl.Buffered(1)
    in_specs = [pl.BlockSpec((tm, d), lambda i: (i, 0)),
                pl.BlockSpec((d, ncol), lambda i: (0, 0), pipeline_mode=once),
                pl.BlockSpec((tm, tabs.shape[1]), lambda i: (i, 0))]
    out_specs = [pl.BlockSpec((tm, c), lambda i: (i, 0)) for c in out_cols]
    args = [x2, w, tabs]
    if wt is not None:
        in_specs.append(pl.BlockSpec(wt.shape, lambda i: (0, 0), pipeline_mode=once))
        out_specs.append(pl.BlockSpec((wt.shape[0], tm), lambda i: (0, i)))
        out_shape.append(jax.ShapeDtypeStruct((wt.shape[0], t), BF16))
        args.append(wt)
    for dil in dilations:
        out_specs.append(pl.BlockSpec((tm // dil, dil * n_dil * LANES), lambda i: (i, 0)))
        out_shape.append(jax.ShapeDtypeStruct((t // dil, dil * n_dil * LANES), BF16))
    return pl.pallas_call(
        functools.partial(_proj_kernel, plan=tuple(plan), n_main=len(out_cols), transposed=wt is not None,
                          dilations=tuple(dilations), n_dil=n_dil, seq=seq),
        grid=(t // tm,),
        in_specs=in_specs,
        out_specs=out_specs,
        out_shape=out_shape,
        scratch_shapes=[pltpu.VMEM((tm, LANES), F32)] if dilations else [],
        compiler_params=_cparams("parallel"),
        name="proj_rope",
    )(*args)


def _rope_tables(positions):
    t = positions.size
    inv_freq = jnp.asarray(ROPE_THETA ** (-np.arange(0, ROT_DIM, 2) / ROT_DIM), F32)
    ang = positions.astype(F32).reshape(t, 1) * inv_freq
    trig = jnp.concatenate([jnp.cos(ang), jnp.sin(ang)], axis=1)
    place = np.zeros((2 * ROT_HALF, 6 * LANES), np.float32)
    const = np.zeros((6 * LANES,), np.float32)
    for base, heads in ((0, (0,)), (3 * LANES, (0, HEAD_DIM))):
        const[base:base + LANES] = 1.0
        for h in heads:
            for i in range(ROT_HALF):
                place[i, base + h + i] = 1.0
                place[i, base + h + ROT_HALF + i] = 1.0
                const[base + h + i] = const[base + h + ROT_HALF + i] = 0.0
                place[ROT_HALF + i, base + LANES + h + i] = -1.0
                place[ROT_HALF + i, base + 2 * LANES + h + ROT_HALF + i] = 1.0
    return jnp.dot(trig, jnp.asarray(place), precision=lax.Precision.HIGHEST) + jnp.asarray(const)


def _pad_heads_cols(w, n_heads):
    d = w.shape[0]
    w = w.reshape(d, n_heads, HEAD_DIM)
    return jnp.pad(w, ((0, 0), (0, 0), (0, LANES - HEAD_DIM))).reshape(d, n_heads * LANES)


def _pad_heads_rows(w, n_heads):
    d = w.shape[1]
    w = w.reshape(n_heads, HEAD_DIM, d)
    return jnp.pad(w, ((0, 0), (0, LANES - HEAD_DIM), (0, 0))).reshape(n_heads * LANES, d)


def _band_kernel(*refs, back, max_dist, length, qrows, heads, shared_kv, in_flight, has_sink):
    if has_sink:
        sink_ref, q_ref, k_ref, v_ref, o_ref, bias_sc = refs
    else:
        q_ref, k_ref, v_ref, o_ref, bias_sc = refs
    nq = length // qrows
    window = qrows + back
    chains = [tuple(range(heads))] if shared_kv else [(h,) for h in range(heads)]
    stack = len(chains[0])
    lane = lax.broadcasted_iota(jnp.int32, (stack * qrows, LANES), 1)
    first_head = pl.program_id(2) * heads

    def band_bias(offset):
        row = lax.broadcasted_iota(jnp.int32, (qrows, window), 0)
        col = lax.broadcasted_iota(jnp.int32, (qrows, window), 1)
        dist = offset + row - col
        tile = jnp.where((dist >= 0) & (dist <= max_dist), 0.0, NEG)
        return jnp.concatenate([tile] * stack, axis=0) if stack > 1 else tile

    bias_sc[...] = band_bias(back)

    def block(hs, q0, k0, bias):
        kv = slice(0, LANES) if shared_kv else slice(hs[0] * LANES, (hs[0] + 1) * LANES)
        q = jnp.concatenate([q_ref[0, pl.ds(q0, qrows), h * LANES:(h + 1) * LANES] for h in hs], axis=0)
        q = q * (HEAD_DIM ** -0.5)
        k = k_ref[0, pl.ds(k0, window), kv]
        v = v_ref[0, pl.ds(k0, window), kv]
        s = lax.dot_general(q, k, _NT, preferred_element_type=F32) + bias
        m = jnp.max(s, axis=-1, keepdims=True)
        e = jnp.exp(s - m)
        den = jnp.sum(e, axis=-1, keepdims=True)
        if has_sink:
            sink = jnp.concatenate([jnp.full((qrows, 1), sink_ref[first_head + h], F32) for h in hs], axis=0)
            den = den + jnp.exp(sink - m)
        o = jnp.dot(e.astype(BF16), v, preferred_element_type=F32) / den
        lse = m + jnp.log(den)
        out = jnp.where(lane < HEAD_DIM, o, lse)
        for i, h in enumerate(hs):
            o_ref[0, pl.ds(q0, qrows), h * LANES:(h + 1) * LANES] = out[i * qrows:(i + 1) * qrows]

    n_clipped = min(-(-back // qrows), nq)
    for qi in range(n_clipped):
        bias = band_bias(qi * qrows)
        for hs in chains:
            block(hs, qi * qrows, 0, bias)

    steady = nq - n_clipped
    per_iter = max(1, min(in_flight // heads, steady))
    n_iter = steady // per_iter

    def body(it, carry):
        aligned = lambda v: v if isinstance(v, int) else pl.multiple_of(v, QBLK)
        for j in range(per_iter):
            q0 = aligned((n_clipped + it * per_iter + j) * qrows)
            for hs in chains:
                block(hs, q0, aligned(q0 - back), bias_sc[...])
        return carry

    if n_iter == 1:
        body(0, 0)
    elif n_iter:
        lax.fori_loop(0, n_iter, body, 0)
    for qi in range(n_clipped + n_iter * per_iter, nq):
        for hs in chains:
            block(hs, qi * qrows, qi * qrows - back, bias_sc[...])


def _band_attention(qkv, *, batch, seq, dilation, nblk, q_off, k_off, v_off, n_q_heads, rep, max_dist, sinks=None,
                    heads=1, in_flight=16):
    length = seq // dilation
    back = min(-(-max_dist // QBLK) * QBLK, length - QBLK)
    qrows = QBLK
    shared_kv = rep > 1
    assert n_q_heads % heads == 0 and (rep == 1 or rep % heads == 0)
    kv_heads = 1 if shared_kv else heads
    arr = qkv.reshape(batch, length, dilation * nblk * LANES)
    assert q_off % heads == 0 and k_off % kv_heads == 0 and v_off % kv_heads == 0 and nblk % heads == 0
    qspec = pl.BlockSpec((1, length, heads * LANES),
                         lambda b, r, h: (b, 0, (r * nblk + q_off) // heads + h))
    kvspec = lambda off: pl.BlockSpec(
        (1, length, kv_heads * LANES),
        lambda b, r, h: (b, 0, (r * nblk + off) // kv_heads + (h * heads // rep if shared_kv else h)))
    in_specs = [qspec, kvspec(k_off), kvspec(v_off)]
    args = [arr, arr, arr]
    if sinks is not None:
        in_specs = [pl.BlockSpec(memory_space=pltpu.SMEM)] + in_specs
        args = [sinks.reshape(-1).astype(F32)] + args
    out = pl.pallas_call(
        functools.partial(_band_kernel, back=back, max_dist=max_dist, length=length, qrows=qrows, heads=heads,
                          shared_kv=shared_kv, in_flight=in_flight, has_sink=sinks is not None),
        grid=(batch, dilation, n_q_heads // heads),
        in_specs=in_specs,
        out_specs=pl.BlockSpec((1, length, heads * LANES), lambda b, r, h: (b, 0, r * (n_q_heads // heads) + h)),
        out_shape=jax.ShapeDtypeStruct((batch, length, dilation * n_q_heads * LANES), F32),
        scratch_shapes=[pltpu.VMEM(((heads if shared_kv else 1) * qrows, qrows + back), F32)],
        compiler_params=_cparams("parallel", "parallel", "parallel"),
        name="band_attention",
    )(*args)
    return out.reshape(batch * length, dilation * n_q_heads * LANES)


def _outproj_even_kernel(o1_ref, o2_ref, o3_ref, ob_ref, x_ref, w_ref, g_ref, b_ref, out_ref, *nat_refs,
                         n_a, n_b, dilations):
    tm = x_ref.shape[0]
    lane = lax.broadcasted_iota(jnp.int32, (tm, LANES), 1)
    real = lane < HEAD_DIM
    pattern_refs = []
    nat_refs = list(nat_refs)
    for o_ref, dil in zip((o1_ref, o2_ref, o3_ref), dilations):
        if dil == 1:
            pattern_refs.append(o_ref)
            continue
        nat = nat_refs.pop(0)
        for r in range(dil):
            for h in range(n_a):
                col = (r * n_a + h) * LANES
                nat[h, pl.ds(r, tm // dil, stride=dil), :] = o_ref[:, col:col + LANES]
        pattern_refs.append(nat)
    parts = []
    for h in range(n_a):
        sl = slice(h * LANES, (h + 1) * LANES)
        outs = [r[:, sl] if r.ndim == 2 else r[h] for r in pattern_refs]
        lses = [jnp.where(real, pltpu.roll(a, HEAD_DIM, 1), a) for a in outs]
        m = jnp.maximum(jnp.maximum(lses[0], lses[1]), lses[2])
        es = [jnp.exp(l - m) for l in lses]
        num = es[0] * outs[0] + es[1] * outs[1] + es[2] * outs[2]
        den = es[0] + es[1] + es[2]
        parts.append(jnp.where(real, num / den, 0.0).astype(BF16))
    for h in range(n_b):
        parts.append(jnp.where(real, ob_ref[:, h * LANES:(h + 1) * LANES], 0.0).astype(BF16))
    a = jnp.concatenate(parts, axis=1)
    mixed = jnp.dot(a, w_ref[...], preferred_element_type=F32)
    y = DEEPNORM_ALPHA * x_ref[...] + mixed
    out_ref[...] = _layer_norm(y, g_ref[...], b_ref[...])


def _outproj_even(outs, dilations, ob, x2, w, g, b, tm=256):
    t, d = x2.shape
    n_a, n_b = outs[0].shape[1] // (dilations[0] * LANES), ob.shape[1] // LANES
    row = lambda c: pl.BlockSpec((tm, c), lambda i: (i, 0))
    grouped = lambda a, dil: pl.BlockSpec((tm // dil, a.shape[1]), lambda i: (i, 0))
    full = lambda a: pl.BlockSpec(a.shape, lambda i: (0,) * a.ndim)
    return pl.pallas_call(
        functools.partial(_outproj_even_kernel, n_a=n_a, n_b=n_b, dilations=tuple(dilations)),
        grid=(t // tm,),
        in_specs=[grouped(o, dil) for o, dil in zip(outs, dilations)] + [row(ob.shape[1]), row(d),
                                                                          full(w), full(g), full(b)],
        out_specs=row(d),
        out_shape=jax.ShapeDtypeStruct((t, d), F32),
        scratch_shapes=[pltpu.VMEM((n_a, tm, LANES), F32) for dil in dilations if dil > 1],
        compiler_params=_cparams("parallel"),
        name="outproj_even_ln",
    )(*outs, ob, x2, w, g, b)


def _outproj_odd_kernel(oc_ref, os_ref, ow_ref, gate_ref, e_ref, x_ref, w_ref, g_ref, b_ref, out_ref):
    gate = jax.nn.sigmoid(gate_ref[...])
    ghi, glo = _split_bf16(gate)
    acc = None
    for j, o_ref in enumerate((oc_ref, os_ref, ow_ref)):
        ej = e_ref[j]
        gfull = jnp.dot(ghi, ej, preferred_element_type=F32) + jnp.dot(glo, ej, preferred_element_type=F32)
        term = gfull * o_ref[...]
        acc = term if acc is None else acc + term
    mixed = jnp.dot(acc.astype(BF16), w_ref[...], preferred_element_type=F32)
    y = DEEPNORM_ALPHA * x_ref[...] + mixed
    out_ref[...] = _layer_norm(y, g_ref[...], b_ref[...])


def _gate_expanders(n_heads):
    e = np.zeros((3, LANES, n_heads * LANES), np.float32)
    for j in range(3):
        for h in range(n_heads):
            e[j, 3 * h + j, h * LANES:h * LANES + HEAD_DIM] = 1.0
    return jnp.asarray(e, BF16)


def _outproj_odd(oc, osl, ow, gate, x2, w, g, b, tm=256):
    t, d = x2.shape
    n_heads = oc.shape[1] // LANES
    e = _gate_expanders(n_heads)
    row = lambda c: pl.BlockSpec((tm, c), lambda i: (i, 0))
    full = lambda a: pl.BlockSpec(a.shape, lambda i: (0,) * a.ndim)
    return pl.pallas_call(
        _outproj_odd_kernel,
        grid=(t // tm,),
        in_specs=[row(oc.shape[1]), row(osl.shape[1]), row(ow.shape[1]), row(LANES), full(e), row(d),
                  full(w), full(g), full(b)],
        out_specs=row(d),
        out_shape=jax.ShapeDtypeStruct((t, d), F32),
        compiler_params=_cparams("parallel"),
        name="outproj_odd_ln",
    )(oc, osl, ow, gate, e, x2, w, g, b)


def _router_kernel(x_ref, whi_ref, wlo_ref, bias_ref, eidx_ref, gate_ref, rank_ref, cnt_ref):
    n_exp = whi_ref.shape[0]
    tm = x_ref.shape[0]
    per_group = n_exp // N_GROUPS
    xhi, xlo = _split_bf16(x_ref[...])
    whi, wlo = whi_ref[...], wlo_ref[...]
    dg = lambda a, b: lax.dot_general(a, b, _NT, preferred_element_type=F32)
    logits = dg(whi, xhi) + dg(whi, xlo) + dg(wlo, xhi)
    aff = jax.nn.sigmoid(logits)
    biased = aff + bias_ref[...]
    gio = lax.broadcasted_iota(jnp.int32, (per_group, tm), 0).astype(F32)
    blocks, scores = [], []
    for g in range(N_GROUPS):
        blk = biased[g * per_group:(g + 1) * per_group, :]
        m1 = jnp.max(blk, axis=0, keepdims=True)
        first = jnp.min(jnp.where(blk == m1, gio, float(per_group)), axis=0, keepdims=True)
        m2 = jnp.max(jnp.where(gio == first, -jnp.inf, blk), axis=0, keepdims=True)
        blocks.append(blk)
        scores.append(m1 + m2)
    masked = []
    for g in range(N_GROUPS):
        rank = jnp.zeros((1, tm), F32)
        for o in range(N_GROUPS):
            if o == g:
                continue
            beats = scores[o] >= scores[g] if o < g else scores[o] > scores[g]
            rank = rank + jnp.where(beats, 1.0, 0.0)
        masked.append(jnp.where(rank < TOPK_GROUPS, blocks[g], -jnp.inf))
    cur = jnp.concatenate(masked, axis=0)
    eio = lax.broadcasted_iota(jnp.int32, (n_exp, tm), 0).astype(F32)
    ids, gs = [], []
    for _ in range(TOP_K):
        m = jnp.max(cur, axis=0, keepdims=True)
        idx = jnp.min(jnp.where(cur == m, eio, float(n_exp)), axis=0, keepdims=True)
        hit = eio == idx
        gs.append(jnp.sum(jnp.where(hit, aff, 0.0), axis=0, keepdims=True))
        ids.append(idx)
        cur = jnp.where(hit, -jnp.inf, cur)
    gates = jnp.concatenate(gs, axis=0)
    gates = gates / jnp.sum(gates, axis=0, keepdims=True) * ROUTED_SCALE
    eidx_ref[...] = jnp.concatenate(ids, axis=0).astype(jnp.int32)
    gate_ref[...] = gates
    @pl.when(pl.program_id(0) == 0)
    def _():
        cnt_ref[...] = jnp.zeros(cnt_ref.shape, F32)

    onehot = jnp.zeros((n_exp, tm), F32)
    for idx in ids:
        onehot = onehot + jnp.where(eio == idx, 1.0, 0.0)
    earlier = jnp.where(lax.broadcasted_iota(jnp.int32, (tm, tm), 0) < lax.broadcasted_iota(jnp.int32, (tm, tm), 1),
                        1.0, 0.0).astype(BF16)
    before = cnt_ref[...] + jnp.dot(onehot.astype(BF16), earlier, preferred_element_type=F32)
    ranks = [jnp.sum(jnp.where(eio == idx, before, 0.0), axis=0, keepdims=True) for idx in ids]
    rank_ref[...] = jnp.concatenate(ranks, axis=0).astype(jnp.int32)
    cnt_ref[...] = cnt_ref[...] + jnp.sum(onehot, axis=1, keepdims=True)


def _router(x2, router_w, router_b, tm=256):
    t, d = x2.shape
    n_exp = router_w.shape[1]
    whi, wlo = _split_bf16(router_w.T)
    bias = router_b.reshape(n_exp, 1).astype(F32)
    full = lambda a: pl.BlockSpec(a.shape, lambda i: (0,) * a.ndim)
    per_tok = pl.BlockSpec((TOP_K, tm), lambda i: (0, i))
    return pl.pallas_call(
        _router_kernel,
        grid=(t // tm,),
        in_specs=[pl.BlockSpec((tm, d), lambda i: (i, 0)), full(whi), full(wlo), full(bias)],
        out_specs=[per_tok, per_tok, per_tok, pl.BlockSpec((n_exp, 1), lambda i: (0, 0))],
        out_shape=[jax.ShapeDtypeStruct((TOP_K, t), jnp.int32), jax.ShapeDtypeStruct((TOP_K, t), F32),
                   jax.ShapeDtypeStruct((TOP_K, t), jnp.int32), jax.ShapeDtypeStruct((n_exp, 1), F32)],
        compiler_params=_cparams("arbitrary"),
        name="moe_router",
    )(x2, whi, wlo, bias)


def _moe_dest_kernel(eidx_ref, rank_ref, start_ref, dest_ref):
    n_exp = start_ref.shape[0]
    tm = eidx_ref.shape[1]
    eio = lax.broadcasted_iota(jnp.int32, (n_exp, tm), 0)
    start = start_ref[...]
    rows = []
    for k in range(TOP_K):
        seg = jnp.sum(jnp.where(eio == eidx_ref[k:k + 1, :], start, 0.0), axis=0, keepdims=True)
        rows.append(seg.astype(jnp.int32) + rank_ref[k:k + 1, :])
    dest_ref[...] = jnp.concatenate(rows, axis=0)


def _moe_dest(eidx, rank, seg_start, tm=256):
    t = eidx.shape[1]
    per_tok = pl.BlockSpec((TOP_K, tm), lambda i: (0, i))
    return pl.pallas_call(
        _moe_dest_kernel,
        grid=(t // tm,),
        in_specs=[per_tok, per_tok, pl.BlockSpec(seg_start.shape, lambda i: (0, 0))],
        out_specs=per_tok,
        out_shape=jax.ShapeDtypeStruct((TOP_K, t), jnp.int32),
        compiler_params=_cparams("parallel"),
        name="moe_dest",
    )(eidx, rank, seg_start)


def _to_slabs(ref, value):
    rows, width = value.shape
    n_chunks = width // LANES
    for c in range(n_chunks):
        ref[pl.ds(c, rows, stride=n_chunks), :] = value[:, c * LANES:(c + 1) * LANES]


def _from_slabs(ref, first_row, rows, n_chunks):
    return jnp.concatenate([ref[pl.ds(first_row * n_chunks + c, rows, stride=n_chunks), :]
                            for c in range(n_chunks)], axis=1)


def _moe_zero_kernel(blk_ref, xs_ref):
    xs_ref[...] = jnp.zeros(xs_ref.shape, xs_ref.dtype)


def _moe_zero_padding(zero_blocks, n_blocks, n_chunks):
    grid_spec = pltpu.PrefetchScalarGridSpec(
        num_scalar_prefetch=1,
        grid=(zero_blocks.shape[0],),
        in_specs=[],
        out_specs=pl.BlockSpec((MOE_BLOCK * n_chunks, LANES), lambda i, blk: (blk[i], 0)),
    )
    return pl.pallas_call(
        _moe_zero_kernel,
        grid_spec=grid_spec,
        out_shape=jax.ShapeDtypeStruct((n_blocks * MOE_BLOCK * n_chunks, LANES), jnp.uint32),
        compiler_params=_cparams("arbitrary"),
        name="moe_zero_padding",
    )(zero_blocks)


def _pack_bf16_pairs(x):
    half = x.shape[1] // 2
    lo = lax.bitcast_convert_type(x[:, :half].astype(BF16).astype(F32), jnp.uint32)
    hi = lax.bitcast_convert_type(x[:, half:].astype(BF16).astype(F32), jnp.uint32)
    return jnp.right_shift(lo, jnp.uint32(16)) | (hi & jnp.uint32(0xFFFF0000))


def _unpack_bf16_pairs(w):
    lo = lax.bitcast_convert_type(jnp.left_shift(w, jnp.uint32(16)), F32).astype(BF16)
    hi = lax.bitcast_convert_type(w & jnp.uint32(0xFFFF0000), F32).astype(BF16)
    return jnp.concatenate([lo, hi], axis=1)


def _moe_dispatch_kernel(dest_ref, x_ref, xs_in, xs_out, buf, sem):
    del xs_in
    tm = x_ref.shape[0]
    n_chunks = buf.shape[0] // tm
    _to_slabs(buf, _pack_bf16_pairs(x_ref[...]))

    def issue(i, c):
        src = buf.at[pl.ds(pl.multiple_of(i * n_chunks, n_chunks), n_chunks)]
        for k in range(TOP_K):
            r = pl.multiple_of(dest_ref[k, i] * n_chunks, n_chunks)
            pltpu.make_async_copy(src, xs_out.at[pl.ds(r, n_chunks)], sem).start()
        return c

    lax.fori_loop(0, tm, issue, 0)
    for _ in range(TOP_K):
        pltpu.make_async_copy(buf, xs_out.at[pl.ds(0, tm * n_chunks)], sem).wait()


def _moe_dispatch(dest, x2, xs, tm=256):
    t, d = x2.shape
    n_chunks = d // 2 // LANES
    return pl.pallas_call(
        _moe_dispatch_kernel,
        grid=(t // tm,),
        in_specs=[pl.BlockSpec((TOP_K, tm), lambda i: (0, i), memory_space=pltpu.SMEM),
                  pl.BlockSpec((tm, d), lambda i: (i, 0)),
                  pl.BlockSpec(memory_space=pl.ANY)],
        out_specs=pl.BlockSpec(memory_space=pl.ANY),
        out_shape=jax.ShapeDtypeStruct(xs.shape, xs.dtype),
        scratch_shapes=[pltpu.VMEM((tm * n_chunks, LANES), jnp.uint32), pltpu.SemaphoreType.DMA(())],
        input_output_aliases={2: 0},
        compiler_params=_cparams("arbitrary"),
        name="moe_dispatch",
    )(dest, x2, xs)


def _moe_ffn_kernel(be_ref, nu_ref, xs_ref, wg_ref, wu_ref, wd_ref, y_ref, wg_sc, wu_sc, wd_sc):
    b = pl.program_id(0)

    @pl.when(b < nu_ref[0])
    def _():
        @pl.when((b == 0) | (be_ref[b] != be_ref[jnp.maximum(b - 1, 0)]))
        def _():
            wg_sc[...] = wg_ref[0].astype(BF16)
            wu_sc[...] = wu_ref[0].astype(BF16)
            wd_sc[...] = wd_ref[0].astype(BF16)

        d = wg_sc.shape[0]
        xb = _unpack_bf16_pairs(_from_slabs(xs_ref, 0, MOE_BLOCK, d // 2 // LANES))
        gp = jnp.dot(xb, wg_sc[...], preferred_element_type=F32)
        up = jnp.dot(xb, wu_sc[...], preferred_element_type=F32)
        h = (_silu(gp) * up).astype(BF16)
        _to_slabs(y_ref, jnp.dot(h, wd_sc[...], preferred_element_type=F32))


def _moe_ffn(xs, block_expert, n_used, w_gate, w_up, w_down, layer):
    n_blocks = block_expert.shape[0]
    d, ff = w_gate.shape[2], w_gate.shape[3]
    last = lambda b, nu: jnp.minimum(b, nu[0] - 1)
    grid_spec = pltpu.PrefetchScalarGridSpec(
        num_scalar_prefetch=2,
        grid=(n_blocks,),
        in_specs=[
            pl.BlockSpec((MOE_BLOCK * (d // 2 // LANES), LANES), lambda b, be, nu: (last(b, nu), 0)),
            pl.BlockSpec((None, 1, d, ff), lambda b, be, nu: (layer, be[last(b, nu)], 0, 0)),
            pl.BlockSpec((None, 1, d, ff), lambda b, be, nu: (layer, be[last(b, nu)], 0, 0)),
            pl.BlockSpec((None, 1, ff, d), lambda b, be, nu: (layer, be[last(b, nu)], 0, 0)),
        ],
        out_specs=pl.BlockSpec((MOE_BLOCK * (d // LANES), LANES), lambda b, be, nu: (last(b, nu), 0)),
        scratch_shapes=[pltpu.VMEM((d, ff), BF16), pltpu.VMEM((d, ff), BF16), pltpu.VMEM((ff, d), BF16)],
    )
    return pl.pallas_call(
        _moe_ffn_kernel,
        grid_spec=grid_spec,
        out_shape=jax.ShapeDtypeStruct((n_blocks * MOE_BLOCK * (d // LANES), LANES), F32),
        compiler_params=_cparams("arbitrary"),
        name="moe_expert_ffn",
    )(block_expert, n_used, xs, w_gate, w_up, w_down)


def _moe_combine_kernel(dest_ref, y_hbm, gate_ref, x_ref, sg_ref, su_ref, sd_ref, g_ref, b_ref, out_ref, buf, sem):
    tm, d = x_ref.shape
    n_chunks = d // LANES

    def issue(i, c):
        for k in range(TOP_K):
            r = pl.multiple_of(dest_ref[k, i] * n_chunks, n_chunks)
            slot = pl.multiple_of((k * tm + i) * n_chunks, n_chunks)
            pltpu.make_async_copy(y_hbm.at[pl.ds(r, n_chunks)], buf.at[pl.ds(slot, n_chunks)], sem).start()
        return c

    lax.fori_loop(0, tm, issue, 0)
    x = x_ref[...]
    xb = x.astype(BF16)
    hs = _silu(jnp.dot(xb, sg_ref[...], preferred_element_type=F32)) * jnp.dot(xb, su_ref[...], preferred_element_type=F32)
    shared = jnp.dot(hs.astype(BF16), sd_ref[...], preferred_element_type=F32)
    pltpu.make_async_copy(y_hbm.at[pl.ds(0, TOP_K * tm * n_chunks)], buf, sem).wait()
    gates = gate_ref[...]
    routed = _from_slabs(buf, 0, tm, n_chunks) * gates[:, 0:1]
    for k in range(1, TOP_K):
        routed = routed + _from_slabs(buf, k * tm, tm, n_chunks) * gates[:, k:k + 1]
    y = DEEPNORM_ALPHA * x + (routed + shared)
    out_ref[...] = _layer_norm(y, g_ref[...], b_ref[...])


def _moe_combine(dest, y, gates_t, x2, sh_gate, sh_up, sh_down, g, b, tm=128):
    t, d = x2.shape
    row = lambda c: pl.BlockSpec((tm, c), lambda i: (i, 0))
    full = lambda a: pl.BlockSpec(a.shape, lambda i: (0,) * a.ndim)
    return pl.pallas_call(
        _moe_combine_kernel,
        grid=(t // tm,),
        in_specs=[pl.BlockSpec((TOP_K, tm), lambda i: (0, i), memory_space=pltpu.SMEM),
                  pl.BlockSpec(memory_space=pl.ANY),
                  row(TOP_K), row(d), full(sh_gate), full(sh_up), full(sh_down), full(g), full(b)],
        out_specs=row(d),
        out_shape=jax.ShapeDtypeStruct((t, d), F32),
        scratch_shapes=[pltpu.VMEM((TOP_K * tm * (d // LANES), LANES), F32), pltpu.SemaphoreType.DMA(())],
        compiler_params=_cparams("arbitrary"),
        name="moe_combine_ln",
    )(dest, y, gates_t, x2, sh_gate, sh_up, sh_down, g, b)


def _moe_segments(counts, n_tok):
    n_exp = counts.shape[0]
    n_blocks = -(-n_tok * TOP_K // MOE_BLOCK) + n_exp
    nblk = (counts.reshape(n_exp).astype(jnp.int32) + MOE_BLOCK - 1) // MOE_BLOCK
    blk_end = jnp.cumsum(nblk)
    seg_start = ((blk_end - nblk) * MOE_BLOCK).astype(F32).reshape(n_exp, 1)
    block_expert = jnp.sum((blk_end[None, :] <= jnp.arange(n_blocks)[:, None]).astype(jnp.int32), axis=1)
    block_expert = jnp.minimum(block_expert, n_exp - 1)
    n_used = blk_end[-1]
    zero_blocks = jnp.maximum(blk_end - 1, 0).astype(jnp.int32)
    return seg_start, block_expert.astype(jnp.int32), n_used.astype(jnp.int32).reshape(1), zero_blocks, n_blocks


def _moe_layer(x2, router_w, router_b, w_gate, w_up, w_down, layer, sh_gate, sh_up, sh_down, g, b):
    t, d = x2.shape
    eidx, gates, rank, counts = _router(x2, router_w, router_b)
    seg_start, block_expert, n_used, zero_blocks, n_blocks = _moe_segments(counts, t)
    dest = _moe_dest(eidx, rank, seg_start)
    xs = _moe_dispatch(dest, x2, _moe_zero_padding(zero_blocks, n_blocks, d // 2 // LANES))
    y = _moe_ffn(xs, block_expert, n_used, w_gate, w_up, w_down, layer)
    return _moe_combine(dest, y, gates.T, x2, sh_gate.astype(BF16), sh_up.astype(BF16), sh_down.astype(BF16), g, b)


def _compress_kernel(x_ref, pa_ref, pb_ref, wa_ref, wb_ref, w2_ref, out_ref):
    x = x_ref[0]
    nc = x.shape[0]
    ha = jnp.dot((x + pa_ref[...]).astype(BF16), wa_ref[...], preferred_element_type=F32)
    hb = jnp.dot((x + pb_ref[...]).astype(BF16), wb_ref[...], preferred_element_type=F32)
    h = ha + pltpu.roll(hb, nc - 1, 0)
    h = jax.nn.gelu(h, approximate=True)
    out_ref[0] = jnp.dot(h.astype(BF16), w2_ref[...], preferred_element_type=F32).astype(out_ref.dtype)


def _compress(kc, pos, w1, w2, batch, seq):
    g = C_KV_HEADS
    nch = seq // CMP_STRIDE
    half = CMP_LEN // 2
    x = kc.reshape(batch, nch, half * g * HEAD_DIM)
    eye = jnp.eye(g, dtype=F32)
    w1r = w1.reshape(CMP_LEN, HEAD_DIM, CMP_HIDDEN)
    expand = lambda wpart: jnp.einsum('jdh,ge->jgdeh', wpart, eye).reshape(half * g * HEAD_DIM, g * CMP_HIDDEN)
    wa, wb = expand(w1r[:half]).astype(BF16), expand(w1r[half:]).astype(BF16)
    w2e = jnp.einsum('hd,ge->ghed', jnp.pad(w2, ((0, 0), (0, LANES - HEAD_DIM))), eye)
    w2e = w2e.reshape(g * CMP_HIDDEN, g * LANES).astype(BF16)
    tile_pos = lambda p: jnp.broadcast_to(p[:, None, :], (half, g, HEAD_DIM)).reshape(1, half * g * HEAD_DIM)
    pa, pb = tile_pos(pos[:half]), tile_pos(pos[half:])
    full = lambda a: pl.BlockSpec(a.shape, lambda i: (0,) * a.ndim)
    return pl.pallas_call(
        _compress_kernel,
        grid=(batch,),
        in_specs=[pl.BlockSpec((1, nch, x.shape[2]), lambda i: (i, 0, 0)), full(pa), full(pb), full(wa), full(wb),
                  full(w2e)],
        out_specs=pl.BlockSpec((1, nch, g * LANES), lambda i: (i, 0, 0)),
        out_shape=jax.ShapeDtypeStruct((batch, nch, g * LANES), BF16),
        compiler_params=_cparams("parallel"),
        name="nsa_compress",
    )(x, pa, pb, wa, wb, w2e)


def _nsa_cmp_kernel(q_ref, kc_ref, vc_ref, ovt_ref, o_ref, drop_ref, *, rep, n_sel, n_real):
    tq = q_ref.shape[1]
    nc = kc_ref.shape[1]
    nsb = ovt_ref.shape[0]
    t0 = pl.program_id(2) * tq
    scale = HEAD_DIM ** -0.5
    kc = kc_ref[0]
    vc = vc_ref[0]
    tpos = t0 + lax.broadcasted_iota(jnp.int32, (tq, nc), 0)
    cend = lax.broadcasted_iota(jnp.int32, (tq, nc), 1) * CMP_STRIDE + (CMP_LEN - 1)
    cmask = cend <= tpos
    psum = jnp.zeros((tq, nc), F32)
    outs = []
    for r in range(rep):
        q = q_ref[0, :, r * LANES:(r + 1) * LANES]
        sc = lax.dot_general(q, kc, _NT, preferred_element_type=F32) * scale
        sc = jnp.where(cmask, sc, NEG)
        m = jnp.max(sc, axis=-1, keepdims=True)
        ex = jnp.where(cmask, jnp.exp(sc - m), 0.0)
        den = jnp.sum(ex, axis=-1, keepdims=True)
        pc = ex / jnp.where(den > 0, den, 1.0)
        outs.append(jnp.dot(pc.astype(BF16), vc, preferred_element_type=F32))
        psum = psum + pc
    o_ref[0] = jnp.concatenate(outs, axis=1)
    phi, plo = _split_bf16(psum)
    ovt = ovt_ref[...]
    imp = (lax.dot_general(ovt, phi, _NT, preferred_element_type=F32)
           + lax.dot_general(ovt, plo, _NT, preferred_element_type=F32))
    jblk = lax.broadcasted_iota(jnp.int32, (nsb, tq), 0)
    cur = jnp.right_shift(t0 + lax.broadcasted_iota(jnp.int32, (nsb, tq), 1), SLC_SHIFT)
    forced = (jblk == 0) | (jblk == cur) | (jblk == cur - 1)
    score = jnp.where(jblk > cur, -1.0, jnp.where(forced, SELECT_FORCE, imp))
    rank = jnp.zeros((nsb, tq), F32)
    for k in range(n_real):
        rowk = score[k:k + 1, :]
        ge = jnp.where(rowk >= score, 1.0, 0.0)
        gt = jnp.where(rowk > score, 1.0, 0.0)
        rank = rank + jnp.where(jblk > k, ge, gt)
    drop = jnp.where(jblk <= cur, jnp.where(rank < n_sel, 0.0, 1.0), 1.0)
    drop = jnp.concatenate([drop, jnp.ones((LANES - nsb, tq), F32)], axis=0).T
    drop_ref[0, 0] = pltpu.roll(drop, HEAD_DIM, 1).astype(drop_ref.dtype)


def _nsa_cmp(q, kcmp, vcmp, batch, seq, q_off_blocks, tq=256):
    g = C_KV_HEADS
    rep = C_HEADS // g
    nc = kcmp.shape[1]
    nsb = seq // SLC_BLOCK
    n_sel = min(SLC_TOP_N, nsb)
    cs = np.arange(nc)[:, None] * CMP_STRIDE
    js = np.arange(nsb)[None, :] * SLC_BLOCK
    overlap = np.clip(np.minimum(cs + CMP_LEN, js + SLC_BLOCK) - np.maximum(cs, js), 0, None) / CMP_LEN
    overlap[(seq - CMP_LEN) // CMP_STRIDE + 1:] = 0.0
    nsb_pad = -(-nsb // 8) * 8
    ovt = jnp.asarray(np.pad(overlap.T, ((0, nsb_pad - nsb), (0, 0))), BF16)
    q3 = q.reshape(batch, seq, q.shape[1])
    n_real, nsb = nsb, nsb_pad
    assert nsb <= LANES - HEAD_DIM
    o, drop = pl.pallas_call(
        functools.partial(_nsa_cmp_kernel, rep=rep, n_sel=n_sel, n_real=n_real),
        grid=(batch, g, seq // tq),
        in_specs=[pl.BlockSpec((1, tq, rep * LANES), lambda b, gi, i: (b, i, q_off_blocks // rep + gi)),
                  pl.BlockSpec((1, nc, LANES), lambda b, gi, i: (b, 0, gi)),
                  pl.BlockSpec((1, nc, LANES), lambda b, gi, i: (b, 0, gi)),
                  pl.BlockSpec(ovt.shape, lambda b, gi, i: (0, 0))],
        out_specs=[pl.BlockSpec((1, tq, rep * LANES), lambda b, gi, i: (b, i, gi)),
                   pl.BlockSpec((1, 1, tq, LANES), lambda b, gi, i: (b, gi, i, 0))],
        out_shape=[jax.ShapeDtypeStruct((batch, seq, C_HEADS * LANES), F32),
                   jax.ShapeDtypeStruct((batch, g, seq, LANES), BF16)],
        compiler_params=_cparams("parallel", "parallel", "parallel"),
        name="nsa_compressed_select",
    )(q3, kcmp, vcmp, ovt)
    return o.reshape(batch * seq, C_HEADS * LANES), drop


def _nsa_slc_kernel(q_ref, k_ref, vt_ref, drop_ref, o_ref, *, rep, kt):
    tq = q_ref.shape[1]
    t0 = pl.program_id(2) * tq
    n_kt = (t0 + tq + kt - 1) // kt
    upper = lax.broadcasted_iota(jnp.int32, (tq, LANES), 1) >= HEAD_DIM
    drop = drop_ref[0, 0]
    qs = [jnp.where(upper, drop, q_ref[0, :, r * LANES:(r + 1) * LANES] * (HEAD_DIM ** -0.5))
          for r in range(rep)]

    def tile(k0, carry, bias):
        k = k_ref[0, pl.ds(k0, kt), :]
        vt = vt_ref[:, pl.ds(k0, kt)]
        new = []
        for qr, (m, l, acc) in zip(qs, carry):
            s = lax.dot_general(k, qr, _NT, preferred_element_type=F32)
            if bias is not None:
                s = s + bias
            m_new = jnp.maximum(m, jnp.max(s, axis=0, keepdims=True))
            e = jnp.exp(s - m_new)
            corr = jnp.exp(m - m_new)
            l = l * corr + jnp.sum(e, axis=0, keepdims=True)
            acc = acc * corr + jnp.dot(vt, e.astype(BF16), preferred_element_type=F32)
            new.append((m_new, l, acc))
        return tuple(new)

    init = tuple((jnp.full((1, tq), NEG, F32), jnp.zeros((1, tq), F32), jnp.zeros((LANES, tq), F32))
                 for _ in range(rep))
    carry = lax.fori_loop(0, n_kt - 1, lambda j, c: tile(pl.multiple_of(j * kt, kt), c, None), init)
    k_last = pl.multiple_of((n_kt - 1) * kt, kt)
    kpos = k_last + lax.broadcasted_iota(jnp.int32, (kt, tq), 0)
    tpos = t0 + lax.broadcasted_iota(jnp.int32, (kt, tq), 1)
    carry = tile(k_last, carry, jnp.where(kpos <= tpos, 0.0, NEG))
    o_ref[0] = jnp.concatenate([(acc / l).T for _, l, acc in carry], axis=1)


def _nsa_slc(qkv, vt, drop, batch, seq, nblk, q_off, k_off, tq=128, kt=1024):
    g = C_KV_HEADS
    rep = C_HEADS // g
    kt = min(kt, seq)
    arr = qkv.reshape(batch, seq, nblk * LANES)
    out = pl.pallas_call(
        functools.partial(_nsa_slc_kernel, rep=rep, kt=kt),
        grid=(batch, g, seq // tq),
        in_specs=[pl.BlockSpec((1, tq, rep * LANES), lambda b, gi, i: (b, i, q_off // rep + gi)),
                  pl.BlockSpec((1, seq, LANES), lambda b, gi, i: (b, 0, k_off + gi)),
                  pl.BlockSpec((LANES, seq), lambda b, gi, i: (gi, b)),
                  pl.BlockSpec((1, 1, tq, LANES), lambda b, gi, i: (b, gi, i, 0))],
        out_specs=pl.BlockSpec((1, tq, rep * LANES), lambda b, gi, i: (b, i, gi)),
        out_shape=jax.ShapeDtypeStruct((batch, seq, C_HEADS * LANES), F32),
        compiler_params=_cparams("parallel", "parallel", "parallel"),
        name="nsa_selected",
    )(arr, arr, vt, drop)
    return out.reshape(batch * seq, C_HEADS * LANES)


def _even_mixer_layer(x2, batch, seq, tabs, w_in, sinks, w_out, g, b):
    d = x2.shape[1]
    n_heads_in = 3 * A_HEADS + B_Q_HEADS + 2 * B_KV_HEADS
    w = _pad_heads_cols(w_in, n_heads_in).astype(BF16)
    rope = [1] * (2 * A_HEADS) + [0] * A_HEADS + [1] * B_Q_HEADS + [1] * B_KV_HEADS + [0] * B_KV_HEADS
    plan = [(0, c, rope[c]) for c in range(n_heads_in)]
    dilations = [dil for _, dil in A_PATTERNS]
    regroup = [dil for dil in dilations if dil > 1]
    n_a_blocks = 3 * A_HEADS
    qkv, *grouped = _proj(x2, w, tabs, plan, [n_heads_in * LANES], [BF16], dilations=regroup, n_dil=n_a_blocks)
    outs = []
    for window, dilation in A_PATTERNS:
        src, nblk = (qkv, n_heads_in) if dilation == 1 else (grouped[regroup.index(dilation)], n_a_blocks)
        outs.append(_band_attention(src, batch=batch, seq=seq, dilation=dilation, nblk=nblk, q_off=0,
                                    k_off=A_HEADS, v_off=2 * A_HEADS, n_q_heads=A_HEADS, rep=1,
                                    max_dist=window // dilation,
                                    heads=min(4, dilation)))
    qb_off = 3 * A_HEADS
    ob = _band_attention(qkv, batch=batch, seq=seq, dilation=1, nblk=n_heads_in, q_off=qb_off,
                         k_off=qb_off + B_Q_HEADS, v_off=qb_off + B_Q_HEADS + B_KV_HEADS, n_q_heads=B_Q_HEADS,
                         rep=B_Q_HEADS // B_KV_HEADS, max_dist=B_WINDOW - 1, sinks=sinks)
    w_o = _pad_heads_rows(w_out, A_HEADS + B_Q_HEADS).astype(BF16)
    return _outproj_even(outs, dilations, ob, x2, w_o, g.reshape(1, d), b.reshape(1, d))


def _odd_mixer_layer(x2, batch, seq, tabs, w_in, cmpk_pos, cmpk_w1, cmpk_w2, cmpv_pos, cmpv_w1, cmpv_w2, w_out, g, b):
    d = x2.shape[1]
    kvw = C_KV_HEADS * HEAD_DIM
    qw = C_HEADS * HEAD_DIM
    sizes = [qw] + [kvw] * 6 + [3 * C_HEADS]
    offs = np.concatenate([[0], np.cumsum(sizes)])
    wq, wkc, wvc, wks, wvs, wkw, wvw, wgt = [w_in[:, offs[i]:offs[i + 1]] for i in range(8)]
    ph = lambda wpart, n: _pad_heads_cols(wpart, n)
    w = jnp.concatenate([ph(wq, C_HEADS), ph(wks, C_KV_HEADS), ph(wkw, C_KV_HEADS), ph(wvw, C_KV_HEADS),
                         wkc, wvc, jnp.pad(wgt, ((0, 0), (0, LANES - 3 * C_HEADS)))], axis=1).astype(BF16)
    wvs_t = ph(wvs, C_KV_HEADS).T.astype(BF16)
    n16 = C_HEADS + 3 * C_KV_HEADS
    rope16 = [1] * C_HEADS + [TAG_BLOCK] * C_KV_HEADS + [1] * C_KV_HEADS + [0] * C_KV_HEADS
    n_kc = kvw // LANES
    plan = ([(0, c, rope16[c]) for c in range(n16)] + [(1, c, 2) for c in range(n_kc)]
            + [(2, c, 0) for c in range(n_kc)] + [(3, 0, 0)])
    qkv, kc, vc, gate, vs_t = _proj(x2, w, tabs, plan, [n16 * LANES, kvw, kvw, LANES], [BF16, F32, F32, F32],
                                    wt=wvs_t, seq=seq)
    kcmp = _compress(kc, cmpk_pos, cmpk_w1, cmpk_w2, batch, seq)
    vcmp = _compress(vc, cmpv_pos, cmpv_w1, cmpv_w2, batch, seq)
    o_cmp, drop = _nsa_cmp(qkv, kcmp, vcmp, batch, seq, 0)
    ks_off = C_HEADS
    o_slc = _nsa_slc(qkv, vs_t, drop, batch, seq, n16, 0, ks_off)
    kw_off = ks_off + C_KV_HEADS
    o_win = _band_attention(qkv, batch=batch, seq=seq, dilation=1, nblk=n16, q_off=0, k_off=kw_off,
                            v_off=kw_off + C_KV_HEADS, n_q_heads=C_HEADS, rep=C_HEADS // C_KV_HEADS,
                            max_dist=NSA_WINDOW - 1, heads=C_HEADS // C_KV_HEADS)
    w_o = _pad_heads_rows(w_out, C_HEADS).astype(BF16)
    return _outproj_odd(o_cmp, o_slc, o_win, gate, x2, w_o, g.reshape(1, d), b.reshape(1, d))


def kernel(x, positions, even_w_in, even_sinks, even_w_out, odd_w_in, odd_cmpk_pos, odd_cmpk_w1, odd_cmpk_w2, odd_cmpv_pos, odd_cmpv_w1, odd_cmpv_w2, odd_w_out, mix_ln_g, mix_ln_b, moe_router_w, moe_router_b, moe_w_gate, moe_w_up, moe_w_down, moe_sh_gate, moe_sh_up, moe_sh_down, ffn_ln_g, ffn_ln_b):
    batch, seq, d = x.shape
    x2 = x.reshape(batch * seq, d)
    tabs = _rope_tables(positions)
    depth = mix_ln_g.shape[0]
    for layer in range(depth):
        j = layer // 2
        if layer % 2 == 0:
            x2 = _even_mixer_layer(x2, batch, seq, tabs, even_w_in[j], even_sinks[j], even_w_out[j],
                                   mix_ln_g[layer], mix_ln_b[layer])
        else:
            x2 = _odd_mixer_layer(x2, batch, seq, tabs, odd_w_in[j], odd_cmpk_pos[j], odd_cmpk_w1[j], odd_cmpk_w2[j],
                                  odd_cmpv_pos[j], odd_cmpv_w1[j], odd_cmpv_w2[j], odd_w_out[j],
                                  mix_ln_g[layer], mix_ln_b[layer])
        x2 = _moe_layer(x2, moe_router_w[layer], moe_router_b[layer], moe_w_gate, moe_w_up, moe_w_down, layer,
                        moe_sh_gate[layer], moe_sh_up[layer], moe_sh_down[layer],
                        ffn_ln_g[layer].reshape(1, d), ffn_ln_b[layer].reshape(1, d))
    return x2.reshape(batch, seq, d)
```

```python
import functools

import numpy as np
import jax
import jax.numpy as jnp
from jax import lax
from jax.experimental import pallas as pl
from jax.experimental.pallas import tpu as pltpu

F32 = jnp.float32
BF16 = jnp.bfloat16

LANES = 128
HEAD_DIM = 64
ROT_DIM = HEAD_DIM // 4
ROT_HALF = ROT_DIM // 2
ROPE_THETA = 500000.0
QBLK = 128
A_HEADS = 8
A_PATTERNS = ((128, 1), (512, 4), (2048, 16))
B_Q_HEADS = 8
B_KV_HEADS = 2
B_WINDOW = 128
C_HEADS = 16
C_KV_HEADS = 4
CMP_LEN = 32
CMP_STRIDE = 16
CMP_HIDDEN = 2 * HEAD_DIM
SLC_BLOCK = 64
SLC_SHIFT = 6
SLC_TOP_N = 16
NSA_WINDOW = 512
SELECT_FORCE = 1.0e4
N_EXPERTS = 256
TOP_K = 8
N_GROUPS = 8
TOPK_GROUPS = 4
ROUTED_SCALE = 2.5
MOE_BLOCK = 256
DEPTH = 2
DEEPNORM_ALPHA = (2 * DEPTH) ** 0.25
LN_EPS = 1e-5
NEG = -1.0e30
VMEM_LIMIT = 56 * 1024 * 1024

_NT = (((1,), (1,)), ((), ()))


def _cparams(*sem):
    return pltpu.CompilerParams(dimension_semantics=sem, vmem_limit_bytes=VMEM_LIMIT)


def _split_bf16(a):
    hi = a.astype(BF16)
    lo = (a - hi.astype(F32)).astype(BF16)
    return hi, lo


def _layer_norm(y, g, b):
    mu = jnp.mean(y, axis=-1, keepdims=True)
    d = y - mu
    var = jnp.mean(d * d, axis=-1, keepdims=True)
    return d * lax.rsqrt(var + LN_EPS) * g + b


def _silu(a):
    return a * jax.nn.sigmoid(a)


TAG_BLOCK = 3


def _proj_kernel(*refs, plan, n_main, transposed, dilations, n_dil, seq):
    n_in = 4 if transposed else 3
    x_ref, w_ref, tab_ref = refs[:3]
    out_refs = refs[n_in:n_in + n_main]
    extra = list(refs[n_in + n_main:])
    x = x_ref[...].astype(BF16)
    tm = x.shape[0]
    if transposed:
        out_t = extra.pop(0)
        out_t[...] = lax.dot_general(refs[3][...], x, _NT, preferred_element_type=F32).astype(out_t.dtype)
    dil_refs = [extra.pop(0) for _ in dilations]
    stage = extra.pop(0) if dilations else None
    nblk = len(plan)
    for c0 in range(0, nblk, 2):
        nb = min(2, nblk - c0)
        acc = jnp.dot(x, w_ref[:, c0 * LANES:(c0 + nb) * LANES], preferred_element_type=F32)
        for j in range(nb):
            blk = acc[:, j * LANES:(j + 1) * LANES]
            dst, dblk, mode = plan[c0 + j]
            if mode == TAG_BLOCK:
                pos = lax.rem(pl.program_id(0) * tm, seq) + lax.broadcasted_iota(jnp.int32, (tm, LANES), 0)
                tag = lax.broadcasted_iota(jnp.int32, (tm, LANES), 1) - HEAD_DIM == jnp.right_shift(pos, SLC_SHIFT)
                mode = 1
            else:
                tag = None
            if mode:
                off = (mode - 1) * 3 * LANES
                cos = tab_ref[:, off:off + LANES]
                s_lo = tab_ref[:, off + LANES:off + 2 * LANES]
                s_hi = tab_ref[:, off + 2 * LANES:off + 3 * LANES]
                blk = (blk * cos + pltpu.roll(blk, LANES - ROT_HALF, 1) * s_lo
                       + pltpu.roll(blk, ROT_HALF, 1) * s_hi)
            if tag is not None:
                blk = jnp.where(tag, NEG, blk)
            o_ref = out_refs[dst]
            o_ref[:, dblk * LANES:(dblk + 1) * LANES] = blk.astype(o_ref.dtype)
            c = c0 + j
            if dilations and c < n_dil:
                stage[...] = blk
                for dil, d_ref in zip(dilations, dil_refs):
                    for r in range(dil):
                        col = (r * n_dil + c) * LANES
                        d_ref[:, col:col + LANES] = stage[pl.ds(r, tm // dil, stride=dil), :].astype(d_ref.dtype)


def _proj(x2, w, tabs, plan, out_cols, out_dtypes, wt=None, dilations=(), n_dil=0, seq=0, tm=512):
    t, d = x2.shape
    ncol = w.shape[1]
    out_shape = [jax.ShapeDtypeStruct((t, c), dt) for c, dt in zip(out_cols, out_dtypes)]
    once = pl.Buffered(1)
    in_specs = [pl.BlockSpec((tm, d), lambda i: (i, 0)),
                pl.BlockSpec((d, ncol), lambda i: (0, 0), pipeline_mode=once),
                pl.BlockSpec((tm, tabs.shape[1]), lambda i: (i, 0))]
    out_specs = [pl.BlockSpec((tm, c), lambda i: (i, 0)) for c in out_cols]
    args = [x2, w, tabs]
    if wt is not None:
        in_specs.append(pl.BlockSpec(wt.shape, lambda i: (0, 0), pipeline_mode=once))
        out_specs.append(pl.BlockSpec((wt.shape[0], tm), lambda i: (0, i)))
        out_shape.append(jax.ShapeDtypeStruct((wt.shape[0], t), BF16))
        args.append(wt)
    for dil in dilations:
        out_specs.append(pl.BlockSpec((tm // dil, dil * n_dil * LANES), lambda i: (i, 0)))
        out_shape.append(jax.ShapeDtypeStruct((t // dil, dil * n_dil * LANES), BF16))
    return pl.pallas_call(
        functools.partial(_proj_kernel, plan=tuple(plan), n_main=len(out_cols), transposed=wt is not None,
                          dilations=tuple(dilations), n_dil=n_dil, seq=seq),
        grid=(t // tm,),
        in_specs=in_specs,
        out_specs=out_specs,
        out_shape=out_shape,
        scratch_shapes=[pltpu.VMEM((tm, LANES), F32)] if dilations else [],
        compiler_params=_cparams("parallel"),
        name="proj_rope",
    )(*args)


def _rope_tables(positions):
    t = positions.size
    inv_freq = jnp.asarray(ROPE_THETA ** (-np.arange(0, ROT_DIM, 2) / ROT_DIM), F32)
    ang = positions.astype(F32).reshape(t, 1) * inv_freq
    trig = jnp.concatenate([jnp.cos(ang), jnp.sin(ang)], axis=1)
    place = np.zeros((2 * ROT_HALF, 6 * LANES), np.float32)
    const = np.zeros((6 * LANES,), np.float32)
    for base, heads in ((0, (0,)), (3 * LANES, (0, HEAD_DIM))):
        const[base:base + LANES] = 1.0
        for h in heads:
            for i in range(ROT_HALF):
                place[i, base + h + i] = 1.0
                place[i, base + h + ROT_HALF + i] = 1.0
                const[base + h + i] = const[base + h + ROT_HALF + i] = 0.0
                place[ROT_HALF + i, base + LANES + h + i] = -1.0
                place[ROT_HALF + i, base + 2 * LANES + h + ROT_HALF + i] = 1.0
    return jnp.dot(trig, jnp.asarray(place), precision=lax.Precision.HIGHEST) + jnp.asarray(const)


def _pad_heads_cols(w, n_heads):
    d = w.shape[0]
    w = w.reshape(d, n_heads, HEAD_DIM)
    return jnp.pad(w, ((0, 0), (0, 0), (0, LANES - HEAD_DIM))).reshape(d, n_heads * LANES)


def _pad_heads_rows(w, n_heads):
    d = w.shape[1]
    w = w.reshape(n_heads, HEAD_DIM, d)
    return jnp.pad(w, ((0, 0), (0, LANES - HEAD_DIM), (0, 0))).reshape(n_heads * LANES, d)


def _band_kernel(*refs, back, max_dist, length, qrows, heads, shared_kv, in_flight, has_sink):
    if has_sink:
        sink_ref, q_ref, k_ref, v_ref, o_ref, bias_sc = refs
    else:
        q_ref, k_ref, v_ref, o_ref, bias_sc = refs
    nq = length // qrows
    window = qrows + back
    chains = [tuple(range(heads))] if shared_kv else [(h,) for h in range(heads)]
    stack = len(chains[0])
    lane = lax.broadcasted_iota(jnp.int32, (stack * qrows, LANES), 1)
    first_head = pl.program_id(2) * heads

    def band_bias(offset):
        row = lax.broadcasted_iota(jnp.int32, (qrows, window), 0)
        col = lax.broadcasted_iota(jnp.int32, (qrows, window), 1)
        dist = offset + row - col
        tile = jnp.where((dist >= 0) & (dist <= max_dist), 0.0, NEG)
        return jnp.concatenate([tile] * stack, axis=0) if stack > 1 else tile

    bias_sc[...] = band_bias(back)

    def block(hs, q0, k0, bias):
        kv = slice(0, LANES) if shared_kv else slice(hs[0] * LANES, (hs[0] + 1) * LANES)
        q = jnp.concatenate([q_ref[0, pl.ds(q0, qrows), h * LANES:(h + 1) * LANES] for h in hs], axis=0)
        q = q * (HEAD_DIM ** -0.5)
        k = k_ref[0, pl.ds(k0, window), kv]
        v = v_ref[0, pl.ds(k0, window), kv]
        s = lax.dot_general(q, k, _NT, preferred_element_type=F32) + bias
        m = jnp.max(s, axis=-1, keepdims=True)
        e = jnp.exp(s - m)
        den = jnp.sum(e, axis=-1, keepdims=True)
        if has_sink:
            sink = jnp.concatenate([jnp.full((qrows, 1), sink_ref[first_head + h], F32) for h in hs], axis=0)
            den = den + jnp.exp(sink - m)
        o = jnp.dot(e.astype(BF16), v, preferred_element_type=F32) / den
        lse = m + jnp.log(den)
        out = jnp.where(lane < HEAD_DIM, o, lse)
        for i, h in enumerate(hs):
            o_ref[0, pl.ds(q0, qrows), h * LANES:(h + 1) * LANES] = out[i * qrows:(i + 1) * qrows]

    n_clipped = min(-(-back // qrows), nq)
    for qi in range(n_clipped):
        bias = band_bias(qi * qrows)
        for hs in chains:
            block(hs, qi * qrows, 0, bias)

    steady = nq - n_clipped
    per_iter = max(1, min(in_flight // heads, steady))
    n_iter = steady // per_iter

    def body(it, carry):
        aligned = lambda v: v if isinstance(v, int) else pl.multiple_of(v, QBLK)
        for j in range(per_iter):
            q0 = aligned((n_clipped + it * per_iter + j) * qrows)
            for hs in chains:
                block(hs, q0, aligned(q0 - back), bias_sc[...])
        return carry

    if n_iter == 1:
        body(0, 0)
    elif n_iter:
        lax.fori_loop(0, n_iter, body, 0)
    for qi in range(n_clipped + n_iter * per_iter, nq):
        for hs in chains:
            block(hs, qi * qrows, qi * qrows - back, bias_sc[...])


def _band_attention(qkv, *, batch, seq, dilation, nblk, q_off, k_off, v_off, n_q_heads, rep, max_dist, sinks=None,
                    heads=1, in_flight=16):
    length = seq // dilation
    back = min(-(-max_dist // QBLK) * QBLK, length - QBLK)
    qrows = QBLK
    shared_kv = rep > 1
    assert n_q_heads % heads == 0 and (rep == 1 or rep % heads == 0)
    kv_heads = 1 if shared_kv else heads
    arr = qkv.reshape(batch, length, dilation * nblk * LANES)
    assert q_off % heads == 0 and k_off % kv_heads == 0 and v_off % kv_heads == 0 and nblk % heads == 0
    qspec = pl.BlockSpec((1, length, heads * LANES),
                         lambda b, r, h: (b, 0, (r * nblk + q_off) // heads + h))
    kvspec = lambda off: pl.BlockSpec(
        (1, length, kv_heads * LANES),
        lambda b, r, h: (b, 0, (r * nblk + off) // kv_heads + (h * heads // rep if shared_kv else h)))
    in_specs = [qspec, kvspec(k_off), kvspec(v_off)]
    args = [arr, arr, arr]
    if sinks is not None:
        in_specs = [pl.BlockSpec(memory_space=pltpu.SMEM)] + in_specs
        args = [sinks.reshape(-1).astype(F32)] + args
    out = pl.pallas_call(
        functools.partial(_band_kernel, back=back, max_dist=max_dist, length=length, qrows=qrows, heads=heads,
                          shared_kv=shared_kv, in_flight=in_flight, has_sink=sinks is not None),
        grid=(batch, dilation, n_q_heads // heads),
        in_specs=in_specs,
        out_specs=pl.BlockSpec((1, length, heads * LANES), lambda b, r, h: (b, 0, r * (n_q_heads // heads) + h)),
        out_shape=jax.ShapeDtypeStruct((batch, length, dilation * n_q_heads * LANES), F32),
        scratch_shapes=[pltpu.VMEM(((heads if shared_kv else 1) * qrows, qrows + back), F32)],
        compiler_params=_cparams("parallel", "parallel", "parallel"),
        name="band_attention",
    )(*args)
    return out.reshape(batch * length, dilation * n_q_heads * LANES)


def _outproj_even_kernel(o1_ref, o2_ref, o3_ref, ob_ref, x_ref, w_ref, g_ref, b_ref, out_ref, *nat_refs,
                         n_a, n_b, dilations):
    tm = x_ref.shape[0]
    lane = lax.broadcasted_iota(jnp.int32, (tm, LANES), 1)
    real = lane < HEAD_DIM
    pattern_refs = []
    nat_refs = list(nat_refs)
    for o_ref, dil in zip((o1_ref, o2_ref, o3_ref), dilations):
        if dil == 1:
            pattern_refs.append(o_ref)
            continue
        nat = nat_refs.pop(0)
        for r in range(dil):
            for h in range(n_a):
                col = (r * n_a + h) * LANES
                nat[h, pl.ds(r, tm // dil, stride=dil), :] = o_ref[:, col:col + LANES]
        pattern_refs.append(nat)
    parts = []
    for h in range(n_a):
        sl = slice(h * LANES, (h + 1) * LANES)
        outs = [r[:, sl] if r.ndim == 2 else r[h] for r in pattern_refs]
        lses = [jnp.where(real, pltpu.roll(a, HEAD_DIM, 1), a) for a in outs]
        m = jnp.maximum(jnp.maximum(lses[0], lses[1]), lses[2])
        es = [jnp.exp(l - m) for l in lses]
        num = es[0] * outs[0] + es[1] * outs[1] + es[2] * outs[2]
        den = es[0] + es[1] + es[2]
        parts.append(jnp.where(real, num / den, 0.0).astype(BF16))
    for h in range(n_b):
        parts.append(jnp.where(real, ob_ref[:, h * LANES:(h + 1) * LANES], 0.0).astype(BF16))
    a = jnp.concatenate(parts, axis=1)
    mixed = jnp.dot(a, w_ref[...], preferred_element_type=F32)
    y = DEEPNORM_ALPHA * x_ref[...] + mixed
    out_ref[...] = _layer_norm(y, g_ref[...], b_ref[...])


def _outproj_even(outs, dilations, ob, x2, w, g, b, tm=256):
    t, d = x2.shape
    n_a, n_b = outs[0].shape[1] // (dilations[0] * LANES), ob.shape[1] // LANES
    row = lambda c: pl.BlockSpec((tm, c), lambda i: (i, 0))
    grouped = lambda a, dil: pl.BlockSpec((tm // dil, a.shape[1]), lambda i: (i, 0))
    full = lambda a: pl.BlockSpec(a.shape, lambda i: (0,) * a.ndim)
    return pl.pallas_call(
        functools.partial(_outproj_even_kernel, n_a=n_a, n_b=n_b, dilations=tuple(dilations)),
        grid=(t // tm,),
        in_specs=[grouped(o, dil) for o, dil in zip(outs, dilations)] + [row(ob.shape[1]), row(d),
                                                                          full(w), full(g), full(b)],
        out_specs=row(d),
        out_shape=jax.ShapeDtypeStruct((t, d), F32),
        scratch_shapes=[pltpu.VMEM((n_a, tm, LANES), F32) for dil in dilations if dil > 1],
        compiler_params=_cparams("parallel"),
        name="outproj_even_ln",
    )(*outs, ob, x2, w, g, b)


def _outproj_odd_kernel(oc_ref, os_ref, ow_ref, gate_ref, e_ref, x_ref, w_ref, g_ref, b_ref, out_ref):
    gate = jax.nn.sigmoid(gate_ref[...])
    ghi, glo = _split_bf16(gate)
    acc = None
    for j, o_ref in enumerate((oc_ref, os_ref, ow_ref)):
        ej = e_ref[j]
        gfull = jnp.dot(ghi, ej, preferred_element_type=F32) + jnp.dot(glo, ej, preferred_element_type=F32)
        term = gfull * o_ref[...]
        acc = term if acc is None else acc + term
    mixed = jnp.dot(acc.astype(BF16), w_ref[...], preferred_element_type=F32)
    y = DEEPNORM_ALPHA * x_ref[...] + mixed
    out_ref[...] = _layer_norm(y, g_ref[...], b_ref[...])


def _gate_expanders(n_heads):
    e = np.zeros((3, LANES, n_heads * LANES), np.float32)
    for j in range(3):
        for h in range(n_heads):
            e[j, 3 * h + j, h * LANES:h * LANES + HEAD_DIM] = 1.0
    return jnp.asarray(e, BF16)


def _outproj_odd(oc, osl, ow, gate, x2, w, g, b, tm=256):
    t, d = x2.shape
    n_heads = oc.shape[1] // LANES
    e = _gate_expanders(n_heads)
    row = lambda c: pl.BlockSpec((tm, c), lambda i: (i, 0))
    full = lambda a: pl.BlockSpec(a.shape, lambda i: (0,) * a.ndim)
    return pl.pallas_call(
        _outproj_odd_kernel,
        grid=(t // tm,),
        in_specs=[row(oc.shape[1]), row(osl.shape[1]), row(ow.shape[1]), row(LANES), full(e), row(d),
                  full(w), full(g), full(b)],
        out_specs=row(d),
        out_shape=jax.ShapeDtypeStruct((t, d), F32),
        compiler_params=_cparams("parallel"),
        name="outproj_odd_ln",
    )(oc, osl, ow, gate, e, x2, w, g, b)


def _router_kernel(x_ref, whi_ref, wlo_ref, bias_ref, eidx_ref, gate_ref, rank_ref, cnt_ref):
    n_exp = whi_ref.shape[0]
    tm = x_ref.shape[0]
    per_group = n_exp // N_GROUPS
    xhi, xlo = _split_bf16(x_ref[...])
    whi, wlo = whi_ref[...], wlo_ref[...]
    dg = lambda a, b: lax.dot_general(a, b, _NT, preferred_element_type=F32)
    logits = dg(whi, xhi) + dg(whi, xlo) + dg(wlo, xhi)
    aff = jax.nn.sigmoid(logits)
    biased = aff + bias_ref[...]
    gio = lax.broadcasted_iota(jnp.int32, (per_group, tm), 0).astype(F32)
    blocks, scores = [], []
    for g in range(N_GROUPS):
        blk = biased[g * per_group:(g + 1) * per_group, :]
        m1 = jnp.max(blk, axis=0, keepdims=True)
        first = jnp.min(jnp.where(blk == m1, gio, float(per_group)), axis=0, keepdims=True)
        m2 = jnp.max(jnp.where(gio == first, -jnp.inf, blk), axis=0, keepdims=True)
        blocks.append(blk)
        scores.append(m1 + m2)
    masked = []
    for g in range(N_GROUPS):
        rank = jnp.zeros((1, tm), F32)
        for o in range(N_GROUPS):
            if o == g:
                continue
            beats = scores[o] >= scores[g] if o < g else scores[o] > scores[g]
            rank = rank + jnp.where(beats, 1.0, 0.0)
        masked.append(jnp.where(rank < TOPK_GROUPS, blocks[g], -jnp.inf))
    cur = jnp.concatenate(masked, axis=0)
    eio = lax.broadcasted_iota(jnp.int32, (n_exp, tm), 0).astype(F32)
    ids, gs = [], []
    for _ in range(TOP_K):
        m = jnp.max(cur, axis=0, keepdims=True)
        idx = jnp.min(jnp.where(cur == m, eio, float(n_exp)), axis=0, keepdims=True)
        hit = eio == idx
        gs.append(jnp.sum(jnp.where(hit, aff, 0.0), axis=0, keepdims=True))
        ids.append(idx)
        cur = jnp.where(hit, -jnp.inf, cur)
    gates = jnp.concatenate(gs, axis=0)
    gates = gates / jnp.sum(gates, axis=0, keepdims=True) * ROUTED_SCALE
    eidx_ref[...] = jnp.concatenate(ids, axis=0).astype(jnp.int32)
    gate_ref[...] = gates
    @pl.when(pl.program_id(0) == 0)
    def _():
        cnt_ref[...] = jnp.zeros(cnt_ref.shape, F32)

    onehot = jnp.zeros((n_exp, tm), F32)
    for idx in ids:
        onehot = onehot + jnp.where(eio == idx, 1.0, 0.0)
    earlier = jnp.where(lax.broadcasted_iota(jnp.int32, (tm, tm), 0) < lax.broadcasted_iota(jnp.int32, (tm, tm), 1),
                        1.0, 0.0).astype(BF16)
    before = cnt_ref[...] + jnp.dot(onehot.astype(BF16), earlier, preferred_element_type=F32)
    ranks = [jnp.sum(jnp.where(eio == idx, before, 0.0), axis=0, keepdims=True) for idx in ids]
    rank_ref[...] = jnp.concatenate(ranks, axis=0).astype(jnp.int32)
    cnt_ref[...] = cnt_ref[...] + jnp.sum(onehot, axis=1, keepdims=True)


def _router(x2, router_w, router_b, tm=256):
    t, d = x2.shape
    n_exp = router_w.shape[1]
    whi, wlo = _split_bf16(router_w.T)
    bias = router_b.reshape(n_exp, 1).astype(F32)
    full = lambda a: pl.BlockSpec(a.shape, lambda i: (0,) * a.ndim)
    per_tok = pl.BlockSpec((TOP_K, tm), lambda i: (0, i))
    return pl.pallas_call(
        _router_kernel,
        grid=(t // tm,),
        in_specs=[pl.BlockSpec((tm, d), lambda i: (i, 0)), full(whi), full(wlo), full(bias)],
        out_specs=[per_tok, per_tok, per_tok, pl.BlockSpec((n_exp, 1), lambda i: (0, 0))],
        out_shape=[jax.ShapeDtypeStruct((TOP_K, t), jnp.int32), jax.ShapeDtypeStruct((TOP_K, t), F32),
                   jax.ShapeDtypeStruct((TOP_K, t), jnp.int32), jax.ShapeDtypeStruct((n_exp, 1), F32)],
        compiler_params=_cparams("arbitrary"),
        name="moe_router",
    )(x2, whi, wlo, bias)


def _moe_dest_kernel(eidx_ref, rank_ref, start_ref, dest_ref):
    n_exp = start_ref.shape[0]
    tm = eidx_ref.shape[1]
    eio = lax.broadcasted_iota(jnp.int32, (n_exp, tm), 0)
    start = start_ref[...]
    rows = []
    for k in range(TOP_K):
        seg = jnp.sum(jnp.where(eio == eidx_ref[k:k + 1, :], start, 0.0), axis=0, keepdims=True)
        rows.append(seg.astype(jnp.int32) + rank_ref[k:k + 1, :])
    dest_ref[...] = jnp.concatenate(rows, axis=0)


def _moe_dest(eidx, rank, seg_start, tm=256):
    t = eidx.shape[1]
    per_tok = pl.BlockSpec((TOP_K, tm), lambda i: (0, i))
    return pl.pallas_call(
        _moe_dest_kernel,
        grid=(t // tm,),
        in_specs=[per_tok, per_tok, pl.BlockSpec(seg_start.shape, lambda i: (0, 0))],
        out_specs=per_tok,
        out_shape=jax.ShapeDtypeStruct((TOP_K, t), jnp.int32),
        compiler_params=_cparams("parallel"),
        name="moe_dest",
    )(eidx, rank, seg_start)


def _to_slabs(ref, value):
    rows, width = value.shape
    n_chunks = width // LANES
    for c in range(n_chunks):
        ref[pl.ds(c, rows, stride=n_chunks), :] = value[:, c * LANES:(c + 1) * LANES]


def _from_slabs(ref, first_row, rows, n_chunks):
    return jnp.concatenate([ref[pl.ds(first_row * n_chunks + c, rows, stride=n_chunks), :]
                            for c in range(n_chunks)], axis=1)


def _moe_zero_kernel(blk_ref, xs_ref):
    xs_ref[...] = jnp.zeros(xs_ref.shape, xs_ref.dtype)


def _moe_zero_padding(zero_blocks, n_blocks, n_chunks):
    grid_spec = pltpu.PrefetchScalarGridSpec(
        num_scalar_prefetch=1,
        grid=(zero_blocks.shape[0],),
        in_specs=[],
        out_specs=pl.BlockSpec((MOE_BLOCK * n_chunks, LANES), lambda i, blk: (blk[i], 0)),
    )
    return pl.pallas_call(
        _moe_zero_kernel,
        grid_spec=grid_spec,
        out_shape=jax.ShapeDtypeStruct((n_blocks * MOE_BLOCK * n_chunks, LANES), jnp.uint32),
        compiler_params=_cparams("arbitrary"),
        name="moe_zero_padding",
    )(zero_blocks)


def _pack_bf16_pairs(x):
    half = x.shape[1] // 2
    lo = lax.bitcast_convert_type(x[:, :half].astype(BF16).astype(F32), jnp.uint32)
    hi = lax.bitcast_convert_type(x[:, half:].astype(BF16).astype(F32), jnp.uint32)
    return jnp.right_shift(lo, jnp.uint32(16)) | (hi & jnp.uint32(0xFFFF0000))


def _unpack_bf16_pairs(w):
    lo = lax.bitcast_convert_type(jnp.left_shift(w, jnp.uint32(16)), F32).astype(BF16)
    hi = lax.bitcast_convert_type(w & jnp.uint32(0xFFFF0000), F32).astype(BF16)
    return jnp.concatenate([lo, hi], axis=1)


def _moe_dispatch_kernel(dest_ref, x_ref, xs_in, xs_out, buf, sem):
    del xs_in
    tm = x_ref.shape[0]
    n_chunks = buf.shape[0] // tm
    _to_slabs(buf, _pack_bf16_pairs(x_ref[...]))

    def issue(i, c):
        src = buf.at[pl.ds(pl.multiple_of(i * n_chunks, n_chunks), n_chunks)]
        for k in range(TOP_K):
            r = pl.multiple_of(dest_ref[k, i] * n_chunks, n_chunks)
            pltpu.make_async_copy(src, xs_out.at[pl.ds(r, n_chunks)], sem).start(priority=k % 2)
        return c

    lax.fori_loop(0, tm, issue, 0)
    for _ in range(TOP_K):
        pltpu.make_async_copy(buf, xs_out.at[pl.ds(0, tm * n_chunks)], sem).wait()


def _moe_dispatch(dest, x2, xs, tm=256):
    t, d = x2.shape
    n_chunks = d // 2 // LANES
    return pl.pallas_call(
        _moe_dispatch_kernel,
        grid=(t // tm,),
        in_specs=[pl.BlockSpec((TOP_K, tm), lambda i: (0, i), memory_space=pltpu.SMEM),
                  pl.BlockSpec((tm, d), lambda i: (i, 0)),
                  pl.BlockSpec(memory_space=pl.ANY)],
        out_specs=pl.BlockSpec(memory_space=pl.ANY),
        out_shape=jax.ShapeDtypeStruct(xs.shape, xs.dtype),
        scratch_shapes=[pltpu.VMEM((tm * n_chunks, LANES), jnp.uint32), pltpu.SemaphoreType.DMA(())],
        input_output_aliases={2: 0},
        compiler_params=_cparams("arbitrary"),
        name="moe_dispatch",
    )(dest, x2, xs)


def _moe_ffn_kernel(be_ref, nu_ref, xs_ref, wg_ref, wu_ref, wd_ref, y_ref, wg_sc, wu_sc, wd_sc):
    b = pl.program_id(0)

    @pl.when(b < nu_ref[0])
    def _():
        @pl.when((b == 0) | (be_ref[b] != be_ref[jnp.maximum(b - 1, 0)]))
        def _():
            wg_sc[...] = wg_ref[0].astype(BF16)
            wu_sc[...] = wu_ref[0].astype(BF16)
            wd_sc[...] = wd_ref[0].astype(BF16)

        d = wg_sc.shape[0]
        xb = _unpack_bf16_pairs(_from_slabs(xs_ref, 0, MOE_BLOCK, d // 2 // LANES))
        gp = jnp.dot(xb, wg_sc[...], preferred_element_type=F32)
        up = jnp.dot(xb, wu_sc[...], preferred_element_type=F32)
        h = (_silu(gp) * up).astype(BF16)
        _to_slabs(y_ref, jnp.dot(h, wd_sc[...], preferred_element_type=F32))


def _moe_ffn(xs, block_expert, n_used, w_gate, w_up, w_down, layer):
    n_blocks = block_expert.shape[0]
    d, ff = w_gate.shape[2], w_gate.shape[3]
    last = lambda b, nu: jnp.minimum(b, nu[0] - 1)
    grid_spec = pltpu.PrefetchScalarGridSpec(
        num_scalar_prefetch=2,
        grid=(n_blocks,),
        in_specs=[
            pl.BlockSpec((MOE_BLOCK * (d // 2 // LANES), LANES), lambda b, be, nu: (last(b, nu), 0)),
            pl.BlockSpec((None, 1, d, ff), lambda b, be, nu: (layer, be[last(b, nu)], 0, 0)),
            pl.BlockSpec((None, 1, d, ff), lambda b, be, nu: (layer, be[last(b, nu)], 0, 0)),
            pl.BlockSpec((None, 1, ff, d), lambda b, be, nu: (layer, be[last(b, nu)], 0, 0)),
        ],
        out_specs=pl.BlockSpec((MOE_BLOCK * (d // LANES), LANES), lambda b, be, nu: (last(b, nu), 0)),
        scratch_shapes=[pltpu.VMEM((d, ff), BF16), pltpu.VMEM((d, ff), BF16), pltpu.VMEM((ff, d), BF16)],
    )
    return pl.pallas_call(
        _moe_ffn_kernel,
        grid_spec=grid_spec,
        out_shape=jax.ShapeDtypeStruct((n_blocks * MOE_BLOCK * (d // LANES), LANES), F32),
        compiler_params=_cparams("arbitrary"),
        name="moe_expert_ffn",
    )(block_expert, n_used, xs, w_gate, w_up, w_down)


def _moe_combine_kernel(dest_ref, y_hbm, gate_ref, x_ref, sg_ref, su_ref, sd_ref, g_ref, b_ref, out_ref, buf, sem):
    tm, d = x_ref.shape
    n_chunks = d // LANES

    def issue(i, c):
        for k in range(TOP_K):
            r = pl.multiple_of(dest_ref[k, i] * n_chunks, n_chunks)
            slot = pl.multiple_of((k * tm + i) * n_chunks, n_chunks)
            pltpu.make_async_copy(y_hbm.at[pl.ds(r, n_chunks)], buf.at[pl.ds(slot, n_chunks)],
                                  sem).start(priority=k % 2)
        return c

    lax.fori_loop(0, tm, issue, 0)
    x = x_ref[...]
    xb = x.astype(BF16)
    hs = _silu(jnp.dot(xb, sg_ref[...], preferred_element_type=F32)) * jnp.dot(xb, su_ref[...], preferred_element_type=F32)
    shared = jnp.dot(hs.astype(BF16), sd_ref[...], preferred_element_type=F32)
    pltpu.make_async_copy(y_hbm.at[pl.ds(0, TOP_K * tm * n_chunks)], buf, sem).wait()
    gates = gate_ref[...]
    routed = _from_slabs(buf, 0, tm, n_chunks) * gates[:, 0:1]
    for k in range(1, TOP_K):
        routed = routed + _from_slabs(buf, k * tm, tm, n_chunks) * gates[:, k:k + 1]
    y = DEEPNORM_ALPHA * x + (routed + shared)
    out_ref[...] = _layer_norm(y, g_ref[...], b_ref[...])


def _moe_combine(dest, y, gates_t, x2, sh_gate, sh_up, sh_down, g, b, tm=128):
    t, d = x2.shape
    row = lambda c: pl.BlockSpec((tm, c), lambda i: (i, 0))
    full = lambda a: pl.BlockSpec(a.shape, lambda i: (0,) * a.ndim)
    return pl.pallas_call(
        _moe_combine_kernel,
        grid=(t // tm,),
        in_specs=[pl.BlockSpec((TOP_K, tm), lambda i: (0, i), memory_space=pltpu.SMEM),
                  pl.BlockSpec(memory_space=pl.ANY),
                  row(TOP_K), row(d), full(sh_gate), full(sh_up), full(sh_down), full(g), full(b)],
        out_specs=row(d),
        out_shape=jax.ShapeDtypeStruct((t, d), F32),
        scratch_shapes=[pltpu.VMEM((TOP_K * tm * (d // LANES), LANES), F32), pltpu.SemaphoreType.DMA(())],
        compiler_params=_cparams("arbitrary"),
        name="moe_combine_ln",
    )(dest, y, gates_t, x2, sh_gate, sh_up, sh_down, g, b)


def _moe_segments(counts, n_tok):
    n_exp = counts.shape[0]
    n_blocks = -(-n_tok * TOP_K // MOE_BLOCK) + n_exp
    nblk = (counts.reshape(n_exp).astype(jnp.int32) + MOE_BLOCK - 1) // MOE_BLOCK
    blk_end = jnp.cumsum(nblk)
    seg_start = ((blk_end - nblk) * MOE_BLOCK).astype(F32).reshape(n_exp, 1)
    block_expert = jnp.sum((blk_end[None, :] <= jnp.arange(n_blocks)[:, None]).astype(jnp.int32), axis=1)
    block_expert = jnp.minimum(block_expert, n_exp - 1)
    n_used = blk_end[-1]
    zero_blocks = jnp.maximum(blk_end - 1, 0).astype(jnp.int32)
    return seg_start, block_expert.astype(jnp.int32), n_used.astype(jnp.int32).reshape(1), zero_blocks, n_blocks


def _moe_layer(x2, router_w, router_b, w_gate, w_up, w_down, layer, sh_gate, sh_up, sh_down, g, b):
    t, d = x2.shape
    eidx, gates, rank, counts = _router(x2, router_w, router_b)
    seg_start, block_expert, n_used, zero_blocks, n_blocks = _moe_segments(counts, t)
    dest = _moe_dest(eidx, rank, seg_start)
    xs = _moe_dispatch(dest, x2, _moe_zero_padding(zero_blocks, n_blocks, d // 2 // LANES))
    y = _moe_ffn(xs, block_expert, n_used, w_gate, w_up, w_down, layer)
    return _moe_combine(dest, y, gates.T, x2, sh_gate.astype(BF16), sh_up.astype(BF16), sh_down.astype(BF16), g, b)


def _compress_kernel(x_ref, pa_ref, pb_ref, wa_ref, wb_ref, w2_ref, out_ref):
    x = x_ref[0]
    nc = x.shape[0]
    ha = jnp.dot((x + pa_ref[...]).astype(BF16), wa_ref[...], preferred_element_type=F32)
    hb = jnp.dot((x + pb_ref[...]).astype(BF16), wb_ref[...], preferred_element_type=F32)
    h = ha + pltpu.roll(hb, nc - 1, 0)
    h = jax.nn.gelu(h, approximate=True)
    out_ref[0] = jnp.dot(h.astype(BF16), w2_ref[...], preferred_element_type=F32).astype(out_ref.dtype)


def _compress(kc, pos, w1, w2, batch, seq):
    g = C_KV_HEADS
    nch = seq // CMP_STRIDE
    half = CMP_LEN // 2
    x = kc.reshape(batch, nch, half * g * HEAD_DIM)
    eye = jnp.eye(g, dtype=F32)
    w1r = w1.reshape(CMP_LEN, HEAD_DIM, CMP_HIDDEN)
    expand = lambda wpart: jnp.einsum('jdh,ge->jgdeh', wpart, eye).reshape(half * g * HEAD_DIM, g * CMP_HIDDEN)
    wa, wb = expand(w1r[:half]).astype(BF16), expand(w1r[half:]).astype(BF16)
    w2e = jnp.einsum('hd,ge->ghed', jnp.pad(w2, ((0, 0), (0, LANES - HEAD_DIM))), eye)
    w2e = w2e.reshape(g * CMP_HIDDEN, g * LANES).astype(BF16)
    tile_pos = lambda p: jnp.broadcast_to(p[:, None, :], (half, g, HEAD_DIM)).reshape(1, half * g * HEAD_DIM)
    pa, pb = tile_pos(pos[:half]), tile_pos(pos[half:])
    full = lambda a: pl.BlockSpec(a.shape, lambda i: (0,) * a.ndim)
    return pl.pallas_call(
        _compress_kernel,
        grid=(batch,),
        in_specs=[pl.BlockSpec((1, nch, x.shape[2]), lambda i: (i, 0, 0)), full(pa), full(pb), full(wa), full(wb),
                  full(w2e)],
        out_specs=pl.BlockSpec((1, nch, g * LANES), lambda i: (i, 0, 0)),
        out_shape=jax.ShapeDtypeStruct((batch, nch, g * LANES), BF16),
        compiler_params=_cparams("parallel"),
        name="nsa_compress",
    )(x, pa, pb, wa, wb, w2e)


def _nsa_cmp_kernel(q_ref, kc_ref, vc_ref, ovt_ref, o_ref, drop_ref, *, rep, n_sel, n_real):
    tq = q_ref.shape[1]
    nc = kc_ref.shape[1]
    nsb = ovt_ref.shape[0]
    t0 = pl.program_id(2) * tq
    scale = HEAD_DIM ** -0.5
    kc = kc_ref[0]
    vc = vc_ref[0]
    tpos = t0 + lax.broadcasted_iota(jnp.int32, (tq, nc), 0)
    cend = lax.broadcasted_iota(jnp.int32, (tq, nc), 1) * CMP_STRIDE + (CMP_LEN - 1)
    cmask = cend <= tpos
    psum = jnp.zeros((tq, nc), F32)
    outs = []
    for r in range(rep):
        q = q_ref[0, :, r * LANES:(r + 1) * LANES]
        sc = lax.dot_general(q, kc, _NT, preferred_element_type=F32) * scale
        sc = jnp.where(cmask, sc, NEG)
        m = jnp.max(sc, axis=-1, keepdims=True)
        ex = jnp.where(cmask, jnp.exp(sc - m), 0.0)
        den = jnp.sum(ex, axis=-1, keepdims=True)
        pc = ex / jnp.where(den > 0, den, 1.0)
        outs.append(jnp.dot(pc.astype(BF16), vc, preferred_element_type=F32))
        psum = psum + pc
    o_ref[0] = jnp.concatenate(outs, axis=1)
    phi, plo = _split_bf16(psum)
    ovt = ovt_ref[...]
    imp = (lax.dot_general(ovt, phi, _NT, preferred_element_type=F32)
           + lax.dot_general(ovt, plo, _NT, preferred_element_type=F32))
    jblk = lax.broadcasted_iota(jnp.int32, (nsb, tq), 0)
    cur = jnp.right_shift(t0 + lax.broadcasted_iota(jnp.int32, (nsb, tq), 1), SLC_SHIFT)
    forced = (jblk == 0) | (jblk == cur) | (jblk == cur - 1)
    score = jnp.where(jblk > cur, -1.0, jnp.where(forced, SELECT_FORCE, imp))
    rank = jnp.zeros((nsb, tq), F32)
    for k in range(n_real):
        rowk = score[k:k + 1, :]
        ge = jnp.where(rowk >= score, 1.0, 0.0)
        gt = jnp.where(rowk > score, 1.0, 0.0)
        rank = rank + jnp.where(jblk > k, ge, gt)
    drop = jnp.where(jblk <= cur, jnp.where(rank < n_sel, 0.0, 1.0), 1.0)
    drop = jnp.concatenate([drop, jnp.ones((LANES - nsb, tq), F32)], axis=0).T
    drop_ref[0, 0] = pltpu.roll(drop, HEAD_DIM, 1).astype(drop_ref.dtype)


def _nsa_cmp(q, kcmp, vcmp, batch, seq, q_off_blocks, tq=256):
    g = C_KV_HEADS
    rep = C_HEADS // g
    nc = kcmp.shape[1]
    nsb = seq // SLC_BLOCK
    n_sel = min(SLC_TOP_N, nsb)
    cs = np.arange(nc)[:, None] * CMP_STRIDE
    js = np.arange(nsb)[None, :] * SLC_BLOCK
    overlap = np.clip(np.minimum(cs + CMP_LEN, js + SLC_BLOCK) - np.maximum(cs, js), 0, None) / CMP_LEN
    overlap[(seq - CMP_LEN) // CMP_STRIDE + 1:] = 0.0
    nsb_pad = -(-nsb // 8) * 8
    ovt = jnp.asarray(np.pad(overlap.T, ((0, nsb_pad - nsb), (0, 0))), BF16)
    q3 = q.reshape(batch, seq, q.shape[1])
    n_real, nsb = nsb, nsb_pad
    assert nsb <= LANES - HEAD_DIM
    o, drop = pl.pallas_call(
        functools.partial(_nsa_cmp_kernel, rep=rep, n_sel=n_sel, n_real=n_real),
        grid=(batch, g, seq // tq),
        in_specs=[pl.BlockSpec((1, tq, rep * LANES), lambda b, gi, i: (b, i, q_off_blocks // rep + gi)),
                  pl.BlockSpec((1, nc, LANES), lambda b, gi, i: (b, 0, gi)),
                  pl.BlockSpec((1, nc, LANES), lambda b, gi, i: (b, 0, gi)),
                  pl.BlockSpec(ovt.shape, lambda b, gi, i: (0, 0))],
        out_specs=[pl.BlockSpec((1, tq, rep * LANES), lambda b, gi, i: (b, i, gi)),
                   pl.BlockSpec((1, 1, tq, LANES), lambda b, gi, i: (b, gi, i, 0))],
        out_shape=[jax.ShapeDtypeStruct((batch, seq, C_HEADS * LANES), F32),
                   jax.ShapeDtypeStruct((batch, g, seq, LANES), BF16)],
        compiler_params=_cparams("parallel", "parallel", "parallel"),
        name="nsa_compressed_select",
    )(q3, kcmp, vcmp, ovt)
    return o.reshape(batch * seq, C_HEADS * LANES), drop


def _nsa_slc_kernel(q_ref, k_ref, vt_ref, drop_ref, o_ref, *, rep, kt):
    tq = q_ref.shape[1]
    t0 = pl.program_id(2) * tq
    n_kt = (t0 + tq + kt - 1) // kt
    upper = lax.broadcasted_iota(jnp.int32, (tq, LANES), 1) >= HEAD_DIM
    drop = drop_ref[0, 0]
    qs = [jnp.where(upper, drop, q_ref[0, :, r * LANES:(r + 1) * LANES] * (HEAD_DIM ** -0.5))
          for r in range(rep)]

    def tile(k0, carry, bias):
        k = k_ref[0, pl.ds(k0, kt), :]
        vt = vt_ref[:, pl.ds(k0, kt)]
        new = []
        for qr, (m, l, acc) in zip(qs, carry):
            s = lax.dot_general(k, qr, _NT, preferred_element_type=F32)
            if bias is not None:
                s = s + bias
            m_new = jnp.maximum(m, jnp.max(s, axis=0, keepdims=True))
            e = jnp.exp(s - m_new)
            corr = jnp.exp(m - m_new)
            l = l * corr + jnp.sum(e, axis=0, keepdims=True)
            acc = acc * corr + jnp.dot(vt, e.astype(BF16), preferred_element_type=F32)
            new.append((m_new, l, acc))
        return tuple(new)

    init = tuple((jnp.full((1, tq), NEG, F32), jnp.zeros((1, tq), F32), jnp.zeros((LANES, tq), F32))
                 for _ in range(rep))
    carry = lax.fori_loop(0, n_kt - 1, lambda j, c: tile(pl.multiple_of(j * kt, kt), c, None), init)
    k_last = pl.multiple_of((n_kt - 1) * kt, kt)
    kpos = k_last + lax.broadcasted_iota(jnp.int32, (kt, tq), 0)
    tpos = t0 + lax.broadcasted_iota(jnp.int32, (kt, tq), 1)
    carry = tile(k_last, carry, jnp.where(kpos <= tpos, 0.0, NEG))
    o_ref[0] = jnp.concatenate([(acc / l).T for _, l, acc in carry], axis=1)


def _nsa_slc(qkv, vt, drop, batch, seq, nblk, q_off, k_off, tq=128, kt=1024):
    g = C_KV_HEADS
    rep = C_HEADS // g
    kt = min(kt, seq)
    arr = qkv.reshape(batch, seq, nblk * LANES)
    out = pl.pallas_call(
        functools.partial(_nsa_slc_kernel, rep=rep, kt=kt),
        grid=(batch, g, seq // tq),
        in_specs=[pl.BlockSpec((1, tq, rep * LANES), lambda b, gi, i: (b, i, q_off // rep + gi)),
                  pl.BlockSpec((1, seq, LANES), lambda b, gi, i: (b, 0, k_off + gi)),
                  pl.BlockSpec((LANES, seq), lambda b, gi, i: (gi, b)),
                  pl.BlockSpec((1, 1, tq, LANES), lambda b, gi, i: (b, gi, i, 0))],
        out_specs=pl.BlockSpec((1, tq, rep * LANES), lambda b, gi, i: (b, i, gi)),
        out_shape=jax.ShapeDtypeStruct((batch, seq, C_HEADS * LANES), F32),
        compiler_params=_cparams("parallel", "parallel", "parallel"),
        name="nsa_selected",
    )(arr, arr, vt, drop)
    return out.reshape(batch * seq, C_HEADS * LANES)


def _even_mixer_layer(x2, batch, seq, tabs, w_in, sinks, w_out, g, b):
    d = x2.shape[1]
    n_heads_in = 3 * A_HEADS + B_Q_HEADS + 2 * B_KV_HEADS
    w = _pad_heads_cols(w_in, n_heads_in).astype(BF16)
    rope = [1] * (2 * A_HEADS) + [0] * A_HEADS + [1] * B_Q_HEADS + [1] * B_KV_HEADS + [0] * B_KV_HEADS
    plan = [(0, c, rope[c]) for c in range(n_heads_in)]
    dilations = [dil for _, dil in A_PATTERNS]
    regroup = [dil for dil in dilations if dil > 1]
    n_a_blocks = 3 * A_HEADS
    qkv, *grouped = _proj(x2, w, tabs, plan, [n_heads_in * LANES], [BF16], dilations=regroup, n_dil=n_a_blocks)
    outs = []
    for window, dilation in A_PATTERNS:
        src, nblk = (qkv, n_heads_in) if dilation == 1 else (grouped[regroup.index(dilation)], n_a_blocks)
        outs.append(_band_attention(src, batch=batch, seq=seq, dilation=dilation, nblk=nblk, q_off=0,
                                    k_off=A_HEADS, v_off=2 * A_HEADS, n_q_heads=A_HEADS, rep=1,
                                    max_dist=window // dilation,
                                    heads=min(4, dilation)))
    qb_off = 3 * A_HEADS
    ob = _band_attention(qkv, batch=batch, seq=seq, dilation=1, nblk=n_heads_in, q_off=qb_off,
                         k_off=qb_off + B_Q_HEADS, v_off=qb_off + B_Q_HEADS + B_KV_HEADS, n_q_heads=B_Q_HEADS,
                         rep=B_Q_HEADS // B_KV_HEADS, max_dist=B_WINDOW - 1, sinks=sinks)
    w_o = _pad_heads_rows(w_out, A_HEADS + B_Q_HEADS).astype(BF16)
    return _outproj_even(outs, dilations, ob, x2, w_o, g.reshape(1, d), b.reshape(1, d))


def _odd_mixer_layer(x2, batch, seq, tabs, w_in, cmpk_pos, cmpk_w1, cmpk_w2, cmpv_pos, cmpv_w1, cmpv_w2, w_out, g, b):
    d = x2.shape[1]
    kvw = C_KV_HEADS * HEAD_DIM
    qw = C_HEADS * HEAD_DIM
    sizes = [qw] + [kvw] * 6 + [3 * C_HEADS]
    offs = np.concatenate([[0], np.cumsum(sizes)])
    wq, wkc, wvc, wks, wvs, wkw, wvw, wgt = [w_in[:, offs[i]:offs[i + 1]] for i in range(8)]
    ph = lambda wpart, n: _pad_heads_cols(wpart, n)
    w = jnp.concatenate([ph(wq, C_HEADS), ph(wks, C_KV_HEADS), ph(wkw, C_KV_HEADS), ph(wvw, C_KV_HEADS),
                         wkc, wvc, jnp.pad(wgt, ((0, 0), (0, LANES - 3 * C_HEADS)))], axis=1).astype(BF16)
    wvs_t = ph(wvs, C_KV_HEADS).T.astype(BF16)
    n16 = C_HEADS + 3 * C_KV_HEADS
    rope16 = [1] * C_HEADS + [TAG_BLOCK] * C_KV_HEADS + [1] * C_KV_HEADS + [0] * C_KV_HEADS
    n_kc = kvw // LANES
    plan = ([(0, c, rope16[c]) for c in range(n16)] + [(1, c, 2) for c in range(n_kc)]
            + [(2, c, 0) for c in range(n_kc)] + [(3, 0, 0)])
    qkv, kc, vc, gate, vs_t = _proj(x2, w, tabs, plan, [n16 * LANES, kvw, kvw, LANES], [BF16, F32, F32, F32],
                                    wt=wvs_t, seq=seq)
    kcmp = _compress(kc, cmpk_pos, cmpk_w1, cmpk_w2, batch, seq)
    vcmp = _compress(vc, cmpv_pos, cmpv_w1, cmpv_w2, batch, seq)
    o_cmp, drop = _nsa_cmp(qkv, kcmp, vcmp, batch, seq, 0)
    ks_off = C_HEADS
    o_slc = _nsa_slc(qkv, vs_t, drop, batch, seq, n16, 0, ks_off)
    kw_off = ks_off + C_KV_HEADS
    o_win = _band_attention(qkv, batch=batch, seq=seq, dilation=1, nblk=n16, q_off=0, k_off=kw_off,
                            v_off=kw_off + C_KV_HEADS, n_q_heads=C_HEADS, rep=C_HEADS // C_KV_HEADS,
                            max_dist=NSA_WINDOW - 1, heads=C_HEADS // C_KV_HEADS)
    w_o = _pad_heads_rows(w_out, C_HEADS).astype(BF16)
    return _outproj_odd(o_cmp, o_slc, o_win, gate, x2, w_o, g.reshape(1, d), b.reshape(1, d))


def kernel(x, positions, even_w_in, even_sinks, even_w_out, odd_w_in, odd_cmpk_pos, odd_cmpk_w1, odd_cmpk_w2, odd_cmpv_pos, odd_cmpv_w1, odd_cmpv_w2, odd_w_out, mix_ln_g, mix_ln_b, moe_router_w, moe_router_b, moe_w_gate, moe_w_up, moe_w_down, moe_sh_gate, moe_sh_up, moe_sh_down, ffn_ln_g, ffn_ln_b):
    batch, seq, d = x.shape
    x2 = x.reshape(batch * seq, d)
    tabs = _rope_tables(positions)
    depth = mix_ln_g.shape[0]
    for layer in range(depth):
        j = layer // 2
        if layer % 2 == 0:
            x2 = _even_mixer_layer(x2, batch, seq, tabs, even_w_in[j], even_sinks[j], even_w_out[j],
                                   mix_ln_g[layer], mix_ln_b[layer])
        else:
            x2 = _odd_mixer_layer(x2, batch, seq, tabs, odd_w_in[j], odd_cmpk_pos[j], odd_cmpk_w1[j], odd_cmpk_w2[j],
                                  odd_cmpv_pos[j], odd_cmpv_w1[j], odd_cmpv_w2[j], odd_w_out[j],
                                  mix_ln_g[layer], mix_ln_b[layer])
        x2 = _moe_layer(x2, moe_router_w[layer], moe_router_b[layer], moe_w_gate, moe_w_up, moe_w_down, layer,
                        moe_sh_gate[layer], moe_sh_up[layer], moe_sh_down[layer],
                        ffn_ln_g[layer].reshape(1, d), ffn_ln_b[layer].reshape(1, d))
    return x2.reshape(batch, seq, d)
```

```python
import functools

import numpy as np
import jax
import jax.numpy as jnp
from jax import lax
from jax.experimental import pallas as pl
from jax.experimental.pallas import tpu as pltpu

F32 = jnp.float32
BF16 = jnp.bfloat16

LANES = 128
HEAD_DIM = 64
ROT_DIM = HEAD_DIM // 4
ROT_HALF = ROT_DIM // 2
ROPE_THETA = 500000.0
QBLK = 128
A_HEADS = 8
A_PATTERNS = ((128, 1), (512, 4), (2048, 16))
B_Q_HEADS = 8
B_KV_HEADS = 2
B_WINDOW = 128
C_HEADS = 16
C_KV_HEADS = 4
CMP_LEN = 32
CMP_STRIDE = 16
CMP_HIDDEN = 2 * HEAD_DIM
SLC_BLOCK = 64
SLC_SHIFT = 6
SLC_TOP_N = 16
NSA_WINDOW = 512
SELECT_FORCE = 1.0e4
N_EXPERTS = 256
TOP_K = 8
N_GROUPS = 8
TOPK_GROUPS = 4
ROUTED_SCALE = 2.5
MOE_BLOCK = 256
DEPTH = 2
DEEPNORM_ALPHA = (2 * DEPTH) ** 0.25
LN_EPS = 1e-5
NEG = -1.0e30
VMEM_LIMIT = 56 * 1024 * 1024

_NT = (((1,), (1,)), ((), ()))


def _cparams(*sem):
    return pltpu.CompilerParams(dimension_semantics=sem, vmem_limit_bytes=VMEM_LIMIT)


def _split_bf16(a):
    hi = a.astype(BF16)
    lo = (a - hi.astype(F32)).astype(BF16)
    return hi, lo


def _layer_norm(y, g, b):
    mu = jnp.mean(y, axis=-1, keepdims=True)
    d = y - mu
    var = jnp.mean(d * d, axis=-1, keepdims=True)
    return d * lax.rsqrt(var + LN_EPS) * g + b


def _silu(a):
    return a * jax.nn.sigmoid(a)


TAG_BLOCK = 3


def _proj_kernel(*refs, plan, n_main, transposed, dilations, n_dil, seq):
    n_in = 4 if transposed else 3
    x_ref, w_ref, tab_ref = refs[:3]
    out_refs = refs[n_in:n_in + n_main]
    extra = list(refs[n_in + n_main:])
    x = x_ref[...].astype(BF16)
    tm = x.shape[0]
    if transposed:
        out_t = extra.pop(0)
        out_t[...] = lax.dot_general(refs[3][...], x, _NT, preferred_element_type=F32).astype(out_t.dtype)
    dil_refs = [extra.pop(0) for _ in dilations]
    stage = extra.pop(0) if dilations else None
    nblk = len(plan)
    for c0 in range(0, nblk, 2):
        nb = min(2, nblk - c0)
        acc = jnp.dot(x, w_ref[:, c0 * LANES:(c0 + nb) * LANES], preferred_element_type=F32)
        for j in range(nb):
            blk = acc[:, j * LANES:(j + 1) * LANES]
            dst, dblk, mode = plan[c0 + j]
            if mode == TAG_BLOCK:
                pos = lax.rem(pl.program_id(0) * tm, seq) + lax.broadcasted_iota(jnp.int32, (tm, LANES), 0)
                tag = lax.broadcasted_iota(jnp.int32, (tm, LANES), 1) - HEAD_DIM == jnp.right_shift(pos, SLC_SHIFT)
                mode = 1
            else:
                tag = None
            if mode:
                off = (mode - 1) * 3 * LANES
                cos = tab_ref[:, off:off + LANES]
                s_lo = tab_ref[:, off + LANES:off + 2 * LANES]
                s_hi = tab_ref[:, off + 2 * LANES:off + 3 * LANES]
                blk = (blk * cos + pltpu.roll(blk, LANES - ROT_HALF, 1) * s_lo
                       + pltpu.roll(blk, ROT_HALF, 1) * s_hi)
            if tag is not None:
                blk = jnp.where(tag, NEG, blk)
            o_ref = out_refs[dst]
            o_ref[:, dblk * LANES:(dblk + 1) * LANES] = blk.astype(o_ref.dtype)
            c = c0 + j
            if dilations and c < n_dil:
                stage[...] = blk
                for dil, d_ref in zip(dilations, dil_refs):
                    for r in range(dil):
                        col = (r * n_dil + c) * LANES
                        d_ref[:, col:col + LANES] = stage[pl.ds(r, tm // dil, stride=dil), :].astype(d_ref.dtype)


def _proj(x2, w, tabs, plan, out_cols, out_dtypes, wt=None, dilations=(), n_dil=0, seq=0, tm=512):
    t, d = x2.shape
    ncol = w.shape[1]
    out_shape = [jax.ShapeDtypeStruct((t, c), dt) for c, dt in zip(out_cols, out_dtypes)]
    once = pl.Buffered(1)
    in_specs = [pl.BlockSpec((tm, d), lambda i: (i, 0)),
                pl.BlockSpec((d, ncol), lambda i: (0, 0), pipeline_mode=once),
                pl.BlockSpec((tm, tabs.shape[1]), lambda i: (i, 0))]
    out_specs = [pl.BlockSpec((tm, c), lambda i: (i, 0)) for c in out_cols]
    args = [x2, w, tabs]
    if wt is not None:
        in_specs.append(pl.BlockSpec(wt.shape, lambda i: (0, 0), pipeline_mode=once))
        out_specs.append(pl.BlockSpec((wt.shape[0], tm), lambda i: (0, i)))
        out_shape.append(jax.ShapeDtypeStruct((wt.shape[0], t), BF16))
        args.append(wt)
    for dil in dilations:
        out_specs.append(pl.BlockSpec((tm // dil, dil * n_dil * LANES), lambda i: (i, 0)))
        out_shape.append(jax.ShapeDtypeStruct((t // dil, dil * n_dil * LANES), BF16))
    return pl.pallas_call(
        functools.partial(_proj_kernel, plan=tuple(plan), n_main=len(out_cols), transposed=wt is not None,
                          dilations=tuple(dilations), n_dil=n_dil, seq=seq),
        grid=(t // tm,),
        in_specs=in_specs,
        out_specs=out_specs,
        out_shape=out_shape,
        scratch_shapes=[pltpu.VMEM((tm, LANES), F32)] if dilations else [],
        compiler_params=_cparams("parallel"),
        name="proj_rope",
    )(*args)


def _rope_tables(positions):
    t = positions.size
    inv_freq = jnp.asarray(ROPE_THETA ** (-np.arange(0, ROT_DIM, 2) / ROT_DIM), F32)
    ang = positions.astype(F32).reshape(t, 1) * inv_freq
    trig = jnp.concatenate([jnp.cos(ang), jnp.sin(ang)], axis=1)
    place = np.zeros((2 * ROT_HALF, 6 * LANES), np.float32)
    const = np.zeros((6 * LANES,), np.float32)
    for base, heads in ((0, (0,)), (3 * LANES, (0, HEAD_DIM))):
        const[base:base + LANES] = 1.0
        for h in heads:
            for i in range(ROT_HALF):
                place[i, base + h + i] = 1.0
                place[i, base + h + ROT_HALF + i] = 1.0
                const[base + h + i] = const[base + h + ROT_HALF + i] = 0.0
                place[ROT_HALF + i, base + LANES + h + i] = -1.0
                place[ROT_HALF + i, base + 2 * LANES + h + ROT_HALF + i] = 1.0
    return jnp.dot(trig, jnp.asarray(place), precision=lax.Precision.HIGHEST) + jnp.asarray(const)


def _pad_heads_cols(w, n_heads):
    d = w.shape[0]
    w = w.reshape(d, n_heads, HEAD_DIM)
    return jnp.pad(w, ((0, 0), (0, 0), (0, LANES - HEAD_DIM))).reshape(d, n_heads * LANES)


def _pad_heads_rows(w, n_heads):
    d = w.shape[1]
    w = w.reshape(n_heads, HEAD_DIM, d)
    return jnp.pad(w, ((0, 0), (0, LANES - HEAD_DIM), (0, 0))).reshape(n_heads * LANES, d)


def _pack_head_pair(a, b):
    lane = lax.broadcasted_iota(jnp.int32, a.shape, 1)
    return jnp.where(lane < HEAD_DIM, a, pltpu.roll(b, HEAD_DIM, 1))


def _band_kernel(*refs, back, max_dist, length, qrows, heads, shared_kv, in_flight, has_sink, pack_out):
    if has_sink:
        sink_ref, q_ref, k_ref, v_ref, o_ref, bias_sc = refs
    else:
        q_ref, k_ref, v_ref, o_ref, bias_sc = refs
    nq = length // qrows
    window = qrows + back
    chains = [tuple(range(heads))] if shared_kv else [(h,) for h in range(heads)]
    stack = len(chains[0])
    lane = lax.broadcasted_iota(jnp.int32, (stack * qrows, LANES), 1)
    first_head = pl.program_id(2) * heads

    def band_bias(offset):
        row = lax.broadcasted_iota(jnp.int32, (qrows, window), 0)
        col = lax.broadcasted_iota(jnp.int32, (qrows, window), 1)
        dist = offset + row - col
        tile = jnp.where((dist >= 0) & (dist <= max_dist), 0.0, NEG)
        return jnp.concatenate([tile] * stack, axis=0) if stack > 1 else tile

    bias_sc[...] = band_bias(back)

    def block(hs, q0, k0, bias):
        kv = slice(0, LANES) if shared_kv else slice(hs[0] * LANES, (hs[0] + 1) * LANES)
        q = jnp.concatenate([q_ref[0, pl.ds(q0, qrows), h * LANES:(h + 1) * LANES] for h in hs], axis=0)
        q = q * (HEAD_DIM ** -0.5)
        k = k_ref[0, pl.ds(k0, window), kv]
        v = v_ref[0, pl.ds(k0, window), kv]
        s = lax.dot_general(q, k, _NT, preferred_element_type=F32) + bias
        m = jnp.max(s, axis=-1, keepdims=True)
        e = jnp.exp(s - m)
        den = jnp.sum(e, axis=-1, keepdims=True)
        if has_sink:
            sink = jnp.concatenate([jnp.full((qrows, 1), sink_ref[first_head + h], F32) for h in hs], axis=0)
            den = den + jnp.exp(sink - m)
        o = jnp.dot(e.astype(BF16), v, preferred_element_type=F32) / den
        lse = m + jnp.log(den)
        if pack_out:
            for i in range(0, len(hs), 2):
                pair = _pack_head_pair(o[i * qrows:(i + 1) * qrows], o[(i + 1) * qrows:(i + 2) * qrows])
                o_ref[0, pl.ds(q0, qrows), (hs[i] // 2) * LANES:(hs[i] // 2 + 1) * LANES] = pair
            return
        out = jnp.where(lane < HEAD_DIM, o, lse)
        for i, h in enumerate(hs):
            o_ref[0, pl.ds(q0, qrows), h * LANES:(h + 1) * LANES] = out[i * qrows:(i + 1) * qrows]

    n_clipped = min(-(-back // qrows), nq)
    for qi in range(n_clipped):
        bias = band_bias(qi * qrows)
        for hs in chains:
            block(hs, qi * qrows, 0, bias)

    steady = nq - n_clipped
    per_iter = max(1, min(in_flight // heads, steady))
    n_iter = steady // per_iter

    def body(it, carry):
        aligned = lambda v: v if isinstance(v, int) else pl.multiple_of(v, QBLK)
        for j in range(per_iter):
            q0 = aligned((n_clipped + it * per_iter + j) * qrows)
            for hs in chains:
                block(hs, q0, aligned(q0 - back), bias_sc[...])
        return carry

    if n_iter == 1:
        body(0, 0)
    elif n_iter:
        lax.fori_loop(0, n_iter, body, 0)
    for qi in range(n_clipped + n_iter * per_iter, nq):
        for hs in chains:
            block(hs, qi * qrows, qi * qrows - back, bias_sc[...])


def _band_attention(qkv, *, batch, seq, dilation, nblk, q_off, k_off, v_off, n_q_heads, rep, max_dist, sinks=None,
                    heads=1, in_flight=16, pack_out=False):
    length = seq // dilation
    back = min(-(-max_dist // QBLK) * QBLK, length - QBLK)
    qrows = QBLK
    shared_kv = rep > 1
    assert n_q_heads % heads == 0 and (rep == 1 or rep % heads == 0)
    assert not pack_out or (shared_kv and heads % 2 == 0)
    out_lanes = HEAD_DIM if pack_out else LANES
    kv_heads = 1 if shared_kv else heads
    arr = qkv.reshape(batch, length, dilation * nblk * LANES)
    assert q_off % heads == 0 and k_off % kv_heads == 0 and v_off % kv_heads == 0 and nblk % heads == 0
    qspec = pl.BlockSpec((1, length, heads * LANES),
                         lambda b, r, h: (b, 0, (r * nblk + q_off) // heads + h))
    kvspec = lambda off: pl.BlockSpec(
        (1, length, kv_heads * LANES),
        lambda b, r, h: (b, 0, (r * nblk + off) // kv_heads + (h * heads // rep if shared_kv else h)))
    in_specs = [qspec, kvspec(k_off), kvspec(v_off)]
    args = [arr, arr, arr]
    if sinks is not None:
        in_specs = [pl.BlockSpec(memory_space=pltpu.SMEM)] + in_specs
        args = [sinks.reshape(-1).astype(F32)] + args
    out = pl.pallas_call(
        functools.partial(_band_kernel, back=back, max_dist=max_dist, length=length, qrows=qrows, heads=heads,
                          shared_kv=shared_kv, in_flight=in_flight, has_sink=sinks is not None,
                          pack_out=pack_out),
        grid=(batch, dilation, n_q_heads // heads),
        in_specs=in_specs,
        out_specs=pl.BlockSpec((1, length, heads * out_lanes), lambda b, r, h: (b, 0, r * (n_q_heads // heads) + h)),
        out_shape=jax.ShapeDtypeStruct((batch, length, dilation * n_q_heads * out_lanes), F32),
        scratch_shapes=[pltpu.VMEM(((heads if shared_kv else 1) * qrows, qrows + back), F32)],
        compiler_params=_cparams("parallel", "parallel", "parallel"),
        name="band_attention",
    )(*args)
    return out.reshape(batch * length, dilation * n_q_heads * out_lanes)


def _outproj_even_kernel(o1_ref, o2_ref, o3_ref, ob_ref, x_ref, w_ref, g_ref, b_ref, out_ref, *nat_refs,
                         n_a, n_b, dilations):
    tm = x_ref.shape[0]
    lane = lax.broadcasted_iota(jnp.int32, (tm, LANES), 1)
    real = lane < HEAD_DIM
    pattern_refs = []
    nat_refs = list(nat_refs)
    for o_ref, dil in zip((o1_ref, o2_ref, o3_ref), dilations):
        if dil == 1:
            pattern_refs.append(o_ref)
            continue
        nat = nat_refs.pop(0)
        for r in range(dil):
            for h in range(n_a):
                col = (r * n_a + h) * LANES
                nat[h, pl.ds(r, tm // dil, stride=dil), :] = o_ref[:, col:col + LANES]
        pattern_refs.append(nat)
    parts = []
    for h in range(n_a):
        sl = slice(h * LANES, (h + 1) * LANES)
        outs = [r[:, sl] if r.ndim == 2 else r[h] for r in pattern_refs]
        lses = [jnp.where(real, pltpu.roll(a, HEAD_DIM, 1), a) for a in outs]
        m = jnp.maximum(jnp.maximum(lses[0], lses[1]), lses[2])
        es = [jnp.exp(l - m) for l in lses]
        num = es[0] * outs[0] + es[1] * outs[1] + es[2] * outs[2]
        den = es[0] + es[1] + es[2]
        parts.append(jnp.where(real, num / den, 0.0).astype(BF16))
    for h in range(n_b):
        parts.append(jnp.where(real, ob_ref[:, h * LANES:(h + 1) * LANES], 0.0).astype(BF16))
    a = jnp.concatenate(parts, axis=1)
    mixed = jnp.dot(a, w_ref[...], preferred_element_type=F32)
    y = DEEPNORM_ALPHA * x_ref[...] + mixed
    out_ref[...] = _layer_norm(y, g_ref[...], b_ref[...])


def _outproj_even(outs, dilations, ob, x2, w, g, b, tm=256):
    t, d = x2.shape
    n_a, n_b = outs[0].shape[1] // (dilations[0] * LANES), ob.shape[1] // LANES
    row = lambda c: pl.BlockSpec((tm, c), lambda i: (i, 0))
    grouped = lambda a, dil: pl.BlockSpec((tm // dil, a.shape[1]), lambda i: (i, 0))
    full = lambda a: pl.BlockSpec(a.shape, lambda i: (0,) * a.ndim)
    return pl.pallas_call(
        functools.partial(_outproj_even_kernel, n_a=n_a, n_b=n_b, dilations=tuple(dilations)),
        grid=(t // tm,),
        in_specs=[grouped(o, dil) for o, dil in zip(outs, dilations)] + [row(ob.shape[1]), row(d),
                                                                          full(w), full(g), full(b)],
        out_specs=row(d),
        out_shape=jax.ShapeDtypeStruct((t, d), F32),
        scratch_shapes=[pltpu.VMEM((n_a, tm, LANES), F32) for dil in dilations if dil > 1],
        compiler_params=_cparams("parallel"),
        name="outproj_even_ln",
    )(*outs, ob, x2, w, g, b)


def _outproj_odd_kernel(oc_ref, os_ref, ow_ref, gate_ref, e_ref, x_ref, w_ref, g_ref, b_ref, out_ref):
    gate = jax.nn.sigmoid(gate_ref[...])
    ghi, glo = _split_bf16(gate)
    acc = None
    for j, o_ref in enumerate((oc_ref, os_ref, ow_ref)):
        ej = e_ref[j]
        gfull = jnp.dot(ghi, ej, preferred_element_type=F32) + jnp.dot(glo, ej, preferred_element_type=F32)
        term = gfull * o_ref[...]
        acc = term if acc is None else acc + term
    mixed = jnp.dot(acc.astype(BF16), w_ref[...], preferred_element_type=F32)
    y = DEEPNORM_ALPHA * x_ref[...] + mixed
    out_ref[...] = _layer_norm(y, g_ref[...], b_ref[...])


def _gate_expanders(n_heads):
    e = np.zeros((3, LANES, n_heads * HEAD_DIM), np.float32)
    for j in range(3):
        for h in range(n_heads):
            e[j, 3 * h + j, h * HEAD_DIM:(h + 1) * HEAD_DIM] = 1.0
    return jnp.asarray(e, BF16)


def _outproj_odd(oc, osl, ow, gate, x2, w, g, b, tm=256):
    t, d = x2.shape
    n_heads = oc.shape[1] // HEAD_DIM
    e = _gate_expanders(n_heads)
    row = lambda c: pl.BlockSpec((tm, c), lambda i: (i, 0))
    full = lambda a: pl.BlockSpec(a.shape, lambda i: (0,) * a.ndim)
    return pl.pallas_call(
        _outproj_odd_kernel,
        grid=(t // tm,),
        in_specs=[row(oc.shape[1]), row(osl.shape[1]), row(ow.shape[1]), row(LANES), full(e), row(d),
                  full(w), full(g), full(b)],
        out_specs=row(d),
        out_shape=jax.ShapeDtypeStruct((t, d), F32),
        compiler_params=_cparams("parallel"),
        name="outproj_odd_ln",
    )(oc, osl, ow, gate, e, x2, w, g, b)


def _router_kernel(x_ref, whi_ref, wlo_ref, bias_ref, eidx_ref, gate_ref, rank_ref, cnt_ref):
    n_exp = whi_ref.shape[0]
    tm = x_ref.shape[0]
    per_group = n_exp // N_GROUPS
    xhi, xlo = _split_bf16(x_ref[...])
    whi, wlo = whi_ref[...], wlo_ref[...]
    dg = lambda a, b: lax.dot_general(a, b, _NT, preferred_element_type=F32)
    logits = dg(whi, xhi) + dg(whi, xlo) + dg(wlo, xhi)
    aff = jax.nn.sigmoid(logits)
    biased = aff + bias_ref[...]
    gio = lax.broadcasted_iota(jnp.int32, (per_group, tm), 0).astype(F32)
    blocks, scores = [], []
    for g in range(N_GROUPS):
        blk = biased[g * per_group:(g + 1) * per_group, :]
        m1 = jnp.max(blk, axis=0, keepdims=True)
        first = jnp.min(jnp.where(blk == m1, gio, float(per_group)), axis=0, keepdims=True)
        m2 = jnp.max(jnp.where(gio == first, -jnp.inf, blk), axis=0, keepdims=True)
        blocks.append(blk)
        scores.append(m1 + m2)
    masked = []
    for g in range(N_GROUPS):
        rank = jnp.zeros((1, tm), F32)
        for o in range(N_GROUPS):
            if o == g:
                continue
            beats = scores[o] >= scores[g] if o < g else scores[o] > scores[g]
            rank = rank + jnp.where(beats, 1.0, 0.0)
        masked.append(jnp.where(rank < TOPK_GROUPS, blocks[g], -jnp.inf))
    cur = jnp.concatenate(masked, axis=0)
    eio = lax.broadcasted_iota(jnp.int32, (n_exp, tm), 0).astype(F32)
    ids, gs = [], []
    for _ in range(TOP_K):
        m = jnp.max(cur, axis=0, keepdims=True)
        idx = jnp.min(jnp.where(cur == m, eio, float(n_exp)), axis=0, keepdims=True)
        hit = eio == idx
        gs.append(jnp.sum(jnp.where(hit, aff, 0.0), axis=0, keepdims=True))
        ids.append(idx)
        cur = jnp.where(hit, -jnp.inf, cur)
    gates = jnp.concatenate(gs, axis=0)
    gates = gates / jnp.sum(gates, axis=0, keepdims=True) * ROUTED_SCALE
    eidx_ref[...] = jnp.concatenate(ids, axis=0).astype(jnp.int32)
    gate_ref[...] = gates
    @pl.when(pl.program_id(0) == 0)
    def _():
        cnt_ref[...] = jnp.zeros(cnt_ref.shape, F32)

    onehot = jnp.zeros((n_exp, tm), F32)
    for idx in ids:
        onehot = onehot + jnp.where(eio == idx, 1.0, 0.0)
    earlier = jnp.where(lax.broadcasted_iota(jnp.int32, (tm, tm), 0) < lax.broadcasted_iota(jnp.int32, (tm, tm), 1),
                        1.0, 0.0).astype(BF16)
    before = cnt_ref[...] + jnp.dot(onehot.astype(BF16), earlier, preferred_element_type=F32)
    ranks = [jnp.sum(jnp.where(eio == idx, before, 0.0), axis=0, keepdims=True) for idx in ids]
    rank_ref[...] = jnp.concatenate(ranks, axis=0).astype(jnp.int32)
    cnt_ref[...] = cnt_ref[...] + jnp.sum(onehot, axis=1, keepdims=True)


def _router(x2, router_w, router_b, tm=256):
    t, d = x2.shape
    n_exp = router_w.shape[1]
    whi, wlo = _split_bf16(router_w.T)
    bias = router_b.reshape(n_exp, 1).astype(F32)
    full = lambda a: pl.BlockSpec(a.shape, lambda i: (0,) * a.ndim)
    per_tok = pl.BlockSpec((TOP_K, tm), lambda i: (0, i))
    return pl.pallas_call(
        _router_kernel,
        grid=(t // tm,),
        in_specs=[pl.BlockSpec((tm, d), lambda i: (i, 0)), full(whi), full(wlo), full(bias)],
        out_specs=[per_tok, per_tok, per_tok, pl.BlockSpec((n_exp, 1), lambda i: (0, 0))],
        out_shape=[jax.ShapeDtypeStruct((TOP_K, t), jnp.int32), jax.ShapeDtypeStruct((TOP_K, t), F32),
                   jax.ShapeDtypeStruct((TOP_K, t), jnp.int32), jax.ShapeDtypeStruct((n_exp, 1), F32)],
        compiler_params=_cparams("arbitrary"),
        name="moe_router",
    )(x2, whi, wlo, bias)


def _moe_dest_kernel(eidx_ref, rank_ref, start_ref, dest_ref):
    n_exp = start_ref.shape[0]
    tm = eidx_ref.shape[1]
    eio = lax.broadcasted_iota(jnp.int32, (n_exp, tm), 0)
    start = start_ref[...]
    rows = []
    for k in range(TOP_K):
        seg = jnp.sum(jnp.where(eio == eidx_ref[k:k + 1, :], start, 0.0), axis=0, keepdims=True)
        rows.append(seg.astype(jnp.int32) + rank_ref[k:k + 1, :])
    dest_ref[...] = jnp.concatenate(rows, axis=0)


def _moe_dest(eidx, rank, seg_start, tm=256):
    t = eidx.shape[1]
    per_tok = pl.BlockSpec((TOP_K, tm), lambda i: (0, i))
    return pl.pallas_call(
        _moe_dest_kernel,
        grid=(t // tm,),
        in_specs=[per_tok, per_tok, pl.BlockSpec(seg_start.shape, lambda i: (0, 0))],
        out_specs=per_tok,
        out_shape=jax.ShapeDtypeStruct((TOP_K, t), jnp.int32),
        compiler_params=_cparams("parallel"),
        name="moe_dest",
    )(eidx, rank, seg_start)


def _to_slabs(ref, value):
    rows, width = value.shape
    n_chunks = width // LANES
    for c in range(n_chunks):
        ref[pl.ds(c, rows, stride=n_chunks), :] = value[:, c * LANES:(c + 1) * LANES]


def _from_slabs(ref, first_row, rows, n_chunks):
    return jnp.concatenate([ref[pl.ds(first_row * n_chunks + c, rows, stride=n_chunks), :]
                            for c in range(n_chunks)], axis=1)


def _pack_bf16_pairs(x):
    half = x.shape[1] // 2
    lo = lax.bitcast_convert_type(x[:, :half].astype(BF16).astype(F32), jnp.uint32)
    hi = lax.bitcast_convert_type(x[:, half:].astype(BF16).astype(F32), jnp.uint32)
    return jnp.right_shift(lo, jnp.uint32(16)) | (hi & jnp.uint32(0xFFFF0000))


def _unpack_bf16_pairs(w):
    lo = lax.bitcast_convert_type(jnp.left_shift(w, jnp.uint32(16)), F32).astype(BF16)
    hi = lax.bitcast_convert_type(w & jnp.uint32(0xFFFF0000), F32).astype(BF16)
    return jnp.concatenate([lo, hi], axis=1)


def _moe_dispatch_kernel(zb_ref, dest_ref, x_ref, xs_out, buf, zbuf, sem, zsem):
    tm = x_ref.shape[0]
    n_chunks = buf.shape[0] // tm

    @pl.when(pl.program_id(0) == 0)
    def _():
        rows = zbuf.shape[0]
        zbuf[...] = jnp.zeros(zbuf.shape, zbuf.dtype)

        def zero_copy(e):
            first = pl.multiple_of(zb_ref[e] * rows, rows)
            return pltpu.make_async_copy(zbuf, xs_out.at[pl.ds(first, rows)], zsem)

        def start(e, c):
            @pl.when(zb_ref[e] >= 0)
            def _():
                zero_copy(e).start()
            return c

        def wait(e, c):
            @pl.when(zb_ref[e] >= 0)
            def _():
                zero_copy(e).wait()
            return c

        lax.fori_loop(0, zb_ref.shape[0], start, 0)
        lax.fori_loop(0, zb_ref.shape[0], wait, 0)

    _to_slabs(buf, _pack_bf16_pairs(x_ref[...]))

    def issue(i, c):
        src = buf.at[pl.ds(pl.multiple_of(i * n_chunks, n_chunks), n_chunks)]
        for k in range(TOP_K):
            r = pl.multiple_of(dest_ref[k, i] * n_chunks, n_chunks)
            pltpu.make_async_copy(src, xs_out.at[pl.ds(r, n_chunks)], sem).start(priority=k % 2)
        return c

    lax.fori_loop(0, tm, issue, 0)
    for _ in range(TOP_K):
        pltpu.make_async_copy(buf, xs_out.at[pl.ds(0, tm * n_chunks)], sem).wait()


def _moe_dispatch(zero_blocks, dest, x2, n_blocks, tm=256):
    t, d = x2.shape
    n_chunks = d // 2 // LANES
    grid_spec = pltpu.PrefetchScalarGridSpec(
        num_scalar_prefetch=1,
        grid=(t // tm,),
        in_specs=[pl.BlockSpec((TOP_K, tm), lambda i, zb: (0, i), memory_space=pltpu.SMEM),
                  pl.BlockSpec((tm, d), lambda i, zb: (i, 0))],
        out_specs=pl.BlockSpec(memory_space=pl.ANY),
        scratch_shapes=[pltpu.VMEM((tm * n_chunks, LANES), jnp.uint32),
                        pltpu.VMEM((MOE_BLOCK * n_chunks, LANES), jnp.uint32),
                        pltpu.SemaphoreType.DMA(()), pltpu.SemaphoreType.DMA(())],
    )
    return pl.pallas_call(
        _moe_dispatch_kernel,
        grid_spec=grid_spec,
        out_shape=jax.ShapeDtypeStruct((n_blocks * MOE_BLOCK * n_chunks, LANES), jnp.uint32),
        compiler_params=_cparams("arbitrary"),
        name="moe_dispatch",
    )(zero_blocks, dest, x2)


def _moe_ffn_kernel(be_ref, nu_ref, xs_ref, wg_ref, wu_ref, wd_ref, y_ref, wg_sc, wu_sc, wd_sc):
    b = pl.program_id(0)

    @pl.when(b < nu_ref[0])
    def _():
        @pl.when((b == 0) | (be_ref[b] != be_ref[jnp.maximum(b - 1, 0)]))
        def _():
            wg_sc[...] = wg_ref[0].astype(BF16)
            wu_sc[...] = wu_ref[0].astype(BF16)
            wd_sc[...] = wd_ref[0].astype(BF16)

        d = wg_sc.shape[0]
        xb = _unpack_bf16_pairs(_from_slabs(xs_ref, 0, MOE_BLOCK, d // 2 // LANES))
        gp = jnp.dot(xb, wg_sc[...], preferred_element_type=F32)
        up = jnp.dot(xb, wu_sc[...], preferred_element_type=F32)
        h = (_silu(gp) * up).astype(BF16)
        _to_slabs(y_ref, jnp.dot(h, wd_sc[...], preferred_element_type=F32))


def _moe_ffn(xs, block_expert, n_used, w_gate, w_up, w_down, layer):
    n_blocks = block_expert.shape[0]
    d, ff = w_gate.shape[2], w_gate.shape[3]
    last = lambda b, nu: jnp.minimum(b, nu[0] - 1)
    grid_spec = pltpu.PrefetchScalarGridSpec(
        num_scalar_prefetch=2,
        grid=(n_blocks,),
        in_specs=[
            pl.BlockSpec((MOE_BLOCK * (d // 2 // LANES), LANES), lambda b, be, nu: (last(b, nu), 0)),
            pl.BlockSpec((None, 1, d, ff), lambda b, be, nu: (layer, be[last(b, nu)], 0, 0)),
            pl.BlockSpec((None, 1, d, ff), lambda b, be, nu: (layer, be[last(b, nu)], 0, 0)),
            pl.BlockSpec((None, 1, ff, d), lambda b, be, nu: (layer, be[last(b, nu)], 0, 0)),
        ],
        out_specs=pl.BlockSpec((MOE_BLOCK * (d // LANES), LANES), lambda b, be, nu: (last(b, nu), 0)),
        scratch_shapes=[pltpu.VMEM((d, ff), BF16), pltpu.VMEM((d, ff), BF16), pltpu.VMEM((ff, d), BF16)],
    )
    return pl.pallas_call(
        _moe_ffn_kernel,
        grid_spec=grid_spec,
        out_shape=jax.ShapeDtypeStruct((n_blocks * MOE_BLOCK * (d // LANES), LANES), F32),
        compiler_params=_cparams("arbitrary"),
        name="moe_expert_ffn",
    )(block_expert, n_used, xs, w_gate, w_up, w_down)


def _moe_combine_kernel(dest_ref, y_hbm, gate_ref, x_ref, sg_ref, su_ref, sd_ref, g_ref, b_ref, out_ref, buf, sem):
    tm, d = x_ref.shape
    n_chunks = d // LANES

    def issue(i, c):
        for k in range(TOP_K):
            r = pl.multiple_of(dest_ref[k, i] * n_chunks, n_chunks)
            slot = pl.multiple_of((k * tm + i) * n_chunks, n_chunks)
            pltpu.make_async_copy(y_hbm.at[pl.ds(r, n_chunks)], buf.at[pl.ds(slot, n_chunks)],
                                  sem).start(priority=k % 2)
        return c

    lax.fori_loop(0, tm, issue, 0)
    x = x_ref[...]
    xb = x.astype(BF16)
    hs = _silu(jnp.dot(xb, sg_ref[...], preferred_element_type=F32)) * jnp.dot(xb, su_ref[...], preferred_element_type=F32)
    shared = jnp.dot(hs.astype(BF16), sd_ref[...], preferred_element_type=F32)
    pltpu.make_async_copy(y_hbm.at[pl.ds(0, TOP_K * tm * n_chunks)], buf, sem).wait()
    gates = gate_ref[...]
    routed = _from_slabs(buf, 0, tm, n_chunks) * gates[:, 0:1]
    for k in range(1, TOP_K):
        routed = routed + _from_slabs(buf, k * tm, tm, n_chunks) * gates[:, k:k + 1]
    y = DEEPNORM_ALPHA * x + (routed + shared)
    out_ref[...] = _layer_norm(y, g_ref[...], b_ref[...])


def _moe_combine(dest, y, gates_t, x2, sh_gate, sh_up, sh_down, g, b, tm=128):
    t, d = x2.shape
    row = lambda c: pl.BlockSpec((tm, c), lambda i: (i, 0))
    full = lambda a: pl.BlockSpec(a.shape, lambda i: (0,) * a.ndim)
    return pl.pallas_call(
        _moe_combine_kernel,
        grid=(t // tm,),
        in_specs=[pl.BlockSpec((TOP_K, tm), lambda i: (0, i), memory_space=pltpu.SMEM),
                  pl.BlockSpec(memory_space=pl.ANY),
                  row(TOP_K), row(d), full(sh_gate), full(sh_up), full(sh_down), full(g), full(b)],
        out_specs=row(d),
        out_shape=jax.ShapeDtypeStruct((t, d), F32),
        scratch_shapes=[pltpu.VMEM((TOP_K * tm * (d // LANES), LANES), F32), pltpu.SemaphoreType.DMA(())],
        compiler_params=_cparams("arbitrary"),
        name="moe_combine_ln",
    )(dest, y, gates_t, x2, sh_gate, sh_up, sh_down, g, b)


def _moe_segments(counts, n_tok):
    n_exp = counts.shape[0]
    n_blocks = -(-n_tok * TOP_K // MOE_BLOCK) + n_exp
    nblk = (counts.reshape(n_exp).astype(jnp.int32) + MOE_BLOCK - 1) // MOE_BLOCK
    blk_end = jnp.cumsum(nblk)
    seg_start = ((blk_end - nblk) * MOE_BLOCK).astype(F32).reshape(n_exp, 1)
    block_expert = jnp.sum((blk_end[None, :] <= jnp.arange(n_blocks)[:, None]).astype(jnp.int32), axis=1)
    block_expert = jnp.minimum(block_expert, n_exp - 1)
    n_used = blk_end[-1]
    zero_blocks = jnp.where(nblk > 0, blk_end - 1, -1).astype(jnp.int32)
    return seg_start, block_expert.astype(jnp.int32), n_used.astype(jnp.int32).reshape(1), zero_blocks, n_blocks


def _moe_layer(x2, router_w, router_b, w_gate, w_up, w_down, layer, sh_gate, sh_up, sh_down, g, b):
    t, d = x2.shape
    eidx, gates, rank, counts = _router(x2, router_w, router_b)
    seg_start, block_expert, n_used, zero_blocks, n_blocks = _moe_segments(counts, t)
    dest = _moe_dest(eidx, rank, seg_start)
    xs = _moe_dispatch(zero_blocks, dest, x2, n_blocks)
    y = _moe_ffn(xs, block_expert, n_used, w_gate, w_up, w_down, layer)
    return _moe_combine(dest, y, gates.T, x2, sh_gate.astype(BF16), sh_up.astype(BF16), sh_down.astype(BF16), g, b)


def _compress_kernel(x_ref, pa_ref, pb_ref, wa_ref, wb_ref, w2_ref, out_ref):
    x = x_ref[0]
    nc = x.shape[0]
    ha = jnp.dot((x + pa_ref[...]).astype(BF16), wa_ref[...], preferred_element_type=F32)
    hb = jnp.dot((x + pb_ref[...]).astype(BF16), wb_ref[...], preferred_element_type=F32)
    h = ha + pltpu.roll(hb, nc - 1, 0)
    h = jax.nn.gelu(h, approximate=True)
    out_ref[0] = jnp.dot(h.astype(BF16), w2_ref[...], preferred_element_type=F32).astype(out_ref.dtype)


def _compress(kc, pos, w1, w2, batch, seq):
    g = C_KV_HEADS
    nch = seq // CMP_STRIDE
    half = CMP_LEN // 2
    x = kc.reshape(batch, nch, half * g * HEAD_DIM)
    eye = jnp.eye(g, dtype=F32)
    w1r = w1.reshape(CMP_LEN, HEAD_DIM, CMP_HIDDEN)
    expand = lambda wpart: jnp.einsum('jdh,ge->jgdeh', wpart, eye).reshape(half * g * HEAD_DIM, g * CMP_HIDDEN)
    wa, wb = expand(w1r[:half]).astype(BF16), expand(w1r[half:]).astype(BF16)
    w2e = jnp.einsum('hd,ge->ghed', jnp.pad(w2, ((0, 0), (0, LANES - HEAD_DIM))), eye)
    w2e = w2e.reshape(g * CMP_HIDDEN, g * LANES).astype(BF16)
    tile_pos = lambda p: jnp.broadcast_to(p[:, None, :], (half, g, HEAD_DIM)).reshape(1, half * g * HEAD_DIM)
    pa, pb = tile_pos(pos[:half]), tile_pos(pos[half:])
    full = lambda a: pl.BlockSpec(a.shape, lambda i: (0,) * a.ndim)
    return pl.pallas_call(
        _compress_kernel,
        grid=(batch,),
        in_specs=[pl.BlockSpec((1, nch, x.shape[2]), lambda i: (i, 0, 0)), full(pa), full(pb), full(wa), full(wb),
                  full(w2e)],
        out_specs=pl.BlockSpec((1, nch, g * LANES), lambda i: (i, 0, 0)),
        out_shape=jax.ShapeDtypeStruct((batch, nch, g * LANES), BF16),
        compiler_params=_cparams("parallel"),
        name="nsa_compress",
    )(x, pa, pb, wa, wb, w2e)


def _nsa_cmp_kernel(q_ref, kc_ref, vc_ref, ovt_ref, o_ref, drop_ref, *, rep, n_sel, n_real):
    tq = q_ref.shape[1]
    nc = kc_ref.shape[1]
    nsb = ovt_ref.shape[0]
    t0 = pl.program_id(2) * tq
    scale = HEAD_DIM ** -0.5
    kc = kc_ref[0]
    vc = vc_ref[0]
    tpos = t0 + lax.broadcasted_iota(jnp.int32, (tq, nc), 0)
    cend = lax.broadcasted_iota(jnp.int32, (tq, nc), 1) * CMP_STRIDE + (CMP_LEN - 1)
    cmask = cend <= tpos
    psum = jnp.zeros((tq, nc), F32)
    outs = []
    for r in range(rep):
        q = q_ref[0, :, r * LANES:(r + 1) * LANES]
        sc = lax.dot_general(q, kc, _NT, preferred_element_type=F32) * scale
        sc = jnp.where(cmask, sc, NEG)
        m = jnp.max(sc, axis=-1, keepdims=True)
        ex = jnp.where(cmask, jnp.exp(sc - m), 0.0)
        den = jnp.sum(ex, axis=-1, keepdims=True)
        pc = ex / jnp.where(den > 0, den, 1.0)
        outs.append(jnp.dot(pc.astype(BF16), vc, preferred_element_type=F32))
        psum = psum + pc
    o_ref[0] = jnp.concatenate([_pack_head_pair(outs[r], outs[r + 1]) for r in range(0, rep, 2)], axis=1)
    phi, plo = _split_bf16(psum)
    ovt = ovt_ref[...]
    imp = (lax.dot_general(ovt, phi, _NT, preferred_element_type=F32)
           + lax.dot_general(ovt, plo, _NT, preferred_element_type=F32))
    jblk = lax.broadcasted_iota(jnp.int32, (nsb, tq), 0)
    cur = jnp.right_shift(t0 + lax.broadcasted_iota(jnp.int32, (nsb, tq), 1), SLC_SHIFT)
    forced = (jblk == 0) | (jblk == cur) | (jblk == cur - 1)
    score = jnp.where(jblk > cur, -1.0, jnp.where(forced, SELECT_FORCE, imp))
    rank = jnp.zeros((nsb, tq), F32)
    for k in range(n_real):
        rowk = score[k:k + 1, :]
        ge = jnp.where(rowk >= score, 1.0, 0.0)
        gt = jnp.where(rowk > score, 1.0, 0.0)
        rank = rank + jnp.where(jblk > k, ge, gt)
    drop = jnp.where(jblk <= cur, jnp.where(rank < n_sel, 0.0, 1.0), 1.0)
    drop = jnp.concatenate([drop, jnp.ones((LANES - nsb, tq), F32)], axis=0).T
    drop_ref[0, 0] = pltpu.roll(drop, HEAD_DIM, 1).astype(drop_ref.dtype)


def _nsa_cmp(q, kcmp, vcmp, batch, seq, q_off_blocks, tq=256):
    g = C_KV_HEADS
    rep = C_HEADS // g
    nc = kcmp.shape[1]
    nsb = seq // SLC_BLOCK
    n_sel = min(SLC_TOP_N, nsb)
    cs = np.arange(nc)[:, None] * CMP_STRIDE
    js = np.arange(nsb)[None, :] * SLC_BLOCK
    overlap = np.clip(np.minimum(cs + CMP_LEN, js + SLC_BLOCK) - np.maximum(cs, js), 0, None) / CMP_LEN
    overlap[(seq - CMP_LEN) // CMP_STRIDE + 1:] = 0.0
    nsb_pad = -(-nsb // 8) * 8
    ovt = jnp.asarray(np.pad(overlap.T, ((0, nsb_pad - nsb), (0, 0))), BF16)
    q3 = q.reshape(batch, seq, q.shape[1])
    n_real, nsb = nsb, nsb_pad
    assert nsb <= LANES - HEAD_DIM
    o, drop = pl.pallas_call(
        functools.partial(_nsa_cmp_kernel, rep=rep, n_sel=n_sel, n_real=n_real),
        grid=(batch, g, seq // tq),
        in_specs=[pl.BlockSpec((1, tq, rep * LANES), lambda b, gi, i: (b, i, q_off_blocks // rep + gi)),
                  pl.BlockSpec((1, nc, LANES), lambda b, gi, i: (b, 0, gi)),
                  pl.BlockSpec((1, nc, LANES), lambda b, gi, i: (b, 0, gi)),
                  pl.BlockSpec(ovt.shape, lambda b, gi, i: (0, 0))],
        out_specs=[pl.BlockSpec((1, tq, rep * HEAD_DIM), lambda b, gi, i: (b, i, gi)),
                   pl.BlockSpec((1, 1, tq, LANES), lambda b, gi, i: (b, gi, i, 0))],
        out_shape=[jax.ShapeDtypeStruct((batch, seq, C_HEADS * HEAD_DIM), F32),
                   jax.ShapeDtypeStruct((batch, g, seq, LANES), BF16)],
        compiler_params=_cparams("parallel", "parallel", "parallel"),
        name="nsa_compressed_select",
    )(q3, kcmp, vcmp, ovt)
    return o.reshape(batch * seq, C_HEADS * HEAD_DIM), drop


def _nsa_slc_kernel(q_ref, k_ref, vt_ref, drop_ref, o_ref, *, rep, kt):
    tq = q_ref.shape[1]
    t0 = pl.program_id(2) * tq
    n_kt = (t0 + tq + kt - 1) // kt
    upper = lax.broadcasted_iota(jnp.int32, (tq, LANES), 1) >= HEAD_DIM
    drop = drop_ref[0, 0]
    qs = [jnp.where(upper, drop, q_ref[0, :, r * LANES:(r + 1) * LANES] * (HEAD_DIM ** -0.5))
          for r in range(rep)]

    def tile(k0, carry, bias):
        k = k_ref[0, pl.ds(k0, kt), :]
        vt = vt_ref[:, pl.ds(k0, kt)]
        new = []
        for qr, (m, l, acc) in zip(qs, carry):
            s = lax.dot_general(k, qr, _NT, preferred_element_type=F32)
            if bias is not None:
                s = s + bias
            m_new = jnp.maximum(m, jnp.max(s, axis=0, keepdims=True))
            e = jnp.exp(s - m_new)
            corr = jnp.exp(m - m_new)
            l = l * corr + jnp.sum(e, axis=0, keepdims=True)
            acc = acc * corr + jnp.dot(vt, e.astype(BF16), preferred_element_type=F32)
            new.append((m_new, l, acc))
        return tuple(new)

    init = tuple((jnp.full((1, tq), NEG, F32), jnp.zeros((1, tq), F32), jnp.zeros((LANES, tq), F32))
                 for _ in range(rep))
    carry = lax.fori_loop(0, n_kt - 1, lambda j, c: tile(pl.multiple_of(j * kt, kt), c, None), init)
    k_last = pl.multiple_of((n_kt - 1) * kt, kt)
    kpos = k_last + lax.broadcasted_iota(jnp.int32, (kt, tq), 0)
    tpos = t0 + lax.broadcasted_iota(jnp.int32, (kt, tq), 1)
    carry = tile(k_last, carry, jnp.where(kpos <= tpos, 0.0, NEG))
    outs = [(acc / l).T for _, l, acc in carry]
    o_ref[0] = jnp.concatenate([_pack_head_pair(outs[r], outs[r + 1]) for r in range(0, rep, 2)], axis=1)


def _nsa_slc(qkv, vt, drop, batch, seq, nblk, q_off, k_off, tq=256, kt=1024):
    g = C_KV_HEADS
    rep = C_HEADS // g
    kt = min(kt, seq)
    arr = qkv.reshape(batch, seq, nblk * LANES)
    out = pl.pallas_call(
        functools.partial(_nsa_slc_kernel, rep=rep, kt=kt),
        grid=(batch, g, seq // tq),
        in_specs=[pl.BlockSpec((1, tq, rep * LANES), lambda b, gi, i: (b, i, q_off // rep + gi)),
                  pl.BlockSpec((1, seq, LANES), lambda b, gi, i: (b, 0, k_off + gi)),
                  pl.BlockSpec((LANES, seq), lambda b, gi, i: (gi, b)),
                  pl.BlockSpec((1, 1, tq, LANES), lambda b, gi, i: (b, gi, i, 0))],
        out_specs=pl.BlockSpec((1, tq, rep * HEAD_DIM), lambda b, gi, i: (b, i, gi)),
        out_shape=jax.ShapeDtypeStruct((batch, seq, C_HEADS * HEAD_DIM), F32),
        compiler_params=_cparams("parallel", "parallel", "parallel"),
        name="nsa_selected",
    )(arr, arr, vt, drop)
    return out.reshape(batch * seq, C_HEADS * HEAD_DIM)


def _even_mixer_layer(x2, batch, seq, tabs, w_in, sinks, w_out, g, b):
    d = x2.shape[1]
    n_heads_in = 3 * A_HEADS + B_Q_HEADS + 2 * B_KV_HEADS
    w = _pad_heads_cols(w_in, n_heads_in).astype(BF16)
    rope = [1] * (2 * A_HEADS) + [0] * A_HEADS + [1] * B_Q_HEADS + [1] * B_KV_HEADS + [0] * B_KV_HEADS
    plan = [(0, c, rope[c]) for c in range(n_heads_in)]
    dilations = [dil for _, dil in A_PATTERNS]
    regroup = [dil for dil in dilations if dil > 1]
    n_a_blocks = 3 * A_HEADS
    qkv, *grouped = _proj(x2, w, tabs, plan, [n_heads_in * LANES], [BF16], dilations=regroup, n_dil=n_a_blocks)
    outs = []
    for window, dilation in A_PATTERNS:
        src, nblk = (qkv, n_heads_in) if dilation == 1 else (grouped[regroup.index(dilation)], n_a_blocks)
        outs.append(_band_attention(src, batch=batch, seq=seq, dilation=dilation, nblk=nblk, q_off=0,
                                    k_off=A_HEADS, v_off=2 * A_HEADS, n_q_heads=A_HEADS, rep=1,
                                    max_dist=window // dilation,
                                    heads=min(4, dilation)))
    qb_off = 3 * A_HEADS
    ob = _band_attention(qkv, batch=batch, seq=seq, dilation=1, nblk=n_heads_in, q_off=qb_off,
                         k_off=qb_off + B_Q_HEADS, v_off=qb_off + B_Q_HEADS + B_KV_HEADS, n_q_heads=B_Q_HEADS,
                         rep=B_Q_HEADS // B_KV_HEADS, max_dist=B_WINDOW - 1, sinks=sinks)
    w_o = _pad_heads_rows(w_out, A_HEADS + B_Q_HEADS).astype(BF16)
    return _outproj_even(outs, dilations, ob, x2, w_o, g.reshape(1, d), b.reshape(1, d))


def _odd_mixer_layer(x2, batch, seq, tabs, w_in, cmpk_pos, cmpk_w1, cmpk_w2, cmpv_pos, cmpv_w1, cmpv_w2, w_out, g, b):
    d = x2.shape[1]
    kvw = C_KV_HEADS * HEAD_DIM
    qw = C_HEADS * HEAD_DIM
    sizes = [qw] + [kvw] * 6 + [3 * C_HEADS]
    offs = np.concatenate([[0], np.cumsum(sizes)])
    wq, wkc, wvc, wks, wvs, wkw, wvw, wgt = [w_in[:, offs[i]:offs[i + 1]] for i in range(8)]
    ph = lambda wpart, n: _pad_heads_cols(wpart, n)
    w = jnp.concatenate([ph(wq, C_HEADS), ph(wks, C_KV_HEADS), ph(wkw, C_KV_HEADS), ph(wvw, C_KV_HEADS),
                         wkc, wvc, jnp.pad(wgt, ((0, 0), (0, LANES - 3 * C_HEADS)))], axis=1).astype(BF16)
    wvs_t = ph(wvs, C_KV_HEADS).T.astype(BF16)
    n16 = C_HEADS + 3 * C_KV_HEADS
    rope16 = [1] * C_HEADS + [TAG_BLOCK] * C_KV_HEADS + [1] * C_KV_HEADS + [0] * C_KV_HEADS
    n_kc = kvw // LANES
    plan = ([(0, c, rope16[c]) for c in range(n16)] + [(1, c, 2) for c in range(n_kc)]
            + [(2, c, 0) for c in range(n_kc)] + [(3, 0, 0)])
    qkv, kc, vc, gate, vs_t = _proj(x2, w, tabs, plan, [n16 * LANES, kvw, kvw, LANES], [BF16, F32, F32, F32],
                                    wt=wvs_t, seq=seq)
    kcmp = _compress(kc, cmpk_pos, cmpk_w1, cmpk_w2, batch, seq)
    vcmp = _compress(vc, cmpv_pos, cmpv_w1, cmpv_w2, batch, seq)
    o_cmp, drop = _nsa_cmp(qkv, kcmp, vcmp, batch, seq, 0)
    ks_off = C_HEADS
    o_slc = _nsa_slc(qkv, vs_t, drop, batch, seq, n16, 0, ks_off)
    kw_off = ks_off + C_KV_HEADS
    o_win = _band_attention(qkv, batch=batch, seq=seq, dilation=1, nblk=n16, q_off=0, k_off=kw_off,
                            v_off=kw_off + C_KV_HEADS, n_q_heads=C_HEADS, rep=C_HEADS // C_KV_HEADS,
                            max_dist=NSA_WINDOW - 1, heads=C_HEADS // C_KV_HEADS, pack_out=True)
    return _outproj_odd(o_cmp, o_slc, o_win, gate, x2, w_out.astype(BF16), g.reshape(1, d), b.reshape(1, d))


def kernel(x, positions, even_w_in, even_sinks, even_w_out, odd_w_in, odd_cmpk_pos, odd_cmpk_w1, odd_cmpk_w2, odd_cmpv_pos, odd_cmpv_w1, odd_cmpv_w2, odd_w_out, mix_ln_g, mix_ln_b, moe_router_w, moe_router_b, moe_w_gate, moe_w_up, moe_w_down, moe_sh_gate, moe_sh_up, moe_sh_down, ffn_ln_g, ffn_ln_b):
    batch, seq, d = x.shape
    x2 = x.reshape(batch * seq, d)
    tabs = _rope_tables(positions)
    depth = mix_ln_g.shape[0]
    for layer in range(depth):
        j = layer // 2
        if layer % 2 == 0:
            x2 = _even_mixer_layer(x2, batch, seq, tabs, even_w_in[j], even_sinks[j], even_w_out[j],
                                   mix_ln_g[layer], mix_ln_b[layer])
        else:
            x2 = _odd_mixer_layer(x2, batch, seq, tabs, odd_w_in[j], odd_cmpk_pos[j], odd_cmpk_w1[j], odd_cmpk_w2[j],
                                  odd_cmpv_pos[j], odd_cmpv_w1[j], odd_cmpv_w2[j], odd_w_out[j],
                                  mix_ln_g[layer], mix_ln_b[layer])
        x2 = _moe_layer(x2, moe_router_w[layer], moe_router_b[layer], moe_w_gate, moe_w_up, moe_w_down, layer,
                        moe_sh_gate[layer], moe_sh_up[layer], moe_sh_down[layer],
                        ffn_ln_g[layer].reshape(1, d), ffn_ln_b[layer].reshape(1, d))
    return x2.reshape(batch, seq, d)
```

```python
import functools

import numpy as np
import jax
import jax.numpy as jnp
from jax import lax
from jax.experimental import pallas as pl
from jax.experimental.pallas import tpu as pltpu

F32 = jnp.float32
BF16 = jnp.bfloat16

LANES = 128
HEAD_DIM = 64
ROT_DIM = HEAD_DIM // 4
ROT_HALF = ROT_DIM // 2
ROPE_THETA = 500000.0
QBLK = 128
A_HEADS = 8
A_PATTERNS = ((128, 1), (512, 4), (2048, 16))
B_Q_HEADS = 8
B_KV_HEADS = 2
B_WINDOW = 128
C_HEADS = 16
C_KV_HEADS = 4
CMP_LEN = 32
CMP_STRIDE = 16
CMP_HIDDEN = 2 * HEAD_DIM
SLC_BLOCK = 64
SLC_SHIFT = 6
SLC_TOP_N = 16
NSA_WINDOW = 512
SELECT_FORCE = 1.0e4
N_EXPERTS = 256
TOP_K = 8
N_GROUPS = 8
TOPK_GROUPS = 4
ROUTED_SCALE = 2.5
MOE_BLOCK = 256
DEPTH = 2
DEEPNORM_ALPHA = (2 * DEPTH) ** 0.25
LN_EPS = 1e-5
NEG = -1.0e30
VMEM_LIMIT = 56 * 1024 * 1024

_NT = (((1,), (1,)), ((), ()))


def _cparams(*sem):
    return pltpu.CompilerParams(dimension_semantics=sem, vmem_limit_bytes=VMEM_LIMIT)


def _split_bf16(a):
    hi = a.astype(BF16)
    lo = (a - hi.astype(F32)).astype(BF16)
    return hi, lo


def _layer_norm(y, g, b):
    mu = jnp.mean(y, axis=-1, keepdims=True)
    d = y - mu
    var = jnp.mean(d * d, axis=-1, keepdims=True)
    return d * lax.rsqrt(var + LN_EPS) * g + b


def _silu(a):
    return a * jax.nn.sigmoid(a)


TAG_BLOCK = 3


def _proj_kernel(*refs, plan, n_main, transposed, dilations, n_dil, seq):
    n_in = 4 if transposed else 3
    x_ref, w_ref, tab_ref = refs[:3]
    out_refs = refs[n_in:n_in + n_main]
    extra = list(refs[n_in + n_main:])
    x = x_ref[...].astype(BF16)
    tm = x.shape[0]
    if transposed:
        out_t = extra.pop(0)
        out_t[...] = lax.dot_general(refs[3][...], x, _NT, preferred_element_type=F32).astype(out_t.dtype)
    dil_refs = [extra.pop(0) for _ in dilations]
    stage = extra.pop(0) if dilations else None
    nblk = len(plan)
    for c0 in range(0, nblk, 2):
        nb = min(2, nblk - c0)
        acc = jnp.dot(x, w_ref[:, c0 * LANES:(c0 + nb) * LANES], preferred_element_type=F32)
        for j in range(nb):
            blk = acc[:, j * LANES:(j + 1) * LANES]
            dst, dblk, mode = plan[c0 + j]
            if mode == TAG_BLOCK:
                pos = lax.rem(pl.program_id(0) * tm, seq) + lax.broadcasted_iota(jnp.int32, (tm, LANES), 0)
                tag = lax.broadcasted_iota(jnp.int32, (tm, LANES), 1) - HEAD_DIM == jnp.right_shift(pos, SLC_SHIFT)
                mode = 1
            else:
                tag = None
            if mode:
                off = (mode - 1) * 3 * LANES
                cos = tab_ref[:, off:off + LANES]
                s_lo = tab_ref[:, off + LANES:off + 2 * LANES]
                s_hi = tab_ref[:, off + 2 * LANES:off + 3 * LANES]
                blk = (blk * cos + pltpu.roll(blk, LANES - ROT_HALF, 1) * s_lo
                       + pltpu.roll(blk, ROT_HALF, 1) * s_hi)
            if tag is not None:
                blk = jnp.where(tag, NEG, blk)
            o_ref = out_refs[dst]
            o_ref[:, dblk * LANES:(dblk + 1) * LANES] = blk.astype(o_ref.dtype)
            c = c0 + j
            if dilations and c < n_dil:
                stage[...] = blk
                for dil, d_ref in zip(dilations, dil_refs):
                    for r in range(dil):
                        col = (r * n_dil + c) * LANES
                        d_ref[:, col:col + LANES] = stage[pl.ds(r, tm // dil, stride=dil), :].astype(d_ref.dtype)


def _proj(x2, w, tabs, plan, out_cols, out_dtypes, wt=None, dilations=(), n_dil=0, seq=0, tm=512):
    t, d = x2.shape
    ncol = w.shape[1]
    out_shape = [jax.ShapeDtypeStruct((t, c), dt) for c, dt in zip(out_cols, out_dtypes)]
    once = pl.Buffered(1)
    in_specs = [pl.BlockSpec((tm, d), lambda i: (i, 0)),
                pl.BlockSpec((d, ncol), lambda i: (0, 0), pipeline_mode=once),
                pl.BlockSpec((tm, tabs.shape[1]), lambda i: (i, 0))]
    out_specs = [pl.BlockSpec((tm, c), lambda i: (i, 0)) for c in out_cols]
    args = [x2, w, tabs]
    if wt is not None:
        in_specs.append(pl.BlockSpec(wt.shape, lambda i: (0, 0), pipeline_mode=once))
        out_specs.append(pl.BlockSpec((wt.shape[0], tm), lambda i: (0, i)))
        out_shape.append(jax.ShapeDtypeStruct((wt.shape[0], t), BF16))
        args.append(wt)
    for dil in dilations:
        out_specs.append(pl.BlockSpec((tm // dil, dil * n_dil * LANES), lambda i: (i, 0)))
        out_shape.append(jax.ShapeDtypeStruct((t // dil, dil * n_dil * LANES), BF16))
    return pl.pallas_call(
        functools.partial(_proj_kernel, plan=tuple(plan), n_main=len(out_cols), transposed=wt is not None,
                          dilations=tuple(dilations), n_dil=n_dil, seq=seq),
        grid=(t // tm,),
        in_specs=in_specs,
        out_specs=out_specs,
        out_shape=out_shape,
        scratch_shapes=[pltpu.VMEM((tm, LANES), F32)] if dilations else [],
        compiler_params=_cparams("parallel"),
        name="proj_rope",
    )(*args)


def _rope_tables(positions):
    t = positions.size
    inv_freq = jnp.asarray(ROPE_THETA ** (-np.arange(0, ROT_DIM, 2) / ROT_DIM), F32)
    ang = positions.astype(F32).reshape(t, 1) * inv_freq
    trig = jnp.concatenate([jnp.cos(ang), jnp.sin(ang)], axis=1)
    place = np.zeros((2 * ROT_HALF, 6 * LANES), np.float32)
    const = np.zeros((6 * LANES,), np.float32)
    for base, heads in ((0, (0,)), (3 * LANES, (0, HEAD_DIM))):
        const[base:base + LANES] = 1.0
        for h in heads:
            for i in range(ROT_HALF):
                place[i, base + h + i] = 1.0
                place[i, base + h + ROT_HALF + i] = 1.0
                const[base + h + i] = const[base + h + ROT_HALF + i] = 0.0
                place[ROT_HALF + i, base + LANES + h + i] = -1.0
                place[ROT_HALF + i, base + 2 * LANES + h + ROT_HALF + i] = 1.0
    return jnp.dot(trig, jnp.asarray(place), precision=lax.Precision.HIGHEST) + jnp.asarray(const)


def _pad_heads_cols(w, n_heads):
    d = w.shape[0]
    w = w.reshape(d, n_heads, HEAD_DIM)
    return jnp.pad(w, ((0, 0), (0, 0), (0, LANES - HEAD_DIM))).reshape(d, n_heads * LANES)


def _pad_heads_rows(w, n_heads):
    d = w.shape[1]
    w = w.reshape(n_heads, HEAD_DIM, d)
    return jnp.pad(w, ((0, 0), (0, LANES - HEAD_DIM), (0, 0))).reshape(n_heads * LANES, d)


def _pack_head_pair(a, b):
    lane = lax.broadcasted_iota(jnp.int32, a.shape, 1)
    return jnp.where(lane < HEAD_DIM, a, pltpu.roll(b, HEAD_DIM, 1))


def _band_kernel(*refs, back, max_dist, length, qrows, heads, shared_kv, in_flight, has_sink, pack_out):
    if has_sink:
        sink_ref, q_ref, k_ref, v_ref, o_ref, bias_sc = refs
    else:
        q_ref, k_ref, v_ref, o_ref, bias_sc = refs
    nq = length // qrows
    window = qrows + back
    chains = [tuple(range(heads))] if shared_kv else [(h,) for h in range(heads)]
    stack = len(chains[0])
    lane = lax.broadcasted_iota(jnp.int32, (stack * qrows, LANES), 1)
    first_head = pl.program_id(2) * heads

    def band_bias(offset):
        row = lax.broadcasted_iota(jnp.int32, (qrows, window), 0)
        col = lax.broadcasted_iota(jnp.int32, (qrows, window), 1)
        dist = offset + row - col
        tile = jnp.where((dist >= 0) & (dist <= max_dist), 0.0, NEG)
        return jnp.concatenate([tile] * stack, axis=0) if stack > 1 else tile

    bias_sc[...] = band_bias(back)

    def block(hs, q0, k0, bias):
        kv = slice(0, LANES) if shared_kv else slice(hs[0] * LANES, (hs[0] + 1) * LANES)
        q = jnp.concatenate([q_ref[0, pl.ds(q0, qrows), h * LANES:(h + 1) * LANES] for h in hs], axis=0)
        q = q * (HEAD_DIM ** -0.5)
        k = k_ref[0, pl.ds(k0, window), kv]
        v = v_ref[0, pl.ds(k0, window), kv]
        s = lax.dot_general(q, k, _NT, preferred_element_type=F32) + bias
        m = jnp.max(s, axis=-1, keepdims=True)
        e = jnp.exp(s - m)
        den = jnp.sum(e, axis=-1, keepdims=True)
        if has_sink:
            sink = jnp.concatenate([jnp.full((qrows, 1), sink_ref[first_head + h], F32) for h in hs], axis=0)
            den = den + jnp.exp(sink - m)
        o = jnp.dot(e.astype(BF16), v, preferred_element_type=F32) / den
        lse = m + jnp.log(den)
        if pack_out:
            for i in range(0, len(hs), 2):
                pair = _pack_head_pair(o[i * qrows:(i + 1) * qrows], o[(i + 1) * qrows:(i + 2) * qrows])
                o_ref[0, pl.ds(q0, qrows), (hs[i] // 2) * LANES:(hs[i] // 2 + 1) * LANES] = pair
            return
        out = jnp.where(lane < HEAD_DIM, o, lse)
        for i, h in enumerate(hs):
            o_ref[0, pl.ds(q0, qrows), h * LANES:(h + 1) * LANES] = out[i * qrows:(i + 1) * qrows]

    n_clipped = min(-(-back // qrows), nq)
    for qi in range(n_clipped):
        bias = band_bias(qi * qrows)
        for hs in chains:
            block(hs, qi * qrows, 0, bias)

    steady = nq - n_clipped
    per_iter = max(1, min(in_flight // heads, steady))
    n_iter = steady // per_iter

    def body(it, carry):
        aligned = lambda v: v if isinstance(v, int) else pl.multiple_of(v, QBLK)
        for j in range(per_iter):
            q0 = aligned((n_clipped + it * per_iter + j) * qrows)
            for hs in chains:
                block(hs, q0, aligned(q0 - back), bias_sc[...])
        return carry

    if n_iter == 1:
        body(0, 0)
    elif n_iter:
        lax.fori_loop(0, n_iter, body, 0)
    for qi in range(n_clipped + n_iter * per_iter, nq):
        for hs in chains:
            block(hs, qi * qrows, qi * qrows - back, bias_sc[...])


def _band_attention(qkv, *, batch, seq, dilation, nblk, q_off, k_off, v_off, n_q_heads, rep, max_dist, sinks=None,
                    heads=1, in_flight=16, pack_out=False):
    length = seq // dilation
    back = min(-(-max_dist // QBLK) * QBLK, length - QBLK)
    qrows = QBLK
    shared_kv = rep > 1
    assert n_q_heads % heads == 0 and (rep == 1 or rep % heads == 0)
    assert not pack_out or (shared_kv and heads % 2 == 0)
    out_lanes = HEAD_DIM if pack_out else LANES
    kv_heads = 1 if shared_kv else heads
    arr = qkv.reshape(batch, length, dilation * nblk * LANES)
    assert q_off % heads == 0 and k_off % kv_heads == 0 and v_off % kv_heads == 0 and nblk % heads == 0
    qspec = pl.BlockSpec((1, length, heads * LANES),
                         lambda b, r, h: (b, 0, (r * nblk + q_off) // heads + h))
    kvspec = lambda off: pl.BlockSpec(
        (1, length, kv_heads * LANES),
        lambda b, r, h: (b, 0, (r * nblk + off) // kv_heads + (h * heads // rep if shared_kv else h)))
    in_specs = [qspec, kvspec(k_off), kvspec(v_off)]
    args = [arr, arr, arr]
    if sinks is not None:
        in_specs = [pl.BlockSpec(memory_space=pltpu.SMEM)] + in_specs
        args = [sinks.reshape(-1).astype(F32)] + args
    out = pl.pallas_call(
        functools.partial(_band_kernel, back=back, max_dist=max_dist, length=length, qrows=qrows, heads=heads,
                          shared_kv=shared_kv, in_flight=in_flight, has_sink=sinks is not None,
                          pack_out=pack_out),
        grid=(batch, dilation, n_q_heads // heads),
        in_specs=in_specs,
        out_specs=pl.BlockSpec((1, length, heads * out_lanes), lambda b, r, h: (b, 0, r * (n_q_heads // heads) + h)),
        out_shape=jax.ShapeDtypeStruct((batch, length, dilation * n_q_heads * out_lanes), F32),
        scratch_shapes=[pltpu.VMEM(((heads if shared_kv else 1) * qrows, qrows + back), F32)],
        compiler_params=_cparams("parallel", "parallel", "parallel"),
        name="band_attention",
    )(*args)
    return out.reshape(batch * length, dilation * n_q_heads * out_lanes)


def _outproj_even_kernel(o1_ref, o2_ref, o3_ref, ob_ref, x_ref, w_ref, g_ref, b_ref, out_ref, *nat_refs,
                         n_a, n_b, dilations):
    tm = x_ref.shape[0]
    lane = lax.broadcasted_iota(jnp.int32, (tm, LANES), 1)
    real = lane < HEAD_DIM
    pattern_refs = []
    nat_refs = list(nat_refs)
    for o_ref, dil in zip((o1_ref, o2_ref, o3_ref), dilations):
        if dil == 1:
            pattern_refs.append(o_ref)
            continue
        nat = nat_refs.pop(0)
        for r in range(dil):
            for h in range(n_a):
                col = (r * n_a + h) * LANES
                nat[h, pl.ds(r, tm // dil, stride=dil), :] = o_ref[:, col:col + LANES]
        pattern_refs.append(nat)
    parts = []
    for h in range(n_a):
        sl = slice(h * LANES, (h + 1) * LANES)
        outs = [r[:, sl] if r.ndim == 2 else r[h] for r in pattern_refs]
        lses = [jnp.where(real, pltpu.roll(a, HEAD_DIM, 1), a) for a in outs]
        m = jnp.maximum(jnp.maximum(lses[0], lses[1]), lses[2])
        es = [jnp.exp(l - m) for l in lses]
        num = es[0] * outs[0] + es[1] * outs[1] + es[2] * outs[2]
        den = es[0] + es[1] + es[2]
        parts.append(jnp.where(real, num / den, 0.0).astype(BF16))
    for h in range(n_b):
        parts.append(jnp.where(real, ob_ref[:, h * LANES:(h + 1) * LANES], 0.0).astype(BF16))
    a = jnp.concatenate(parts, axis=1)
    mixed = jnp.dot(a, w_ref[...], preferred_element_type=F32)
    y = DEEPNORM_ALPHA * x_ref[...] + mixed
    out_ref[...] = _layer_norm(y, g_ref[...], b_ref[...])


def _outproj_even(outs, dilations, ob, x2, w, g, b, tm=256):
    t, d = x2.shape
    n_a, n_b = outs[0].shape[1] // (dilations[0] * LANES), ob.shape[1] // LANES
    row = lambda c: pl.BlockSpec((tm, c), lambda i: (i, 0))
    grouped = lambda a, dil: pl.BlockSpec((tm // dil, a.shape[1]), lambda i: (i, 0))
    full = lambda a: pl.BlockSpec(a.shape, lambda i: (0,) * a.ndim)
    return pl.pallas_call(
        functools.partial(_outproj_even_kernel, n_a=n_a, n_b=n_b, dilations=tuple(dilations)),
        grid=(t // tm,),
        in_specs=[grouped(o, dil) for o, dil in zip(outs, dilations)] + [row(ob.shape[1]), row(d),
                                                                          full(w), full(g), full(b)],
        out_specs=row(d),
        out_shape=jax.ShapeDtypeStruct((t, d), F32),
        scratch_shapes=[pltpu.VMEM((n_a, tm, LANES), F32) for dil in dilations if dil > 1],
        compiler_params=_cparams("parallel"),
        name="outproj_even_ln",
    )(*outs, ob, x2, w, g, b)


def _outproj_odd_kernel(oc_ref, os_ref, ow_ref, gate_ref, e_ref, x_ref, w_ref, g_ref, b_ref, out_ref):
    gate = jax.nn.sigmoid(gate_ref[...])
    ghi, glo = _split_bf16(gate)
    acc = None
    for j, o_ref in enumerate((oc_ref, os_ref, ow_ref)):
        ej = e_ref[j]
        gfull = jnp.dot(ghi, ej, preferred_element_type=F32) + jnp.dot(glo, ej, preferred_element_type=F32)
        term = gfull * o_ref[...]
        acc = term if acc is None else acc + term
    mixed = jnp.dot(acc.astype(BF16), w_ref[...], preferred_element_type=F32)
    y = DEEPNORM_ALPHA * x_ref[...] + mixed
    out_ref[...] = _layer_norm(y, g_ref[...], b_ref[...])


def _gate_expanders(n_heads):
    e = np.zeros((3, LANES, n_heads * HEAD_DIM), np.float32)
    for j in range(3):
        for h in range(n_heads):
            e[j, 3 * h + j, h * HEAD_DIM:(h + 1) * HEAD_DIM] = 1.0
    return jnp.asarray(e, BF16)


def _outproj_odd(oc, osl, ow, gate, x2, w, g, b, tm=256):
    t, d = x2.shape
    n_heads = oc.shape[1] // HEAD_DIM
    e = _gate_expanders(n_heads)
    row = lambda c: pl.BlockSpec((tm, c), lambda i: (i, 0))
    full = lambda a: pl.BlockSpec(a.shape, lambda i: (0,) * a.ndim)
    return pl.pallas_call(
        _outproj_odd_kernel,
        grid=(t // tm,),
        in_specs=[row(oc.shape[1]), row(osl.shape[1]), row(ow.shape[1]), row(LANES), full(e), row(d),
                  full(w), full(g), full(b)],
        out_specs=row(d),
        out_shape=jax.ShapeDtypeStruct((t, d), F32),
        compiler_params=_cparams("parallel"),
        name="outproj_odd_ln",
    )(oc, osl, ow, gate, e, x2, w, g, b)


def _router_kernel(x_ref, whi_ref, wlo_ref, bias_ref, eidx_ref, gate_ref, rank_ref, cnt_ref):
    n_exp = whi_ref.shape[0]
    tm = x_ref.shape[0]
    per_group = n_exp // N_GROUPS
    xhi, xlo = _split_bf16(x_ref[...])
    whi, wlo = whi_ref[...], wlo_ref[...]
    dg = lambda a, b: lax.dot_general(a, b, _NT, preferred_element_type=F32)
    logits = dg(whi, xhi) + dg(whi, xlo) + dg(wlo, xhi)
    aff = jax.nn.sigmoid(logits)
    biased = aff + bias_ref[...]
    gio = lax.broadcasted_iota(jnp.int32, (per_group, tm), 0).astype(F32)
    blocks, scores = [], []
    for g in range(N_GROUPS):
        blk = biased[g * per_group:(g + 1) * per_group, :]
        m1 = jnp.max(blk, axis=0, keepdims=True)
        first = jnp.min(jnp.where(blk == m1, gio, float(per_group)), axis=0, keepdims=True)
        m2 = jnp.max(jnp.where(gio == first, -jnp.inf, blk), axis=0, keepdims=True)
        blocks.append(blk)
        scores.append(m1 + m2)
    masked = []
    for g in range(N_GROUPS):
        rank = jnp.zeros((1, tm), F32)
        for o in range(N_GROUPS):
            if o == g:
                continue
            beats = scores[o] >= scores[g] if o < g else scores[o] > scores[g]
            rank = rank + jnp.where(beats, 1.0, 0.0)
        masked.append(jnp.where(rank < TOPK_GROUPS, blocks[g], -jnp.inf))
    cur = jnp.concatenate(masked, axis=0)
    eio = lax.broadcasted_iota(jnp.int32, (n_exp, tm), 0).astype(F32)
    ids, gs = [], []
    for _ in range(TOP_K):
        m = jnp.max(cur, axis=0, keepdims=True)
        idx = jnp.min(jnp.where(cur == m, eio, float(n_exp)), axis=0, keepdims=True)
        hit = eio == idx
        gs.append(jnp.sum(jnp.where(hit, aff, 0.0), axis=0, keepdims=True))
        ids.append(idx)
        cur = jnp.where(hit, -jnp.inf, cur)
    gates = jnp.concatenate(gs, axis=0)
    gates = gates / jnp.sum(gates, axis=0, keepdims=True) * ROUTED_SCALE
    eidx_ref[...] = jnp.concatenate(ids, axis=0).astype(jnp.int32)
    gate_ref[...] = gates
    @pl.when(pl.program_id(0) == 0)
    def _():
        cnt_ref[...] = jnp.zeros(cnt_ref.shape, F32)

    onehot = jnp.zeros((n_exp, tm), F32)
    for idx in ids:
        onehot = onehot + jnp.where(eio == idx, 1.0, 0.0)
    earlier = jnp.where(lax.broadcasted_iota(jnp.int32, (tm, tm), 0) < lax.broadcasted_iota(jnp.int32, (tm, tm), 1),
                        1.0, 0.0).astype(BF16)
    before = cnt_ref[...] + jnp.dot(onehot.astype(BF16), earlier, preferred_element_type=F32)
    ranks = [jnp.sum(jnp.where(eio == idx, before, 0.0), axis=0, keepdims=True) for idx in ids]
    rank_ref[...] = jnp.concatenate(ranks, axis=0).astype(jnp.int32)
    cnt_ref[...] = cnt_ref[...] + jnp.sum(onehot, axis=1, keepdims=True)


def _router(x2, router_w, router_b, tm=256):
    t, d = x2.shape
    n_exp = router_w.shape[1]
    whi, wlo = _split_bf16(router_w.T)
    bias = router_b.reshape(n_exp, 1).astype(F32)
    full = lambda a: pl.BlockSpec(a.shape, lambda i: (0,) * a.ndim)
    per_tok = pl.BlockSpec((TOP_K, tm), lambda i: (0, i))
    return pl.pallas_call(
        _router_kernel,
        grid=(t // tm,),
        in_specs=[pl.BlockSpec((tm, d), lambda i: (i, 0)), full(whi), full(wlo), full(bias)],
        out_specs=[per_tok, per_tok, per_tok, pl.BlockSpec((n_exp, 1), lambda i: (0, 0))],
        out_shape=[jax.ShapeDtypeStruct((TOP_K, t), jnp.int32), jax.ShapeDtypeStruct((TOP_K, t), F32),
                   jax.ShapeDtypeStruct((TOP_K, t), jnp.int32), jax.ShapeDtypeStruct((n_exp, 1), F32)],
        compiler_params=_cparams("arbitrary"),
        name="moe_router",
    )(x2, whi, wlo, bias)


def _moe_dest_kernel(eidx_ref, rank_ref, start_ref, dest_ref):
    n_exp = start_ref.shape[0]
    tm = eidx_ref.shape[1]
    eio = lax.broadcasted_iota(jnp.int32, (n_exp, tm), 0)
    start = start_ref[...]
    rows = []
    for k in range(TOP_K):
        seg = jnp.sum(jnp.where(eio == eidx_ref[k:k + 1, :], start, 0.0), axis=0, keepdims=True)
        rows.append(seg.astype(jnp.int32) + rank_ref[k:k + 1, :])
    dest_ref[...] = jnp.concatenate(rows, axis=0)


def _moe_dest(eidx, rank, seg_start, tm=256):
    t = eidx.shape[1]
    per_tok = pl.BlockSpec((TOP_K, tm), lambda i: (0, i))
    return pl.pallas_call(
        _moe_dest_kernel,
        grid=(t // tm,),
        in_specs=[per_tok, per_tok, pl.BlockSpec(seg_start.shape, lambda i: (0, 0))],
        out_specs=per_tok,
        out_shape=jax.ShapeDtypeStruct((TOP_K, t), jnp.int32),
        compiler_params=_cparams("parallel"),
        name="moe_dest",
    )(eidx, rank, seg_start)


def _to_slabs(ref, value):
    rows, width = value.shape
    n_chunks = width // LANES
    for c in range(n_chunks):
        ref[pl.ds(c, rows, stride=n_chunks), :] = value[:, c * LANES:(c + 1) * LANES]


def _from_slabs(ref, first_row, rows, n_chunks):
    return jnp.concatenate([ref[pl.ds(first_row * n_chunks + c, rows, stride=n_chunks), :]
                            for c in range(n_chunks)], axis=1)


def _pack_bf16_pairs(x):
    half = x.shape[1] // 2
    lo = lax.bitcast_convert_type(x[:, :half].astype(BF16).astype(F32), jnp.uint32)
    hi = lax.bitcast_convert_type(x[:, half:].astype(BF16).astype(F32), jnp.uint32)
    return jnp.right_shift(lo, jnp.uint32(16)) | (hi & jnp.uint32(0xFFFF0000))


def _unpack_bf16_pairs(w):
    lo = lax.bitcast_convert_type(jnp.left_shift(w, jnp.uint32(16)), F32).astype(BF16)
    hi = lax.bitcast_convert_type(w & jnp.uint32(0xFFFF0000), F32).astype(BF16)
    return jnp.concatenate([lo, hi], axis=1)


def _moe_dispatch_kernel(zb_ref, dest_ref, x_ref, xs_out, buf, zbuf, sem, zsem):
    tm = x_ref.shape[0]
    n_chunks = buf.shape[0] // tm

    @pl.when(pl.program_id(0) == 0)
    def _():
        rows = zbuf.shape[0]
        zbuf[...] = jnp.zeros(zbuf.shape, zbuf.dtype)

        def zero_copy(e):
            first = pl.multiple_of(zb_ref[e] * rows, rows)
            return pltpu.make_async_copy(zbuf, xs_out.at[pl.ds(first, rows)], zsem)

        def start(e, c):
            @pl.when(zb_ref[e] >= 0)
            def _():
                zero_copy(e).start()
            return c

        def wait(e, c):
            @pl.when(zb_ref[e] >= 0)
            def _():
                zero_copy(e).wait()
            return c

        lax.fori_loop(0, zb_ref.shape[0], start, 0)
        lax.fori_loop(0, zb_ref.shape[0], wait, 0)

    _to_slabs(buf, _pack_bf16_pairs(x_ref[...]))

    def issue(i, c):
        src = buf.at[pl.ds(pl.multiple_of(i * n_chunks, n_chunks), n_chunks)]
        for k in range(TOP_K):
            r = pl.multiple_of(dest_ref[k, i] * n_chunks, n_chunks)
            pltpu.make_async_copy(src, xs_out.at[pl.ds(r, n_chunks)], sem).start(priority=k % 2)
        return c

    lax.fori_loop(0, tm, issue, 0)
    for _ in range(TOP_K):
        pltpu.make_async_copy(buf, xs_out.at[pl.ds(0, tm * n_chunks)], sem).wait()


def _moe_dispatch(zero_blocks, dest, x2, n_blocks, tm=256):
    t, d = x2.shape
    n_chunks = d // 2 // LANES
    grid_spec = pltpu.PrefetchScalarGridSpec(
        num_scalar_prefetch=1,
        grid=(t // tm,),
        in_specs=[pl.BlockSpec((TOP_K, tm), lambda i, zb: (0, i), memory_space=pltpu.SMEM),
                  pl.BlockSpec((tm, d), lambda i, zb: (i, 0))],
        out_specs=pl.BlockSpec(memory_space=pl.ANY),
        scratch_shapes=[pltpu.VMEM((tm * n_chunks, LANES), jnp.uint32),
                        pltpu.VMEM((MOE_BLOCK * n_chunks, LANES), jnp.uint32),
                        pltpu.SemaphoreType.DMA(()), pltpu.SemaphoreType.DMA(())],
    )
    return pl.pallas_call(
        _moe_dispatch_kernel,
        grid_spec=grid_spec,
        out_shape=jax.ShapeDtypeStruct((n_blocks * MOE_BLOCK * n_chunks, LANES), jnp.uint32),
        compiler_params=_cparams("arbitrary"),
        name="moe_dispatch",
    )(zero_blocks, dest, x2)


def _moe_ffn_kernel(be_ref, nu_ref, xs_ref, wg_ref, wu_ref, wd_ref, y_ref, wg_sc, wu_sc, wd_sc):
    b = pl.program_id(0)

    @pl.when(b < nu_ref[0])
    def _():
        @pl.when((b == 0) | (be_ref[b] != be_ref[jnp.maximum(b - 1, 0)]))
        def _():
            wg_sc[...] = wg_ref[0].astype(BF16)
            wu_sc[...] = wu_ref[0].astype(BF16)
            wd_sc[...] = wd_ref[0].astype(BF16)

        d = wg_sc.shape[0]
        xb = _unpack_bf16_pairs(_from_slabs(xs_ref, 0, MOE_BLOCK, d // 2 // LANES))
        gp = jnp.dot(xb, wg_sc[...], preferred_element_type=F32)
        up = jnp.dot(xb, wu_sc[...], preferred_element_type=F32)
        h = (_silu(gp) * up).astype(BF16)
        _to_slabs(y_ref, jnp.dot(h, wd_sc[...], preferred_element_type=F32))


def _moe_ffn(xs, block_expert, n_used, w_gate, w_up, w_down, layer):
    n_blocks = block_expert.shape[0]
    d, ff = w_gate.shape[2], w_gate.shape[3]
    last = lambda b, nu: jnp.minimum(b, nu[0] - 1)
    grid_spec = pltpu.PrefetchScalarGridSpec(
        num_scalar_prefetch=2,
        grid=(n_blocks,),
        in_specs=[
            pl.BlockSpec((MOE_BLOCK * (d // 2 // LANES), LANES), lambda b, be, nu: (last(b, nu), 0)),
            pl.BlockSpec((None, 1, d, ff), lambda b, be, nu: (layer, be[last(b, nu)], 0, 0)),
            pl.BlockSpec((None, 1, d, ff), lambda b, be, nu: (layer, be[last(b, nu)], 0, 0)),
            pl.BlockSpec((None, 1, ff, d), lambda b, be, nu: (layer, be[last(b, nu)], 0, 0)),
        ],
        out_specs=pl.BlockSpec((MOE_BLOCK * (d // LANES), LANES), lambda b, be, nu: (last(b, nu), 0)),
        scratch_shapes=[pltpu.VMEM((d, ff), BF16), pltpu.VMEM((d, ff), BF16), pltpu.VMEM((ff, d), BF16)],
    )
    return pl.pallas_call(
        _moe_ffn_kernel,
        grid_spec=grid_spec,
        out_shape=jax.ShapeDtypeStruct((n_blocks * MOE_BLOCK * (d // LANES), LANES), F32),
        compiler_params=_cparams("arbitrary"),
        name="moe_expert_ffn",
    )(block_expert, n_used, xs, w_gate, w_up, w_down)


def _moe_combine_kernel(dest_ref, y_hbm, gate_ref, x_ref, sg_ref, su_ref, sd_ref, g_ref, b_ref, out_ref, buf, sem):
    tm, d = x_ref.shape
    n_chunks = d // LANES

    def issue(i, c):
        for k in range(TOP_K):
            r = pl.multiple_of(dest_ref[k, i] * n_chunks, n_chunks)
            slot = pl.multiple_of((k * tm + i) * n_chunks, n_chunks)
            pltpu.make_async_copy(y_hbm.at[pl.ds(r, n_chunks)], buf.at[pl.ds(slot, n_chunks)],
                                  sem).start(priority=k % 2)
        return c

    lax.fori_loop(0, tm, issue, 0)
    x = x_ref[...]
    xb = x.astype(BF16)
    hs = _silu(jnp.dot(xb, sg_ref[...], preferred_element_type=F32)) * jnp.dot(xb, su_ref[...], preferred_element_type=F32)
    shared = jnp.dot(hs.astype(BF16), sd_ref[...], preferred_element_type=F32)
    pltpu.make_async_copy(y_hbm.at[pl.ds(0, TOP_K * tm * n_chunks)], buf, sem).wait()
    gates = gate_ref[...]
    routed = _from_slabs(buf, 0, tm, n_chunks) * gates[:, 0:1]
    for k in range(1, TOP_K):
        routed = routed + _from_slabs(buf, k * tm, tm, n_chunks) * gates[:, k:k + 1]
    y = DEEPNORM_ALPHA * x + (routed + shared)
    out_ref[...] = _layer_norm(y, g_ref[...], b_ref[...])


def _moe_combine(dest, y, gates_t, x2, sh_gate, sh_up, sh_down, g, b, tm=256):
    t, d = x2.shape
    row = lambda c: pl.BlockSpec((tm, c), lambda i: (i, 0))
    full = lambda a: pl.BlockSpec(a.shape, lambda i: (0,) * a.ndim)
    return pl.pallas_call(
        _moe_combine_kernel,
        grid=(t // tm,),
        in_specs=[pl.BlockSpec((TOP_K, tm), lambda i: (0, i), memory_space=pltpu.SMEM),
                  pl.BlockSpec(memory_space=pl.ANY),
                  row(TOP_K), row(d), full(sh_gate), full(sh_up), full(sh_down), full(g), full(b)],
        out_specs=row(d),
        out_shape=jax.ShapeDtypeStruct((t, d), F32),
        scratch_shapes=[pltpu.VMEM((TOP_K * tm * (d // LANES), LANES), F32), pltpu.SemaphoreType.DMA(())],
        compiler_params=_cparams("arbitrary"),
        name="moe_combine_ln",
    )(dest, y, gates_t, x2, sh_gate, sh_up, sh_down, g, b)


def _moe_segments(counts, n_tok):
    n_exp = counts.shape[0]
    n_blocks = -(-n_tok * TOP_K // MOE_BLOCK) + n_exp
    nblk = (counts.reshape(n_exp).astype(jnp.int32) + MOE_BLOCK - 1) // MOE_BLOCK
    blk_end = jnp.cumsum(nblk)
    seg_start = ((blk_end - nblk) * MOE_BLOCK).astype(F32).reshape(n_exp, 1)
    block_expert = jnp.sum((blk_end[None, :] <= jnp.arange(n_blocks)[:, None]).astype(jnp.int32), axis=1)
    block_expert = jnp.minimum(block_expert, n_exp - 1)
    n_used = blk_end[-1]
    zero_blocks = jnp.where(nblk > 0, blk_end - 1, -1).astype(jnp.int32)
    return seg_start, block_expert.astype(jnp.int32), n_used.astype(jnp.int32).reshape(1), zero_blocks, n_blocks


def _moe_layer(x2, router_w, router_b, w_gate, w_up, w_down, layer, sh_gate, sh_up, sh_down, g, b):
    t, d = x2.shape
    eidx, gates, rank, counts = _router(x2, router_w, router_b)
    seg_start, block_expert, n_used, zero_blocks, n_blocks = _moe_segments(counts, t)
    dest = _moe_dest(eidx, rank, seg_start)
    xs = _moe_dispatch(zero_blocks, dest, x2, n_blocks)
    y = _moe_ffn(xs, block_expert, n_used, w_gate, w_up, w_down, layer)
    return _moe_combine(dest, y, gates.T, x2, sh_gate.astype(BF16), sh_up.astype(BF16), sh_down.astype(BF16), g, b)


def _compress_kernel(x_ref, pa_ref, pb_ref, wa_ref, wb_ref, w2_ref, out_ref):
    x = x_ref[0]
    nc = x.shape[0]
    ha = jnp.dot((x + pa_ref[...]).astype(BF16), wa_ref[...], preferred_element_type=F32)
    hb = jnp.dot((x + pb_ref[...]).astype(BF16), wb_ref[...], preferred_element_type=F32)
    h = ha + pltpu.roll(hb, nc - 1, 0)
    h = jax.nn.gelu(h, approximate=True)
    out_ref[0] = jnp.dot(h.astype(BF16), w2_ref[...], preferred_element_type=F32).astype(out_ref.dtype)


def _compress(kc, pos, w1, w2, batch, seq):
    g = C_KV_HEADS
    nch = seq // CMP_STRIDE
    half = CMP_LEN // 2
    x = kc.reshape(batch, nch, half * g * HEAD_DIM)
    eye = jnp.eye(g, dtype=F32)
    w1r = w1.reshape(CMP_LEN, HEAD_DIM, CMP_HIDDEN)
    expand = lambda wpart: jnp.einsum('jdh,ge->jgdeh', wpart, eye).reshape(half * g * HEAD_DIM, g * CMP_HIDDEN)
    wa, wb = expand(w1r[:half]).astype(BF16), expand(w1r[half:]).astype(BF16)
    w2e = jnp.einsum('hd,ge->ghed', jnp.pad(w2, ((0, 0), (0, LANES - HEAD_DIM))), eye)
    w2e = w2e.reshape(g * CMP_HIDDEN, g * LANES).astype(BF16)
    tile_pos = lambda p: jnp.broadcast_to(p[:, None, :], (half, g, HEAD_DIM)).reshape(1, half * g * HEAD_DIM)
    pa, pb = tile_pos(pos[:half]), tile_pos(pos[half:])
    full = lambda a: pl.BlockSpec(a.shape, lambda i: (0,) * a.ndim)
    return pl.pallas_call(
        _compress_kernel,
        grid=(batch,),
        in_specs=[pl.BlockSpec((1, nch, x.shape[2]), lambda i: (i, 0, 0)), full(pa), full(pb), full(wa), full(wb),
                  full(w2e)],
        out_specs=pl.BlockSpec((1, nch, g * LANES), lambda i: (i, 0, 0)),
        out_shape=jax.ShapeDtypeStruct((batch, nch, g * LANES), BF16),
        compiler_params=_cparams("parallel"),
        name="nsa_compress",
    )(x, pa, pb, wa, wb, w2e)


def _nsa_cmp_kernel(q_ref, kc_ref, vc_ref, ovt_ref, o_ref, drop_ref, *, rep, n_sel, n_real):
    tq = q_ref.shape[1]
    nc = kc_ref.shape[1]
    nsb = ovt_ref.shape[0]
    t0 = pl.program_id(2) * tq
    scale = HEAD_DIM ** -0.5
    kc = kc_ref[0]
    vc = vc_ref[0]
    tpos = t0 + lax.broadcasted_iota(jnp.int32, (tq, nc), 0)
    cend = lax.broadcasted_iota(jnp.int32, (tq, nc), 1) * CMP_STRIDE + (CMP_LEN - 1)
    cmask = cend <= tpos
    psum = jnp.zeros((tq, nc), F32)
    outs = []
    for r in range(rep):
        q = q_ref[0, :, r * LANES:(r + 1) * LANES]
        sc = lax.dot_general(q, kc, _NT, preferred_element_type=F32) * scale
        sc = jnp.where(cmask, sc, NEG)
        m = jnp.max(sc, axis=-1, keepdims=True)
        ex = jnp.where(cmask, jnp.exp(sc - m), 0.0)
        den = jnp.sum(ex, axis=-1, keepdims=True)
        pc = ex / jnp.where(den > 0, den, 1.0)
        outs.append(jnp.dot(pc.astype(BF16), vc, preferred_element_type=F32))
        psum = psum + pc
    o_ref[0] = jnp.concatenate([_pack_head_pair(outs[r], outs[r + 1]) for r in range(0, rep, 2)], axis=1)
    phi, plo = _split_bf16(psum)
    ovt = ovt_ref[...]
    imp = (lax.dot_general(ovt, phi, _NT, preferred_element_type=F32)
           + lax.dot_general(ovt, plo, _NT, preferred_element_type=F32))
    jblk = lax.broadcasted_iota(jnp.int32, (nsb, tq), 0)
    cur = jnp.right_shift(t0 + lax.broadcasted_iota(jnp.int32, (nsb, tq), 1), SLC_SHIFT)
    forced = (jblk == 0) | (jblk == cur) | (jblk == cur - 1)
    score = jnp.where(jblk > cur, -1.0, jnp.where(forced, SELECT_FORCE, imp))
    rank = jnp.zeros((nsb, tq), F32)
    for k in range(n_real):
        rowk = score[k:k + 1, :]
        ge = jnp.where(rowk >= score, 1.0, 0.0)
        gt = jnp.where(rowk > score, 1.0, 0.0)
        rank = rank + jnp.where(jblk > k, ge, gt)
    drop = jnp.where(jblk <= cur, jnp.where(rank < n_sel, 0.0, 1.0), 1.0)
    drop = jnp.concatenate([drop, jnp.ones((LANES - nsb, tq), F32)], axis=0).T
    drop_ref[0, 0] = pltpu.roll(drop, HEAD_DIM, 1).astype(drop_ref.dtype)


def _nsa_cmp(q, kcmp, vcmp, batch, seq, q_off_blocks, tq=256):
    g = C_KV_HEADS
    rep = C_HEADS // g
    nc = kcmp.shape[1]
    nsb = seq // SLC_BLOCK
    n_sel = min(SLC_TOP_N, nsb)
    cs = np.arange(nc)[:, None] * CMP_STRIDE
    js = np.arange(nsb)[None, :] * SLC_BLOCK
    overlap = np.clip(np.minimum(cs + CMP_LEN, js + SLC_BLOCK) - np.maximum(cs, js), 0, None) / CMP_LEN
    overlap[(seq - CMP_LEN) // CMP_STRIDE + 1:] = 0.0
    nsb_pad = -(-nsb // 8) * 8
    ovt = jnp.asarray(np.pad(overlap.T, ((0, nsb_pad - nsb), (0, 0))), BF16)
    q3 = q.reshape(batch, seq, q.shape[1])
    n_real, nsb = nsb, nsb_pad
    assert nsb <= LANES - HEAD_DIM
    o, drop = pl.pallas_call(
        functools.partial(_nsa_cmp_kernel, rep=rep, n_sel=n_sel, n_real=n_real),
        grid=(batch, g, seq // tq),
        in_specs=[pl.BlockSpec((1, tq, rep * LANES), lambda b, gi, i: (b, i, q_off_blocks // rep + gi)),
                  pl.BlockSpec((1, nc, LANES), lambda b, gi, i: (b, 0, gi)),
                  pl.BlockSpec((1, nc, LANES), lambda b, gi, i: (b, 0, gi)),
                  pl.BlockSpec(ovt.shape, lambda b, gi, i: (0, 0))],
        out_specs=[pl.BlockSpec((1, tq, rep * HEAD_DIM), lambda b, gi, i: (b, i, gi)),
                   pl.BlockSpec((1, 1, tq, LANES), lambda b, gi, i: (b, gi, i, 0))],
        out_shape=[jax.ShapeDtypeStruct((batch, seq, C_HEADS * HEAD_DIM), F32),
                   jax.ShapeDtypeStruct((batch, g, seq, LANES), BF16)],
        compiler_params=_cparams("parallel", "parallel", "parallel"),
        name="nsa_compressed_select",
    )(q3, kcmp, vcmp, ovt)
    return o.reshape(batch * seq, C_HEADS * HEAD_DIM), drop


def _nsa_slc_kernel(q_ref, k_ref, vt_ref, drop_ref, o_ref, *, rep, kt):
    tq = q_ref.shape[1]
    t0 = pl.program_id(2) * tq
    n_kt = (t0 + tq + kt - 1) // kt
    upper = lax.broadcasted_iota(jnp.int32, (tq, LANES), 1) >= HEAD_DIM
    drop = drop_ref[0, 0]
    qs = [jnp.where(upper, drop, q_ref[0, :, r * LANES:(r + 1) * LANES] * (HEAD_DIM ** -0.5))
          for r in range(rep)]

    def tile(k0, carry, bias, size=kt):
        k = k_ref[0, pl.ds(k0, size), :]
        vt = vt_ref[:, pl.ds(k0, size)]
        new = []
        for qr, (m, l, acc) in zip(qs, carry):
            s = lax.dot_general(k, qr, _NT, preferred_element_type=F32)
            if bias is not None:
                s = s + bias
            m_new = jnp.maximum(m, jnp.max(s, axis=0, keepdims=True))
            e = jnp.exp(s - m_new)
            corr = jnp.exp(m - m_new)
            l = l * corr + jnp.sum(e, axis=0, keepdims=True)
            acc = acc * corr + jnp.dot(vt, e.astype(BF16), preferred_element_type=F32)
            new.append((m_new, l, acc))
        return tuple(new)

    init = tuple((jnp.full((1, tq), NEG, F32), jnp.zeros((1, tq), F32), jnp.zeros((LANES, tq), F32))
                 for _ in range(rep))
    carry = lax.fori_loop(0, n_kt - 1, lambda j, c: tile(pl.multiple_of(j * kt, kt), c, None), init)
    k_last = pl.multiple_of((n_kt - 1) * kt, kt)
    remaining = t0 + tq - k_last

    def finish(size):
        kpos = k_last + lax.broadcasted_iota(jnp.int32, (size, tq), 0)
        tpos = t0 + lax.broadcasted_iota(jnp.int32, (size, tq), 1)
        final = tile(k_last, carry, jnp.where(kpos <= tpos, 0.0, NEG), size)
        outs = [(acc / l).T for _, l, acc in final]
        o_ref[0] = jnp.concatenate([_pack_head_pair(outs[r], outs[r + 1]) for r in range(0, rep, 2)], axis=1)

    sizes = [s for s in (kt // 4, kt // 2) if s >= tq and s % LANES == 0] + [kt]
    lower = 0
    for size in sizes:
        pl.when((remaining > lower) & (remaining <= size))(functools.partial(finish, size))
        lower = size


def _nsa_slc(qkv, vt, drop, batch, seq, nblk, q_off, k_off, tq=128, kt=1024):
    g = C_KV_HEADS
    rep = C_HEADS // g
    kt = min(kt, seq)
    arr = qkv.reshape(batch, seq, nblk * LANES)
    out = pl.pallas_call(
        functools.partial(_nsa_slc_kernel, rep=rep, kt=kt),
        grid=(batch, g, seq // tq),
        in_specs=[pl.BlockSpec((1, tq, rep * LANES), lambda b, gi, i: (b, i, q_off // rep + gi)),
                  pl.BlockSpec((1, seq, LANES), lambda b, gi, i: (b, 0, k_off + gi)),
                  pl.BlockSpec((LANES, seq), lambda b, gi, i: (gi, b)),
                  pl.BlockSpec((1, 1, tq, LANES), lambda b, gi, i: (b, gi, i, 0))],
        out_specs=pl.BlockSpec((1, tq, rep * HEAD_DIM), lambda b, gi, i: (b, i, gi)),
        out_shape=jax.ShapeDtypeStruct((batch, seq, C_HEADS * HEAD_DIM), F32),
        compiler_params=_cparams("parallel", "parallel", "parallel"),
        name="nsa_selected",
    )(arr, arr, vt, drop)
    return out.reshape(batch * seq, C_HEADS * HEAD_DIM)


def _even_mixer_layer(x2, batch, seq, tabs, w_in, sinks, w_out, g, b):
    d = x2.shape[1]
    n_heads_in = 3 * A_HEADS + B_Q_HEADS + 2 * B_KV_HEADS
    w = _pad_heads_cols(w_in, n_heads_in).astype(BF16)
    rope = [1] * (2 * A_HEADS) + [0] * A_HEADS + [1] * B_Q_HEADS + [1] * B_KV_HEADS + [0] * B_KV_HEADS
    plan = [(0, c, rope[c]) for c in range(n_heads_in)]
    dilations = [dil for _, dil in A_PATTERNS]
    regroup = [dil for dil in dilations if dil > 1]
    n_a_blocks = 3 * A_HEADS
    qkv, *grouped = _proj(x2, w, tabs, plan, [n_heads_in * LANES], [BF16], dilations=regroup, n_dil=n_a_blocks)
    outs = []
    for window, dilation in A_PATTERNS:
        src, nblk = (qkv, n_heads_in) if dilation == 1 else (grouped[regroup.index(dilation)], n_a_blocks)
        outs.append(_band_attention(src, batch=batch, seq=seq, dilation=dilation, nblk=nblk, q_off=0,
                                    k_off=A_HEADS, v_off=2 * A_HEADS, n_q_heads=A_HEADS, rep=1,
                                    max_dist=window // dilation,
                                    heads=min(4, dilation)))
    qb_off = 3 * A_HEADS
    ob = _band_attention(qkv, batch=batch, seq=seq, dilation=1, nblk=n_heads_in, q_off=qb_off,
                         k_off=qb_off + B_Q_HEADS, v_off=qb_off + B_Q_HEADS + B_KV_HEADS, n_q_heads=B_Q_HEADS,
                         rep=B_Q_HEADS // B_KV_HEADS, max_dist=B_WINDOW - 1, sinks=sinks)
    w_o = _pad_heads_rows(w_out, A_HEADS + B_Q_HEADS).astype(BF16)
    return _outproj_even(outs, dilations, ob, x2, w_o, g.reshape(1, d), b.reshape(1, d))


def _odd_mixer_layer(x2, batch, seq, tabs, w_in, cmpk_pos, cmpk_w1, cmpk_w2, cmpv_pos, cmpv_w1, cmpv_w2, w_out, g, b):
    d = x2.shape[1]
    kvw = C_KV_HEADS * HEAD_DIM
    qw = C_HEADS * HEAD_DIM
    sizes = [qw] + [kvw] * 6 + [3 * C_HEADS]
    offs = np.concatenate([[0], np.cumsum(sizes)])
    wq, wkc, wvc, wks, wvs, wkw, wvw, wgt = [w_in[:, offs[i]:offs[i + 1]] for i in range(8)]
    ph = lambda wpart, n: _pad_heads_cols(wpart, n)
    w = jnp.concatenate([ph(wq, C_HEADS), ph(wks, C_KV_HEADS), ph(wkw, C_KV_HEADS), ph(wvw, C_KV_HEADS),
                         wkc, wvc, jnp.pad(wgt, ((0, 0), (0, LANES - 3 * C_HEADS)))], axis=1).astype(BF16)
    wvs_t = ph(wvs, C_KV_HEADS).T.astype(BF16)
    n16 = C_HEADS + 3 * C_KV_HEADS
    rope16 = [1] * C_HEADS + [TAG_BLOCK] * C_KV_HEADS + [1] * C_KV_HEADS + [0] * C_KV_HEADS
    n_kc = kvw // LANES
    plan = ([(0, c, rope16[c]) for c in range(n16)] + [(1, c, 2) for c in range(n_kc)]
            + [(2, c, 0) for c in range(n_kc)] + [(3, 0, 0)])
    qkv, kc, vc, gate, vs_t = _proj(x2, w, tabs, plan, [n16 * LANES, kvw, kvw, LANES], [BF16, F32, F32, F32],
                                    wt=wvs_t, seq=seq)
    kcmp = _compress(kc, cmpk_pos, cmpk_w1, cmpk_w2, batch, seq)
    vcmp = _compress(vc, cmpv_pos, cmpv_w1, cmpv_w2, batch, seq)
    o_cmp, drop = _nsa_cmp(qkv, kcmp, vcmp, batch, seq, 0)
    ks_off = C_HEADS
    o_slc = _nsa_slc(qkv, vs_t, drop, batch, seq, n16, 0, ks_off)
    kw_off = ks_off + C_KV_HEADS
    o_win = _band_attention(qkv, batch=batch, seq=seq, dilation=1, nblk=n16, q_off=0, k_off=kw_off,
                            v_off=kw_off + C_KV_HEADS, n_q_heads=C_HEADS, rep=C_HEADS // C_KV_HEADS,
                            max_dist=NSA_WINDOW - 1, heads=C_HEADS // C_KV_HEADS, pack_out=True)
    return _outproj_odd(o_cmp, o_slc, o_win, gate, x2, w_out.astype(BF16), g.reshape(1, d), b.reshape(1, d))


def kernel(x, positions, even_w_in, even_sinks, even_w_out, odd_w_in, odd_cmpk_pos, odd_cmpk_w1, odd_cmpk_w2, odd_cmpv_pos, odd_cmpv_w1, odd_cmpv_w2, odd_w_out, mix_ln_g, mix_ln_b, moe_router_w, moe_router_b, moe_w_gate, moe_w_up, moe_w_down, moe_sh_gate, moe_sh_up, moe_sh_down, ffn_ln_g, ffn_ln_b):
    batch, seq, d = x.shape
    x2 = x.reshape(batch * seq, d)
    tabs = _rope_tables(positions)
    depth = mix_ln_g.shape[0]
    for layer in range(depth):
        j = layer // 2
        if layer % 2 == 0:
            x2 = _even_mixer_layer(x2, batch, seq, tabs, even_w_in[j], even_sinks[j], even_w_out[j],
                                   mix_ln_g[layer], mix_ln_b[layer])
        else:
            x2 = _odd_mixer_layer(x2, batch, seq, tabs, odd_w_in[j], odd_cmpk_pos[j], odd_cmpk_w1[j], odd_cmpk_w2[j],
                                  odd_cmpv_pos[j], odd_cmpv_w1[j], odd_cmpv_w2[j], odd_w_out[j],
                                  mix_ln_g[layer], mix_ln_b[layer])
        x2 = _moe_layer(x2, moe_router_w[layer], moe_router_b[layer], moe_w_gate, moe_w_up, moe_w_down, layer,
                        moe_sh_gate[layer], moe_sh_up[layer], moe_sh_down[layer],
                        ffn_ln_g[layer].reshape(1, d), ffn_ln_b[layer].reshape(1, d))
    return x2.reshape(batch, seq, d)
```

```python
import functools

import numpy as np
import jax
import jax.numpy as jnp
from jax import lax
from jax.experimental import pallas as pl
from jax.experimental.pallas import tpu as pltpu

F32 = jnp.float32
BF16 = jnp.bfloat16

LANES = 128
HEAD_DIM = 64
ROT_DIM = HEAD_DIM // 4
ROT_HALF = ROT_DIM // 2
ROPE_THETA = 500000.0
QBLK = 128
A_HEADS = 8
A_PATTERNS = ((128, 1), (512, 4), (2048, 16))
B_Q_HEADS = 8
B_KV_HEADS = 2
B_WINDOW = 128
C_HEADS = 16
C_KV_HEADS = 4
CMP_LEN = 32
CMP_STRIDE = 16
CMP_HIDDEN = 2 * HEAD_DIM
SLC_BLOCK = 64
SLC_SHIFT = 6
SLC_TOP_N = 16
NSA_WINDOW = 512
SELECT_FORCE = 1.0e4
N_EXPERTS = 256
TOP_K = 8
N_GROUPS = 8
TOPK_GROUPS = 4
ROUTED_SCALE = 2.5
MOE_BLOCK = 256
DEPTH = 2
DEEPNORM_ALPHA = (2 * DEPTH) ** 0.25
LN_EPS = 1e-5
NEG = -1.0e30
VMEM_LIMIT = 56 * 1024 * 1024

_NT = (((1,), (1,)), ((), ()))


def _cparams(*sem):
    return pltpu.CompilerParams(dimension_semantics=sem, vmem_limit_bytes=VMEM_LIMIT)


def _split_bf16(a):
    hi = a.astype(BF16)
    lo = (a - hi.astype(F32)).astype(BF16)
    return hi, lo


def _layer_norm(y, g, b):
    mu = jnp.mean(y, axis=-1, keepdims=True)
    d = y - mu
    var = jnp.mean(d * d, axis=-1, keepdims=True)
    return d * lax.rsqrt(var + LN_EPS) * g + b


def _silu(a):
    return a * jax.nn.sigmoid(a)


TAG_BLOCK = 3


def _proj_kernel(*refs, plan, n_main, transposed, dilations, n_dil, seq):
    n_in = 4 if transposed else 3
    x_ref, w_ref, tab_ref = refs[:3]
    out_refs = refs[n_in:n_in + n_main]
    extra = list(refs[n_in + n_main:])
    x = x_ref[...].astype(BF16)
    tm = x.shape[0]
    if transposed:
        out_t = extra.pop(0)
        out_t[...] = lax.dot_general(refs[3][...], x, _NT, preferred_element_type=F32).astype(out_t.dtype)
    dil_refs = [extra.pop(0) for _ in dilations]
    stage = extra.pop(0) if dilations else None
    nblk = len(plan)
    for c0 in range(0, nblk, 2):
        nb = min(2, nblk - c0)
        acc = jnp.dot(x, w_ref[:, c0 * LANES:(c0 + nb) * LANES], preferred_element_type=F32)
        for j in range(nb):
            blk = acc[:, j * LANES:(j + 1) * LANES]
            dst, dblk, mode = plan[c0 + j]
            if mode == TAG_BLOCK:
                pos = lax.rem(pl.program_id(0) * tm, seq) + lax.broadcasted_iota(jnp.int32, (tm, LANES), 0)
                tag = lax.broadcasted_iota(jnp.int32, (tm, LANES), 1) - HEAD_DIM == jnp.right_shift(pos, SLC_SHIFT)
                mode = 1
            else:
                tag = None
            if mode:
                off = (mode - 1) * 3 * LANES
                cos = tab_ref[:, off:off + LANES]
                s_lo = tab_ref[:, off + LANES:off + 2 * LANES]
                s_hi = tab_ref[:, off + 2 * LANES:off + 3 * LANES]
                blk = (blk * cos + pltpu.roll(blk, LANES - ROT_HALF, 1) * s_lo
                       + pltpu.roll(blk, ROT_HALF, 1) * s_hi)
            if tag is not None:
                blk = jnp.where(tag, NEG, blk)
            o_ref = out_refs[dst]
            o_ref[:, dblk * LANES:(dblk + 1) * LANES] = blk.astype(o_ref.dtype)
            c = c0 + j
            if dilations and c < n_dil:
                stage[...] = blk
                for dil, d_ref in zip(dilations, dil_refs):
                    for r in range(dil):
                        col = (r * n_dil + c) * LANES
                        d_ref[:, col:col + LANES] = stage[pl.ds(r, tm // dil, stride=dil), :].astype(d_ref.dtype)


def _proj(x2, w, tabs, plan, out_cols, out_dtypes, wt=None, dilations=(), n_dil=0, seq=0, tm=512):
    t, d = x2.shape
    ncol = w.shape[1]
    out_shape = [jax.ShapeDtypeStruct((t, c), dt) for c, dt in zip(out_cols, out_dtypes)]
    once = pl.Buffered(1)
    in_specs = [pl.BlockSpec((tm, d), lambda i: (i, 0)),
                pl.BlockSpec((d, ncol), lambda i: (0, 0), pipeline_mode=once),
                pl.BlockSpec((tm, tabs.shape[1]), lambda i: (i, 0))]
    out_specs = [pl.BlockSpec((tm, c), lambda i: (i, 0)) for c in out_cols]
    args = [x2, w, tabs]
    if wt is not None:
        in_specs.append(pl.BlockSpec(wt.shape, lambda i: (0, 0), pipeline_mode=once))
        out_specs.append(pl.BlockSpec((wt.shape[0], tm), lambda i: (0, i)))
        out_shape.append(jax.ShapeDtypeStruct((wt.shape[0], t), BF16))
        args.append(wt)
    for dil in dilations:
        out_specs.append(pl.BlockSpec((tm // dil, dil * n_dil * LANES), lambda i: (i, 0)))
        out_shape.append(jax.ShapeDtypeStruct((t // dil, dil * n_dil * LANES), BF16))
    return pl.pallas_call(
        functools.partial(_proj_kernel, plan=tuple(plan), n_main=len(out_cols), transposed=wt is not None,
                          dilations=tuple(dilations), n_dil=n_dil, seq=seq),
        grid=(t // tm,),
        in_specs=in_specs,
        out_specs=out_specs,
        out_shape=out_shape,
        scratch_shapes=[pltpu.VMEM((tm, LANES), F32)] if dilations else [],
        compiler_params=_cparams("parallel"),
        name="proj_rope",
    )(*args)


def _rope_tables(positions):
    t = positions.size
    inv_freq = jnp.asarray(ROPE_THETA ** (-np.arange(0, ROT_DIM, 2) / ROT_DIM), F32)
    ang = positions.astype(F32).reshape(t, 1) * inv_freq
    trig = jnp.concatenate([jnp.cos(ang), jnp.sin(ang)], axis=1)
    place = np.zeros((2 * ROT_HALF, 6 * LANES), np.float32)
    const = np.zeros((6 * LANES,), np.float32)
    for base, heads in ((0, (0,)), (3 * LANES, (0, HEAD_DIM))):
        const[base:base + LANES] = 1.0
        for h in heads:
            for i in range(ROT_HALF):
                place[i, base + h + i] = 1.0
                place[i, base + h + ROT_HALF + i] = 1.0
                const[base + h + i] = const[base + h + ROT_HALF + i] = 0.0
                place[ROT_HALF + i, base + LANES + h + i] = -1.0
                place[ROT_HALF + i, base + 2 * LANES + h + ROT_HALF + i] = 1.0
    return jnp.dot(trig, jnp.asarray(place), precision=lax.Precision.HIGHEST) + jnp.asarray(const)


def _pad_heads_cols(w, n_heads):
    d = w.shape[0]
    w = w.reshape(d, n_heads, HEAD_DIM)
    return jnp.pad(w, ((0, 0), (0, 0), (0, LANES - HEAD_DIM))).reshape(d, n_heads * LANES)


def _pad_heads_rows(w, n_heads):
    d = w.shape[1]
    w = w.reshape(n_heads, HEAD_DIM, d)
    return jnp.pad(w, ((0, 0), (0, LANES - HEAD_DIM), (0, 0))).reshape(n_heads * LANES, d)


def _pack_head_pair(a, b):
    lane = lax.broadcasted_iota(jnp.int32, a.shape, 1)
    return jnp.where(lane < HEAD_DIM, a, pltpu.roll(b, HEAD_DIM, 1))


def _band_kernel(*refs, back, max_dist, length, qrows, heads, shared_kv, in_flight, has_sink, pack_out):
    if has_sink:
        sink_ref, q_ref, k_ref, v_ref, o_ref, bias_sc = refs
    else:
        q_ref, k_ref, v_ref, o_ref, bias_sc = refs
    nq = length // qrows
    window = qrows + back
    chains = [tuple(range(heads))] if shared_kv else [(h,) for h in range(heads)]
    stack = len(chains[0])
    lane = lax.broadcasted_iota(jnp.int32, (stack * qrows, LANES), 1)
    first_head = pl.program_id(2) * heads

    def band_bias(offset):
        row = lax.broadcasted_iota(jnp.int32, (qrows, window), 0)
        col = lax.broadcasted_iota(jnp.int32, (qrows, window), 1)
        dist = offset + row - col
        tile = jnp.where((dist >= 0) & (dist <= max_dist), 0.0, NEG)
        return jnp.concatenate([tile] * stack, axis=0) if stack > 1 else tile

    bias_sc[...] = band_bias(back)

    def block(hs, q0, k0, bias):
        kv = slice(0, LANES) if shared_kv else slice(hs[0] * LANES, (hs[0] + 1) * LANES)
        q = jnp.concatenate([q_ref[0, pl.ds(q0, qrows), h * LANES:(h + 1) * LANES] for h in hs], axis=0)
        q = q * (HEAD_DIM ** -0.5)
        k = k_ref[0, pl.ds(k0, window), kv]
        v = v_ref[0, pl.ds(k0, window), kv]
        s = lax.dot_general(q, k, _NT, preferred_element_type=F32) + bias
        m = jnp.max(s, axis=-1, keepdims=True)
        e = jnp.exp(s - m)
        den = jnp.sum(e, axis=-1, keepdims=True)
        if has_sink:
            sink = jnp.concatenate([jnp.full((qrows, 1), sink_ref[first_head + h], F32) for h in hs], axis=0)
            den = den + jnp.exp(sink - m)
        o = jnp.dot(e.astype(BF16), v, preferred_element_type=F32) / den
        lse = m + jnp.log(den)
        if pack_out:
            for i in range(0, len(hs), 2):
                pair = _pack_head_pair(o[i * qrows:(i + 1) * qrows], o[(i + 1) * qrows:(i + 2) * qrows])
                o_ref[0, pl.ds(q0, qrows), (hs[i] // 2) * LANES:(hs[i] // 2 + 1) * LANES] = pair
            return
        out = jnp.where(lane < HEAD_DIM, o, lse)
        for i, h in enumerate(hs):
            o_ref[0, pl.ds(q0, qrows), h * LANES:(h + 1) * LANES] = out[i * qrows:(i + 1) * qrows]

    n_clipped = min(-(-back // qrows), nq)
    for qi in range(n_clipped):
        bias = band_bias(qi * qrows)
        for hs in chains:
            block(hs, qi * qrows, 0, bias)

    steady = nq - n_clipped
    per_iter = max(1, min(in_flight // heads, steady))
    n_iter = steady // per_iter

    def body(it, carry):
        aligned = lambda v: v if isinstance(v, int) else pl.multiple_of(v, QBLK)
        for j in range(per_iter):
            q0 = aligned((n_clipped + it * per_iter + j) * qrows)
            for hs in chains:
                block(hs, q0, aligned(q0 - back), bias_sc[...])
        return carry

    if n_iter == 1:
        body(0, 0)
    elif n_iter:
        lax.fori_loop(0, n_iter, body, 0)
    for qi in range(n_clipped + n_iter * per_iter, nq):
        for hs in chains:
            block(hs, qi * qrows, qi * qrows - back, bias_sc[...])


def _band_attention(qkv, *, batch, seq, dilation, nblk, q_off, k_off, v_off, n_q_heads, rep, max_dist, sinks=None,
                    heads=1, in_flight=16, pack_out=False):
    length = seq // dilation
    back = min(-(-max_dist // QBLK) * QBLK, length - QBLK)
    qrows = QBLK
    shared_kv = rep > 1
    assert n_q_heads % heads == 0 and (rep == 1 or rep % heads == 0)
    assert not pack_out or (shared_kv and heads % 2 == 0)
    out_lanes = HEAD_DIM if pack_out else LANES
    kv_heads = 1 if shared_kv else heads
    arr = qkv.reshape(batch, length, dilation * nblk * LANES)
    assert q_off % heads == 0 and k_off % kv_heads == 0 and v_off % kv_heads == 0 and nblk % heads == 0
    qspec = pl.BlockSpec((1, length, heads * LANES),
                         lambda b, r, h: (b, 0, (r * nblk + q_off) // heads + h))
    kvspec = lambda off: pl.BlockSpec(
        (1, length, kv_heads * LANES),
        lambda b, r, h: (b, 0, (r * nblk + off) // kv_heads + (h * heads // rep if shared_kv else h)))
    in_specs = [qspec, kvspec(k_off), kvspec(v_off)]
    args = [arr, arr, arr]
    if sinks is not None:
        in_specs = [pl.BlockSpec(memory_space=pltpu.SMEM)] + in_specs
        args = [sinks.reshape(-1).astype(F32)] + args
    out = pl.pallas_call(
        functools.partial(_band_kernel, back=back, max_dist=max_dist, length=length, qrows=qrows, heads=heads,
                          shared_kv=shared_kv, in_flight=in_flight, has_sink=sinks is not None,
                          pack_out=pack_out),
        grid=(batch, dilation, n_q_heads // heads),
        in_specs=in_specs,
        out_specs=pl.BlockSpec((1, length, heads * out_lanes), lambda b, r, h: (b, 0, r * (n_q_heads // heads) + h)),
        out_shape=jax.ShapeDtypeStruct((batch, length, dilation * n_q_heads * out_lanes), F32),
        scratch_shapes=[pltpu.VMEM(((heads if shared_kv else 1) * qrows, qrows + back), F32)],
        compiler_params=_cparams("parallel", "parallel", "parallel"),
        name="band_attention",
    )(*args)
    return out.reshape(batch * length, dilation * n_q_heads * out_lanes)


def _outproj_even_kernel(o1_ref, o2_ref, o3_ref, ob_ref, x_ref, w_ref, g_ref, b_ref, out_ref, *nat_refs,
                         n_a, n_b, dilations):
    tm = x_ref.shape[0]
    lane = lax.broadcasted_iota(jnp.int32, (tm, LANES), 1)
    real = lane < HEAD_DIM
    pattern_refs = []
    nat_refs = list(nat_refs)
    for o_ref, dil in zip((o1_ref, o2_ref, o3_ref), dilations):
        if dil == 1:
            pattern_refs.append(o_ref)
            continue
        nat = nat_refs.pop(0)
        for r in range(dil):
            for h in range(n_a):
                col = (r * n_a + h) * LANES
                nat[h, pl.ds(r, tm // dil, stride=dil), :] = o_ref[:, col:col + LANES]
        pattern_refs.append(nat)
    parts = []
    for h in range(n_a):
        sl = slice(h * LANES, (h + 1) * LANES)
        outs = [r[:, sl] if r.ndim == 2 else r[h] for r in pattern_refs]
        lses = [jnp.where(real, pltpu.roll(a, HEAD_DIM, 1), a) for a in outs]
        m = jnp.maximum(jnp.maximum(lses[0], lses[1]), lses[2])
        es = [jnp.exp(l - m) for l in lses]
        num = es[0] * outs[0] + es[1] * outs[1] + es[2] * outs[2]
        den = es[0] + es[1] + es[2]
        parts.append(jnp.where(real, num / den, 0.0).astype(BF16))
    for h in range(n_b):
        parts.append(jnp.where(real, ob_ref[:, h * LANES:(h + 1) * LANES], 0.0).astype(BF16))
    a = jnp.concatenate(parts, axis=1)
    mixed = jnp.dot(a, w_ref[...], preferred_element_type=F32)
    y = DEEPNORM_ALPHA * x_ref[...] + mixed
    out_ref[...] = _layer_norm(y, g_ref[...], b_ref[...])


def _outproj_even(outs, dilations, ob, x2, w, g, b, tm=256):
    t, d = x2.shape
    n_a, n_b = outs[0].shape[1] // (dilations[0] * LANES), ob.shape[1] // LANES
    row = lambda c: pl.BlockSpec((tm, c), lambda i: (i, 0))
    grouped = lambda a, dil: pl.BlockSpec((tm // dil, a.shape[1]), lambda i: (i, 0))
    full = lambda a: pl.BlockSpec(a.shape, lambda i: (0,) * a.ndim)
    return pl.pallas_call(
        functools.partial(_outproj_even_kernel, n_a=n_a, n_b=n_b, dilations=tuple(dilations)),
        grid=(t // tm,),
        in_specs=[grouped(o, dil) for o, dil in zip(outs, dilations)] + [row(ob.shape[1]), row(d),
                                                                          full(w), full(g), full(b)],
        out_specs=row(d),
        out_shape=jax.ShapeDtypeStruct((t, d), F32),
        scratch_shapes=[pltpu.VMEM((n_a, tm, LANES), F32) for dil in dilations if dil > 1],
        compiler_params=_cparams("parallel"),
        name="outproj_even_ln",
    )(*outs, ob, x2, w, g, b)


def _outproj_odd_kernel(oc_ref, os_ref, ow_ref, gate_ref, e_ref, x_ref, w_ref, g_ref, b_ref, out_ref):
    gate = jax.nn.sigmoid(gate_ref[...])
    ghi, glo = _split_bf16(gate)
    acc = None
    for j, o_ref in enumerate((oc_ref, os_ref, ow_ref)):
        ej = e_ref[j]
        gfull = jnp.dot(ghi, ej, preferred_element_type=F32) + jnp.dot(glo, ej, preferred_element_type=F32)
        term = gfull * o_ref[...]
        acc = term if acc is None else acc + term
    mixed = jnp.dot(acc.astype(BF16), w_ref[...], preferred_element_type=F32)
    y = DEEPNORM_ALPHA * x_ref[...] + mixed
    out_ref[...] = _layer_norm(y, g_ref[...], b_ref[...])


def _gate_expanders(n_heads):
    e = np.zeros((3, LANES, n_heads * HEAD_DIM), np.float32)
    for j in range(3):
        for h in range(n_heads):
            e[j, 3 * h + j, h * HEAD_DIM:(h + 1) * HEAD_DIM] = 1.0
    return jnp.asarray(e, BF16)


def _outproj_odd(oc, osl, ow, gate, x2, w, g, b, tm=256):
    t, d = x2.shape
    n_heads = oc.shape[1] // HEAD_DIM
    e = _gate_expanders(n_heads)
    row = lambda c: pl.BlockSpec((tm, c), lambda i: (i, 0))
    full = lambda a: pl.BlockSpec(a.shape, lambda i: (0,) * a.ndim)
    return pl.pallas_call(
        _outproj_odd_kernel,
        grid=(t // tm,),
        in_specs=[row(oc.shape[1]), row(osl.shape[1]), row(ow.shape[1]), row(LANES), full(e), row(d),
                  full(w), full(g), full(b)],
        out_specs=row(d),
        out_shape=jax.ShapeDtypeStruct((t, d), F32),
        compiler_params=_cparams("parallel"),
        name="outproj_odd_ln",
    )(oc, osl, ow, gate, e, x2, w, g, b)


def _router_kernel(x_ref, whi_ref, wlo_ref, bias_ref, eidx_ref, gate_ref, rank_ref, cnt_ref):
    n_exp = whi_ref.shape[0]
    tm = x_ref.shape[0]
    per_group = n_exp // N_GROUPS
    xhi, xlo = _split_bf16(x_ref[...])
    whi, wlo = whi_ref[...], wlo_ref[...]
    dg = lambda a, b: lax.dot_general(a, b, _NT, preferred_element_type=F32)
    logits = dg(whi, xhi) + dg(whi, xlo) + dg(wlo, xhi)
    aff = jax.nn.sigmoid(logits)
    biased = aff + bias_ref[...]
    gio = lax.broadcasted_iota(jnp.int32, (per_group, tm), 0).astype(F32)
    blocks, scores = [], []
    for g in range(N_GROUPS):
        blk = biased[g * per_group:(g + 1) * per_group, :]
        m1 = jnp.max(blk, axis=0, keepdims=True)
        first = jnp.min(jnp.where(blk == m1, gio, float(per_group)), axis=0, keepdims=True)
        m2 = jnp.max(jnp.where(gio == first, -jnp.inf, blk), axis=0, keepdims=True)
        blocks.append(blk)
        scores.append(m1 + m2)
    masked = []
    for g in range(N_GROUPS):
        rank = jnp.zeros((1, tm), F32)
        for o in range(N_GROUPS):
            if o == g:
                continue
            beats = scores[o] >= scores[g] if o < g else scores[o] > scores[g]
            rank = rank + jnp.where(beats, 1.0, 0.0)
        masked.append(jnp.where(rank < TOPK_GROUPS, blocks[g], -jnp.inf))
    cur = jnp.concatenate(masked, axis=0)
    eio = lax.broadcasted_iota(jnp.int32, (n_exp, tm), 0).astype(F32)
    ids, gs = [], []
    for _ in range(TOP_K):
        m = jnp.max(cur, axis=0, keepdims=True)
        idx = jnp.min(jnp.where(cur == m, eio, float(n_exp)), axis=0, keepdims=True)
        hit = eio == idx
        gs.append(jnp.sum(jnp.where(hit, aff, 0.0), axis=0, keepdims=True))
        ids.append(idx)
        cur = jnp.where(hit, -jnp.inf, cur)
    gates = jnp.concatenate(gs, axis=0)
    gates = gates / jnp.sum(gates, axis=0, keepdims=True) * ROUTED_SCALE
    eidx_ref[...] = jnp.concatenate(ids, axis=0).astype(jnp.int32)
    gate_ref[...] = gates
    @pl.when(pl.program_id(0) == 0)
    def _():
        cnt_ref[...] = jnp.zeros(cnt_ref.shape, F32)

    onehot = jnp.zeros((n_exp, tm), F32)
    for idx in ids:
        onehot = onehot + jnp.where(eio == idx, 1.0, 0.0)
    earlier = jnp.where(lax.broadcasted_iota(jnp.int32, (tm, tm), 0) < lax.broadcasted_iota(jnp.int32, (tm, tm), 1),
                        1.0, 0.0).astype(BF16)
    before = cnt_ref[...] + jnp.dot(onehot.astype(BF16), earlier, preferred_element_type=F32)
    ranks = [jnp.sum(jnp.where(eio == idx, before, 0.0), axis=0, keepdims=True) for idx in ids]
    rank_ref[...] = jnp.concatenate(ranks, axis=0).astype(jnp.int32)
    cnt_ref[...] = cnt_ref[...] + jnp.sum(onehot, axis=1, keepdims=True)


def _router(x2, router_w, router_b, tm=256):
    t, d = x2.shape
    n_exp = router_w.shape[1]
    whi, wlo = _split_bf16(router_w.T)
    bias = router_b.reshape(n_exp, 1).astype(F32)
    full = lambda a: pl.BlockSpec(a.shape, lambda i: (0,) * a.ndim)
    per_tok = pl.BlockSpec((TOP_K, tm), lambda i: (0, i))
    return pl.pallas_call(
        _router_kernel,
        grid=(t // tm,),
        in_specs=[pl.BlockSpec((tm, d), lambda i: (i, 0)), full(whi), full(wlo), full(bias)],
        out_specs=[per_tok, per_tok, per_tok, pl.BlockSpec((n_exp, 1), lambda i: (0, 0))],
        out_shape=[jax.ShapeDtypeStruct((TOP_K, t), jnp.int32), jax.ShapeDtypeStruct((TOP_K, t), F32),
                   jax.ShapeDtypeStruct((TOP_K, t), jnp.int32), jax.ShapeDtypeStruct((n_exp, 1), F32)],
        compiler_params=_cparams("arbitrary"),
        name="moe_router",
    )(x2, whi, wlo, bias)


def _moe_dest_kernel(eidx_ref, rank_ref, start_ref, dest_ref):
    n_exp = start_ref.shape[0]
    tm = eidx_ref.shape[1]
    eio = lax.broadcasted_iota(jnp.int32, (n_exp, tm), 0)
    start = start_ref[...]
    rows = []
    for k in range(TOP_K):
        seg = jnp.sum(jnp.where(eio == eidx_ref[k:k + 1, :], start, 0.0), axis=0, keepdims=True)
        rows.append(seg.astype(jnp.int32) + rank_ref[k:k + 1, :])
    dest_ref[...] = jnp.concatenate(rows, axis=0)


def _moe_dest(eidx, rank, seg_start, tm=256):
    t = eidx.shape[1]
    per_tok = pl.BlockSpec((TOP_K, tm), lambda i: (0, i))
    return pl.pallas_call(
        _moe_dest_kernel,
        grid=(t // tm,),
        in_specs=[per_tok, per_tok, pl.BlockSpec(seg_start.shape, lambda i: (0, 0))],
        out_specs=per_tok,
        out_shape=jax.ShapeDtypeStruct((TOP_K, t), jnp.int32),
        compiler_params=_cparams("parallel"),
        name="moe_dest",
    )(eidx, rank, seg_start)


def _to_slabs(ref, value):
    rows, width = value.shape
    n_chunks = width // LANES
    for c in range(n_chunks):
        ref[pl.ds(c, rows, stride=n_chunks), :] = value[:, c * LANES:(c + 1) * LANES]


def _from_slabs(ref, first_row, rows, n_chunks):
    return jnp.concatenate([ref[pl.ds(first_row * n_chunks + c, rows, stride=n_chunks), :]
                            for c in range(n_chunks)], axis=1)


def _pack_bf16_pairs(x):
    half = x.shape[1] // 2
    lo = lax.bitcast_convert_type(x[:, :half].astype(BF16).astype(F32), jnp.uint32)
    hi = lax.bitcast_convert_type(x[:, half:].astype(BF16).astype(F32), jnp.uint32)
    return jnp.right_shift(lo, jnp.uint32(16)) | (hi & jnp.uint32(0xFFFF0000))


def _unpack_bf16_pairs(w):
    lo = lax.bitcast_convert_type(jnp.left_shift(w, jnp.uint32(16)), F32).astype(BF16)
    hi = lax.bitcast_convert_type(w & jnp.uint32(0xFFFF0000), F32).astype(BF16)
    return jnp.concatenate([lo, hi], axis=1)


def _moe_dispatch_kernel(zb_ref, dest_ref, x_ref, xs_out, buf, zbuf, sem, zsem):
    tm = x_ref.shape[0]
    n_chunks = buf.shape[0] // tm

    @pl.when(pl.program_id(0) == 0)
    def _():
        rows = zbuf.shape[0]
        zbuf[...] = jnp.zeros(zbuf.shape, zbuf.dtype)

        def zero_copy(e):
            first = pl.multiple_of(zb_ref[e] * rows, rows)
            return pltpu.make_async_copy(zbuf, xs_out.at[pl.ds(first, rows)], zsem)

        def start(e, c):
            @pl.when(zb_ref[e] >= 0)
            def _():
                zero_copy(e).start()
            return c

        def wait(e, c):
            @pl.when(zb_ref[e] >= 0)
            def _():
                zero_copy(e).wait()
            return c

        lax.fori_loop(0, zb_ref.shape[0], start, 0)
        lax.fori_loop(0, zb_ref.shape[0], wait, 0)

    _to_slabs(buf, _pack_bf16_pairs(x_ref[...]))

    def issue(i, c):
        src = buf.at[pl.ds(pl.multiple_of(i * n_chunks, n_chunks), n_chunks)]
        for k in range(TOP_K):
            r = pl.multiple_of(dest_ref[k, i] * n_chunks, n_chunks)
            pltpu.make_async_copy(src, xs_out.at[pl.ds(r, n_chunks)], sem).start(priority=k % 2)
        return c

    lax.fori_loop(0, tm, issue, 0)
    for _ in range(TOP_K):
        pltpu.make_async_copy(buf, xs_out.at[pl.ds(0, tm * n_chunks)], sem).wait()


def _moe_dispatch(zero_blocks, dest, x2, n_blocks, tm=256):
    t, d = x2.shape
    n_chunks = d // 2 // LANES
    grid_spec = pltpu.PrefetchScalarGridSpec(
        num_scalar_prefetch=1,
        grid=(t // tm,),
        in_specs=[pl.BlockSpec((TOP_K, tm), lambda i, zb: (0, i), memory_space=pltpu.SMEM),
                  pl.BlockSpec((tm, d), lambda i, zb: (i, 0))],
        out_specs=pl.BlockSpec(memory_space=pl.ANY),
        scratch_shapes=[pltpu.VMEM((tm * n_chunks, LANES), jnp.uint32),
                        pltpu.VMEM((MOE_BLOCK * n_chunks, LANES), jnp.uint32),
                        pltpu.SemaphoreType.DMA(()), pltpu.SemaphoreType.DMA(())],
    )
    return pl.pallas_call(
        _moe_dispatch_kernel,
        grid_spec=grid_spec,
        out_shape=jax.ShapeDtypeStruct((n_blocks * MOE_BLOCK * n_chunks, LANES), jnp.uint32),
        compiler_params=_cparams("arbitrary"),
        name="moe_dispatch",
    )(zero_blocks, dest, x2)


def _moe_ffn_kernel(nblk_ref, first_ref, wg_ref, wu_ref, wd_ref, xs_hbm, y_hbm,
                    wg_sc, wu_sc, wd_sc, xbuf, ybuf, in_sem, out_sem):
    e = pl.program_id(0)
    n = nblk_ref[e]
    d = wg_sc.shape[0]
    in_rows, out_rows = xbuf.shape[1], ybuf.shape[1]

    def fetch(j, slot):
        rows = pl.ds(pl.multiple_of((first_ref[e] + j) * in_rows, in_rows), in_rows)
        return pltpu.make_async_copy(xs_hbm.at[rows], xbuf.at[slot], in_sem.at[slot])

    def write_back(j, slot):
        rows = pl.ds(pl.multiple_of((first_ref[e] + j) * out_rows, out_rows), out_rows)
        return pltpu.make_async_copy(ybuf.at[slot], y_hbm.at[rows], out_sem.at[slot])

    @pl.when(n > 0)
    def _():
        fetch(0, 0).start()
        wg_sc[...] = wg_ref[0].astype(BF16)
        wu_sc[...] = wu_ref[0].astype(BF16)
        wd_sc[...] = wd_ref[0].astype(BF16)

        def block(j, c):
            slot = lax.rem(j, 2)
            fetch(j, slot).wait()

            @pl.when(j + 1 < n)
            def _():
                fetch(j + 1, 1 - slot).start()

            @pl.when(j >= 2)
            def _():
                write_back(j - 2, slot).wait()

            xb = _unpack_bf16_pairs(_from_slabs(xbuf.at[slot], 0, MOE_BLOCK, d // 2 // LANES))
            gp = jnp.dot(xb, wg_sc[...], preferred_element_type=F32)
            up = jnp.dot(xb, wu_sc[...], preferred_element_type=F32)
            h = (_silu(gp) * up).astype(BF16)
            _to_slabs(ybuf.at[slot], jnp.dot(h, wd_sc[...], preferred_element_type=F32))
            write_back(j, slot).start()
            return c

        lax.fori_loop(0, n, block, 0)

        @pl.when(n >= 2)
        def _():
            write_back(n - 2, lax.rem(n, 2)).wait()

        write_back(n - 1, lax.rem(n - 1, 2)).wait()


def _moe_ffn(xs, nblk, first_blk, n_blocks, w_gate, w_up, w_down, layer):
    n_exp, d, ff = w_gate.shape[1], w_gate.shape[2], w_gate.shape[3]
    in_rows, out_rows = MOE_BLOCK * (d // 2 // LANES), MOE_BLOCK * (d // LANES)
    grid_spec = pltpu.PrefetchScalarGridSpec(
        num_scalar_prefetch=2,
        grid=(n_exp,),
        in_specs=[
            pl.BlockSpec((None, 1, d, ff), lambda e, nb, fb: (layer, e, 0, 0)),
            pl.BlockSpec((None, 1, d, ff), lambda e, nb, fb: (layer, e, 0, 0)),
            pl.BlockSpec((None, 1, ff, d), lambda e, nb, fb: (layer, e, 0, 0)),
            pl.BlockSpec(memory_space=pl.ANY),
        ],
        out_specs=pl.BlockSpec(memory_space=pl.ANY),
        scratch_shapes=[pltpu.VMEM((d, ff), BF16), pltpu.VMEM((d, ff), BF16), pltpu.VMEM((ff, d), BF16),
                        pltpu.VMEM((2, in_rows, LANES), jnp.uint32), pltpu.VMEM((2, out_rows, LANES), F32),
                        pltpu.SemaphoreType.DMA((2,)), pltpu.SemaphoreType.DMA((2,))],
    )
    return pl.pallas_call(
        _moe_ffn_kernel,
        grid_spec=grid_spec,
        out_shape=jax.ShapeDtypeStruct((n_blocks * out_rows, LANES), F32),
        compiler_params=_cparams("arbitrary"),
        name="moe_expert_ffn",
    )(nblk, first_blk, w_gate, w_up, w_down, xs)


def _moe_combine_kernel(dest_ref, y_hbm, gate_ref, x_ref, sg_ref, su_ref, sd_ref, g_ref, b_ref, out_ref, buf, sem):
    tm, d = x_ref.shape
    n_chunks = d // LANES

    def issue(i, c):
        for k in range(TOP_K):
            r = pl.multiple_of(dest_ref[k, i] * n_chunks, n_chunks)
            slot = pl.multiple_of((k * tm + i) * n_chunks, n_chunks)
            pltpu.make_async_copy(y_hbm.at[pl.ds(r, n_chunks)], buf.at[pl.ds(slot, n_chunks)],
                                  sem).start(priority=k % 2)
        return c

    lax.fori_loop(0, tm, issue, 0)
    x = x_ref[...]
    xb = x.astype(BF16)
    hs = _silu(jnp.dot(xb, sg_ref[...], preferred_element_type=F32)) * jnp.dot(xb, su_ref[...], preferred_element_type=F32)
    shared = jnp.dot(hs.astype(BF16), sd_ref[...], preferred_element_type=F32)
    pltpu.make_async_copy(y_hbm.at[pl.ds(0, TOP_K * tm * n_chunks)], buf, sem).wait()
    gates = gate_ref[...]
    routed = _from_slabs(buf, 0, tm, n_chunks) * gates[:, 0:1]
    for k in range(1, TOP_K):
        routed = routed + _from_slabs(buf, k * tm, tm, n_chunks) * gates[:, k:k + 1]
    y = DEEPNORM_ALPHA * x + (routed + shared)
    out_ref[...] = _layer_norm(y, g_ref[...], b_ref[...])


def _moe_combine(dest, y, gates_t, x2, sh_gate, sh_up, sh_down, g, b, tm=256):
    t, d = x2.shape
    row = lambda c: pl.BlockSpec((tm, c), lambda i: (i, 0))
    full = lambda a: pl.BlockSpec(a.shape, lambda i: (0,) * a.ndim)
    return pl.pallas_call(
        _moe_combine_kernel,
        grid=(t // tm,),
        in_specs=[pl.BlockSpec((TOP_K, tm), lambda i: (0, i), memory_space=pltpu.SMEM),
                  pl.BlockSpec(memory_space=pl.ANY),
                  row(TOP_K), row(d), full(sh_gate), full(sh_up), full(sh_down), full(g), full(b)],
        out_specs=row(d),
        out_shape=jax.ShapeDtypeStruct((t, d), F32),
        scratch_shapes=[pltpu.VMEM((TOP_K * tm * (d // LANES), LANES), F32), pltpu.SemaphoreType.DMA(())],
        compiler_params=_cparams("arbitrary"),
        name="moe_combine_ln",
    )(dest, y, gates_t, x2, sh_gate, sh_up, sh_down, g, b)


def _moe_segments(counts, n_tok):
    n_exp = counts.shape[0]
    n_blocks = -(-n_tok * TOP_K // MOE_BLOCK) + n_exp
    nblk = (counts.reshape(n_exp).astype(jnp.int32) + MOE_BLOCK - 1) // MOE_BLOCK
    blk_end = jnp.cumsum(nblk)
    first_blk = (blk_end - nblk).astype(jnp.int32)
    seg_start = (first_blk * MOE_BLOCK).astype(F32).reshape(n_exp, 1)
    zero_blocks = jnp.where(nblk > 0, blk_end - 1, -1).astype(jnp.int32)
    return seg_start, nblk, first_blk, zero_blocks, n_blocks


def _moe_layer(x2, router_w, router_b, w_gate, w_up, w_down, layer, sh_gate, sh_up, sh_down, g, b):
    t, d = x2.shape
    eidx, gates, rank, counts = _router(x2, router_w, router_b)
    seg_start, nblk, first_blk, zero_blocks, n_blocks = _moe_segments(counts, t)
    dest = _moe_dest(eidx, rank, seg_start)
    xs = _moe_dispatch(zero_blocks, dest, x2, n_blocks)
    y = _moe_ffn(xs, nblk, first_blk, n_blocks, w_gate, w_up, w_down, layer)
    return _moe_combine(dest, y, gates.T, x2, sh_gate.astype(BF16), sh_up.astype(BF16), sh_down.astype(BF16), g, b)


def _compress_kernel(x_ref, pa_ref, pb_ref, wa_ref, wb_ref, w2_ref, out_ref):
    x = x_ref[0]
    nc = x.shape[0]
    ha = jnp.dot((x + pa_ref[...]).astype(BF16), wa_ref[...], preferred_element_type=F32)
    hb = jnp.dot((x + pb_ref[...]).astype(BF16), wb_ref[...], preferred_element_type=F32)
    h = ha + pltpu.roll(hb, nc - 1, 0)
    h = jax.nn.gelu(h, approximate=True)
    out_ref[0] = jnp.dot(h.astype(BF16), w2_ref[...], preferred_element_type=F32).astype(out_ref.dtype)


def _compress(kc, pos, w1, w2, batch, seq):
    g = C_KV_HEADS
    nch = seq // CMP_STRIDE
    half = CMP_LEN // 2
    x = kc.reshape(batch, nch, half * g * HEAD_DIM)
    eye = jnp.eye(g, dtype=F32)
    w1r = w1.reshape(CMP_LEN, HEAD_DIM, CMP_HIDDEN)
    expand = lambda wpart: jnp.einsum('jdh,ge->jgdeh', wpart, eye).reshape(half * g * HEAD_DIM, g * CMP_HIDDEN)
    wa, wb = expand(w1r[:half]).astype(BF16), expand(w1r[half:]).astype(BF16)
    w2e = jnp.einsum('hd,ge->ghed', jnp.pad(w2, ((0, 0), (0, LANES - HEAD_DIM))), eye)
    w2e = w2e.reshape(g * CMP_HIDDEN, g * LANES).astype(BF16)
    tile_pos = lambda p: jnp.broadcast_to(p[:, None, :], (half, g, HEAD_DIM)).reshape(1, half * g * HEAD_DIM)
    pa, pb = tile_pos(pos[:half]), tile_pos(pos[half:])
    full = lambda a: pl.BlockSpec(a.shape, lambda i: (0,) * a.ndim)
    return pl.pallas_call(
        _compress_kernel,
        grid=(batch,),
        in_specs=[pl.BlockSpec((1, nch, x.shape[2]), lambda i: (i, 0, 0)), full(pa), full(pb), full(wa), full(wb),
                  full(w2e)],
        out_specs=pl.BlockSpec((1, nch, g * LANES), lambda i: (i, 0, 0)),
        out_shape=jax.ShapeDtypeStruct((batch, nch, g * LANES), BF16),
        compiler_params=_cparams("parallel"),
        name="nsa_compress",
    )(x, pa, pb, wa, wb, w2e)


def _nsa_cmp_kernel(q_ref, kc_ref, vc_ref, ovt_ref, o_ref, drop_ref, *, rep, n_sel, n_real):
    tq = q_ref.shape[1]
    nc = kc_ref.shape[1]
    nsb = ovt_ref.shape[0]
    t0 = pl.program_id(2) * tq
    scale = HEAD_DIM ** -0.5
    kc = kc_ref[0]
    vc = vc_ref[0]
    tpos = t0 + lax.broadcasted_iota(jnp.int32, (tq, nc), 0)
    cend = lax.broadcasted_iota(jnp.int32, (tq, nc), 1) * CMP_STRIDE + (CMP_LEN - 1)
    cmask = cend <= tpos
    psum = jnp.zeros((tq, nc), F32)
    outs = []
    for r in range(rep):
        q = q_ref[0, :, r * LANES:(r + 1) * LANES]
        sc = lax.dot_general(q, kc, _NT, preferred_element_type=F32) * scale
        sc = jnp.where(cmask, sc, NEG)
        m = jnp.max(sc, axis=-1, keepdims=True)
        ex = jnp.where(cmask, jnp.exp(sc - m), 0.0)
        den = jnp.sum(ex, axis=-1, keepdims=True)
        pc = ex / jnp.where(den > 0, den, 1.0)
        outs.append(jnp.dot(pc.astype(BF16), vc, preferred_element_type=F32))
        psum = psum + pc
    o_ref[0] = jnp.concatenate([_pack_head_pair(outs[r], outs[r + 1]) for r in range(0, rep, 2)], axis=1)
    phi, plo = _split_bf16(psum)
    ovt = ovt_ref[...]
    imp = (lax.dot_general(ovt, phi, _NT, preferred_element_type=F32)
           + lax.dot_general(ovt, plo, _NT, preferred_element_type=F32))
    jblk = lax.broadcasted_iota(jnp.int32, (nsb, tq), 0)
    cur = jnp.right_shift(t0 + lax.broadcasted_iota(jnp.int32, (nsb, tq), 1), SLC_SHIFT)
    forced = (jblk == 0) | (jblk == cur) | (jblk == cur - 1)
    score = jnp.where(jblk > cur, -1.0, jnp.where(forced, SELECT_FORCE, imp))
    rank = jnp.zeros((nsb, tq), F32)
    for k in range(n_real):
        rowk = score[k:k + 1, :]
        ge = jnp.where(rowk >= score, 1.0, 0.0)
        gt = jnp.where(rowk > score, 1.0, 0.0)
        rank = rank + jnp.where(jblk > k, ge, gt)
    drop = jnp.where(jblk <= cur, jnp.where(rank < n_sel, 0.0, 1.0), 1.0)
    drop = jnp.concatenate([drop, jnp.ones((LANES - nsb, tq), F32)], axis=0).T
    drop_ref[0, 0] = pltpu.roll(drop, HEAD_DIM, 1).astype(drop_ref.dtype)


def _nsa_cmp(q, kcmp, vcmp, batch, seq, q_off_blocks, tq=256):
    g = C_KV_HEADS
    rep = C_HEADS // g
    nc = kcmp.shape[1]
    nsb = seq // SLC_BLOCK
    n_sel = min(SLC_TOP_N, nsb)
    cs = np.arange(nc)[:, None] * CMP_STRIDE
    js = np.arange(nsb)[None, :] * SLC_BLOCK
    overlap = np.clip(np.minimum(cs + CMP_LEN, js + SLC_BLOCK) - np.maximum(cs, js), 0, None) / CMP_LEN
    overlap[(seq - CMP_LEN) // CMP_STRIDE + 1:] = 0.0
    nsb_pad = -(-nsb // 8) * 8
    ovt = jnp.asarray(np.pad(overlap.T, ((0, nsb_pad - nsb), (0, 0))), BF16)
    q3 = q.reshape(batch, seq, q.shape[1])
    n_real, nsb = nsb, nsb_pad
    assert nsb <= LANES - HEAD_DIM
    o, drop = pl.pallas_call(
        functools.partial(_nsa_cmp_kernel, rep=rep, n_sel=n_sel, n_real=n_real),
        grid=(batch, g, seq // tq),
        in_specs=[pl.BlockSpec((1, tq, rep * LANES), lambda b, gi, i: (b, i, q_off_blocks // rep + gi)),
                  pl.BlockSpec((1, nc, LANES), lambda b, gi, i: (b, 0, gi)),
                  pl.BlockSpec((1, nc, LANES), lambda b, gi, i: (b, 0, gi)),
                  pl.BlockSpec(ovt.shape, lambda b, gi, i: (0, 0))],
        out_specs=[pl.BlockSpec((1, tq, rep * HEAD_DIM), lambda b, gi, i: (b, i, gi)),
                   pl.BlockSpec((1, 1, tq, LANES), lambda b, gi, i: (b, gi, i, 0))],
        out_shape=[jax.ShapeDtypeStruct((batch, seq, C_HEADS * HEAD_DIM), F32),
                   jax.ShapeDtypeStruct((batch, g, seq, LANES), BF16)],
        compiler_params=_cparams("parallel", "parallel", "parallel"),
        name="nsa_compressed_select",
    )(q3, kcmp, vcmp, ovt)
    return o.reshape(batch * seq, C_HEADS * HEAD_DIM), drop


def _nsa_slc_kernel(q_ref, k_ref, vt_ref, drop_ref, o_ref, *, rep, kt):
    tq = q_ref.shape[1]
    t0 = pl.program_id(2) * tq
    n_kt = (t0 + tq + kt - 1) // kt
    upper = lax.broadcasted_iota(jnp.int32, (tq, LANES), 1) >= HEAD_DIM
    drop = drop_ref[0, 0]
    qs = [jnp.where(upper, drop, q_ref[0, :, r * LANES:(r + 1) * LANES] * (HEAD_DIM ** -0.5))
          for r in range(rep)]

    def tile(k0, carry, bias, size=kt):
        k = k_ref[0, pl.ds(k0, size), :]
        vt = vt_ref[:, pl.ds(k0, size)]
        new = []
        for qr, (m, l, acc) in zip(qs, carry):
            s = lax.dot_general(k, qr, _NT, preferred_element_type=F32)
            if bias is not None:
                s = s + bias
            m_new = jnp.maximum(m, jnp.max(s, axis=0, keepdims=True))
            e = jnp.exp(s - m_new)
            corr = jnp.exp(m - m_new)
            l = l * corr + jnp.sum(e, axis=0, keepdims=True)
            acc = acc * corr + jnp.dot(vt, e.astype(BF16), preferred_element_type=F32)
            new.append((m_new, l, acc))
        return tuple(new)

    init = tuple((jnp.full((1, tq), NEG, F32), jnp.zeros((1, tq), F32), jnp.zeros((LANES, tq), F32))
                 for _ in range(rep))
    carry = lax.fori_loop(0, n_kt - 1, lambda j, c: tile(pl.multiple_of(j * kt, kt), c, None), init)
    k_last = pl.multiple_of((n_kt - 1) * kt, kt)
    remaining = t0 + tq - k_last

    def finish(size):
        kpos = k_last + lax.broadcasted_iota(jnp.int32, (size, tq), 0)
        tpos = t0 + lax.broadcasted_iota(jnp.int32, (size, tq), 1)
        final = tile(k_last, carry, jnp.where(kpos <= tpos, 0.0, NEG), size)
        outs = [(acc / l).T for _, l, acc in final]
        o_ref[0] = jnp.concatenate([_pack_head_pair(outs[r], outs[r + 1]) for r in range(0, rep, 2)], axis=1)

    sizes = [s for s in (kt // 4, kt // 2) if s >= tq and s % LANES == 0] + [kt]
    lower = 0
    for size in sizes:
        pl.when((remaining > lower) & (remaining <= size))(functools.partial(finish, size))
        lower = size


def _nsa_slc(qkv, vt, drop, batch, seq, nblk, q_off, k_off, tq=128, kt=1024):
    g = C_KV_HEADS
    rep = C_HEADS // g
    kt = min(kt, seq)
    arr = qkv.reshape(batch, seq, nblk * LANES)
    out = pl.pallas_call(
        functools.partial(_nsa_slc_kernel, rep=rep, kt=kt),
        grid=(batch, g, seq // tq),
        in_specs=[pl.BlockSpec((1, tq, rep * LANES), lambda b, gi, i: (b, i, q_off // rep + gi)),
                  pl.BlockSpec((1, seq, LANES), lambda b, gi, i: (b, 0, k_off + gi)),
                  pl.BlockSpec((LANES, seq), lambda b, gi, i: (gi, b)),
                  pl.BlockSpec((1, 1, tq, LANES), lambda b, gi, i: (b, gi, i, 0))],
        out_specs=pl.BlockSpec((1, tq, rep * HEAD_DIM), lambda b, gi, i: (b, i, gi)),
        out_shape=jax.ShapeDtypeStruct((batch, seq, C_HEADS * HEAD_DIM), F32),
        compiler_params=_cparams("parallel", "parallel", "parallel"),
        name="nsa_selected",
    )(arr, arr, vt, drop)
    return out.reshape(batch * seq, C_HEADS * HEAD_DIM)


def _even_mixer_layer(x2, batch, seq, tabs, w_in, sinks, w_out, g, b):
    d = x2.shape[1]
    n_heads_in = 3 * A_HEADS + B_Q_HEADS + 2 * B_KV_HEADS
    w = _pad_heads_cols(w_in, n_heads_in).astype(BF16)
    rope = [1] * (2 * A_HEADS) + [0] * A_HEADS + [1] * B_Q_HEADS + [1] * B_KV_HEADS + [0] * B_KV_HEADS
    plan = [(0, c, rope[c]) for c in range(n_heads_in)]
    dilations = [dil for _, dil in A_PATTERNS]
    regroup = [dil for dil in dilations if dil > 1]
    n_a_blocks = 3 * A_HEADS
    qkv, *grouped = _proj(x2, w, tabs, plan, [n_heads_in * LANES], [BF16], dilations=regroup, n_dil=n_a_blocks)
    outs = []
    for window, dilation in A_PATTERNS:
        src, nblk = (qkv, n_heads_in) if dilation == 1 else (grouped[regroup.index(dilation)], n_a_blocks)
        outs.append(_band_attention(src, batch=batch, seq=seq, dilation=dilation, nblk=nblk, q_off=0,
                                    k_off=A_HEADS, v_off=2 * A_HEADS, n_q_heads=A_HEADS, rep=1,
                                    max_dist=window // dilation,
                                    heads=min(4, dilation)))
    qb_off = 3 * A_HEADS
    ob = _band_attention(qkv, batch=batch, seq=seq, dilation=1, nblk=n_heads_in, q_off=qb_off,
                         k_off=qb_off + B_Q_HEADS, v_off=qb_off + B_Q_HEADS + B_KV_HEADS, n_q_heads=B_Q_HEADS,
                         rep=B_Q_HEADS // B_KV_HEADS, max_dist=B_WINDOW - 1, sinks=sinks)
    w_o = _pad_heads_rows(w_out, A_HEADS + B_Q_HEADS).astype(BF16)
    return _outproj_even(outs, dilations, ob, x2, w_o, g.reshape(1, d), b.reshape(1, d))


def _odd_mixer_layer(x2, batch, seq, tabs, w_in, cmpk_pos, cmpk_w1, cmpk_w2, cmpv_pos, cmpv_w1, cmpv_w2, w_out, g, b):
    d = x2.shape[1]
    kvw = C_KV_HEADS * HEAD_DIM
    qw = C_HEADS * HEAD_DIM
    sizes = [qw] + [kvw] * 6 + [3 * C_HEADS]
    offs = np.concatenate([[0], np.cumsum(sizes)])
    wq, wkc, wvc, wks, wvs, wkw, wvw, wgt = [w_in[:, offs[i]:offs[i + 1]] for i in range(8)]
    ph = lambda wpart, n: _pad_heads_cols(wpart, n)
    w = jnp.concatenate([ph(wq, C_HEADS), ph(wks, C_KV_HEADS), ph(wkw, C_KV_HEADS), ph(wvw, C_KV_HEADS),
                         wkc, wvc, jnp.pad(wgt, ((0, 0), (0, LANES - 3 * C_HEADS)))], axis=1).astype(BF16)
    wvs_t = ph(wvs, C_KV_HEADS).T.astype(BF16)
    n16 = C_HEADS + 3 * C_KV_HEADS
    rope16 = [1] * C_HEADS + [TAG_BLOCK] * C_KV_HEADS + [1] * C_KV_HEADS + [0] * C_KV_HEADS
    n_kc = kvw // LANES
    plan = ([(0, c, rope16[c]) for c in range(n16)] + [(1, c, 2) for c in range(n_kc)]
            + [(2, c, 0) for c in range(n_kc)] + [(3, 0, 0)])
    qkv, kc, vc, gate, vs_t = _proj(x2, w, tabs, plan, [n16 * LANES, kvw, kvw, LANES], [BF16, F32, F32, F32],
                                    wt=wvs_t, seq=seq)
    kcmp = _compress(kc, cmpk_pos, cmpk_w1, cmpk_w2, batch, seq)
    vcmp = _compress(vc, cmpv_pos, cmpv_w1, cmpv_w2, batch, seq)
    o_cmp, drop = _nsa_cmp(qkv, kcmp, vcmp, batch, seq, 0)
    ks_off = C_HEADS
    o_slc = _nsa_slc(qkv, vs_t, drop, batch, seq, n16, 0, ks_off)
    kw_off = ks_off + C_KV_HEADS
    o_win = _band_attention(qkv, batch=batch, seq=seq, dilation=1, nblk=n16, q_off=0, k_off=kw_off,
                            v_off=kw_off + C_KV_HEADS, n_q_heads=C_HEADS, rep=C_HEADS // C_KV_HEADS,
                            max_dist=NSA_WINDOW - 1, heads=C_HEADS // C_KV_HEADS, pack_out=True)
    return _outproj_odd(o_cmp, o_slc, o_win, gate, x2, w_out.astype(BF16), g.reshape(1, d), b.reshape(1, d))


def kernel(x, positions, even_w_in, even_sinks, even_w_out, odd_w_in, odd_cmpk_pos, odd_cmpk_w1, odd_cmpk_w2, odd_cmpv_pos, odd_cmpv_w1, odd_cmpv_w2, odd_w_out, mix_ln_g, mix_ln_b, moe_router_w, moe_router_b, moe_w_gate, moe_w_up, moe_w_down, moe_sh_gate, moe_sh_up, moe_sh_down, ffn_ln_g, ffn_ln_b):
    batch, seq, d = x.shape
    x2 = x.reshape(batch * seq, d)
    tabs = _rope_tables(positions)
    depth = mix_ln_g.shape[0]
    for layer in range(depth):
        j = layer // 2
        if layer % 2 == 0:
            x2 = _even_mixer_layer(x2, batch, seq, tabs, even_w_in[j], even_sinks[j], even_w_out[j],
                                   mix_ln_g[layer], mix_ln_b[layer])
        else:
            x2 = _odd_mixer_layer(x2, batch, seq, tabs, odd_w_in[j], odd_cmpk_pos[j], odd_cmpk_w1[j], odd_cmpk_w2[j],
                                  odd_cmpv_pos[j], odd_cmpv_w1[j], odd_cmpv_w2[j], odd_w_out[j],
                                  mix_ln_g[layer], mix_ln_b[layer])
        x2 = _moe_layer(x2, moe_router_w[layer], moe_router_b[layer], moe_w_gate, moe_w_up, moe_w_down, layer,
                        moe_sh_gate[layer], moe_sh_up[layer], moe_sh_down[layer],
                        ffn_ln_g[layer].reshape(1, d), ffn_ln_b[layer].reshape(1, d))
    return x2.reshape(batch, seq, d)
```

```python
import functools

import numpy as np
import jax
import jax.numpy as jnp
from jax import lax
from jax.experimental import pallas as pl
from jax.experimental.pallas import tpu as pltpu

F32 = jnp.float32
BF16 = jnp.bfloat16

LANES = 128
SUBLANES = 8
HEAD_DIM = 64
ROT_DIM = HEAD_DIM // 4
ROT_HALF = ROT_DIM // 2
ROPE_THETA = 500000.0
QBLK = 128
A_HEADS = 8
A_PATTERNS = ((128, 1), (512, 4), (2048, 16))
B_Q_HEADS = 8
B_KV_HEADS = 2
B_WINDOW = 128
C_HEADS = 16
C_KV_HEADS = 4
CMP_LEN = 32
CMP_STRIDE = 16
CMP_HIDDEN = 2 * HEAD_DIM
SLC_BLOCK = 64
SLC_SHIFT = 6
SLC_TOP_N = 16
NSA_WINDOW = 512
SELECT_FORCE = 1.0e4
N_EXPERTS = 256
TOP_K = 8
N_GROUPS = 8
TOPK_GROUPS = 4
ROUTED_SCALE = 2.5
MOE_BLOCK = 256
DEPTH = 2
DEEPNORM_ALPHA = (2 * DEPTH) ** 0.25
LN_EPS = 1e-5
NEG = -1.0e30
VMEM_LIMIT = 56 * 1024 * 1024

_NT = (((1,), (1,)), ((), ()))


def _cparams(*sem):
    return pltpu.CompilerParams(dimension_semantics=sem, vmem_limit_bytes=VMEM_LIMIT)


def _split_bf16(a):
    hi = a.astype(BF16)
    lo = (a - hi.astype(F32)).astype(BF16)
    return hi, lo


def _layer_norm(y, g, b):
    mu = jnp.mean(y, axis=-1, keepdims=True)
    d = y - mu
    var = jnp.mean(d * d, axis=-1, keepdims=True)
    return d * lax.rsqrt(var + LN_EPS) * g + b


def _silu(a):
    return a * jax.nn.sigmoid(a)


TAG_BLOCK = 3


def _proj_kernel(*refs, plan, n_main, transposed, dilations, n_dil, seq):
    n_in = 4 if transposed else 3
    x_ref, w_ref, tab_ref = refs[:3]
    out_refs = refs[n_in:n_in + n_main]
    extra = list(refs[n_in + n_main:])
    x = x_ref[...].astype(BF16)
    tm = x.shape[0]
    if transposed:
        out_t = extra.pop(0)
        out_t[...] = lax.dot_general(refs[3][...], x, _NT, preferred_element_type=F32).astype(out_t.dtype)
    dil_refs = [extra.pop(0) for _ in dilations]
    stage = extra.pop(0) if dilations else None
    nblk = len(plan)
    for c0 in range(0, nblk, 2):
        nb = min(2, nblk - c0)
        acc = jnp.dot(x, w_ref[:, c0 * LANES:(c0 + nb) * LANES], preferred_element_type=F32)
        for j in range(nb):
            blk = acc[:, j * LANES:(j + 1) * LANES]
            dst, dblk, mode = plan[c0 + j]
            if mode == TAG_BLOCK:
                pos = lax.rem(pl.program_id(0) * tm, seq) + lax.broadcasted_iota(jnp.int32, (tm, LANES), 0)
                tag = lax.broadcasted_iota(jnp.int32, (tm, LANES), 1) - HEAD_DIM == jnp.right_shift(pos, SLC_SHIFT)
                mode = 1
            else:
                tag = None
            if mode:
                off = (mode - 1) * 3 * LANES
                cos = tab_ref[:, off:off + LANES]
                s_lo = tab_ref[:, off + LANES:off + 2 * LANES]
                s_hi = tab_ref[:, off + 2 * LANES:off + 3 * LANES]
                blk = (blk * cos + pltpu.roll(blk, LANES - ROT_HALF, 1) * s_lo
                       + pltpu.roll(blk, ROT_HALF, 1) * s_hi)
            if tag is not None:
                blk = jnp.where(tag, NEG, blk)
            o_ref = out_refs[dst]
            o_ref[:, dblk * LANES:(dblk + 1) * LANES] = blk.astype(o_ref.dtype)
            c = c0 + j
            if dilations and c < n_dil:
                stage[...] = blk
                for dil, d_ref in zip(dilations, dil_refs):
                    for r in range(dil):
                        col = (r * n_dil + c) * LANES
                        d_ref[:, col:col + LANES] = stage[pl.ds(r, tm // dil, stride=dil), :].astype(d_ref.dtype)


def _proj(x2, w, tabs, plan, out_cols, out_dtypes, wt=None, dilations=(), n_dil=0, seq=0, tm=512):
    t, d = x2.shape
    ncol = w.shape[1]
    out_shape = [jax.ShapeDtypeStruct((t, c), dt) for c, dt in zip(out_cols, out_dtypes)]
    once = pl.Buffered(1)
    in_specs = [pl.BlockSpec((tm, d), lambda i: (i, 0)),
                pl.BlockSpec((d, ncol), lambda i: (0, 0), pipeline_mode=once),
                pl.BlockSpec((tm, tabs.shape[1]), lambda i: (i, 0))]
    out_specs = [pl.BlockSpec((tm, c), lambda i: (i, 0)) for c in out_cols]
    args = [x2, w, tabs]
    if wt is not None:
        in_specs.append(pl.BlockSpec(wt.shape, lambda i: (0, 0), pipeline_mode=once))
        out_specs.append(pl.BlockSpec((wt.shape[0], tm), lambda i: (0, i)))
        out_shape.append(jax.ShapeDtypeStruct((wt.shape[0], t), BF16))
        args.append(wt)
    for dil in dilations:
        out_specs.append(pl.BlockSpec((tm // dil, dil * n_dil * LANES), lambda i: (i, 0)))
        out_shape.append(jax.ShapeDtypeStruct((t // dil, dil * n_dil * LANES), BF16))
    return pl.pallas_call(
        functools.partial(_proj_kernel, plan=tuple(plan), n_main=len(out_cols), transposed=wt is not None,
                          dilations=tuple(dilations), n_dil=n_dil, seq=seq),
        grid=(t // tm,),
        in_specs=in_specs,
        out_specs=out_specs,
        out_shape=out_shape,
        scratch_shapes=[pltpu.VMEM((tm, LANES), F32)] if dilations else [],
        compiler_params=_cparams("parallel"),
        name="proj_rope",
    )(*args)


def _rope_tables(positions):
    t = positions.size
    inv_freq = jnp.asarray(ROPE_THETA ** (-np.arange(0, ROT_DIM, 2) / ROT_DIM), F32)
    ang = positions.astype(F32).reshape(t, 1) * inv_freq
    trig = jnp.concatenate([jnp.cos(ang), jnp.sin(ang)], axis=1)
    place = np.zeros((2 * ROT_HALF, 6 * LANES), np.float32)
    const = np.zeros((6 * LANES,), np.float32)
    for base, heads in ((0, (0,)), (3 * LANES, (0, HEAD_DIM))):
        const[base:base + LANES] = 1.0
        for h in heads:
            for i in range(ROT_HALF):
                place[i, base + h + i] = 1.0
                place[i, base + h + ROT_HALF + i] = 1.0
                const[base + h + i] = const[base + h + ROT_HALF + i] = 0.0
                place[ROT_HALF + i, base + LANES + h + i] = -1.0
                place[ROT_HALF + i, base + 2 * LANES + h + ROT_HALF + i] = 1.0
    return jnp.dot(trig, jnp.asarray(place), precision=lax.Precision.HIGHEST) + jnp.asarray(const)


def _pad_heads_cols(w, n_heads):
    d = w.shape[0]
    w = w.reshape(d, n_heads, HEAD_DIM)
    return jnp.pad(w, ((0, 0), (0, 0), (0, LANES - HEAD_DIM))).reshape(d, n_heads * LANES)


def _pad_heads_rows(w, n_heads):
    d = w.shape[1]
    w = w.reshape(n_heads, HEAD_DIM, d)
    return jnp.pad(w, ((0, 0), (0, LANES - HEAD_DIM), (0, 0))).reshape(n_heads * LANES, d)


def _pack_head_pair(a, b):
    lane = lax.broadcasted_iota(jnp.int32, a.shape, 1)
    return jnp.where(lane < HEAD_DIM, a, pltpu.roll(b, HEAD_DIM, 1))


def _band_kernel(*refs, back, max_dist, length, qrows, heads, shared_kv, in_flight, has_sink, pack_out):
    if has_sink:
        sink_ref, q_ref, k_ref, v_ref, o_ref, bias_sc = refs
    else:
        q_ref, k_ref, v_ref, o_ref, bias_sc = refs
    nq = length // qrows
    window = qrows + back
    chains = [tuple(range(heads))] if shared_kv else [(h,) for h in range(heads)]
    stack = len(chains[0])
    lane = lax.broadcasted_iota(jnp.int32, (stack * qrows, LANES), 1)
    first_head = pl.program_id(2) * heads

    def band_bias(offset):
        row = lax.broadcasted_iota(jnp.int32, (qrows, window), 0)
        col = lax.broadcasted_iota(jnp.int32, (qrows, window), 1)
        dist = offset + row - col
        tile = jnp.where((dist >= 0) & (dist <= max_dist), 0.0, NEG)
        return jnp.concatenate([tile] * stack, axis=0) if stack > 1 else tile

    bias_sc[...] = band_bias(back)

    def block(hs, q0, k0, bias):
        kv = slice(0, LANES) if shared_kv else slice(hs[0] * LANES, (hs[0] + 1) * LANES)
        q = jnp.concatenate([q_ref[0, pl.ds(q0, qrows), h * LANES:(h + 1) * LANES] for h in hs], axis=0)
        q = q * (HEAD_DIM ** -0.5)
        k = k_ref[0, pl.ds(k0, window), kv]
        v = v_ref[0, pl.ds(k0, window), kv]
        s = lax.dot_general(q, k, _NT, preferred_element_type=F32) + bias
        m = jnp.max(s, axis=-1, keepdims=True)
        e = jnp.exp(s - m)
        den = jnp.sum(e, axis=-1, keepdims=True)
        if has_sink:
            sink = jnp.concatenate([jnp.full((qrows, 1), sink_ref[first_head + h], F32) for h in hs], axis=0)
            den = den + jnp.exp(sink - m)
        o = jnp.dot(e.astype(BF16), v, preferred_element_type=F32) / den
        lse = m + jnp.log(den)
        if pack_out:
            for i in range(0, len(hs), 2):
                pair = _pack_head_pair(o[i * qrows:(i + 1) * qrows], o[(i + 1) * qrows:(i + 2) * qrows])
                o_ref[0, pl.ds(q0, qrows), (hs[i] // 2) * LANES:(hs[i] // 2 + 1) * LANES] = pair
            return
        out = jnp.where(lane < HEAD_DIM, o, lse)
        for i, h in enumerate(hs):
            o_ref[0, pl.ds(q0, qrows), h * LANES:(h + 1) * LANES] = out[i * qrows:(i + 1) * qrows]

    n_clipped = min(-(-back // qrows), nq)
    for qi in range(n_clipped):
        bias = band_bias(qi * qrows)
        for hs in chains:
            block(hs, qi * qrows, 0, bias)

    steady = nq - n_clipped
    per_iter = max(1, min(in_flight // heads, steady))
    n_iter = steady // per_iter

    def body(it, carry):
        aligned = lambda v: v if isinstance(v, int) else pl.multiple_of(v, QBLK)
        for j in range(per_iter):
            q0 = aligned((n_clipped + it * per_iter + j) * qrows)
            for hs in chains:
                block(hs, q0, aligned(q0 - back), bias_sc[...])
        return carry

    if n_iter == 1:
        body(0, 0)
    elif n_iter:
        lax.fori_loop(0, n_iter, body, 0)
    for qi in range(n_clipped + n_iter * per_iter, nq):
        for hs in chains:
            block(hs, qi * qrows, qi * qrows - back, bias_sc[...])


def _band_attention(qkv, *, batch, seq, dilation, nblk, q_off, k_off, v_off, n_q_heads, rep, max_dist, sinks=None,
                    heads=1, in_flight=16, pack_out=False):
    length = seq // dilation
    back = min(-(-max_dist // QBLK) * QBLK, length - QBLK)
    qrows = QBLK
    shared_kv = rep > 1
    assert n_q_heads % heads == 0 and (rep == 1 or rep % heads == 0)
    assert not pack_out or (shared_kv and heads % 2 == 0)
    out_lanes = HEAD_DIM if pack_out else LANES
    kv_heads = 1 if shared_kv else heads
    arr = qkv.reshape(batch, length, dilation * nblk * LANES)
    assert q_off % heads == 0 and k_off % kv_heads == 0 and v_off % kv_heads == 0 and nblk % heads == 0
    qspec = pl.BlockSpec((1, length, heads * LANES),
                         lambda b, r, h: (b, 0, (r * nblk + q_off) // heads + h))
    kvspec = lambda off: pl.BlockSpec(
        (1, length, kv_heads * LANES),
        lambda b, r, h: (b, 0, (r * nblk + off) // kv_heads + (h * heads // rep if shared_kv else h)))
    in_specs = [qspec, kvspec(k_off), kvspec(v_off)]
    args = [arr, arr, arr]
    if sinks is not None:
        in_specs = [pl.BlockSpec(memory_space=pltpu.SMEM)] + in_specs
        args = [sinks.reshape(-1).astype(F32)] + args
    out = pl.pallas_call(
        functools.partial(_band_kernel, back=back, max_dist=max_dist, length=length, qrows=qrows, heads=heads,
                          shared_kv=shared_kv, in_flight=in_flight, has_sink=sinks is not None,
                          pack_out=pack_out),
        grid=(batch, dilation, n_q_heads // heads),
        in_specs=in_specs,
        out_specs=pl.BlockSpec((1, length, heads * out_lanes), lambda b, r, h: (b, 0, r * (n_q_heads // heads) + h)),
        out_shape=jax.ShapeDtypeStruct((batch, length, dilation * n_q_heads * out_lanes), F32),
        scratch_shapes=[pltpu.VMEM(((heads if shared_kv else 1) * qrows, qrows + back), F32)],
        compiler_params=_cparams("parallel", "parallel", "parallel"),
        name="band_attention",
    )(*args)
    return out.reshape(batch * length, dilation * n_q_heads * out_lanes)


def _outproj_even_kernel(o1_ref, o2_ref, o3_ref, ob_ref, x_ref, w_ref, g_ref, b_ref, out_ref, *nat_refs,
                         n_a, n_b, dilations):
    tm = x_ref.shape[0]
    lane = lax.broadcasted_iota(jnp.int32, (tm, LANES), 1)
    real = lane < HEAD_DIM
    pattern_refs = []
    nat_refs = list(nat_refs)
    for o_ref, dil in zip((o1_ref, o2_ref, o3_ref), dilations):
        if dil == 1:
            pattern_refs.append(o_ref)
            continue
        nat = nat_refs.pop(0)
        for r in range(dil):
            for h in range(n_a):
                col = (r * n_a + h) * LANES
                nat[h, pl.ds(r, tm // dil, stride=dil), :] = o_ref[:, col:col + LANES]
        pattern_refs.append(nat)
    parts = []
    for h in range(n_a):
        sl = slice(h * LANES, (h + 1) * LANES)
        outs = [r[:, sl] if r.ndim == 2 else r[h] for r in pattern_refs]
        lses = [jnp.where(real, pltpu.roll(a, HEAD_DIM, 1), a) for a in outs]
        m = jnp.maximum(jnp.maximum(lses[0], lses[1]), lses[2])
        es = [jnp.exp(l - m) for l in lses]
        num = es[0] * outs[0] + es[1] * outs[1] + es[2] * outs[2]
        den = es[0] + es[1] + es[2]
        parts.append(jnp.where(real, num / den, 0.0).astype(BF16))
    for h in range(n_b):
        parts.append(jnp.where(real, ob_ref[:, h * LANES:(h + 1) * LANES], 0.0).astype(BF16))
    a = jnp.concatenate(parts, axis=1)
    mixed = jnp.dot(a, w_ref[...], preferred_element_type=F32)
    y = DEEPNORM_ALPHA * x_ref[...] + mixed
    out_ref[...] = _layer_norm(y, g_ref[...], b_ref[...])


def _outproj_even(outs, dilations, ob, x2, w, g, b, tm=256):
    t, d = x2.shape
    n_a, n_b = outs[0].shape[1] // (dilations[0] * LANES), ob.shape[1] // LANES
    row = lambda c: pl.BlockSpec((tm, c), lambda i: (i, 0))
    grouped = lambda a, dil: pl.BlockSpec((tm // dil, a.shape[1]), lambda i: (i, 0))
    full = lambda a: pl.BlockSpec(a.shape, lambda i: (0,) * a.ndim)
    return pl.pallas_call(
        functools.partial(_outproj_even_kernel, n_a=n_a, n_b=n_b, dilations=tuple(dilations)),
        grid=(t // tm,),
        in_specs=[grouped(o, dil) for o, dil in zip(outs, dilations)] + [row(ob.shape[1]), row(d),
                                                                          full(w), full(g), full(b)],
        out_specs=row(d),
        out_shape=jax.ShapeDtypeStruct((t, d), F32),
        scratch_shapes=[pltpu.VMEM((n_a, tm, LANES), F32) for dil in dilations if dil > 1],
        compiler_params=_cparams("parallel"),
        name="outproj_even_ln",
    )(*outs, ob, x2, w, g, b)


def _outproj_odd_kernel(oc_ref, os_ref, ow_ref, gate_ref, e_ref, x_ref, w_ref, g_ref, b_ref, out_ref):
    gate = jax.nn.sigmoid(gate_ref[...])
    ghi, glo = _split_bf16(gate)
    acc = None
    for j, o_ref in enumerate((oc_ref, os_ref, ow_ref)):
        ej = e_ref[j]
        gfull = jnp.dot(ghi, ej, preferred_element_type=F32) + jnp.dot(glo, ej, preferred_element_type=F32)
        term = gfull * o_ref[...]
        acc = term if acc is None else acc + term
    mixed = jnp.dot(acc.astype(BF16), w_ref[...], preferred_element_type=F32)
    y = DEEPNORM_ALPHA * x_ref[...] + mixed
    out_ref[...] = _layer_norm(y, g_ref[...], b_ref[...])


def _gate_expanders(n_heads):
    e = np.zeros((3, LANES, n_heads * HEAD_DIM), np.float32)
    for j in range(3):
        for h in range(n_heads):
            e[j, 3 * h + j, h * HEAD_DIM:(h + 1) * HEAD_DIM] = 1.0
    return jnp.asarray(e, BF16)


def _outproj_odd(oc, osl, ow, gate, x2, w, g, b, tm=256):
    t, d = x2.shape
    n_heads = oc.shape[1] // HEAD_DIM
    e = _gate_expanders(n_heads)
    row = lambda c: pl.BlockSpec((tm, c), lambda i: (i, 0))
    full = lambda a: pl.BlockSpec(a.shape, lambda i: (0,) * a.ndim)
    return pl.pallas_call(
        _outproj_odd_kernel,
        grid=(t // tm,),
        in_specs=[row(oc.shape[1]), row(osl.shape[1]), row(ow.shape[1]), row(LANES), full(e), row(d),
                  full(w), full(g), full(b)],
        out_specs=row(d),
        out_shape=jax.ShapeDtypeStruct((t, d), F32),
        compiler_params=_cparams("parallel"),
        name="outproj_odd_ln",
    )(oc, osl, ow, gate, e, x2, w, g, b)


def _router_kernel(x_ref, whi_ref, wlo_ref, bias_ref, eidx_ref, gate_ref, rank_ref, cnt_ref):
    n_exp = whi_ref.shape[0]
    tm = x_ref.shape[0]
    per_group = n_exp // N_GROUPS
    xhi, xlo = _split_bf16(x_ref[...])
    whi, wlo = whi_ref[...], wlo_ref[...]
    dg = lambda a, b: lax.dot_general(a, b, _NT, preferred_element_type=F32)
    logits = dg(whi, xhi) + dg(whi, xlo) + dg(wlo, xhi)
    aff = jax.nn.sigmoid(logits)
    biased = aff + bias_ref[...]
    gio = lax.broadcasted_iota(jnp.int32, (per_group, tm), 0).astype(F32)
    blocks, scores = [], []
    for g in range(N_GROUPS):
        blk = biased[g * per_group:(g + 1) * per_group, :]
        m1 = jnp.max(blk, axis=0, keepdims=True)
        first = jnp.min(jnp.where(blk == m1, gio, float(per_group)), axis=0, keepdims=True)
        m2 = jnp.max(jnp.where(gio == first, -jnp.inf, blk), axis=0, keepdims=True)
        blocks.append(blk)
        scores.append(m1 + m2)
    masked = []
    for g in range(N_GROUPS):
        rank = jnp.zeros((1, tm), F32)
        for o in range(N_GROUPS):
            if o == g:
                continue
            beats = scores[o] >= scores[g] if o < g else scores[o] > scores[g]
            rank = rank + jnp.where(beats, 1.0, 0.0)
        masked.append(jnp.where(rank < TOPK_GROUPS, blocks[g], -jnp.inf))
    cur = jnp.concatenate(masked, axis=0)
    eio = lax.broadcasted_iota(jnp.int32, (n_exp, tm), 0).astype(F32)
    ids, gs = [], []
    for _ in range(TOP_K):
        m = jnp.max(cur, axis=0, keepdims=True)
        idx = jnp.min(jnp.where(cur == m, eio, float(n_exp)), axis=0, keepdims=True)
        hit = eio == idx
        gs.append(jnp.sum(jnp.where(hit, aff, 0.0), axis=0, keepdims=True))
        ids.append(idx)
        cur = jnp.where(hit, -jnp.inf, cur)
    gates = jnp.concatenate(gs, axis=0)
    gates = gates / jnp.sum(gates, axis=0, keepdims=True) * ROUTED_SCALE
    eidx_ref[...] = jnp.concatenate(ids, axis=0).astype(jnp.int32)
    gate_ref[...] = gates
    @pl.when(pl.program_id(0) == 0)
    def _():
        cnt_ref[...] = jnp.zeros(cnt_ref.shape, F32)

    onehot = jnp.zeros((n_exp, tm), F32)
    for idx in ids:
        onehot = onehot + jnp.where(eio == idx, 1.0, 0.0)
    earlier = jnp.where(lax.broadcasted_iota(jnp.int32, (tm, tm), 0) < lax.broadcasted_iota(jnp.int32, (tm, tm), 1),
                        1.0, 0.0).astype(BF16)
    before = cnt_ref[...] + jnp.dot(onehot.astype(BF16), earlier, preferred_element_type=F32)
    ranks = [jnp.sum(jnp.where(eio == idx, before, 0.0), axis=0, keepdims=True) for idx in ids]
    rank_ref[...] = jnp.concatenate(ranks, axis=0).astype(jnp.int32)
    cnt_ref[...] = cnt_ref[...] + jnp.sum(onehot, axis=1, keepdims=True)


def _router(x2, router_w, router_b, tm=256):
    t, d = x2.shape
    n_exp = router_w.shape[1]
    whi, wlo = _split_bf16(router_w.T)
    bias = router_b.reshape(n_exp, 1).astype(F32)
    full = lambda a: pl.BlockSpec(a.shape, lambda i: (0,) * a.ndim)
    per_tok = pl.BlockSpec((TOP_K, tm), lambda i: (0, i))
    return pl.pallas_call(
        _router_kernel,
        grid=(t // tm,),
        in_specs=[pl.BlockSpec((tm, d), lambda i: (i, 0)), full(whi), full(wlo), full(bias)],
        out_specs=[per_tok, per_tok, per_tok, pl.BlockSpec((n_exp, 1), lambda i: (0, 0))],
        out_shape=[jax.ShapeDtypeStruct((TOP_K, t), jnp.int32), jax.ShapeDtypeStruct((TOP_K, t), F32),
                   jax.ShapeDtypeStruct((TOP_K, t), jnp.int32), jax.ShapeDtypeStruct((n_exp, 1), F32)],
        compiler_params=_cparams("arbitrary"),
        name="moe_router",
    )(x2, whi, wlo, bias)


def _moe_dest_kernel(eidx_ref, rank_ref, start_ref, dest_ref):
    n_exp = start_ref.shape[0]
    tm = eidx_ref.shape[1]
    eio = lax.broadcasted_iota(jnp.int32, (n_exp, tm), 0)
    start = start_ref[...]
    rows = []
    for k in range(TOP_K):
        seg = jnp.sum(jnp.where(eio == eidx_ref[k:k + 1, :], start, 0.0), axis=0, keepdims=True)
        rows.append(seg.astype(jnp.int32) + rank_ref[k:k + 1, :])
    dest_ref[...] = jnp.concatenate(rows, axis=0)


def _moe_dest(eidx, rank, seg_start, tm=256):
    t = eidx.shape[1]
    per_tok = pl.BlockSpec((TOP_K, tm), lambda i: (0, i))
    return pl.pallas_call(
        _moe_dest_kernel,
        grid=(t // tm,),
        in_specs=[per_tok, per_tok, pl.BlockSpec(seg_start.shape, lambda i: (0, 0))],
        out_specs=per_tok,
        out_shape=jax.ShapeDtypeStruct((TOP_K, t), jnp.int32),
        compiler_params=_cparams("parallel"),
        name="moe_dest",
    )(eidx, rank, seg_start)


def _to_slabs(ref, value):
    rows, width = value.shape
    n_chunks = width // LANES
    for c in range(n_chunks):
        ref[pl.ds(c, rows, stride=n_chunks), :] = value[:, c * LANES:(c + 1) * LANES]


def _from_slabs(ref, first_row, rows, n_chunks):
    return jnp.concatenate([ref[pl.ds(first_row * n_chunks + c, rows, stride=n_chunks), :]
                            for c in range(n_chunks)], axis=1)


def _pack_bf16_pairs(x):
    half = x.shape[1] // 2
    lo = lax.bitcast_convert_type(x[:, :half].astype(BF16).astype(F32), jnp.uint32)
    hi = lax.bitcast_convert_type(x[:, half:].astype(BF16).astype(F32), jnp.uint32)
    return jnp.right_shift(lo, jnp.uint32(16)) | (hi & jnp.uint32(0xFFFF0000))


def _unpack_bf16_pairs(w):
    lo = lax.bitcast_convert_type(jnp.left_shift(w, jnp.uint32(16)), F32).astype(BF16)
    hi = lax.bitcast_convert_type(w & jnp.uint32(0xFFFF0000), F32).astype(BF16)
    return jnp.concatenate([lo, hi], axis=1)


def _moe_dispatch_kernel(zb_ref, dest_ref, x_ref, xs_out, buf, zbuf, sem, zsem):
    tm = x_ref.shape[0]
    n_chunks = buf.shape[0] // tm

    @pl.when(pl.program_id(0) == 0)
    def _():
        rows = zbuf.shape[0]
        zbuf[...] = jnp.zeros(zbuf.shape, zbuf.dtype)

        def zero_copy(e):
            first = pl.multiple_of(zb_ref[e] * rows, rows)
            return pltpu.make_async_copy(zbuf, xs_out.at[pl.ds(first, rows)], zsem)

        def start(e, c):
            @pl.when(zb_ref[e] >= 0)
            def _():
                zero_copy(e).start()
            return c

        def wait(e, c):
            @pl.when(zb_ref[e] >= 0)
            def _():
                zero_copy(e).wait()
            return c

        lax.fori_loop(0, zb_ref.shape[0], start, 0)
        lax.fori_loop(0, zb_ref.shape[0], wait, 0)

    _to_slabs(buf, _pack_bf16_pairs(x_ref[...]))

    def issue(i, c):
        src = buf.at[pl.ds(pl.multiple_of(i * n_chunks, n_chunks), n_chunks)]
        for k in range(TOP_K):
            r = pl.multiple_of(dest_ref[k, i] * n_chunks, n_chunks)
            pltpu.make_async_copy(src, xs_out.at[pl.ds(r, n_chunks)], sem).start(priority=k % 2)
        return c

    lax.fori_loop(0, tm, issue, 0)
    for _ in range(TOP_K):
        pltpu.make_async_copy(buf, xs_out.at[pl.ds(0, tm * n_chunks)], sem).wait()


def _moe_dispatch(zero_blocks, dest, x2, n_blocks, tm=256):
    t, d = x2.shape
    n_chunks = d // 2 // LANES
    grid_spec = pltpu.PrefetchScalarGridSpec(
        num_scalar_prefetch=1,
        grid=(t // tm,),
        in_specs=[pl.BlockSpec((TOP_K, tm), lambda i, zb: (0, i), memory_space=pltpu.SMEM),
                  pl.BlockSpec((tm, d), lambda i, zb: (i, 0))],
        out_specs=pl.BlockSpec(memory_space=pl.ANY),
        scratch_shapes=[pltpu.VMEM((tm * n_chunks, LANES), jnp.uint32),
                        pltpu.VMEM((MOE_BLOCK * n_chunks, LANES), jnp.uint32),
                        pltpu.SemaphoreType.DMA(()), pltpu.SemaphoreType.DMA(())],
    )
    return pl.pallas_call(
        _moe_dispatch_kernel,
        grid_spec=grid_spec,
        out_shape=jax.ShapeDtypeStruct((n_blocks * MOE_BLOCK * n_chunks, LANES), jnp.uint32),
        compiler_params=_cparams("arbitrary"),
        name="moe_dispatch",
    )(zero_blocks, dest, x2)


def _moe_ffn_kernel(be_ref, nu_ref, xs_ref, wg_ref, wu_ref, wd_ref, y_ref, wg_sc, wu_sc, wd_sc):
    b = pl.program_id(0)

    @pl.when(b < nu_ref[0])
    def _():
        @pl.when((b == 0) | (be_ref[b] != be_ref[jnp.maximum(b - 1, 0)]))
        def _():
            wg_sc[...] = wg_ref[0].astype(BF16)
            wu_sc[...] = wu_ref[0].astype(BF16)
            wd_sc[...] = wd_ref[0].astype(BF16)

        d = wg_sc.shape[0]
        xb = _unpack_bf16_pairs(_from_slabs(xs_ref, 0, MOE_BLOCK, d // 2 // LANES))
        gp = jnp.dot(xb, wg_sc[...], preferred_element_type=F32)
        up = jnp.dot(xb, wu_sc[...], preferred_element_type=F32)
        h = (_silu(gp) * up).astype(BF16)
        _to_slabs(y_ref, jnp.dot(h, wd_sc[...], preferred_element_type=F32))


def _moe_ffn(xs, block_expert, n_used, w_gate, w_up, w_down, layer):
    n_blocks = block_expert.shape[0]
    d, ff = w_gate.shape[2], w_gate.shape[3]
    last = lambda b, nu: jnp.minimum(b, nu[0] - 1)
    grid_spec = pltpu.PrefetchScalarGridSpec(
        num_scalar_prefetch=2,
        grid=(n_blocks,),
        in_specs=[
            pl.BlockSpec((MOE_BLOCK * (d // 2 // LANES), LANES), lambda b, be, nu: (last(b, nu), 0)),
            pl.BlockSpec((None, 1, d, ff), lambda b, be, nu: (layer, be[last(b, nu)], 0, 0)),
            pl.BlockSpec((None, 1, d, ff), lambda b, be, nu: (layer, be[last(b, nu)], 0, 0)),
            pl.BlockSpec((None, 1, ff, d), lambda b, be, nu: (layer, be[last(b, nu)], 0, 0)),
        ],
        out_specs=pl.BlockSpec((MOE_BLOCK * (d // LANES), LANES), lambda b, be, nu: (last(b, nu), 0)),
        scratch_shapes=[pltpu.VMEM((d, ff), BF16), pltpu.VMEM((d, ff), BF16), pltpu.VMEM((ff, d), BF16)],
    )
    return pl.pallas_call(
        _moe_ffn_kernel,
        grid_spec=grid_spec,
        out_shape=jax.ShapeDtypeStruct((n_blocks * MOE_BLOCK * (d // LANES), LANES), F32),
        compiler_params=_cparams("arbitrary"),
        name="moe_expert_ffn",
    )(block_expert, n_used, xs, w_gate, w_up, w_down)


def _moe_combine_kernel(dest_ref, y_hbm, gate_ref, x_ref, sg_ref, su_ref, sd_ref, g_ref, b_ref, out_ref, buf, sem):
    tm, d = x_ref.shape
    n_chunks = d // LANES

    def issue(i, c):
        for k in range(TOP_K):
            r = pl.multiple_of(dest_ref[k, i] * n_chunks, n_chunks)
            slot = pl.multiple_of((k * tm + i) * n_chunks, n_chunks)
            pltpu.make_async_copy(y_hbm.at[pl.ds(r, n_chunks)], buf.at[pl.ds(slot, n_chunks)],
                                  sem).start(priority=k % 2)
        return c

    lax.fori_loop(0, tm, issue, 0)
    x = x_ref[...]
    xb = x.astype(BF16)
    hs = _silu(jnp.dot(xb, sg_ref[...], preferred_element_type=F32)) * jnp.dot(xb, su_ref[...], preferred_element_type=F32)
    shared = jnp.dot(hs.astype(BF16), sd_ref[...], preferred_element_type=F32)
    pltpu.make_async_copy(y_hbm.at[pl.ds(0, TOP_K * tm * n_chunks)], buf, sem).wait()
    gates = gate_ref[...]
    routed = _from_slabs(buf, 0, tm, n_chunks) * gates[:, 0:1]
    for k in range(1, TOP_K):
        routed = routed + _from_slabs(buf, k * tm, tm, n_chunks) * gates[:, k:k + 1]
    y = DEEPNORM_ALPHA * x + (routed + shared)
    out_ref[...] = _layer_norm(y, g_ref[...], b_ref[...])


def _moe_combine(dest, y, gates_t, x2, sh_gate, sh_up, sh_down, g, b, tm=256):
    t, d = x2.shape
    row = lambda c: pl.BlockSpec((tm, c), lambda i: (i, 0))
    full = lambda a: pl.BlockSpec(a.shape, lambda i: (0,) * a.ndim)
    return pl.pallas_call(
        _moe_combine_kernel,
        grid=(t // tm,),
        in_specs=[pl.BlockSpec((TOP_K, tm), lambda i: (0, i), memory_space=pltpu.SMEM),
                  pl.BlockSpec(memory_space=pl.ANY),
                  row(TOP_K), row(d), full(sh_gate), full(sh_up), full(sh_down), full(g), full(b)],
        out_specs=row(d),
        out_shape=jax.ShapeDtypeStruct((t, d), F32),
        scratch_shapes=[pltpu.VMEM((TOP_K * tm * (d // LANES), LANES), F32), pltpu.SemaphoreType.DMA(())],
        compiler_params=_cparams("arbitrary"),
        name="moe_combine_ln",
    )(dest, y, gates_t, x2, sh_gate, sh_up, sh_down, g, b)


def _moe_segments(counts, n_tok):
    n_exp = counts.shape[0]
    n_blocks = -(-n_tok * TOP_K // MOE_BLOCK) + n_exp
    nblk = (counts.reshape(n_exp).astype(jnp.int32) + MOE_BLOCK - 1) // MOE_BLOCK
    blk_end = jnp.cumsum(nblk)
    seg_start = ((blk_end - nblk) * MOE_BLOCK).astype(F32).reshape(n_exp, 1)
    block_expert = jnp.sum((blk_end[None, :] <= jnp.arange(n_blocks)[:, None]).astype(jnp.int32), axis=1)
    block_expert = jnp.minimum(block_expert, n_exp - 1)
    n_used = blk_end[-1]
    zero_blocks = jnp.where(nblk > 0, blk_end - 1, -1).astype(jnp.int32)
    return seg_start, block_expert.astype(jnp.int32), n_used.astype(jnp.int32).reshape(1), zero_blocks, n_blocks


def _moe_layer(x2, router_w, router_b, w_gate, w_up, w_down, layer, sh_gate, sh_up, sh_down, g, b):
    t, d = x2.shape
    eidx, gates, rank, counts = _router(x2, router_w, router_b)
    seg_start, block_expert, n_used, zero_blocks, n_blocks = _moe_segments(counts, t)
    dest = _moe_dest(eidx, rank, seg_start)
    xs = _moe_dispatch(zero_blocks, dest, x2, n_blocks)
    y = _moe_ffn(xs, block_expert, n_used, w_gate, w_up, w_down, layer)
    return _moe_combine(dest, y, gates.T, x2, sh_gate.astype(BF16), sh_up.astype(BF16), sh_down.astype(BF16), g, b)


def _compress_kernel(x_ref, pa_ref, pb_ref, wa_ref, wb_ref, w2_ref, out_ref):
    x = x_ref[0]
    nc = x.shape[0]
    ha = jnp.dot((x + pa_ref[...]).astype(BF16), wa_ref[...], preferred_element_type=F32)
    hb = jnp.dot((x + pb_ref[...]).astype(BF16), wb_ref[...], preferred_element_type=F32)
    h = ha + pltpu.roll(hb, nc - 1, 0)
    h = jax.nn.gelu(h, approximate=True)
    out_ref[0] = jnp.dot(h.astype(BF16), w2_ref[...], preferred_element_type=F32).astype(out_ref.dtype)


def _compress(kc, pos, w1, w2, batch, seq):
    g = C_KV_HEADS
    nch = seq // CMP_STRIDE
    half = CMP_LEN // 2
    x = kc.reshape(batch, nch, half * g * HEAD_DIM)
    eye = jnp.eye(g, dtype=F32)
    w1r = w1.reshape(CMP_LEN, HEAD_DIM, CMP_HIDDEN)
    expand = lambda wpart: jnp.einsum('jdh,ge->jgdeh', wpart, eye).reshape(half * g * HEAD_DIM, g * CMP_HIDDEN)
    wa, wb = expand(w1r[:half]).astype(BF16), expand(w1r[half:]).astype(BF16)
    w2e = jnp.einsum('hd,ge->ghed', jnp.pad(w2, ((0, 0), (0, LANES - HEAD_DIM))), eye)
    w2e = w2e.reshape(g * CMP_HIDDEN, g * LANES).astype(BF16)
    tile_pos = lambda p: jnp.broadcast_to(p[:, None, :], (half, g, HEAD_DIM)).reshape(1, half * g * HEAD_DIM)
    pa, pb = tile_pos(pos[:half]), tile_pos(pos[half:])
    full = lambda a: pl.BlockSpec(a.shape, lambda i: (0,) * a.ndim)
    return pl.pallas_call(
        _compress_kernel,
        grid=(batch,),
        in_specs=[pl.BlockSpec((1, nch, x.shape[2]), lambda i: (i, 0, 0)), full(pa), full(pb), full(wa), full(wb),
                  full(w2e)],
        out_specs=pl.BlockSpec((1, nch, g * LANES), lambda i: (i, 0, 0)),
        out_shape=jax.ShapeDtypeStruct((batch, nch, g * LANES), BF16),
        compiler_params=_cparams("parallel"),
        name="nsa_compress",
    )(x, pa, pb, wa, wb, w2e)


def _nsa_cmp_kernel(q_ref, kc_ref, vc_ref, ovt_ref, o_ref, drop_ref, *, rep, n_sel, n_real):
    tq = q_ref.shape[1]
    nc = kc_ref.shape[1]
    nsb = ovt_ref.shape[0]
    t0 = pl.program_id(2) * tq
    scale = HEAD_DIM ** -0.5
    kc = kc_ref[0]
    vc = vc_ref[0]
    tpos = t0 + lax.broadcasted_iota(jnp.int32, (tq, nc), 0)
    cend = lax.broadcasted_iota(jnp.int32, (tq, nc), 1) * CMP_STRIDE + (CMP_LEN - 1)
    cmask = cend <= tpos
    psum = jnp.zeros((tq, nc), F32)
    outs = []
    for r in range(rep):
        q = q_ref[0, :, r * LANES:(r + 1) * LANES]
        sc = lax.dot_general(q, kc, _NT, preferred_element_type=F32) * scale
        sc = jnp.where(cmask, sc, NEG)
        m = jnp.max(sc, axis=-1, keepdims=True)
        ex = jnp.where(cmask, jnp.exp(sc - m), 0.0)
        den = jnp.sum(ex, axis=-1, keepdims=True)
        pc = ex / jnp.where(den > 0, den, 1.0)
        outs.append(jnp.dot(pc.astype(BF16), vc, preferred_element_type=F32))
        psum = psum + pc
    o_ref[0] = jnp.concatenate([_pack_head_pair(outs[r], outs[r + 1]) for r in range(0, rep, 2)], axis=1)
    phi, plo = _split_bf16(psum)
    ovt = ovt_ref[...]
    imp = (lax.dot_general(ovt, phi, _NT, preferred_element_type=F32)
           + lax.dot_general(ovt, plo, _NT, preferred_element_type=F32))
    jblk = lax.broadcasted_iota(jnp.int32, (nsb, tq), 0)
    cur = jnp.right_shift(t0 + lax.broadcasted_iota(jnp.int32, (nsb, tq), 1), SLC_SHIFT)
    forced = (jblk == 0) | (jblk == cur) | (jblk == cur - 1)
    score = jnp.where(jblk > cur, -1.0, jnp.where(forced, SELECT_FORCE, imp))
    rank = jnp.zeros((nsb, tq), F32)
    for k in range(n_real):
        rowk = score[k:k + 1, :]
        ge = jnp.where(rowk >= score, 1.0, 0.0)
        gt = jnp.where(rowk > score, 1.0, 0.0)
        rank = rank + jnp.where(jblk > k, ge, gt)
    drop = jnp.where(jblk <= cur, jnp.where(rank < n_sel, 0.0, 1.0), 1.0)
    drop = jnp.concatenate([drop, jnp.ones((LANES - nsb, tq), F32)], axis=0).T
    drop_ref[0, 0] = pltpu.roll(drop, HEAD_DIM, 1).astype(drop_ref.dtype)


def _nsa_cmp(q, kcmp, vcmp, batch, seq, q_off_blocks, tq=256):
    g = C_KV_HEADS
    rep = C_HEADS // g
    nc = kcmp.shape[1]
    nsb = seq // SLC_BLOCK
    n_sel = min(SLC_TOP_N, nsb)
    cs = np.arange(nc)[:, None] * CMP_STRIDE
    js = np.arange(nsb)[None, :] * SLC_BLOCK
    overlap = np.clip(np.minimum(cs + CMP_LEN, js + SLC_BLOCK) - np.maximum(cs, js), 0, None) / CMP_LEN
    overlap[(seq - CMP_LEN) // CMP_STRIDE + 1:] = 0.0
    nsb_pad = -(-nsb // SUBLANES) * SUBLANES
    ovt = jnp.asarray(np.pad(overlap.T, ((0, nsb_pad - nsb), (0, 0))), BF16)
    q3 = q.reshape(batch, seq, q.shape[1])
    n_real, nsb = nsb, nsb_pad
    assert nsb <= LANES - HEAD_DIM
    o, drop = pl.pallas_call(
        functools.partial(_nsa_cmp_kernel, rep=rep, n_sel=n_sel, n_real=n_real),
        grid=(batch, g, seq // tq),
        in_specs=[pl.BlockSpec((1, tq, rep * LANES), lambda b, gi, i: (b, i, q_off_blocks // rep + gi)),
                  pl.BlockSpec((1, nc, LANES), lambda b, gi, i: (b, 0, gi)),
                  pl.BlockSpec((1, nc, LANES), lambda b, gi, i: (b, 0, gi)),
                  pl.BlockSpec(ovt.shape, lambda b, gi, i: (0, 0))],
        out_specs=[pl.BlockSpec((1, tq, rep * HEAD_DIM), lambda b, gi, i: (b, i, gi)),
                   pl.BlockSpec((1, 1, tq, LANES), lambda b, gi, i: (b, gi, i, 0))],
        out_shape=[jax.ShapeDtypeStruct((batch, seq, C_HEADS * HEAD_DIM), F32),
                   jax.ShapeDtypeStruct((batch, g, seq, LANES), BF16)],
        compiler_params=_cparams("parallel", "parallel", "parallel"),
        name="nsa_compressed_select",
    )(q3, kcmp, vcmp, ovt)
    return o.reshape(batch * seq, C_HEADS * HEAD_DIM), drop


def _nsa_slc_kernel(q_ref, k_ref, vt_ref, drop_ref, o_ref, *, rep, kt):
    tq = q_ref.shape[1]
    t0 = pl.program_id(2) * tq
    n_kt = (t0 + tq + kt - 1) // kt
    upper = lax.broadcasted_iota(jnp.int32, (tq, LANES), 1) >= HEAD_DIM
    drop = drop_ref[0, 0]
    qs = [jnp.where(upper, drop, q_ref[0, :, r * LANES:(r + 1) * LANES] * (HEAD_DIM ** -0.5))
          for r in range(rep)]

    def tile(k0, carry, bias, size=kt):
        k = k_ref[0, pl.ds(k0, size), :]
        vt = vt_ref[:, pl.ds(k0, size)]
        new = []
        for qr, (m, l, acc) in zip(qs, carry):
            s = lax.dot_general(k, qr, _NT, preferred_element_type=F32)
            if bias is not None:
                s = s + bias
            m_new = jnp.maximum(m, jnp.max(s, axis=0, keepdims=True))
            e = jnp.exp(s - m_new)
            corr = jnp.exp(m - m_new)
            l = l * corr + jnp.sum(e, axis=0, keepdims=True)
            acc = acc * corr + jnp.dot(vt, e.astype(BF16), preferred_element_type=F32)
            new.append((m_new, l, acc))
        return tuple(new)

    init = tuple((jnp.full((1, tq), NEG, F32), jnp.zeros((1, tq), F32), jnp.zeros((LANES, tq), F32))
                 for _ in range(rep))
    carry = lax.fori_loop(0, n_kt - 1, lambda j, c: tile(pl.multiple_of(j * kt, kt), c, None), init)
    k_last = pl.multiple_of((n_kt - 1) * kt, kt)
    remaining = t0 + tq - k_last

    def finish(size):
        kpos = k_last + lax.broadcasted_iota(jnp.int32, (size, tq), 0)
        tpos = t0 + lax.broadcasted_iota(jnp.int32, (size, tq), 1)
        final = tile(k_last, carry, jnp.where(kpos <= tpos, 0.0, NEG), size)
        outs = [(acc / l).T for _, l, acc in final]
        o_ref[0] = jnp.concatenate([_pack_head_pair(outs[r], outs[r + 1]) for r in range(0, rep, 2)], axis=1)

    sizes = [s for s in (kt // 4, kt // 2, 3 * kt // 4) if s >= tq and s % LANES == 0] + [kt]
    lower = 0
    for size in sizes:
        pl.when((remaining > lower) & (remaining <= size))(functools.partial(finish, size))
        lower = size


def _nsa_slc(qkv, vt, drop, batch, seq, nblk, q_off, k_off, tq=128, kt=1024):
    g = C_KV_HEADS
    rep = C_HEADS // g
    kt = min(kt, seq)
    arr = qkv.reshape(batch, seq, nblk * LANES)
    out = pl.pallas_call(
        functools.partial(_nsa_slc_kernel, rep=rep, kt=kt),
        grid=(batch, g, seq // tq),
        in_specs=[pl.BlockSpec((1, tq, rep * LANES), lambda b, gi, i: (b, i, q_off // rep + gi)),
                  pl.BlockSpec((1, seq, LANES), lambda b, gi, i: (b, 0, k_off + gi)),
                  pl.BlockSpec((LANES, seq), lambda b, gi, i: (gi, b)),
                  pl.BlockSpec((1, 1, tq, LANES), lambda b, gi, i: (b, gi, i, 0))],
        out_specs=pl.BlockSpec((1, tq, rep * HEAD_DIM), lambda b, gi, i: (b, i, gi)),
        out_shape=jax.ShapeDtypeStruct((batch, seq, C_HEADS * HEAD_DIM), F32),
        compiler_params=_cparams("parallel", "parallel", "parallel"),
        name="nsa_selected",
    )(arr, arr, vt, drop)
    return out.reshape(batch * seq, C_HEADS * HEAD_DIM)


def _even_mixer_layer(x2, batch, seq, tabs, w_in, sinks, w_out, g, b):
    d = x2.shape[1]
    n_heads_in = 3 * A_HEADS + B_Q_HEADS + 2 * B_KV_HEADS
    w = _pad_heads_cols(w_in, n_heads_in).astype(BF16)
    rope = [1] * (2 * A_HEADS) + [0] * A_HEADS + [1] * B_Q_HEADS + [1] * B_KV_HEADS + [0] * B_KV_HEADS
    plan = [(0, c, rope[c]) for c in range(n_heads_in)]
    dilations = [dil for _, dil in A_PATTERNS]
    regroup = [dil for dil in dilations if dil > 1]
    n_a_blocks = 3 * A_HEADS
    qkv, *grouped = _proj(x2, w, tabs, plan, [n_heads_in * LANES], [BF16], dilations=regroup, n_dil=n_a_blocks)
    outs = []
    for window, dilation in A_PATTERNS:
        src, nblk = (qkv, n_heads_in) if dilation == 1 else (grouped[regroup.index(dilation)], n_a_blocks)
        outs.append(_band_attention(src, batch=batch, seq=seq, dilation=dilation, nblk=nblk, q_off=0,
                                    k_off=A_HEADS, v_off=2 * A_HEADS, n_q_heads=A_HEADS, rep=1,
                                    max_dist=window // dilation,
                                    heads=min(4, dilation)))
    qb_off = 3 * A_HEADS
    ob = _band_attention(qkv, batch=batch, seq=seq, dilation=1, nblk=n_heads_in, q_off=qb_off,
                         k_off=qb_off + B_Q_HEADS, v_off=qb_off + B_Q_HEADS + B_KV_HEADS, n_q_heads=B_Q_HEADS,
                         rep=B_Q_HEADS // B_KV_HEADS, max_dist=B_WINDOW - 1, sinks=sinks)
    w_o = _pad_heads_rows(w_out, A_HEADS + B_Q_HEADS).astype(BF16)
    return _outproj_even(outs, dilations, ob, x2, w_o, g.reshape(1, d), b.reshape(1, d))


def _odd_mixer_layer(x2, batch, seq, tabs, w_in, cmpk_pos, cmpk_w1, cmpk_w2, cmpv_pos, cmpv_w1, cmpv_w2, w_out, g, b):
    d = x2.shape[1]
    kvw = C_KV_HEADS * HEAD_DIM
    qw = C_HEADS * HEAD_DIM
    sizes = [qw] + [kvw] * 6 + [3 * C_HEADS]
    offs = np.concatenate([[0], np.cumsum(sizes)])
    wq, wkc, wvc, wks, wvs, wkw, wvw, wgt = [w_in[:, offs[i]:offs[i + 1]] for i in range(8)]
    ph = lambda wpart, n: _pad_heads_cols(wpart, n)
    w = jnp.concatenate([ph(wq, C_HEADS), ph(wks, C_KV_HEADS), ph(wkw, C_KV_HEADS), ph(wvw, C_KV_HEADS),
                         wkc, wvc, jnp.pad(wgt, ((0, 0), (0, LANES - 3 * C_HEADS)))], axis=1).astype(BF16)
    wvs_t = ph(wvs, C_KV_HEADS).T.astype(BF16)
    n16 = C_HEADS + 3 * C_KV_HEADS
    rope16 = [1] * C_HEADS + [TAG_BLOCK] * C_KV_HEADS + [1] * C_KV_HEADS + [0] * C_KV_HEADS
    n_kc = kvw // LANES
    plan = ([(0, c, rope16[c]) for c in range(n16)] + [(1, c, 2) for c in range(n_kc)]
            + [(2, c, 0) for c in range(n_kc)] + [(3, 0, 0)])
    qkv, kc, vc, gate, vs_t = _proj(x2, w, tabs, plan, [n16 * LANES, kvw, kvw, LANES], [BF16, F32, F32, F32],
                                    wt=wvs_t, seq=seq)
    kcmp = _compress(kc, cmpk_pos, cmpk_w1, cmpk_w2, batch, seq)
    vcmp = _compress(vc, cmpv_pos, cmpv_w1, cmpv_w2, batch, seq)
    o_cmp, drop = _nsa_cmp(qkv, kcmp, vcmp, batch, seq, 0)
    ks_off = C_HEADS
    o_slc = _nsa_slc(qkv, vs_t, drop, batch, seq, n16, 0, ks_off)
    kw_off = ks_off + C_KV_HEADS
    o_win = _band_attention(qkv, batch=batch, seq=seq, dilation=1, nblk=n16, q_off=0, k_off=kw_off,
                            v_off=kw_off + C_KV_HEADS, n_q_heads=C_HEADS, rep=C_HEADS // C_KV_HEADS,
                            max_dist=NSA_WINDOW - 1, heads=C_HEADS // C_KV_HEADS, pack_out=True)
    return _outproj_odd(o_cmp, o_slc, o_win, gate, x2, w_out.astype(BF16), g.reshape(1, d), b.reshape(1, d))


def kernel(x, positions, even_w_in, even_sinks, even_w_out, odd_w_in, odd_cmpk_pos, odd_cmpk_w1, odd_cmpk_w2, odd_cmpv_pos, odd_cmpv_w1, odd_cmpv_w2, odd_w_out, mix_ln_g, mix_ln_b, moe_router_w, moe_router_b, moe_w_gate, moe_w_up, moe_w_down, moe_sh_gate, moe_sh_up, moe_sh_down, ffn_ln_g, ffn_ln_b):
    batch, seq, d = x.shape
    x2 = x.reshape(batch * seq, d)
    tabs = _rope_tables(positions)
    depth = mix_ln_g.shape[0]
    for layer in range(depth):
        j = layer // 2
        if layer % 2 == 0:
            x2 = _even_mixer_layer(x2, batch, seq, tabs, even_w_in[j], even_sinks[j], even_w_out[j],
                                   mix_ln_g[layer], mix_ln_b[layer])
        else:
            x2 = _odd_mixer_layer(x2, batch, seq, tabs, odd_w_in[j], odd_cmpk_pos[j], odd_cmpk_w1[j], odd_cmpk_w2[j],
                                  odd_cmpv_pos[j], odd_cmpv_w1[j], odd_cmpv_w2[j], odd_w_out[j],
                                  mix_ln_g[layer], mix_ln_b[layer])
        x2 = _moe_layer(x2, moe_router_w[layer], moe_router_b[layer], moe_w_gate, moe_w_up, moe_w_down, layer,
                        moe_sh_gate[layer], moe_sh_up[layer], moe_sh_down[layer],
                        ffn_ln_g[layer].reshape(1, d), ffn_ln_b[layer].reshape(1, d))
    return x2.reshape(batch, seq, d)
```

```python
import functools

import numpy as np
import jax
import jax.numpy as jnp
from jax import lax
from jax.experimental import pallas as pl
from jax.experimental.pallas import tpu as pltpu

F32 = jnp.float32
BF16 = jnp.bfloat16

LANES = 128
SUBLANES = 8
HEAD_DIM = 64
ROT_DIM = HEAD_DIM // 4
ROT_HALF = ROT_DIM // 2
ROPE_THETA = 500000.0
QBLK = 128
A_HEADS = 8
A_PATTERNS = ((128, 1), (512, 4), (2048, 16))
B_Q_HEADS = 8
B_KV_HEADS = 2
B_WINDOW = 128
C_HEADS = 16
C_KV_HEADS = 4
CMP_LEN = 32
CMP_STRIDE = 16
CMP_HIDDEN = 2 * HEAD_DIM
SLC_BLOCK = 64
SLC_SHIFT = 6
SLC_TOP_N = 16
NSA_WINDOW = 512
SELECT_FORCE = 1.0e4
N_EXPERTS = 256
TOP_K = 8
N_GROUPS = 8
TOPK_GROUPS = 4
ROUTED_SCALE = 2.5
MOE_BLOCK = 256
DEPTH = 2
DEEPNORM_ALPHA = (2 * DEPTH) ** 0.25
LN_EPS = 1e-5
NEG = -1.0e30
VMEM_LIMIT = 56 * 1024 * 1024

_NT = (((1,), (1,)), ((), ()))


def _cparams(*sem):
    return pltpu.CompilerParams(dimension_semantics=sem, vmem_limit_bytes=VMEM_LIMIT)


def _split_bf16(a):
    hi = a.astype(BF16)
    lo = (a - hi.astype(F32)).astype(BF16)
    return hi, lo


def _layer_norm(y, g, b):
    mu = jnp.mean(y, axis=-1, keepdims=True)
    d = y - mu
    var = jnp.mean(d * d, axis=-1, keepdims=True)
    return d * lax.rsqrt(var + LN_EPS) * g + b


def _silu(a):
    return a * jax.nn.sigmoid(a)


TAG_BLOCK = 3


def _proj_kernel(*refs, plan, n_main, transposed, dilations, n_dil, seq):
    n_in = 4 if transposed else 3
    x_ref, w_ref, tab_ref = refs[:3]
    out_refs = refs[n_in:n_in + n_main]
    extra = list(refs[n_in + n_main:])
    x = x_ref[...].astype(BF16)
    tm = x.shape[0]
    if transposed:
        out_t = extra.pop(0)
        out_t[...] = lax.dot_general(refs[3][...], x, _NT, preferred_element_type=F32).astype(out_t.dtype)
    dil_refs = [extra.pop(0) for _ in dilations]
    stage = extra.pop(0) if dilations else None
    nblk = len(plan)
    for c0 in range(0, nblk, 2):
        nb = min(2, nblk - c0)
        acc = jnp.dot(x, w_ref[:, c0 * LANES:(c0 + nb) * LANES], preferred_element_type=F32)
        for j in range(nb):
            blk = acc[:, j * LANES:(j + 1) * LANES]
            dst, dblk, mode = plan[c0 + j]
            if mode == TAG_BLOCK:
                pos = lax.rem(pl.program_id(0) * tm, seq) + lax.broadcasted_iota(jnp.int32, (tm, LANES), 0)
                tag = lax.broadcasted_iota(jnp.int32, (tm, LANES), 1) - HEAD_DIM == jnp.right_shift(pos, SLC_SHIFT)
                mode = 1
            else:
                tag = None
            if mode:
                off = (mode - 1) * 3 * LANES
                cos = tab_ref[:, off:off + LANES]
                s_lo = tab_ref[:, off + LANES:off + 2 * LANES]
                s_hi = tab_ref[:, off + 2 * LANES:off + 3 * LANES]
                blk = (blk * cos + pltpu.roll(blk, LANES - ROT_HALF, 1) * s_lo
                       + pltpu.roll(blk, ROT_HALF, 1) * s_hi)
            if tag is not None:
                blk = jnp.where(tag, NEG, blk)
            o_ref = out_refs[dst]
            o_ref[:, dblk * LANES:(dblk + 1) * LANES] = blk.astype(o_ref.dtype)
            c = c0 + j
            if dilations and c < n_dil:
                stage[...] = blk
                for dil, d_ref in zip(dilations, dil_refs):
                    for r in range(dil):
                        col = (r * n_dil + c) * LANES
                        d_ref[:, col:col + LANES] = stage[pl.ds(r, tm // dil, stride=dil), :].astype(d_ref.dtype)


def _proj(x2, w, tabs, plan, out_cols, out_dtypes, wt=None, dilations=(), n_dil=0, seq=0, tm=512):
    t, d = x2.shape
    ncol = w.shape[1]
    out_shape = [jax.ShapeDtypeStruct((t, c), dt) for c, dt in zip(out_cols, out_dtypes)]
    once = pl.Buffered(1)
    in_specs = [pl.BlockSpec((tm, d), lambda i: (i, 0)),
                pl.BlockSpec((d, ncol), lambda i: (0, 0), pipeline_mode=once),
                pl.BlockSpec((tm, tabs.shape[1]), lambda i: (i, 0))]
    out_specs = [pl.BlockSpec((tm, c), lambda i: (i, 0)) for c in out_cols]
    args = [x2, w, tabs]
    if wt is not None:
        in_specs.append(pl.BlockSpec(wt.shape, lambda i: (0, 0), pipeline_mode=once))
        out_specs.append(pl.BlockSpec((wt.shape[0], tm), lambda i: (0, i)))
        out_shape.append(jax.ShapeDtypeStruct((wt.shape[0], t), BF16))
        args.append(wt)
    for dil in dilations:
        out_specs.append(pl.BlockSpec((tm // dil, dil * n_dil * LANES), lambda i: (i, 0)))
        out_shape.append(jax.ShapeDtypeStruct((t // dil, dil * n_dil * LANES), BF16))
    return pl.pallas_call(
        functools.partial(_proj_kernel, plan=tuple(plan), n_main=len(out_cols), transposed=wt is not None,
                          dilations=tuple(dilations), n_dil=n_dil, seq=seq),
        grid=(t // tm,),
        in_specs=in_specs,
        out_specs=out_specs,
        out_shape=out_shape,
        scratch_shapes=[pltpu.VMEM((tm, LANES), F32)] if dilations else [],
        compiler_params=_cparams("parallel"),
        name="proj_rope",
    )(*args)


def _rope_tables(positions):
    t = positions.size
    inv_freq = jnp.asarray(ROPE_THETA ** (-np.arange(0, ROT_DIM, 2) / ROT_DIM), F32)
    ang = positions.astype(F32).reshape(t, 1) * inv_freq
    trig = jnp.concatenate([jnp.cos(ang), jnp.sin(ang)], axis=1)
    place = np.zeros((2 * ROT_HALF, 6 * LANES), np.float32)
    const = np.zeros((6 * LANES,), np.float32)
    for base, heads in ((0, (0,)), (3 * LANES, (0, HEAD_DIM))):
        const[base:base + LANES] = 1.0
        for h in heads:
            for i in range(ROT_HALF):
                place[i, base + h + i] = 1.0
                place[i, base + h + ROT_HALF + i] = 1.0
                const[base + h + i] = const[base + h + ROT_HALF + i] = 0.0
                place[ROT_HALF + i, base + LANES + h + i] = -1.0
                place[ROT_HALF + i, base + 2 * LANES + h + ROT_HALF + i] = 1.0
    return jnp.dot(trig, jnp.asarray(place), precision=lax.Precision.HIGHEST) + jnp.asarray(const)


def _pad_heads_cols(w, n_heads):
    d = w.shape[0]
    w = w.reshape(d, n_heads, HEAD_DIM)
    return jnp.pad(w, ((0, 0), (0, 0), (0, LANES - HEAD_DIM))).reshape(d, n_heads * LANES)


def _pad_heads_rows(w, n_heads):
    d = w.shape[1]
    w = w.reshape(n_heads, HEAD_DIM, d)
    return jnp.pad(w, ((0, 0), (0, LANES - HEAD_DIM), (0, 0))).reshape(n_heads * LANES, d)


def _pack_head_pair(a, b):
    lane = lax.broadcasted_iota(jnp.int32, a.shape, 1)
    return jnp.where(lane < HEAD_DIM, a, pltpu.roll(b, HEAD_DIM, 1))


def _band_kernel(*refs, back, max_dist, length, qrows, heads, shared_kv, in_flight, has_sink, pack_out):
    if has_sink:
        sink_ref, q_ref, k_ref, v_ref, o_ref, bias_sc = refs
    else:
        q_ref, k_ref, v_ref, o_ref, bias_sc = refs
    nq = length // qrows
    window = qrows + back
    chains = [tuple(range(heads))] if shared_kv else [(h,) for h in range(heads)]
    stack = len(chains[0])
    lane = lax.broadcasted_iota(jnp.int32, (stack * qrows, LANES), 1)
    first_head = pl.program_id(2) * heads

    def band_bias(offset):
        row = lax.broadcasted_iota(jnp.int32, (qrows, window), 0)
        col = lax.broadcasted_iota(jnp.int32, (qrows, window), 1)
        dist = offset + row - col
        tile = jnp.where((dist >= 0) & (dist <= max_dist), 0.0, NEG)
        return jnp.concatenate([tile] * stack, axis=0) if stack > 1 else tile

    bias_sc[...] = band_bias(back)

    def block(hs, q0, k0, bias):
        kv = slice(0, LANES) if shared_kv else slice(hs[0] * LANES, (hs[0] + 1) * LANES)
        q = jnp.concatenate([q_ref[0, pl.ds(q0, qrows), h * LANES:(h + 1) * LANES] for h in hs], axis=0)
        q = q * (HEAD_DIM ** -0.5)
        k = k_ref[0, pl.ds(k0, window), kv]
        v = v_ref[0, pl.ds(k0, window), kv]
        s = lax.dot_general(q, k, _NT, preferred_element_type=F32) + bias
        m = jnp.max(s, axis=-1, keepdims=True)
        e = jnp.exp(s - m)
        den = jnp.sum(e, axis=-1, keepdims=True)
        if has_sink:
            sink = jnp.concatenate([jnp.full((qrows, 1), sink_ref[first_head + h], F32) for h in hs], axis=0)
            den = den + jnp.exp(sink - m)
        o = jnp.dot(e.astype(BF16), v, preferred_element_type=F32) / den
        lse = m + jnp.log(den)
        if pack_out:
            for i in range(0, len(hs), 2):
                pair = _pack_head_pair(o[i * qrows:(i + 1) * qrows], o[(i + 1) * qrows:(i + 2) * qrows])
                o_ref[0, pl.ds(q0, qrows), (hs[i] // 2) * LANES:(hs[i] // 2 + 1) * LANES] = pair
            return
        out = jnp.where(lane < HEAD_DIM, o, lse)
        for i, h in enumerate(hs):
            o_ref[0, pl.ds(q0, qrows), h * LANES:(h + 1) * LANES] = out[i * qrows:(i + 1) * qrows]

    n_clipped = min(-(-back // qrows), nq)
    for qi in range(n_clipped):
        bias = band_bias(qi * qrows)
        for hs in chains:
            block(hs, qi * qrows, 0, bias)

    steady = nq - n_clipped
    per_iter = max(1, min(in_flight // heads, steady))
    n_iter = steady // per_iter

    def body(it, carry):
        aligned = lambda v: v if isinstance(v, int) else pl.multiple_of(v, QBLK)
        for j in range(per_iter):
            q0 = aligned((n_clipped + it * per_iter + j) * qrows)
            for hs in chains:
                block(hs, q0, aligned(q0 - back), bias_sc[...])
        return carry

    if n_iter == 1:
        body(0, 0)
    elif n_iter:
        lax.fori_loop(0, n_iter, body, 0)
    for qi in range(n_clipped + n_iter * per_iter, nq):
        for hs in chains:
            block(hs, qi * qrows, qi * qrows - back, bias_sc[...])


def _band_attention(qkv, *, batch, seq, dilation, nblk, q_off, k_off, v_off, n_q_heads, rep, max_dist, sinks=None,
                    heads=1, in_flight=16, pack_out=False):
    length = seq // dilation
    back = min(-(-max_dist // QBLK) * QBLK, length - QBLK)
    qrows = QBLK
    shared_kv = rep > 1
    assert n_q_heads % heads == 0 and (rep == 1 or rep % heads == 0)
    assert not pack_out or (shared_kv and heads % 2 == 0)
    out_lanes = HEAD_DIM if pack_out else LANES
    kv_heads = 1 if shared_kv else heads
    arr = qkv.reshape(batch, length, dilation * nblk * LANES)
    assert q_off % heads == 0 and k_off % kv_heads == 0 and v_off % kv_heads == 0 and nblk % heads == 0
    qspec = pl.BlockSpec((1, length, heads * LANES),
                         lambda b, r, h: (b, 0, (r * nblk + q_off) // heads + h))
    kvspec = lambda off: pl.BlockSpec(
        (1, length, kv_heads * LANES),
        lambda b, r, h: (b, 0, (r * nblk + off) // kv_heads + (h * heads // rep if shared_kv else h)))
    in_specs = [qspec, kvspec(k_off), kvspec(v_off)]
    args = [arr, arr, arr]
    if sinks is not None:
        in_specs = [pl.BlockSpec(memory_space=pltpu.SMEM)] + in_specs
        args = [sinks.reshape(-1).astype(F32)] + args
    out = pl.pallas_call(
        functools.partial(_band_kernel, back=back, max_dist=max_dist, length=length, qrows=qrows, heads=heads,
                          shared_kv=shared_kv, in_flight=in_flight, has_sink=sinks is not None,
                          pack_out=pack_out),
        grid=(batch, dilation, n_q_heads // heads),
        in_specs=in_specs,
        out_specs=pl.BlockSpec((1, length, heads * out_lanes), lambda b, r, h: (b, 0, r * (n_q_heads // heads) + h)),
        out_shape=jax.ShapeDtypeStruct((batch, length, dilation * n_q_heads * out_lanes), F32),
        scratch_shapes=[pltpu.VMEM(((heads if shared_kv else 1) * qrows, qrows + back), F32)],
        compiler_params=_cparams("parallel", "parallel", "parallel"),
        name="band_attention",
    )(*args)
    return out.reshape(batch * length, dilation * n_q_heads * out_lanes)


def _outproj_even_kernel(o1_ref, o2_ref, o3_ref, ob_ref, x_ref, w_ref, g_ref, b_ref, out_ref, *nat_refs,
                         n_a, n_b, dilations):
    tm = x_ref.shape[0]
    lane = lax.broadcasted_iota(jnp.int32, (tm, LANES), 1)
    real = lane < HEAD_DIM
    pattern_refs = []
    nat_refs = list(nat_refs)
    for o_ref, dil in zip((o1_ref, o2_ref, o3_ref), dilations):
        if dil == 1:
            pattern_refs.append(o_ref)
            continue
        nat = nat_refs.pop(0)
        for r in range(dil):
            for h in range(n_a):
                col = (r * n_a + h) * LANES
                nat[h, pl.ds(r, tm // dil, stride=dil), :] = o_ref[:, col:col + LANES]
        pattern_refs.append(nat)
    parts = []
    for h in range(n_a):
        sl = slice(h * LANES, (h + 1) * LANES)
        outs = [r[:, sl] if r.ndim == 2 else r[h] for r in pattern_refs]
        lses = [jnp.where(real, pltpu.roll(a, HEAD_DIM, 1), a) for a in outs]
        m = jnp.maximum(jnp.maximum(lses[0], lses[1]), lses[2])
        es = [jnp.exp(l - m) for l in lses]
        num = es[0] * outs[0] + es[1] * outs[1] + es[2] * outs[2]
        den = es[0] + es[1] + es[2]
        parts.append(jnp.where(real, num / den, 0.0).astype(BF16))
    for h in range(n_b):
        parts.append(jnp.where(real, ob_ref[:, h * LANES:(h + 1) * LANES], 0.0).astype(BF16))
    a = jnp.concatenate(parts, axis=1)
    mixed = jnp.dot(a, w_ref[...], preferred_element_type=F32)
    y = DEEPNORM_ALPHA * x_ref[...] + mixed
    out_ref[...] = _layer_norm(y, g_ref[...], b_ref[...])


def _outproj_even(outs, dilations, ob, x2, w, g, b, tm=256):
    t, d = x2.shape
    n_a, n_b = outs[0].shape[1] // (dilations[0] * LANES), ob.shape[1] // LANES
    row = lambda c: pl.BlockSpec((tm, c), lambda i: (i, 0))
    grouped = lambda a, dil: pl.BlockSpec((tm // dil, a.shape[1]), lambda i: (i, 0))
    full = lambda a: pl.BlockSpec(a.shape, lambda i: (0,) * a.ndim)
    return pl.pallas_call(
        functools.partial(_outproj_even_kernel, n_a=n_a, n_b=n_b, dilations=tuple(dilations)),
        grid=(t // tm,),
        in_specs=[grouped(o, dil) for o, dil in zip(outs, dilations)] + [row(ob.shape[1]), row(d),
                                                                          full(w), full(g), full(b)],
        out_specs=row(d),
        out_shape=jax.ShapeDtypeStruct((t, d), F32),
        scratch_shapes=[pltpu.VMEM((n_a, tm, LANES), F32) for dil in dilations if dil > 1],
        compiler_params=_cparams("parallel"),
        name="outproj_even_ln",
    )(*outs, ob, x2, w, g, b)


def _outproj_odd_kernel(oc_ref, os_ref, ow_ref, gate_ref, e_ref, x_ref, w_ref, g_ref, b_ref, out_ref):
    gate = jax.nn.sigmoid(gate_ref[...])
    ghi, glo = _split_bf16(gate)
    acc = None
    for j, o_ref in enumerate((oc_ref, os_ref, ow_ref)):
        ej = e_ref[j]
        gfull = jnp.dot(ghi, ej, preferred_element_type=F32) + jnp.dot(glo, ej, preferred_element_type=F32)
        term = gfull * o_ref[...]
        acc = term if acc is None else acc + term
    mixed = jnp.dot(acc.astype(BF16), w_ref[...], preferred_element_type=F32)
    y = DEEPNORM_ALPHA * x_ref[...] + mixed
    out_ref[...] = _layer_norm(y, g_ref[...], b_ref[...])


def _gate_expanders(n_heads):
    e = np.zeros((3, LANES, n_heads * HEAD_DIM), np.float32)
    for j in range(3):
        for h in range(n_heads):
            e[j, 3 * h + j, h * HEAD_DIM:(h + 1) * HEAD_DIM] = 1.0
    return jnp.asarray(e, BF16)


def _outproj_odd(oc, osl, ow, gate, x2, w, g, b, tm=256):
    t, d = x2.shape
    n_heads = oc.shape[1] // HEAD_DIM
    e = _gate_expanders(n_heads)
    row = lambda c: pl.BlockSpec((tm, c), lambda i: (i, 0))
    full = lambda a: pl.BlockSpec(a.shape, lambda i: (0,) * a.ndim)
    return pl.pallas_call(
        _outproj_odd_kernel,
        grid=(t // tm,),
        in_specs=[row(oc.shape[1]), row(osl.shape[1]), row(ow.shape[1]), row(LANES), full(e), row(d),
                  full(w), full(g), full(b)],
        out_specs=row(d),
        out_shape=jax.ShapeDtypeStruct((t, d), F32),
        compiler_params=_cparams("parallel"),
        name="outproj_odd_ln",
    )(oc, osl, ow, gate, e, x2, w, g, b)


def _router_kernel(x_ref, whi_ref, wlo_ref, bias_ref, eidx_ref, gate_ref, rank_ref, cnt_ref):
    n_exp = whi_ref.shape[0]
    tm = x_ref.shape[0]
    per_group = n_exp // N_GROUPS
    xhi, xlo = _split_bf16(x_ref[...])
    whi, wlo = whi_ref[...], wlo_ref[...]
    dg = lambda a, b: lax.dot_general(a, b, _NT, preferred_element_type=F32)
    logits = dg(whi, xhi) + dg(whi, xlo) + dg(wlo, xhi)
    aff = jax.nn.sigmoid(logits)
    biased = aff + bias_ref[...]
    gio = lax.broadcasted_iota(jnp.int32, (per_group, tm), 0).astype(F32)
    blocks, scores = [], []
    for g in range(N_GROUPS):
        blk = biased[g * per_group:(g + 1) * per_group, :]
        m1 = jnp.max(blk, axis=0, keepdims=True)
        first = jnp.min(jnp.where(blk == m1, gio, float(per_group)), axis=0, keepdims=True)
        m2 = jnp.max(jnp.where(gio == first, -jnp.inf, blk), axis=0, keepdims=True)
        blocks.append(blk)
        scores.append(m1 + m2)
    masked = []
    for g in range(N_GROUPS):
        rank = jnp.zeros((1, tm), F32)
        for o in range(N_GROUPS):
            if o == g:
                continue
            beats = scores[o] >= scores[g] if o < g else scores[o] > scores[g]
            rank = rank + jnp.where(beats, 1.0, 0.0)
        masked.append(jnp.where(rank < TOPK_GROUPS, blocks[g], -jnp.inf))
    cur = jnp.concatenate(masked, axis=0)
    eio = lax.broadcasted_iota(jnp.int32, (n_exp, tm), 0).astype(F32)
    ids, gs = [], []
    for _ in range(TOP_K):
        m = jnp.max(cur, axis=0, keepdims=True)
        idx = jnp.min(jnp.where(cur == m, eio, float(n_exp)), axis=0, keepdims=True)
        hit = eio == idx
        gs.append(jnp.sum(jnp.where(hit, aff, 0.0), axis=0, keepdims=True))
        ids.append(idx)
        cur = jnp.where(hit, -jnp.inf, cur)
    gates = jnp.concatenate(gs, axis=0)
    gates = gates / jnp.sum(gates, axis=0, keepdims=True) * ROUTED_SCALE
    eidx_ref[...] = jnp.concatenate(ids, axis=0).astype(jnp.int32)
    gate_ref[...] = gates
    @pl.when(pl.program_id(0) == 0)
    def _():
        cnt_ref[...] = jnp.zeros(cnt_ref.shape, F32)

    onehot = jnp.zeros((n_exp, tm), F32)
    for idx in ids:
        onehot = onehot + jnp.where(eio == idx, 1.0, 0.0)
    earlier = jnp.where(lax.broadcasted_iota(jnp.int32, (tm, tm), 0) < lax.broadcasted_iota(jnp.int32, (tm, tm), 1),
                        1.0, 0.0).astype(BF16)
    before = cnt_ref[...] + jnp.dot(onehot.astype(BF16), earlier, preferred_element_type=F32)
    ranks = [jnp.sum(jnp.where(eio == idx, before, 0.0), axis=0, keepdims=True) for idx in ids]
    rank_ref[...] = jnp.concatenate(ranks, axis=0).astype(jnp.int32)
    cnt_ref[...] = cnt_ref[...] + jnp.sum(onehot, axis=1, keepdims=True)


def _router(x2, router_w, router_b, tm=256):
    t, d = x2.shape
    n_exp = router_w.shape[1]
    whi, wlo = _split_bf16(router_w.T)
    bias = router_b.reshape(n_exp, 1).astype(F32)
    full = lambda a: pl.BlockSpec(a.shape, lambda i: (0,) * a.ndim)
    per_tok = pl.BlockSpec((TOP_K, tm), lambda i: (0, i))
    return pl.pallas_call(
        _router_kernel,
        grid=(t // tm,),
        in_specs=[pl.BlockSpec((tm, d), lambda i: (i, 0)), full(whi), full(wlo), full(bias)],
        out_specs=[per_tok, per_tok, per_tok, pl.BlockSpec((n_exp, 1), lambda i: (0, 0))],
        out_shape=[jax.ShapeDtypeStruct((TOP_K, t), jnp.int32), jax.ShapeDtypeStruct((TOP_K, t), F32),
                   jax.ShapeDtypeStruct((TOP_K, t), jnp.int32), jax.ShapeDtypeStruct((n_exp, 1), F32)],
        compiler_params=_cparams("arbitrary"),
        name="moe_router",
    )(x2, whi, wlo, bias)


def _moe_dest_kernel(eidx_ref, rank_ref, start_ref, dest_ref):
    n_exp = start_ref.shape[0]
    tm = eidx_ref.shape[1]
    eio = lax.broadcasted_iota(jnp.int32, (n_exp, tm), 0)
    start = start_ref[...]
    rows = []
    for k in range(TOP_K):
        seg = jnp.sum(jnp.where(eio == eidx_ref[k:k + 1, :], start, 0.0), axis=0, keepdims=True)
        rows.append(seg.astype(jnp.int32) + rank_ref[k:k + 1, :])
    dest_ref[...] = jnp.concatenate(rows, axis=0)


def _moe_dest(eidx, rank, seg_start, tm=256):
    t = eidx.shape[1]
    per_tok = pl.BlockSpec((TOP_K, tm), lambda i: (0, i))
    return pl.pallas_call(
        _moe_dest_kernel,
        grid=(t // tm,),
        in_specs=[per_tok, per_tok, pl.BlockSpec(seg_start.shape, lambda i: (0, 0))],
        out_specs=per_tok,
        out_shape=jax.ShapeDtypeStruct((TOP_K, t), jnp.int32),
        compiler_params=_cparams("parallel"),
        name="moe_dest",
    )(eidx, rank, seg_start)


def _to_slabs(ref, value):
    rows, width = value.shape
    n_chunks = width // LANES
    for c in range(n_chunks):
        ref[pl.ds(c, rows, stride=n_chunks), :] = value[:, c * LANES:(c + 1) * LANES]


def _from_slabs(ref, first_row, rows, n_chunks):
    return jnp.concatenate([ref[pl.ds(first_row * n_chunks + c, rows, stride=n_chunks), :]
                            for c in range(n_chunks)], axis=1)


def _pack_bf16_pairs(x):
    half = x.shape[1] // 2
    lo = lax.bitcast_convert_type(x[:, :half].astype(BF16).astype(F32), jnp.uint32)
    hi = lax.bitcast_convert_type(x[:, half:].astype(BF16).astype(F32), jnp.uint32)
    return jnp.right_shift(lo, jnp.uint32(16)) | (hi & jnp.uint32(0xFFFF0000))


def _unpack_bf16_pairs(w):
    lo = lax.bitcast_convert_type(jnp.left_shift(w, jnp.uint32(16)), F32).astype(BF16)
    hi = lax.bitcast_convert_type(w & jnp.uint32(0xFFFF0000), F32).astype(BF16)
    return jnp.concatenate([lo, hi], axis=1)


def _moe_dispatch_kernel(zb_ref, dest_ref, x_ref, xs_out, buf, zbuf, sem, zsem):
    tm = x_ref.shape[0]
    n_chunks = buf.shape[0] // tm

    @pl.when(pl.program_id(0) == 0)
    def _():
        rows = zbuf.shape[0]
        zbuf[...] = jnp.zeros(zbuf.shape, zbuf.dtype)

        def zero_copy(e):
            first = pl.multiple_of(zb_ref[e] * rows, rows)
            return pltpu.make_async_copy(zbuf, xs_out.at[pl.ds(first, rows)], zsem)

        def start(e, c):
            @pl.when(zb_ref[e] >= 0)
            def _():
                zero_copy(e).start()
            return c

        def wait(e, c):
            @pl.when(zb_ref[e] >= 0)
            def _():
                zero_copy(e).wait()
            return c

        lax.fori_loop(0, zb_ref.shape[0], start, 0)
        lax.fori_loop(0, zb_ref.shape[0], wait, 0)

    _to_slabs(buf, _pack_bf16_pairs(x_ref[...]))

    def issue(i, c):
        src = buf.at[pl.ds(pl.multiple_of(i * n_chunks, n_chunks), n_chunks)]
        for k in range(TOP_K):
            r = pl.multiple_of(dest_ref[k, i] * n_chunks, n_chunks)
            pltpu.make_async_copy(src, xs_out.at[pl.ds(r, n_chunks)], sem).start(priority=k % 2)
        return c

    lax.fori_loop(0, tm, issue, 0)
    for _ in range(TOP_K):
        pltpu.make_async_copy(buf, xs_out.at[pl.ds(0, tm * n_chunks)], sem).wait()


def _moe_dispatch(zero_blocks, dest, x2, n_blocks, tm=256):
    t, d = x2.shape
    n_chunks = d // 2 // LANES
    grid_spec = pltpu.PrefetchScalarGridSpec(
        num_scalar_prefetch=1,
        grid=(t // tm,),
        in_specs=[pl.BlockSpec((TOP_K, tm), lambda i, zb: (0, i), memory_space=pltpu.SMEM),
                  pl.BlockSpec((tm, d), lambda i, zb: (i, 0))],
        out_specs=pl.BlockSpec(memory_space=pl.ANY),
        scratch_shapes=[pltpu.VMEM((tm * n_chunks, LANES), jnp.uint32),
                        pltpu.VMEM((MOE_BLOCK * n_chunks, LANES), jnp.uint32),
                        pltpu.SemaphoreType.DMA(()), pltpu.SemaphoreType.DMA(())],
    )
    return pl.pallas_call(
        _moe_dispatch_kernel,
        grid_spec=grid_spec,
        out_shape=jax.ShapeDtypeStruct((n_blocks * MOE_BLOCK * n_chunks, LANES), jnp.uint32),
        compiler_params=_cparams("arbitrary"),
        name="moe_dispatch",
    )(zero_blocks, dest, x2)


def _moe_ffn_kernel(be_ref, nu_ref, xs_ref, wg_ref, wu_ref, wd_ref, y_ref, wg_sc, wu_sc, wd_sc):
    b = pl.program_id(0)

    @pl.when(b < nu_ref[0])
    def _():
        @pl.when((b == 0) | (be_ref[b] != be_ref[jnp.maximum(b - 1, 0)]))
        def _():
            wg_sc[...] = wg_ref[0].astype(BF16)
            wu_sc[...] = wu_ref[0].astype(BF16)
            wd_sc[...] = wd_ref[0].astype(BF16)

        d = wg_sc.shape[0]
        xb = _unpack_bf16_pairs(_from_slabs(xs_ref, 0, MOE_BLOCK, d // 2 // LANES))
        gp = jnp.dot(xb, wg_sc[...], preferred_element_type=F32)
        up = jnp.dot(xb, wu_sc[...], preferred_element_type=F32)
        h = (_silu(gp) * up).astype(BF16)
        _to_slabs(y_ref, jnp.dot(h, wd_sc[...], preferred_element_type=F32))


def _moe_ffn(xs, block_expert, n_used, w_gate, w_up, w_down, layer):
    n_blocks = block_expert.shape[0]
    d, ff = w_gate.shape[2], w_gate.shape[3]
    last = lambda b, nu: jnp.minimum(b, nu[0] - 1)
    grid_spec = pltpu.PrefetchScalarGridSpec(
        num_scalar_prefetch=2,
        grid=(n_blocks,),
        in_specs=[
            pl.BlockSpec((MOE_BLOCK * (d // 2 // LANES), LANES), lambda b, be, nu: (last(b, nu), 0)),
            pl.BlockSpec((None, 1, d, ff), lambda b, be, nu: (layer, be[last(b, nu)], 0, 0)),
            pl.BlockSpec((None, 1, d, ff), lambda b, be, nu: (layer, be[last(b, nu)], 0, 0)),
            pl.BlockSpec((None, 1, ff, d), lambda b, be, nu: (layer, be[last(b, nu)], 0, 0)),
        ],
        out_specs=pl.BlockSpec((MOE_BLOCK * (d // LANES), LANES), lambda b, be, nu: (last(b, nu), 0)),
        scratch_shapes=[pltpu.VMEM((d, ff), BF16), pltpu.VMEM((d, ff), BF16), pltpu.VMEM((ff, d), BF16)],
    )
    return pl.pallas_call(
        _moe_ffn_kernel,
        grid_spec=grid_spec,
        out_shape=jax.ShapeDtypeStruct((n_blocks * MOE_BLOCK * (d // LANES), LANES), F32),
        compiler_params=_cparams("arbitrary"),
        name="moe_expert_ffn",
    )(block_expert, n_used, xs, w_gate, w_up, w_down)


def _moe_combine_kernel(dest_ref, y_hbm, gate_ref, x_ref, sg_ref, su_ref, sd_ref, g_ref, b_ref, out_ref, buf, sem):
    tm, d = x_ref.shape
    n_chunks = d // LANES

    def issue(i, c):
        for k in range(TOP_K):
            r = pl.multiple_of(dest_ref[k, i] * n_chunks, n_chunks)
            slot = pl.multiple_of((k * tm + i) * n_chunks, n_chunks)
            pltpu.make_async_copy(y_hbm.at[pl.ds(r, n_chunks)], buf.at[pl.ds(slot, n_chunks)],
                                  sem).start(priority=k % 2)
        return c

    lax.fori_loop(0, tm, issue, 0)
    x = x_ref[...]
    xb = x.astype(BF16)
    hs = _silu(jnp.dot(xb, sg_ref[...], preferred_element_type=F32)) * jnp.dot(xb, su_ref[...], preferred_element_type=F32)
    shared = jnp.dot(hs.astype(BF16), sd_ref[...], preferred_element_type=F32)
    pltpu.make_async_copy(y_hbm.at[pl.ds(0, TOP_K * tm * n_chunks)], buf, sem).wait()
    gates = gate_ref[...]
    routed = _from_slabs(buf, 0, tm, n_chunks) * gates[:, 0:1]
    for k in range(1, TOP_K):
        routed = routed + _from_slabs(buf, k * tm, tm, n_chunks) * gates[:, k:k + 1]
    y = DEEPNORM_ALPHA * x + (routed + shared)
    out_ref[...] = _layer_norm(y, g_ref[...], b_ref[...])


def _moe_combine(dest, y, gates_t, x2, sh_gate, sh_up, sh_down, g, b, tm=256):
    t, d = x2.shape
    row = lambda c: pl.BlockSpec((tm, c), lambda i: (i, 0))
    full = lambda a: pl.BlockSpec(a.shape, lambda i: (0,) * a.ndim)
    return pl.pallas_call(
        _moe_combine_kernel,
        grid=(t // tm,),
        in_specs=[pl.BlockSpec((TOP_K, tm), lambda i: (0, i), memory_space=pltpu.SMEM),
                  pl.BlockSpec(memory_space=pl.ANY),
                  row(TOP_K), row(d), full(sh_gate), full(sh_up), full(sh_down), full(g), full(b)],
        out_specs=row(d),
        out_shape=jax.ShapeDtypeStruct((t, d), F32),
        scratch_shapes=[pltpu.VMEM((TOP_K * tm * (d // LANES), LANES), F32), pltpu.SemaphoreType.DMA(())],
        compiler_params=_cparams("arbitrary"),
        name="moe_combine_ln",
    )(dest, y, gates_t, x2, sh_gate, sh_up, sh_down, g, b)


def _moe_segments(counts, n_tok):
    n_exp = counts.shape[0]
    n_blocks = -(-n_tok * TOP_K // MOE_BLOCK) + n_exp
    nblk = (counts.reshape(n_exp).astype(jnp.int32) + MOE_BLOCK - 1) // MOE_BLOCK
    blk_end = jnp.cumsum(nblk)
    seg_start = ((blk_end - nblk) * MOE_BLOCK).astype(F32).reshape(n_exp, 1)
    block_expert = jnp.sum((blk_end[None, :] <= jnp.arange(n_blocks)[:, None]).astype(jnp.int32), axis=1)
    block_expert = jnp.minimum(block_expert, n_exp - 1)
    n_used = blk_end[-1]
    zero_blocks = jnp.where(nblk > 0, blk_end - 1, -1).astype(jnp.int32)
    return seg_start, block_expert.astype(jnp.int32), n_used.astype(jnp.int32).reshape(1), zero_blocks, n_blocks


def _moe_layer(x2, router_w, router_b, w_gate, w_up, w_down, layer, sh_gate, sh_up, sh_down, g, b):
    t, d = x2.shape
    eidx, gates, rank, counts = _router(x2, router_w, router_b)
    seg_start, block_expert, n_used, zero_blocks, n_blocks = _moe_segments(counts, t)
    dest = _moe_dest(eidx, rank, seg_start)
    xs = _moe_dispatch(zero_blocks, dest, x2, n_blocks)
    y = _moe_ffn(xs, block_expert, n_used, w_gate, w_up, w_down, layer)
    return _moe_combine(dest, y, gates.T, x2, sh_gate.astype(BF16), sh_up.astype(BF16), sh_down.astype(BF16), g, b)


def _compress_kernel(x_ref, pa_ref, pb_ref, wa_ref, wb_ref, w2_ref, out_ref):
    x = x_ref[0]
    nc = x.shape[0]
    ha = jnp.dot((x + pa_ref[...]).astype(BF16), wa_ref[...], preferred_element_type=F32)
    hb = jnp.dot((x + pb_ref[...]).astype(BF16), wb_ref[...], preferred_element_type=F32)
    h = ha + pltpu.roll(hb, nc - 1, 0)
    h = jax.nn.gelu(h, approximate=True)
    out_ref[0] = jnp.dot(h.astype(BF16), w2_ref[...], preferred_element_type=F32).astype(out_ref.dtype)


def _compress(kc, pos, w1, w2, batch, seq):
    g = C_KV_HEADS
    nch = seq // CMP_STRIDE
    half = CMP_LEN // 2
    x = kc.reshape(batch, nch, half * g * HEAD_DIM)
    eye = jnp.eye(g, dtype=F32)
    w1r = w1.reshape(CMP_LEN, HEAD_DIM, CMP_HIDDEN)
    expand = lambda wpart: jnp.einsum('jdh,ge->jgdeh', wpart, eye).reshape(half * g * HEAD_DIM, g * CMP_HIDDEN)
    wa, wb = expand(w1r[:half]).astype(BF16), expand(w1r[half:]).astype(BF16)
    w2e = jnp.einsum('hd,ge->ghed', jnp.pad(w2, ((0, 0), (0, LANES - HEAD_DIM))), eye)
    w2e = w2e.reshape(g * CMP_HIDDEN, g * LANES).astype(BF16)
    tile_pos = lambda p: jnp.broadcast_to(p[:, None, :], (half, g, HEAD_DIM)).reshape(1, half * g * HEAD_DIM)
    pa, pb = tile_pos(pos[:half]), tile_pos(pos[half:])
    full = lambda a: pl.BlockSpec(a.shape, lambda i: (0,) * a.ndim)
    return pl.pallas_call(
        _compress_kernel,
        grid=(batch,),
        in_specs=[pl.BlockSpec((1, nch, x.shape[2]), lambda i: (i, 0, 0)), full(pa), full(pb), full(wa), full(wb),
                  full(w2e)],
        out_specs=pl.BlockSpec((1, nch, g * LANES), lambda i: (i, 0, 0)),
        out_shape=jax.ShapeDtypeStruct((batch, nch, g * LANES), BF16),
        compiler_params=_cparams("parallel"),
        name="nsa_compress",
    )(x, pa, pb, wa, wb, w2e)


def _nsa_cmp_kernel(q_ref, kc_ref, vc_ref, ovt_ref, o_ref, drop_ref, *, rep, n_sel, n_real):
    tq = q_ref.shape[1]
    nc = kc_ref.shape[1]
    nsb = ovt_ref.shape[0]
    t0 = pl.program_id(2) * tq
    scale = HEAD_DIM ** -0.5
    kc = kc_ref[0]
    vc = vc_ref[0]
    tpos = t0 + lax.broadcasted_iota(jnp.int32, (tq, nc), 0)
    cend = lax.broadcasted_iota(jnp.int32, (tq, nc), 1) * CMP_STRIDE + (CMP_LEN - 1)
    cmask = cend <= tpos
    psum = jnp.zeros((tq, nc), F32)
    outs = []
    for r in range(rep):
        q = q_ref[0, :, r * LANES:(r + 1) * LANES]
        sc = lax.dot_general(q, kc, _NT, preferred_element_type=F32) * scale
        sc = jnp.where(cmask, sc, NEG)
        m = jnp.max(sc, axis=-1, keepdims=True)
        ex = jnp.where(cmask, jnp.exp(sc - m), 0.0)
        den = jnp.sum(ex, axis=-1, keepdims=True)
        pc = ex / jnp.where(den > 0, den, 1.0)
        outs.append(jnp.dot(pc.astype(BF16), vc, preferred_element_type=F32))
        psum = psum + pc
    o_ref[0] = jnp.concatenate([_pack_head_pair(outs[r], outs[r + 1]) for r in range(0, rep, 2)], axis=1)
    phi, plo = _split_bf16(psum)
    ovt = ovt_ref[...]
    imp = (lax.dot_general(ovt, phi, _NT, preferred_element_type=F32)
           + lax.dot_general(ovt, plo, _NT, preferred_element_type=F32))
    jblk = lax.broadcasted_iota(jnp.int32, (nsb, tq), 0)
    cur = jnp.right_shift(t0 + lax.broadcasted_iota(jnp.int32, (nsb, tq), 1), SLC_SHIFT)
    forced = (jblk == 0) | (jblk == cur) | (jblk == cur - 1)
    score = jnp.where(jblk > cur, -1.0, jnp.where(forced, SELECT_FORCE, imp))
    rank = jnp.zeros((nsb, tq), F32)
    for k in range(n_real):
        rowk = score[k:k + 1, :]
        ge = jnp.where(rowk >= score, 1.0, 0.0)
        gt = jnp.where(rowk > score, 1.0, 0.0)
        rank = rank + jnp.where(jblk > k, ge, gt)
    drop = jnp.where(jblk <= cur, jnp.where(rank < n_sel, 0.0, 1.0), 1.0)
    drop = jnp.concatenate([drop, jnp.ones((LANES - nsb, tq), F32)], axis=0).T
    drop_ref[0, 0] = pltpu.roll(drop, HEAD_DIM, 1).astype(drop_ref.dtype)


def _nsa_cmp(q, kcmp, vcmp, batch, seq, q_off_blocks, tq=512):
    g = C_KV_HEADS
    rep = C_HEADS // g
    nc = kcmp.shape[1]
    nsb = seq // SLC_BLOCK
    n_sel = min(SLC_TOP_N, nsb)
    cs = np.arange(nc)[:, None] * CMP_STRIDE
    js = np.arange(nsb)[None, :] * SLC_BLOCK
    overlap = np.clip(np.minimum(cs + CMP_LEN, js + SLC_BLOCK) - np.maximum(cs, js), 0, None) / CMP_LEN
    overlap[(seq - CMP_LEN) // CMP_STRIDE + 1:] = 0.0
    nsb_pad = -(-nsb // SUBLANES) * SUBLANES
    ovt = jnp.asarray(np.pad(overlap.T, ((0, nsb_pad - nsb), (0, 0))), BF16)
    q3 = q.reshape(batch, seq, q.shape[1])
    n_real, nsb = nsb, nsb_pad
    assert nsb <= LANES - HEAD_DIM
    o, drop = pl.pallas_call(
        functools.partial(_nsa_cmp_kernel, rep=rep, n_sel=n_sel, n_real=n_real),
        grid=(batch, g, seq // tq),
        in_specs=[pl.BlockSpec((1, tq, rep * LANES), lambda b, gi, i: (b, i, q_off_blocks // rep + gi)),
                  pl.BlockSpec((1, nc, LANES), lambda b, gi, i: (b, 0, gi)),
                  pl.BlockSpec((1, nc, LANES), lambda b, gi, i: (b, 0, gi)),
                  pl.BlockSpec(ovt.shape, lambda b, gi, i: (0, 0))],
        out_specs=[pl.BlockSpec((1, tq, rep * HEAD_DIM), lambda b, gi, i: (b, i, gi)),
                   pl.BlockSpec((1, 1, tq, LANES), lambda b, gi, i: (b, gi, i, 0))],
        out_shape=[jax.ShapeDtypeStruct((batch, seq, C_HEADS * HEAD_DIM), F32),
                   jax.ShapeDtypeStruct((batch, g, seq, LANES), BF16)],
        compiler_params=_cparams("parallel", "parallel", "parallel"),
        name="nsa_compressed_select",
    )(q3, kcmp, vcmp, ovt)
    return o.reshape(batch * seq, C_HEADS * HEAD_DIM), drop


def _nsa_slc_kernel(q_ref, k_ref, vt_ref, drop_ref, o_ref, *, rep, kt):
    tq = q_ref.shape[1]
    t0 = pl.program_id(2) * tq
    n_kt = (t0 + tq + kt - 1) // kt
    upper = lax.broadcasted_iota(jnp.int32, (tq, LANES), 1) >= HEAD_DIM
    drop = drop_ref[0, 0]
    qs = [jnp.where(upper, drop, q_ref[0, :, r * LANES:(r + 1) * LANES] * (HEAD_DIM ** -0.5))
          for r in range(rep)]

    def tile(k0, carry, bias, size=kt):
        k = k_ref[0, pl.ds(k0, size), :]
        vt = vt_ref[:, pl.ds(k0, size)]
        new = []
        for qr, (m, l, acc) in zip(qs, carry):
            s = lax.dot_general(k, qr, _NT, preferred_element_type=F32)
            if bias is not None:
                s = s + bias
            m_new = jnp.maximum(m, jnp.max(s, axis=0, keepdims=True))
            e = jnp.exp(s - m_new)
            corr = jnp.exp(m - m_new)
            l = l * corr + jnp.sum(e, axis=0, keepdims=True)
            acc = acc * corr + jnp.dot(vt, e.astype(BF16), preferred_element_type=F32)
            new.append((m_new, l, acc))
        return tuple(new)

    init = tuple((jnp.full((1, tq), NEG, F32), jnp.zeros((1, tq), F32), jnp.zeros((LANES, tq), F32))
                 for _ in range(rep))
    carry = lax.fori_loop(0, n_kt - 1, lambda j, c: tile(pl.multiple_of(j * kt, kt), c, None), init)
    k_last = pl.multiple_of((n_kt - 1) * kt, kt)
    remaining = t0 + tq - k_last

    def finish(size):
        kpos = k_last + lax.broadcasted_iota(jnp.int32, (size, tq), 0)
        tpos = t0 + lax.broadcasted_iota(jnp.int32, (size, tq), 1)
        final = tile(k_last, carry, jnp.where(kpos <= tpos, 0.0, NEG), size)
        outs = [(acc / l).T for _, l, acc in final]
        o_ref[0] = jnp.concatenate([_pack_head_pair(outs[r], outs[r + 1]) for r in range(0, rep, 2)], axis=1)

    sizes = [s for s in (kt // 4, kt // 2, 3 * kt // 4) if s >= tq and s % LANES == 0] + [kt]
    lower = 0
    for size in sizes:
        pl.when((remaining > lower) & (remaining <= size))(functools.partial(finish, size))
        lower = size


def _nsa_slc(qkv, vt, drop, batch, seq, nblk, q_off, k_off, tq=128, kt=1024):
    g = C_KV_HEADS
    rep = C_HEADS // g
    kt = min(kt, seq)
    arr = qkv.reshape(batch, seq, nblk * LANES)
    out = pl.pallas_call(
        functools.partial(_nsa_slc_kernel, rep=rep, kt=kt),
        grid=(batch, g, seq // tq),
        in_specs=[pl.BlockSpec((1, tq, rep * LANES), lambda b, gi, i: (b, i, q_off // rep + gi)),
                  pl.BlockSpec((1, seq, LANES), lambda b, gi, i: (b, 0, k_off + gi)),
                  pl.BlockSpec((LANES, seq), lambda b, gi, i: (gi, b)),
                  pl.BlockSpec((1, 1, tq, LANES), lambda b, gi, i: (b, gi, i, 0))],
        out_specs=pl.BlockSpec((1, tq, rep * HEAD_DIM), lambda b, gi, i: (b, i, gi)),
        out_shape=jax.ShapeDtypeStruct((batch, seq, C_HEADS * HEAD_DIM), F32),
        compiler_params=_cparams("parallel", "parallel", "parallel"),
        name="nsa_selected",
    )(arr, arr, vt, drop)
    return out.reshape(batch * seq, C_HEADS * HEAD_DIM)


def _even_mixer_layer(x2, batch, seq, tabs, w_in, sinks, w_out, g, b):
    d = x2.shape[1]
    n_heads_in = 3 * A_HEADS + B_Q_HEADS + 2 * B_KV_HEADS
    w = _pad_heads_cols(w_in, n_heads_in).astype(BF16)
    rope = [1] * (2 * A_HEADS) + [0] * A_HEADS + [1] * B_Q_HEADS + [1] * B_KV_HEADS + [0] * B_KV_HEADS
    plan = [(0, c, rope[c]) for c in range(n_heads_in)]
    dilations = [dil for _, dil in A_PATTERNS]
    regroup = [dil for dil in dilations if dil > 1]
    n_a_blocks = 3 * A_HEADS
    qkv, *grouped = _proj(x2, w, tabs, plan, [n_heads_in * LANES], [BF16], dilations=regroup, n_dil=n_a_blocks)
    outs = []
    for window, dilation in A_PATTERNS:
        src, nblk = (qkv, n_heads_in) if dilation == 1 else (grouped[regroup.index(dilation)], n_a_blocks)
        outs.append(_band_attention(src, batch=batch, seq=seq, dilation=dilation, nblk=nblk, q_off=0,
                                    k_off=A_HEADS, v_off=2 * A_HEADS, n_q_heads=A_HEADS, rep=1,
                                    max_dist=window // dilation,
                                    heads=min(4, dilation)))
    qb_off = 3 * A_HEADS
    ob = _band_attention(qkv, batch=batch, seq=seq, dilation=1, nblk=n_heads_in, q_off=qb_off,
                         k_off=qb_off + B_Q_HEADS, v_off=qb_off + B_Q_HEADS + B_KV_HEADS, n_q_heads=B_Q_HEADS,
                         rep=B_Q_HEADS // B_KV_HEADS, max_dist=B_WINDOW - 1, sinks=sinks)
    w_o = _pad_heads_rows(w_out, A_HEADS + B_Q_HEADS).astype(BF16)
    return _outproj_even(outs, dilations, ob, x2, w_o, g.reshape(1, d), b.reshape(1, d))


def _odd_mixer_layer(x2, batch, seq, tabs, w_in, cmpk_pos, cmpk_w1, cmpk_w2, cmpv_pos, cmpv_w1, cmpv_w2, w_out, g, b):
    d = x2.shape[1]
    kvw = C_KV_HEADS * HEAD_DIM
    qw = C_HEADS * HEAD_DIM
    sizes = [qw] + [kvw] * 6 + [3 * C_HEADS]
    offs = np.concatenate([[0], np.cumsum(sizes)])
    wq, wkc, wvc, wks, wvs, wkw, wvw, wgt = [w_in[:, offs[i]:offs[i + 1]] for i in range(8)]
    ph = lambda wpart, n: _pad_heads_cols(wpart, n)
    w = jnp.concatenate([ph(wq, C_HEADS), ph(wks, C_KV_HEADS), ph(wkw, C_KV_HEADS), ph(wvw, C_KV_HEADS),
                         wkc, wvc, jnp.pad(wgt, ((0, 0), (0, LANES - 3 * C_HEADS)))], axis=1).astype(BF16)
    wvs_t = ph(wvs, C_KV_HEADS).T.astype(BF16)
    n16 = C_HEADS + 3 * C_KV_HEADS
    rope16 = [1] * C_HEADS + [TAG_BLOCK] * C_KV_HEADS + [1] * C_KV_HEADS + [0] * C_KV_HEADS
    n_kc = kvw // LANES
    plan = ([(0, c, rope16[c]) for c in range(n16)] + [(1, c, 2) for c in range(n_kc)]
            + [(2, c, 0) for c in range(n_kc)] + [(3, 0, 0)])
    qkv, kc, vc, gate, vs_t = _proj(x2, w, tabs, plan, [n16 * LANES, kvw, kvw, LANES], [BF16, F32, F32, F32],
                                    wt=wvs_t, seq=seq)
    kcmp = _compress(kc, cmpk_pos, cmpk_w1, cmpk_w2, batch, seq)
    vcmp = _compress(vc, cmpv_pos, cmpv_w1, cmpv_w2, batch, seq)
    o_cmp, drop = _nsa_cmp(qkv, kcmp, vcmp, batch, seq, 0)
    ks_off = C_HEADS
    o_slc = _nsa_slc(qkv, vs_t, drop, batch, seq, n16, 0, ks_off)
    kw_off = ks_off + C_KV_HEADS
    o_win = _band_attention(qkv, batch=batch, seq=seq, dilation=1, nblk=n16, q_off=0, k_off=kw_off,
                            v_off=kw_off + C_KV_HEADS, n_q_heads=C_HEADS, rep=C_HEADS // C_KV_HEADS,
                            max_dist=NSA_WINDOW - 1, heads=C_HEADS // C_KV_HEADS, pack_out=True)
    return _outproj_odd(o_cmp, o_slc, o_win, gate, x2, w_out.astype(BF16), g.reshape(1, d), b.reshape(1, d))


def kernel(x, positions, even_w_in, even_sinks, even_w_out, odd_w_in, odd_cmpk_pos, odd_cmpk_w1, odd_cmpk_w2, odd_cmpv_pos, odd_cmpv_w1, odd_cmpv_w2, odd_w_out, mix_ln_g, mix_ln_b, moe_router_w, moe_router_b, moe_w_gate, moe_w_up, moe_w_down, moe_sh_gate, moe_sh_up, moe_sh_down, ffn_ln_g, ffn_ln_b):
    batch, seq, d = x.shape
    x2 = x.reshape(batch * seq, d)
    tabs = _rope_tables(positions)
    depth = mix_ln_g.shape[0]
    for layer in range(depth):
        j = layer // 2
        if layer % 2 == 0:
            x2 = _even_mixer_layer(x2, batch, seq, tabs, even_w_in[j], even_sinks[j], even_w_out[j],
                                   mix_ln_g[layer], mix_ln_b[layer])
        else:
            x2 = _odd_mixer_layer(x2, batch, seq, tabs, odd_w_in[j], odd_cmpk_pos[j], odd_cmpk_w1[j], odd_cmpk_w2[j],
                                  odd_cmpv_pos[j], odd_cmpv_w1[j], odd_cmpv_w2[j], odd_w_out[j],
                                  mix_ln_g[layer], mix_ln_b[layer])
        x2 = _moe_layer(x2, moe_router_w[layer], moe_router_b[layer], moe_w_gate, moe_w_up, moe_w_down, layer,
                        moe_sh_gate[layer], moe_sh_up[layer], moe_sh_down[layer],
                        ffn_ln_g[layer].reshape(1, d), ffn_ln_b[layer].reshape(1, d))
    return x2.reshape(batch, seq, d)
```

```python
import functools

import numpy as np
import jax
import jax.numpy as jnp
from jax import lax
from jax.experimental import pallas as pl
from jax.experimental.pallas import tpu as pltpu

F32 = jnp.float32
BF16 = jnp.bfloat16

LANES = 128
SUBLANES = 8
HEAD_DIM = 64
ROT_DIM = HEAD_DIM // 4
ROT_HALF = ROT_DIM // 2
ROPE_THETA = 500000.0
QBLK = 128
A_HEADS = 8
A_PATTERNS = ((128, 1), (512, 4), (2048, 16))
B_Q_HEADS = 8
B_KV_HEADS = 2
B_WINDOW = 128
C_HEADS = 16
C_KV_HEADS = 4
CMP_LEN = 32
CMP_STRIDE = 16
CMP_HIDDEN = 2 * HEAD_DIM
SLC_BLOCK = 64
SLC_SHIFT = 6
SLC_TOP_N = 16
NSA_WINDOW = 512
SELECT_FORCE = 1.0e4
N_EXPERTS = 256
TOP_K = 8
N_GROUPS = 8
TOPK_GROUPS = 4
ROUTED_SCALE = 2.5
MOE_BLOCK = 256
DEPTH = 2
DEEPNORM_ALPHA = (2 * DEPTH) ** 0.25
LN_EPS = 1e-5
NEG = -1.0e30
VMEM_LIMIT = 56 * 1024 * 1024

_NT = (((1,), (1,)), ((), ()))


def _cparams(*sem):
    return pltpu.CompilerParams(dimension_semantics=sem, vmem_limit_bytes=VMEM_LIMIT)


def _split_bf16(a):
    hi = a.astype(BF16)
    lo = (a - hi.astype(F32)).astype(BF16)
    return hi, lo


def _layer_norm(y, g, b):
    mu = jnp.mean(y, axis=-1, keepdims=True)
    d = y - mu
    var = jnp.mean(d * d, axis=-1, keepdims=True)
    return d * lax.rsqrt(var + LN_EPS) * g + b


def _silu(a):
    return a * jax.nn.sigmoid(a)


TAG_BLOCK = 3


def _proj_kernel(*refs, plan, n_main, transposed, dilations, n_dil, seq):
    n_in = 4 if transposed else 3
    x_ref, w_ref, tab_ref = refs[:3]
    out_refs = refs[n_in:n_in + n_main]
    extra = list(refs[n_in + n_main:])
    x = x_ref[...].astype(BF16)
    tm = x.shape[0]
    if transposed:
        out_t = extra.pop(0)
        out_t[...] = lax.dot_general(refs[3][...], x, _NT, preferred_element_type=F32).astype(out_t.dtype)
    dil_refs = [extra.pop(0) for _ in dilations]
    stage = extra.pop(0) if dilations else None
    nblk = len(plan)
    for c0 in range(0, nblk, 2):
        nb = min(2, nblk - c0)
        acc = jnp.dot(x, w_ref[:, c0 * LANES:(c0 + nb) * LANES], preferred_element_type=F32)
        for j in range(nb):
            blk = acc[:, j * LANES:(j + 1) * LANES]
            dst, dblk, mode = plan[c0 + j]
            if mode == TAG_BLOCK:
                pos = lax.rem(pl.program_id(0) * tm, seq) + lax.broadcasted_iota(jnp.int32, (tm, LANES), 0)
                tag = lax.broadcasted_iota(jnp.int32, (tm, LANES), 1) - HEAD_DIM == jnp.right_shift(pos, SLC_SHIFT)
                mode = 1
            else:
                tag = None
            if mode:
                off = (mode - 1) * 3 * LANES
                cos = tab_ref[:, off:off + LANES]
                s_lo = tab_ref[:, off + LANES:off + 2 * LANES]
                s_hi = tab_ref[:, off + 2 * LANES:off + 3 * LANES]
                blk = (blk * cos + pltpu.roll(blk, LANES - ROT_HALF, 1) * s_lo
                       + pltpu.roll(blk, ROT_HALF, 1) * s_hi)
            if tag is not None:
                blk = jnp.where(tag, NEG, blk)
            o_ref = out_refs[dst]
            o_ref[:, dblk * LANES:(dblk + 1) * LANES] = blk.astype(o_ref.dtype)
            c = c0 + j
            if dilations and c < n_dil:
                stage[...] = blk
                for dil, d_ref in zip(dilations, dil_refs):
                    for r in range(dil):
                        col = (r * n_dil + c) * LANES
                        d_ref[:, col:col + LANES] = stage[pl.ds(r, tm // dil, stride=dil), :].astype(d_ref.dtype)


def _proj(x2, w, tabs, plan, out_cols, out_dtypes, wt=None, dilations=(), n_dil=0, seq=0, tm=512):
    t, d = x2.shape
    ncol = w.shape[1]
    out_shape = [jax.ShapeDtypeStruct((t, c), dt) for c, dt in zip(out_cols, out_dtypes)]
    once = pl.Buffered(1)
    in_specs = [pl.BlockSpec((tm, d), lambda i: (i, 0)),
                pl.BlockSpec((d, ncol), lambda i: (0, 0), pipeline_mode=once),
                pl.BlockSpec((tm, tabs.shape[1]), lambda i: (i, 0))]
    out_specs = [pl.BlockSpec((tm, c), lambda i: (i, 0)) for c in out_cols]
    args = [x2, w, tabs]
    if wt is not None:
        in_specs.append(pl.BlockSpec(wt.shape, lambda i: (0, 0), pipeline_mode=once))
        out_specs.append(pl.BlockSpec((wt.shape[0], tm), lambda i: (0, i)))
        out_shape.append(jax.ShapeDtypeStruct((wt.shape[0], t), BF16))
        args.append(wt)
    for dil in dilations:
        out_specs.append(pl.BlockSpec((tm // dil, dil * n_dil * LANES), lambda i: (i, 0)))
        out_shape.append(jax.ShapeDtypeStruct((t // dil, dil * n_dil * LANES), BF16))
    return pl.pallas_call(
        functools.partial(_proj_kernel, plan=tuple(plan), n_main=len(out_cols), transposed=wt is not None,
                          dilations=tuple(dilations), n_dil=n_dil, seq=seq),
        grid=(t // tm,),
        in_specs=in_specs,
        out_specs=out_specs,
        out_shape=out_shape,
        scratch_shapes=[pltpu.VMEM((tm, LANES), F32)] if dilations else [],
        compiler_params=_cparams("parallel"),
        name="proj_rope",
    )(*args)


def _rope_tables(positions):
    t = positions.size
    inv_freq = jnp.asarray(ROPE_THETA ** (-np.arange(0, ROT_DIM, 2) / ROT_DIM), F32)
    ang = positions.astype(F32).reshape(t, 1) * inv_freq
    trig = jnp.concatenate([jnp.cos(ang), jnp.sin(ang)], axis=1)
    place = np.zeros((2 * ROT_HALF, 6 * LANES), np.float32)
    const = np.zeros((6 * LANES,), np.float32)
    for base, heads in ((0, (0,)), (3 * LANES, (0, HEAD_DIM))):
        const[base:base + LANES] = 1.0
        for h in heads:
            for i in range(ROT_HALF):
                place[i, base + h + i] = 1.0
                place[i, base + h + ROT_HALF + i] = 1.0
                const[base + h + i] = const[base + h + ROT_HALF + i] = 0.0
                place[ROT_HALF + i, base + LANES + h + i] = -1.0
                place[ROT_HALF + i, base + 2 * LANES + h + ROT_HALF + i] = 1.0
    return jnp.dot(trig, jnp.asarray(place), precision=lax.Precision.HIGHEST) + jnp.asarray(const)


def _pad_heads_cols(w, n_heads):
    d = w.shape[0]
    w = w.reshape(d, n_heads, HEAD_DIM)
    return jnp.pad(w, ((0, 0), (0, 0), (0, LANES - HEAD_DIM))).reshape(d, n_heads * LANES)


def _pad_heads_rows(w, n_heads):
    d = w.shape[1]
    w = w.reshape(n_heads, HEAD_DIM, d)
    return jnp.pad(w, ((0, 0), (0, LANES - HEAD_DIM), (0, 0))).reshape(n_heads * LANES, d)


def _pack_head_pair(a, b):
    lane = lax.broadcasted_iota(jnp.int32, a.shape, 1)
    return jnp.where(lane < HEAD_DIM, a, pltpu.roll(b, HEAD_DIM, 1))


def _band_kernel(*refs, back, max_dist, length, qrows, heads, shared_kv, in_flight, has_sink, pack_out):
    if has_sink:
        sink_ref, q_ref, k_ref, v_ref, o_ref, bias_sc = refs
    else:
        q_ref, k_ref, v_ref, o_ref, bias_sc = refs
    nq = length // qrows
    window = qrows + back
    chains = [tuple(range(heads))] if shared_kv else [(h,) for h in range(heads)]
    stack = len(chains[0])
    lane = lax.broadcasted_iota(jnp.int32, (stack * qrows, LANES), 1)
    first_head = pl.program_id(2) * heads

    def band_bias(offset):
        row = lax.broadcasted_iota(jnp.int32, (qrows, window), 0)
        col = lax.broadcasted_iota(jnp.int32, (qrows, window), 1)
        dist = offset + row - col
        tile = jnp.where((dist >= 0) & (dist <= max_dist), 0.0, NEG)
        return jnp.concatenate([tile] * stack, axis=0) if stack > 1 else tile

    bias_sc[...] = band_bias(back)

    def block(hs, q0, k0, bias):
        kv = slice(0, LANES) if shared_kv else slice(hs[0] * LANES, (hs[0] + 1) * LANES)
        q = jnp.concatenate([q_ref[0, pl.ds(q0, qrows), h * LANES:(h + 1) * LANES] for h in hs], axis=0)
        q = q * (HEAD_DIM ** -0.5)
        k = k_ref[0, pl.ds(k0, window), kv]
        v = v_ref[0, pl.ds(k0, window), kv]
        s = lax.dot_general(q, k, _NT, preferred_element_type=F32) + bias
        m = jnp.max(s, axis=-1, keepdims=True)
        e = jnp.exp(s - m)
        den = jnp.sum(e, axis=-1, keepdims=True)
        if has_sink:
            sink = jnp.concatenate([jnp.full((qrows, 1), sink_ref[first_head + h], F32) for h in hs], axis=0)
            den = den + jnp.exp(sink - m)
        o = jnp.dot(e.astype(BF16), v, preferred_element_type=F32) / den
        lse = m + jnp.log(den)
        if pack_out:
            for i in range(0, len(hs), 2):
                pair = _pack_head_pair(o[i * qrows:(i + 1) * qrows], o[(i + 1) * qrows:(i + 2) * qrows])
                o_ref[0, pl.ds(q0, qrows), (hs[i] // 2) * LANES:(hs[i] // 2 + 1) * LANES] = pair
            return
        out = jnp.where(lane < HEAD_DIM, o, lse)
        for i, h in enumerate(hs):
            o_ref[0, pl.ds(q0, qrows), h * LANES:(h + 1) * LANES] = out[i * qrows:(i + 1) * qrows]

    n_clipped = min(-(-back // qrows), nq)
    for qi in range(n_clipped):
        bias = band_bias(qi * qrows)
        for hs in chains:
            block(hs, qi * qrows, 0, bias)

    steady = nq - n_clipped
    per_iter = max(1, min(in_flight // heads, steady))
    n_iter = steady // per_iter

    def body(it, carry):
        aligned = lambda v: v if isinstance(v, int) else pl.multiple_of(v, QBLK)
        for j in range(per_iter):
            q0 = aligned((n_clipped + it * per_iter + j) * qrows)
            for hs in chains:
                block(hs, q0, aligned(q0 - back), bias_sc[...])
        return carry

    if n_iter == 1:
        body(0, 0)
    elif n_iter:
        lax.fori_loop(0, n_iter, body, 0)
    for qi in range(n_clipped + n_iter * per_iter, nq):
        for hs in chains:
            block(hs, qi * qrows, qi * qrows - back, bias_sc[...])


def _band_attention(qkv, *, batch, seq, dilation, nblk, q_off, k_off, v_off, n_q_heads, rep, max_dist, sinks=None,
                    heads=1, in_flight=16, pack_out=False):
    length = seq // dilation
    back = min(-(-max_dist // QBLK) * QBLK, length - QBLK)
    qrows = QBLK
    shared_kv = rep > 1
    assert n_q_heads % heads == 0 and (rep == 1 or rep % heads == 0)
    assert not pack_out or (shared_kv and heads % 2 == 0)
    out_lanes = HEAD_DIM if pack_out else LANES
    kv_heads = 1 if shared_kv else heads
    arr = qkv.reshape(batch, length, dilation * nblk * LANES)
    assert q_off % heads == 0 and k_off % kv_heads == 0 and v_off % kv_heads == 0 and nblk % heads == 0
    qspec = pl.BlockSpec((1, length, heads * LANES),
                         lambda b, r, h: (b, 0, (r * nblk + q_off) // heads + h))
    kvspec = lambda off: pl.BlockSpec(
        (1, length, kv_heads * LANES),
        lambda b, r, h: (b, 0, (r * nblk + off) // kv_heads + (h * heads // rep if shared_kv else h)))
    in_specs = [qspec, kvspec(k_off), kvspec(v_off)]
    args = [arr, arr, arr]
    if sinks is not None:
        in_specs = [pl.BlockSpec(memory_space=pltpu.SMEM)] + in_specs
        args = [sinks.reshape(-1).astype(F32)] + args
    out = pl.pallas_call(
        functools.partial(_band_kernel, back=back, max_dist=max_dist, length=length, qrows=qrows, heads=heads,
                          shared_kv=shared_kv, in_flight=in_flight, has_sink=sinks is not None,
                          pack_out=pack_out),
        grid=(batch, dilation, n_q_heads // heads),
        in_specs=in_specs,
        out_specs=pl.BlockSpec((1, length, heads * out_lanes), lambda b, r, h: (b, 0, r * (n_q_heads // heads) + h)),
        out_shape=jax.ShapeDtypeStruct((batch, length, dilation * n_q_heads * out_lanes), F32),
        scratch_shapes=[pltpu.VMEM(((heads if shared_kv else 1) * qrows, qrows + back), F32)],
        compiler_params=_cparams("parallel", "parallel", "parallel"),
        name="band_attention",
    )(*args)
    return out.reshape(batch * length, dilation * n_q_heads * out_lanes)


def _outproj_even_kernel(o1_ref, o2_ref, o3_ref, ob_ref, x_ref, w_ref, g_ref, b_ref, out_ref, *nat_refs,
                         n_a, n_b, dilations):
    tm = x_ref.shape[0]
    lane = lax.broadcasted_iota(jnp.int32, (tm, LANES), 1)
    real = lane < HEAD_DIM
    pattern_refs = []
    nat_refs = list(nat_refs)
    for o_ref, dil in zip((o1_ref, o2_ref, o3_ref), dilations):
        if dil == 1:
            pattern_refs.append(o_ref)
            continue
        nat = nat_refs.pop(0)
        for r in range(dil):
            for h in range(n_a):
                col = (r * n_a + h) * LANES
                nat[h, pl.ds(r, tm // dil, stride=dil), :] = o_ref[:, col:col + LANES]
        pattern_refs.append(nat)
    parts = []
    for h in range(n_a):
        sl = slice(h * LANES, (h + 1) * LANES)
        outs = [r[:, sl] if r.ndim == 2 else r[h] for r in pattern_refs]
        lses = [jnp.where(real, pltpu.roll(a, HEAD_DIM, 1), a) for a in outs]
        m = jnp.maximum(jnp.maximum(lses[0], lses[1]), lses[2])
        es = [jnp.exp(l - m) for l in lses]
        num = es[0] * outs[0] + es[1] * outs[1] + es[2] * outs[2]
        den = es[0] + es[1] + es[2]
        parts.append(jnp.where(real, num / den, 0.0).astype(BF16))
    for h in range(n_b):
        parts.append(jnp.where(real, ob_ref[:, h * LANES:(h + 1) * LANES], 0.0).astype(BF16))
    a = jnp.concatenate(parts, axis=1)
    mixed = jnp.dot(a, w_ref[...], preferred_element_type=F32)
    y = DEEPNORM_ALPHA * x_ref[...] + mixed
    out_ref[...] = _layer_norm(y, g_ref[...], b_ref[...])


def _outproj_even(outs, dilations, ob, x2, w, g, b, tm=256):
    t, d = x2.shape
    n_a, n_b = outs[0].shape[1] // (dilations[0] * LANES), ob.shape[1] // LANES
    row = lambda c: pl.BlockSpec((tm, c), lambda i: (i, 0))
    grouped = lambda a, dil: pl.BlockSpec((tm // dil, a.shape[1]), lambda i: (i, 0))
    full = lambda a: pl.BlockSpec(a.shape, lambda i: (0,) * a.ndim)
    return pl.pallas_call(
        functools.partial(_outproj_even_kernel, n_a=n_a, n_b=n_b, dilations=tuple(dilations)),
        grid=(t // tm,),
        in_specs=[grouped(o, dil) for o, dil in zip(outs, dilations)] + [row(ob.shape[1]), row(d),
                                                                          full(w), full(g), full(b)],
        out_specs=row(d),
        out_shape=jax.ShapeDtypeStruct((t, d), F32),
        scratch_shapes=[pltpu.VMEM((n_a, tm, LANES), F32) for dil in dilations if dil > 1],
        compiler_params=_cparams("parallel"),
        name="outproj_even_ln",
    )(*outs, ob, x2, w, g, b)


def _outproj_odd_kernel(oc_ref, os_ref, ow_ref, gate_ref, e_ref, x_ref, w_ref, g_ref, b_ref, out_ref):
    gate = jax.nn.sigmoid(gate_ref[...])
    ghi, glo = _split_bf16(gate)
    acc = None
    for j, o_ref in enumerate((oc_ref, os_ref, ow_ref)):
        ej = e_ref[j]
        gfull = jnp.dot(ghi, ej, preferred_element_type=F32) + jnp.dot(glo, ej, preferred_element_type=F32)
        term = gfull * o_ref[...]
        acc = term if acc is None else acc + term
    mixed = jnp.dot(acc.astype(BF16), w_ref[...], preferred_element_type=F32)
    y = DEEPNORM_ALPHA * x_ref[...] + mixed
    out_ref[...] = _layer_norm(y, g_ref[...], b_ref[...])


def _gate_expanders(n_heads):
    e = np.zeros((3, LANES, n_heads * HEAD_DIM), np.float32)
    for j in range(3):
        for h in range(n_heads):
            e[j, 3 * h + j, h * HEAD_DIM:(h + 1) * HEAD_DIM] = 1.0
    return jnp.asarray(e, BF16)


def _outproj_odd(oc, osl, ow, gate, x2, w, g, b, tm=512):
    t, d = x2.shape
    n_heads = oc.shape[1] // HEAD_DIM
    e = _gate_expanders(n_heads)
    row = lambda c: pl.BlockSpec((tm, c), lambda i: (i, 0))
    full = lambda a: pl.BlockSpec(a.shape, lambda i: (0,) * a.ndim)
    return pl.pallas_call(
        _outproj_odd_kernel,
        grid=(t // tm,),
        in_specs=[row(oc.shape[1]), row(osl.shape[1]), row(ow.shape[1]), row(LANES), full(e), row(d),
                  full(w), full(g), full(b)],
        out_specs=row(d),
        out_shape=jax.ShapeDtypeStruct((t, d), F32),
        compiler_params=_cparams("parallel"),
        name="outproj_odd_ln",
    )(oc, osl, ow, gate, e, x2, w, g, b)


def _router_kernel(x_ref, whi_ref, wlo_ref, bias_ref, eidx_ref, gate_ref, rank_ref, cnt_ref):
    n_exp = whi_ref.shape[0]
    tm = x_ref.shape[0]
    per_group = n_exp // N_GROUPS
    xhi, xlo = _split_bf16(x_ref[...])
    whi, wlo = whi_ref[...], wlo_ref[...]
    dg = lambda a, b: lax.dot_general(a, b, _NT, preferred_element_type=F32)
    logits = dg(whi, xhi) + dg(whi, xlo) + dg(wlo, xhi)
    aff = jax.nn.sigmoid(logits)
    biased = aff + bias_ref[...]
    gio = lax.broadcasted_iota(jnp.int32, (per_group, tm), 0).astype(F32)
    blocks, scores = [], []
    for g in range(N_GROUPS):
        blk = biased[g * per_group:(g + 1) * per_group, :]
        m1 = jnp.max(blk, axis=0, keepdims=True)
        first = jnp.min(jnp.where(blk == m1, gio, float(per_group)), axis=0, keepdims=True)
        m2 = jnp.max(jnp.where(gio == first, -jnp.inf, blk), axis=0, keepdims=True)
        blocks.append(blk)
        scores.append(m1 + m2)
    masked = []
    for g in range(N_GROUPS):
        rank = jnp.zeros((1, tm), F32)
        for o in range(N_GROUPS):
            if o == g:
                continue
            beats = scores[o] >= scores[g] if o < g else scores[o] > scores[g]
            rank = rank + jnp.where(beats, 1.0, 0.0)
        masked.append(jnp.where(rank < TOPK_GROUPS, blocks[g], -jnp.inf))
    cur = jnp.concatenate(masked, axis=0)
    eio = lax.broadcasted_iota(jnp.int32, (n_exp, tm), 0).astype(F32)
    ids, gs = [], []
    for _ in range(TOP_K):
        m = jnp.max(cur, axis=0, keepdims=True)
        idx = jnp.min(jnp.where(cur == m, eio, float(n_exp)), axis=0, keepdims=True)
        hit = eio == idx
        gs.append(jnp.sum(jnp.where(hit, aff, 0.0), axis=0, keepdims=True))
        ids.append(idx)
        cur = jnp.where(hit, -jnp.inf, cur)
    gates = jnp.concatenate(gs, axis=0)
    gates = gates / jnp.sum(gates, axis=0, keepdims=True) * ROUTED_SCALE
    eidx_ref[...] = jnp.concatenate(ids, axis=0).astype(jnp.int32)
    gate_ref[...] = gates
    @pl.when(pl.program_id(0) == 0)
    def _():
        cnt_ref[...] = jnp.zeros(cnt_ref.shape, F32)

    onehot = jnp.zeros((n_exp, tm), F32)
    for idx in ids:
        onehot = onehot + jnp.where(eio == idx, 1.0, 0.0)
    earlier = jnp.where(lax.broadcasted_iota(jnp.int32, (tm, tm), 0) < lax.broadcasted_iota(jnp.int32, (tm, tm), 1),
                        1.0, 0.0).astype(BF16)
    before = cnt_ref[...] + jnp.dot(onehot.astype(BF16), earlier, preferred_element_type=F32)
    ranks = [jnp.sum(jnp.where(eio == idx, before, 0.0), axis=0, keepdims=True) for idx in ids]
    rank_ref[...] = jnp.concatenate(ranks, axis=0).astype(jnp.int32)
    cnt_ref[...] = cnt_ref[...] + jnp.sum(onehot, axis=1, keepdims=True)


def _router(x2, router_w, router_b, tm=512):
    t, d = x2.shape
    n_exp = router_w.shape[1]
    whi, wlo = _split_bf16(router_w.T)
    bias = router_b.reshape(n_exp, 1).astype(F32)
    full = lambda a: pl.BlockSpec(a.shape, lambda i: (0,) * a.ndim)
    per_tok = pl.BlockSpec((TOP_K, tm), lambda i: (0, i))
    return pl.pallas_call(
        _router_kernel,
        grid=(t // tm,),
        in_specs=[pl.BlockSpec((tm, d), lambda i: (i, 0)), full(whi), full(wlo), full(bias)],
        out_specs=[per_tok, per_tok, per_tok, pl.BlockSpec((n_exp, 1), lambda i: (0, 0))],
        out_shape=[jax.ShapeDtypeStruct((TOP_K, t), jnp.int32), jax.ShapeDtypeStruct((TOP_K, t), F32),
                   jax.ShapeDtypeStruct((TOP_K, t), jnp.int32), jax.ShapeDtypeStruct((n_exp, 1), F32)],
        compiler_params=_cparams("arbitrary"),
        name="moe_router",
    )(x2, whi, wlo, bias)


def _moe_dest_kernel(eidx_ref, rank_ref, start_ref, dest_ref):
    n_exp = start_ref.shape[0]
    tm = eidx_ref.shape[1]
    eio = lax.broadcasted_iota(jnp.int32, (n_exp, tm), 0)
    start = start_ref[...]
    rows = []
    for k in range(TOP_K):
        seg = jnp.sum(jnp.where(eio == eidx_ref[k:k + 1, :], start, 0.0), axis=0, keepdims=True)
        rows.append(seg.astype(jnp.int32) + rank_ref[k:k + 1, :])
    dest_ref[...] = jnp.concatenate(rows, axis=0)


def _moe_dest(eidx, rank, seg_start, tm=256):
    t = eidx.shape[1]
    per_tok = pl.BlockSpec((TOP_K, tm), lambda i: (0, i))
    return pl.pallas_call(
        _moe_dest_kernel,
        grid=(t // tm,),
        in_specs=[per_tok, per_tok, pl.BlockSpec(seg_start.shape, lambda i: (0, 0))],
        out_specs=per_tok,
        out_shape=jax.ShapeDtypeStruct((TOP_K, t), jnp.int32),
        compiler_params=_cparams("parallel"),
        name="moe_dest",
    )(eidx, rank, seg_start)


def _to_slabs(ref, value):
    rows, width = value.shape
    n_chunks = width // LANES
    for c in range(n_chunks):
        ref[pl.ds(c, rows, stride=n_chunks), :] = value[:, c * LANES:(c + 1) * LANES]


def _from_slabs(ref, first_row, rows, n_chunks):
    return jnp.concatenate([ref[pl.ds(first_row * n_chunks + c, rows, stride=n_chunks), :]
                            for c in range(n_chunks)], axis=1)


def _pack_bf16_pairs(x):
    half = x.shape[1] // 2
    lo = lax.bitcast_convert_type(x[:, :half].astype(BF16).astype(F32), jnp.uint32)
    hi = lax.bitcast_convert_type(x[:, half:].astype(BF16).astype(F32), jnp.uint32)
    return jnp.right_shift(lo, jnp.uint32(16)) | (hi & jnp.uint32(0xFFFF0000))


def _unpack_bf16_pairs(w):
    lo = lax.bitcast_convert_type(jnp.left_shift(w, jnp.uint32(16)), F32).astype(BF16)
    hi = lax.bitcast_convert_type(w & jnp.uint32(0xFFFF0000), F32).astype(BF16)
    return jnp.concatenate([lo, hi], axis=1)


def _moe_dispatch_kernel(zb_ref, dest_ref, x_ref, xs_out, buf, zbuf, sem, zsem):
    tm = x_ref.shape[0]
    n_chunks = buf.shape[0] // tm

    @pl.when(pl.program_id(0) == 0)
    def _():
        rows = zbuf.shape[0]
        zbuf[...] = jnp.zeros(zbuf.shape, zbuf.dtype)

        def zero_copy(e):
            first = pl.multiple_of(zb_ref[e] * rows, rows)
            return pltpu.make_async_copy(zbuf, xs_out.at[pl.ds(first, rows)], zsem)

        def start(e, c):
            @pl.when(zb_ref[e] >= 0)
            def _():
                zero_copy(e).start()
            return c

        def wait(e, c):
            @pl.when(zb_ref[e] >= 0)
            def _():
                zero_copy(e).wait()
            return c

        lax.fori_loop(0, zb_ref.shape[0], start, 0)
        lax.fori_loop(0, zb_ref.shape[0], wait, 0)

    _to_slabs(buf, _pack_bf16_pairs(x_ref[...]))

    def issue(i, c):
        src = buf.at[pl.ds(pl.multiple_of(i * n_chunks, n_chunks), n_chunks)]
        for k in range(TOP_K):
            r = pl.multiple_of(dest_ref[k, i] * n_chunks, n_chunks)
            pltpu.make_async_copy(src, xs_out.at[pl.ds(r, n_chunks)], sem).start(priority=k % 2)
        return c

    lax.fori_loop(0, tm, issue, 0)
    for _ in range(TOP_K):
        pltpu.make_async_copy(buf, xs_out.at[pl.ds(0, tm * n_chunks)], sem).wait()


def _moe_dispatch(zero_blocks, dest, x2, n_blocks, tm=256):
    t, d = x2.shape
    n_chunks = d // 2 // LANES
    grid_spec = pltpu.PrefetchScalarGridSpec(
        num_scalar_prefetch=1,
        grid=(t // tm,),
        in_specs=[pl.BlockSpec((TOP_K, tm), lambda i, zb: (0, i), memory_space=pltpu.SMEM),
                  pl.BlockSpec((tm, d), lambda i, zb: (i, 0))],
        out_specs=pl.BlockSpec(memory_space=pl.ANY),
        scratch_shapes=[pltpu.VMEM((tm * n_chunks, LANES), jnp.uint32),
                        pltpu.VMEM((MOE_BLOCK * n_chunks, LANES), jnp.uint32),
                        pltpu.SemaphoreType.DMA(()), pltpu.SemaphoreType.DMA(())],
    )
    return pl.pallas_call(
        _moe_dispatch_kernel,
        grid_spec=grid_spec,
        out_shape=jax.ShapeDtypeStruct((n_blocks * MOE_BLOCK * n_chunks, LANES), jnp.uint32),
        compiler_params=_cparams("arbitrary"),
        name="moe_dispatch",
    )(zero_blocks, dest, x2)


def _moe_ffn_kernel(be_ref, nu_ref, xs_ref, wg_ref, wu_ref, wd_ref, y_ref, wg_sc, wu_sc, wd_sc):
    b = pl.program_id(0)

    @pl.when(b < nu_ref[0])
    def _():
        @pl.when((b == 0) | (be_ref[b] != be_ref[jnp.maximum(b - 1, 0)]))
        def _():
            wg_sc[...] = wg_ref[0].astype(BF16)
            wu_sc[...] = wu_ref[0].astype(BF16)
            wd_sc[...] = wd_ref[0].astype(BF16)

        d = wg_sc.shape[0]
        xb = _unpack_bf16_pairs(_from_slabs(xs_ref, 0, MOE_BLOCK, d // 2 // LANES))
        gp = jnp.dot(xb, wg_sc[...], preferred_element_type=F32)
        up = jnp.dot(xb, wu_sc[...], preferred_element_type=F32)
        h = (_silu(gp) * up).astype(BF16)
        _to_slabs(y_ref, jnp.dot(h, wd_sc[...], preferred_element_type=F32))


def _moe_ffn(xs, block_expert, n_used, w_gate, w_up, w_down, layer):
    n_blocks = block_expert.shape[0]
    d, ff = w_gate.shape[2], w_gate.shape[3]
    last = lambda b, nu: jnp.minimum(b, nu[0] - 1)
    grid_spec = pltpu.PrefetchScalarGridSpec(
        num_scalar_prefetch=2,
        grid=(n_blocks,),
        in_specs=[
            pl.BlockSpec((MOE_BLOCK * (d // 2 // LANES), LANES), lambda b, be, nu: (last(b, nu), 0)),
            pl.BlockSpec((None, 1, d, ff), lambda b, be, nu: (layer, be[last(b, nu)], 0, 0)),
            pl.BlockSpec((None, 1, d, ff), lambda b, be, nu: (layer, be[last(b, nu)], 0, 0)),
            pl.BlockSpec((None, 1, ff, d), lambda b, be, nu: (layer, be[last(b, nu)], 0, 0)),
        ],
        out_specs=pl.BlockSpec((MOE_BLOCK * (d // LANES), LANES), lambda b, be, nu: (last(b, nu), 0)),
        scratch_shapes=[pltpu.VMEM((d, ff), BF16), pltpu.VMEM((d, ff), BF16), pltpu.VMEM((ff, d), BF16)],
    )
    return pl.pallas_call(
        _moe_ffn_kernel,
        grid_spec=grid_spec,
        out_shape=jax.ShapeDtypeStruct((n_blocks * MOE_BLOCK * (d // LANES), LANES), F32),
        compiler_params=_cparams("arbitrary"),
        name="moe_expert_ffn",
    )(block_expert, n_used, xs, w_gate, w_up, w_down)


def _moe_combine_kernel(dest_ref, y_hbm, gate_ref, x_ref, sg_ref, su_ref, sd_ref, g_ref, b_ref, out_ref, buf, sem):
    tm, d = x_ref.shape
    n_chunks = d // LANES

    def issue(i, c):
        for k in range(TOP_K):
            r = pl.multiple_of(dest_ref[k, i] * n_chunks, n_chunks)
            slot = pl.multiple_of((k * tm + i) * n_chunks, n_chunks)
            pltpu.make_async_copy(y_hbm.at[pl.ds(r, n_chunks)], buf.at[pl.ds(slot, n_chunks)],
                                  sem).start(priority=k % 2)
        return c

    lax.fori_loop(0, tm, issue, 0)
    x = x_ref[...]
    xb = x.astype(BF16)
    hs = _silu(jnp.dot(xb, sg_ref[...], preferred_element_type=F32)) * jnp.dot(xb, su_ref[...], preferred_element_type=F32)
    shared = jnp.dot(hs.astype(BF16), sd_ref[...], preferred_element_type=F32)
    pltpu.make_async_copy(y_hbm.at[pl.ds(0, TOP_K * tm * n_chunks)], buf, sem).wait()
    gates = gate_ref[...]
    routed = _from_slabs(buf, 0, tm, n_chunks) * gates[:, 0:1]
    for k in range(1, TOP_K):
        routed = routed + _from_slabs(buf, k * tm, tm, n_chunks) * gates[:, k:k + 1]
    y = DEEPNORM_ALPHA * x + (routed + shared)
    out_ref[...] = _layer_norm(y, g_ref[...], b_ref[...])


def _moe_combine(dest, y, gates_t, x2, sh_gate, sh_up, sh_down, g, b, tm=256):
    t, d = x2.shape
    row = lambda c: pl.BlockSpec((tm, c), lambda i: (i, 0))
    full = lambda a: pl.BlockSpec(a.shape, lambda i: (0,) * a.ndim)
    return pl.pallas_call(
        _moe_combine_kernel,
        grid=(t // tm,),
        in_specs=[pl.BlockSpec((TOP_K, tm), lambda i: (0, i), memory_space=pltpu.SMEM),
                  pl.BlockSpec(memory_space=pl.ANY),
                  row(TOP_K), row(d), full(sh_gate), full(sh_up), full(sh_down), full(g), full(b)],
        out_specs=row(d),
        out_shape=jax.ShapeDtypeStruct((t, d), F32),
        scratch_shapes=[pltpu.VMEM((TOP_K * tm * (d // LANES), LANES), F32), pltpu.SemaphoreType.DMA(())],
        compiler_params=_cparams("arbitrary"),
        name="moe_combine_ln",
    )(dest, y, gates_t, x2, sh_gate, sh_up, sh_down, g, b)


def _moe_segments(counts, n_tok):
    n_exp = counts.shape[0]
    n_blocks = -(-n_tok * TOP_K // MOE_BLOCK) + n_exp
    nblk = (counts.reshape(n_exp).astype(jnp.int32) + MOE_BLOCK - 1) // MOE_BLOCK
    blk_end = jnp.cumsum(nblk)
    seg_start = ((blk_end - nblk) * MOE_BLOCK).astype(F32).reshape(n_exp, 1)
    block_expert = jnp.sum((blk_end[None, :] <= jnp.arange(n_blocks)[:, None]).astype(jnp.int32), axis=1)
    block_expert = jnp.minimum(block_expert, n_exp - 1)
    n_used = blk_end[-1]
    zero_blocks = jnp.where(nblk > 0, blk_end - 1, -1).astype(jnp.int32)
    return seg_start, block_expert.astype(jnp.int32), n_used.astype(jnp.int32).reshape(1), zero_blocks, n_blocks


def _moe_layer(x2, router_w, router_b, w_gate, w_up, w_down, layer, sh_gate, sh_up, sh_down, g, b):
    t, d = x2.shape
    eidx, gates, rank, counts = _router(x2, router_w, router_b)
    seg_start, block_expert, n_used, zero_blocks, n_blocks = _moe_segments(counts, t)
    dest = _moe_dest(eidx, rank, seg_start)
    xs = _moe_dispatch(zero_blocks, dest, x2, n_blocks)
    y = _moe_ffn(xs, block_expert, n_used, w_gate, w_up, w_down, layer)
    return _moe_combine(dest, y, gates.T, x2, sh_gate.astype(BF16), sh_up.astype(BF16), sh_down.astype(BF16), g, b)


def _compress_kernel(x_ref, pa_ref, pb_ref, wa_ref, wb_ref, w2_ref, out_ref):
    x = x_ref[0]
    nc = x.shape[0]
    ha = jnp.dot((x + pa_ref[...]).astype(BF16), wa_ref[...], preferred_element_type=F32)
    hb = jnp.dot((x + pb_ref[...]).astype(BF16), wb_ref[...], preferred_element_type=F32)
    h = ha + pltpu.roll(hb, nc - 1, 0)
    h = jax.nn.gelu(h, approximate=True)
    out_ref[0] = jnp.dot(h.astype(BF16), w2_ref[...], preferred_element_type=F32).astype(out_ref.dtype)


def _compress(kc, pos, w1, w2, batch, seq):
    g = C_KV_HEADS
    nch = seq // CMP_STRIDE
    half = CMP_LEN // 2
    x = kc.reshape(batch, nch, half * g * HEAD_DIM)
    eye = jnp.eye(g, dtype=F32)
    w1r = w1.reshape(CMP_LEN, HEAD_DIM, CMP_HIDDEN)
    expand = lambda wpart: jnp.einsum('jdh,ge->jgdeh', wpart, eye).reshape(half * g * HEAD_DIM, g * CMP_HIDDEN)
    wa, wb = expand(w1r[:half]).astype(BF16), expand(w1r[half:]).astype(BF16)
    w2e = jnp.einsum('hd,ge->ghed', jnp.pad(w2, ((0, 0), (0, LANES - HEAD_DIM))), eye)
    w2e = w2e.reshape(g * CMP_HIDDEN, g * LANES).astype(BF16)
    tile_pos = lambda p: jnp.broadcast_to(p[:, None, :], (half, g, HEAD_DIM)).reshape(1, half * g * HEAD_DIM)
    pa, pb = tile_pos(pos[:half]), tile_pos(pos[half:])
    full = lambda a: pl.BlockSpec(a.shape, lambda i: (0,) * a.ndim)
    return pl.pallas_call(
        _compress_kernel,
        grid=(batch,),
        in_specs=[pl.BlockSpec((1, nch, x.shape[2]), lambda i: (i, 0, 0)), full(pa), full(pb), full(wa), full(wb),
                  full(w2e)],
        out_specs=pl.BlockSpec((1, nch, g * LANES), lambda i: (i, 0, 0)),
        out_shape=jax.ShapeDtypeStruct((batch, nch, g * LANES), BF16),
        compiler_params=_cparams("parallel"),
        name="nsa_compress",
    )(x, pa, pb, wa, wb, w2e)


def _nsa_cmp_kernel(q_ref, kc_ref, vc_ref, ovt_ref, o_ref, drop_ref, *, rep, n_sel, n_real):
    tq = q_ref.shape[1]
    nc = kc_ref.shape[1]
    nsb = ovt_ref.shape[0]
    t0 = pl.program_id(2) * tq
    scale = HEAD_DIM ** -0.5
    kc = kc_ref[0]
    vc = vc_ref[0]
    tpos = t0 + lax.broadcasted_iota(jnp.int32, (tq, nc), 0)
    cend = lax.broadcasted_iota(jnp.int32, (tq, nc), 1) * CMP_STRIDE + (CMP_LEN - 1)
    cmask = cend <= tpos
    psum = jnp.zeros((tq, nc), F32)
    outs = []
    for r in range(rep):
        q = q_ref[0, :, r * LANES:(r + 1) * LANES]
        sc = lax.dot_general(q, kc, _NT, preferred_element_type=F32) * scale
        sc = jnp.where(cmask, sc, NEG)
        m = jnp.max(sc, axis=-1, keepdims=True)
        ex = jnp.where(cmask, jnp.exp(sc - m), 0.0)
        den = jnp.sum(ex, axis=-1, keepdims=True)
        pc = ex / jnp.where(den > 0, den, 1.0)
        outs.append(jnp.dot(pc.astype(BF16), vc, preferred_element_type=F32))
        psum = psum + pc
    o_ref[0] = jnp.concatenate([_pack_head_pair(outs[r], outs[r + 1]) for r in range(0, rep, 2)], axis=1)
    phi, plo = _split_bf16(psum)
    ovt = ovt_ref[...]
    imp = (lax.dot_general(ovt, phi, _NT, preferred_element_type=F32)
           + lax.dot_general(ovt, plo, _NT, preferred_element_type=F32))
    jblk = lax.broadcasted_iota(jnp.int32, (nsb, tq), 0)
    cur = jnp.right_shift(t0 + lax.broadcasted_iota(jnp.int32, (nsb, tq), 1), SLC_SHIFT)
    forced = (jblk == 0) | (jblk == cur) | (jblk == cur - 1)
    score = jnp.where(jblk > cur, -1.0, jnp.where(forced, SELECT_FORCE, imp))
    rank = jnp.zeros((nsb, tq), F32)
    for k in range(n_real):
        rowk = score[k:k + 1, :]
        ge = jnp.where(rowk >= score, 1.0, 0.0)
        gt = jnp.where(rowk > score, 1.0, 0.0)
        rank = rank + jnp.where(jblk > k, ge, gt)
    drop = jnp.where(jblk <= cur, jnp.where(rank < n_sel, 0.0, 1.0), 1.0)
    drop = jnp.concatenate([drop, jnp.ones((LANES - nsb, tq), F32)], axis=0).T
    drop_ref[0, 0] = pltpu.roll(drop, HEAD_DIM, 1).astype(drop_ref.dtype)


def _nsa_cmp(q, kcmp, vcmp, batch, seq, q_off_blocks, tq=512):
    g = C_KV_HEADS
    rep = C_HEADS // g
    nc = kcmp.shape[1]
    nsb = seq // SLC_BLOCK
    n_sel = min(SLC_TOP_N, nsb)
    cs = np.arange(nc)[:, None] * CMP_STRIDE
    js = np.arange(nsb)[None, :] * SLC_BLOCK
    overlap = np.clip(np.minimum(cs + CMP_LEN, js + SLC_BLOCK) - np.maximum(cs, js), 0, None) / CMP_LEN
    overlap[(seq - CMP_LEN) // CMP_STRIDE + 1:] = 0.0
    nsb_pad = -(-nsb // SUBLANES) * SUBLANES
    ovt = jnp.asarray(np.pad(overlap.T, ((0, nsb_pad - nsb), (0, 0))), BF16)
    q3 = q.reshape(batch, seq, q.shape[1])
    n_real, nsb = nsb, nsb_pad
    assert nsb <= LANES - HEAD_DIM
    o, drop = pl.pallas_call(
        functools.partial(_nsa_cmp_kernel, rep=rep, n_sel=n_sel, n_real=n_real),
        grid=(batch, g, seq // tq),
        in_specs=[pl.BlockSpec((1, tq, rep * LANES), lambda b, gi, i: (b, i, q_off_blocks // rep + gi)),
                  pl.BlockSpec((1, nc, LANES), lambda b, gi, i: (b, 0, gi)),
                  pl.BlockSpec((1, nc, LANES), lambda b, gi, i: (b, 0, gi)),
                  pl.BlockSpec(ovt.shape, lambda b, gi, i: (0, 0))],
        out_specs=[pl.BlockSpec((1, tq, rep * HEAD_DIM), lambda b, gi, i: (b, i, gi)),
                   pl.BlockSpec((1, 1, tq, LANES), lambda b, gi, i: (b, gi, i, 0))],
        out_shape=[jax.ShapeDtypeStruct((batch, seq, C_HEADS * HEAD_DIM), F32),
                   jax.ShapeDtypeStruct((batch, g, seq, LANES), BF16)],
        compiler_params=_cparams("parallel", "parallel", "parallel"),
        name="nsa_compressed_select",
    )(q3, kcmp, vcmp, ovt)
    return o.reshape(batch * seq, C_HEADS * HEAD_DIM), drop


def _nsa_slc_kernel(q_ref, k_ref, vt_ref, drop_ref, o_ref, *, rep, kt):
    tq = q_ref.shape[1]
    t0 = pl.program_id(2) * tq
    n_kt = (t0 + tq + kt - 1) // kt
    upper = lax.broadcasted_iota(jnp.int32, (tq, LANES), 1) >= HEAD_DIM
    drop = drop_ref[0, 0]
    qs = [jnp.where(upper, drop, q_ref[0, :, r * LANES:(r + 1) * LANES] * (HEAD_DIM ** -0.5))
          for r in range(rep)]

    def tile(k0, carry, bias, size=kt):
        k = k_ref[0, pl.ds(k0, size), :]
        vt = vt_ref[:, pl.ds(k0, size)]
        new = []
        for qr, (m, l, acc) in zip(qs, carry):
            s = lax.dot_general(k, qr, _NT, preferred_element_type=F32)
            if bias is not None:
                s = s + bias
            m_new = jnp.maximum(m, jnp.max(s, axis=0, keepdims=True))
            e = jnp.exp(s - m_new)
            corr = jnp.exp(m - m_new)
            l = l * corr + jnp.sum(e, axis=0, keepdims=True)
            acc = acc * corr + jnp.dot(vt, e.astype(BF16), preferred_element_type=F32)
            new.append((m_new, l, acc))
        return tuple(new)

    init = tuple((jnp.full((1, tq), NEG, F32), jnp.zeros((1, tq), F32), jnp.zeros((LANES, tq), F32))
                 for _ in range(rep))
    carry = lax.fori_loop(0, n_kt - 1, lambda j, c: tile(pl.multiple_of(j * kt, kt), c, None), init)
    k_last = pl.multiple_of((n_kt - 1) * kt, kt)
    remaining = t0 + tq - k_last

    def finish(size):
        kpos = k_last + lax.broadcasted_iota(jnp.int32, (size, tq), 0)
        tpos = t0 + lax.broadcasted_iota(jnp.int32, (size, tq), 1)
        final = tile(k_last, carry, jnp.where(kpos <= tpos, 0.0, NEG), size)
        outs = [(acc / l).T for _, l, acc in final]
        o_ref[0] = jnp.concatenate([_pack_head_pair(outs[r], outs[r + 1]) for r in range(0, rep, 2)], axis=1)

    sizes = [s for s in (kt // 4, kt // 2, 3 * kt // 4) if s >= tq and s % LANES == 0] + [kt]
    lower = 0
    for size in sizes:
        pl.when((remaining > lower) & (remaining <= size))(functools.partial(finish, size))
        lower = size


def _nsa_slc(qkv, vt, drop, batch, seq, nblk, q_off, k_off, tq=128, kt=1024):
    g = C_KV_HEADS
    rep = C_HEADS // g
    kt = min(kt, seq)
    arr = qkv.reshape(batch, seq, nblk * LANES)
    out = pl.pallas_call(
        functools.partial(_nsa_slc_kernel, rep=rep, kt=kt),
        grid=(batch, g, seq // tq),
        in_specs=[pl.BlockSpec((1, tq, rep * LANES), lambda b, gi, i: (b, i, q_off // rep + gi)),
                  pl.BlockSpec((1, seq, LANES), lambda b, gi, i: (b, 0, k_off + gi)),
                  pl.BlockSpec((LANES, seq), lambda b, gi, i: (gi, b)),
                  pl.BlockSpec((1, 1, tq, LANES), lambda b, gi, i: (b, gi, i, 0))],
        out_specs=pl.BlockSpec((1, tq, rep * HEAD_DIM), lambda b, gi, i: (b, i, gi)),
        out_shape=jax.ShapeDtypeStruct((batch, seq, C_HEADS * HEAD_DIM), F32),
        compiler_params=_cparams("parallel", "parallel", "parallel"),
        name="nsa_selected",
    )(arr, arr, vt, drop)
    return out.reshape(batch * seq, C_HEADS * HEAD_DIM)


def _even_mixer_layer(x2, batch, seq, tabs, w_in, sinks, w_out, g, b):
    d = x2.shape[1]
    n_heads_in = 3 * A_HEADS + B_Q_HEADS + 2 * B_KV_HEADS
    w = _pad_heads_cols(w_in, n_heads_in).astype(BF16)
    rope = [1] * (2 * A_HEADS) + [0] * A_HEADS + [1] * B_Q_HEADS + [1] * B_KV_HEADS + [0] * B_KV_HEADS
    plan = [(0, c, rope[c]) for c in range(n_heads_in)]
    dilations = [dil for _, dil in A_PATTERNS]
    regroup = [dil for dil in dilations if dil > 1]
    n_a_blocks = 3 * A_HEADS
    qkv, *grouped = _proj(x2, w, tabs, plan, [n_heads_in * LANES], [BF16], dilations=regroup, n_dil=n_a_blocks)
    outs = []
    for window, dilation in A_PATTERNS:
        src, nblk = (qkv, n_heads_in) if dilation == 1 else (grouped[regroup.index(dilation)], n_a_blocks)
        outs.append(_band_attention(src, batch=batch, seq=seq, dilation=dilation, nblk=nblk, q_off=0,
                                    k_off=A_HEADS, v_off=2 * A_HEADS, n_q_heads=A_HEADS, rep=1,
                                    max_dist=window // dilation,
                                    heads=min(8, dilation)))
    qb_off = 3 * A_HEADS
    ob = _band_attention(qkv, batch=batch, seq=seq, dilation=1, nblk=n_heads_in, q_off=qb_off,
                         k_off=qb_off + B_Q_HEADS, v_off=qb_off + B_Q_HEADS + B_KV_HEADS, n_q_heads=B_Q_HEADS,
                         rep=B_Q_HEADS // B_KV_HEADS, max_dist=B_WINDOW - 1, sinks=sinks)
    w_o = _pad_heads_rows(w_out, A_HEADS + B_Q_HEADS).astype(BF16)
    return _outproj_even(outs, dilations, ob, x2, w_o, g.reshape(1, d), b.reshape(1, d))


def _odd_mixer_layer(x2, batch, seq, tabs, w_in, cmpk_pos, cmpk_w1, cmpk_w2, cmpv_pos, cmpv_w1, cmpv_w2, w_out, g, b):
    d = x2.shape[1]
    kvw = C_KV_HEADS * HEAD_DIM
    qw = C_HEADS * HEAD_DIM
    sizes = [qw] + [kvw] * 6 + [3 * C_HEADS]
    offs = np.concatenate([[0], np.cumsum(sizes)])
    wq, wkc, wvc, wks, wvs, wkw, wvw, wgt = [w_in[:, offs[i]:offs[i + 1]] for i in range(8)]
    ph = lambda wpart, n: _pad_heads_cols(wpart, n)
    w = jnp.concatenate([ph(wq, C_HEADS), ph(wks, C_KV_HEADS), ph(wkw, C_KV_HEADS), ph(wvw, C_KV_HEADS),
                         wkc, wvc, jnp.pad(wgt, ((0, 0), (0, LANES - 3 * C_HEADS)))], axis=1).astype(BF16)
    wvs_t = ph(wvs, C_KV_HEADS).T.astype(BF16)
    n16 = C_HEADS + 3 * C_KV_HEADS
    rope16 = [1] * C_HEADS + [TAG_BLOCK] * C_KV_HEADS + [1] * C_KV_HEADS + [0] * C_KV_HEADS
    n_kc = kvw // LANES
    plan = ([(0, c, rope16[c]) for c in range(n16)] + [(1, c, 2) for c in range(n_kc)]
            + [(2, c, 0) for c in range(n_kc)] + [(3, 0, 0)])
    qkv, kc, vc, gate, vs_t = _proj(x2, w, tabs, plan, [n16 * LANES, kvw, kvw, LANES], [BF16, F32, F32, F32],
                                    wt=wvs_t, seq=seq)
    kcmp = _compress(kc, cmpk_pos, cmpk_w1, cmpk_w2, batch, seq)
    vcmp = _compress(vc, cmpv_pos, cmpv_w1, cmpv_w2, batch, seq)
    o_cmp, drop = _nsa_cmp(qkv, kcmp, vcmp, batch, seq, 0)
    ks_off = C_HEADS
    o_slc = _nsa_slc(qkv, vs_t, drop, batch, seq, n16, 0, ks_off)
    kw_off = ks_off + C_KV_HEADS
    o_win = _band_attention(qkv, batch=batch, seq=seq, dilation=1, nblk=n16, q_off=0, k_off=kw_off,
                            v_off=kw_off + C_KV_HEADS, n_q_heads=C_HEADS, rep=C_HEADS // C_KV_HEADS,
                            max_dist=NSA_WINDOW - 1, heads=C_HEADS // C_KV_HEADS, pack_out=True)
    return _outproj_odd(o_cmp, o_slc, o_win, gate, x2, w_out.astype(BF16), g.reshape(1, d), b.reshape(1, d))


def kernel(x, positions, even_w_in, even_sinks, even_w_out, odd_w_in, odd_cmpk_pos, odd_cmpk_w1, odd_cmpk_w2, odd_cmpv_pos, odd_cmpv_w1, odd_cmpv_w2, odd_w_out, mix_ln_g, mix_ln_b, moe_router_w, moe_router_b, moe_w_gate, moe_w_up, moe_w_down, moe_sh_gate, moe_sh_up, moe_sh_down, ffn_ln_g, ffn_ln_b):
    batch, seq, d = x.shape
    x2 = x.reshape(batch * seq, d)
    tabs = _rope_tables(positions)
    depth = mix_ln_g.shape[0]
    for layer in range(depth):
        j = layer // 2
        if layer % 2 == 0:
            x2 = _even_mixer_layer(x2, batch, seq, tabs, even_w_in[j], even_sinks[j], even_w_out[j],
                                   mix_ln_g[layer], mix_ln_b[layer])
        else:
            x2 = _odd_mixer_layer(x2, batch, seq, tabs, odd_w_in[j], odd_cmpk_pos[j], odd_cmpk_w1[j], odd_cmpk_w2[j],
                                  odd_cmpv_pos[j], odd_cmpv_w1[j], odd_cmpv_w2[j], odd_w_out[j],
                                  mix_ln_g[layer], mix_ln_b[layer])
        x2 = _moe_layer(x2, moe_router_w[layer], moe_router_b[layer], moe_w_gate, moe_w_up, moe_w_down, layer,
                        moe_sh_gate[layer], moe_sh_up[layer], moe_sh_down[layer],
                        ffn_ln_g[layer].reshape(1, d), ffn_ln_b[layer].reshape(1, d))
    return x2.reshape(batch, seq, d)
```
